```python
import math
import jax
import jax.numpy as jnp
from jax import lax
import numpy as np

D_MODEL = 1024
BATCH = 16
SEQ = 4096
DEPTH = 4

GRID_W = 64
CTX_LEN = 256
N_MIXERS = 2
N_ATTN_LAYERS = (DEPTH + 1) // 2
N_RWKV_LAYERS = DEPTH // 2

DA_HEAD_DIM = 64
DA_HEADS = D_MODEL // (2 * DA_HEAD_DIM)
ROPE_THETA = 10000.0
Q_BLOCK = 128

RW_HEAD = 64
RW_HEADS = D_MODEL // RW_HEAD
LORA_W = 64
LORA_A = 64
LORA_V = 32
LORA_G = 128
N_SHIFT_MIX = 6
LNX_EPS = 64e-5

D_FF = 2816

EPS = 1e-6

kernel_name = 'hybrid_diffattn_rwkv7_convglu_dit'


def rms_norm(x, g, eps=EPS):
    xf = x.astype(jnp.float32)
    y = xf * lax.rsqrt(jnp.mean(xf * xf, axis=-1, keepdims=True) + eps)
    return (y * g.astype(jnp.float32)).astype(x.dtype)


def modulate(h, shift, scale):
    return h * (1 + scale) + shift


def shift_prev(t):
    return jnp.pad(t, ((0, 0), (1, 0), (0, 0)))[:, :-1]


def shift_next(t):
    return jnp.pad(t, ((0, 0), (0, 1), (0, 0)))[:, 1:]


def axial_rope_tables(rows, dtype):
    n_tok = rows * GRID_W
    t = jnp.arange(n_tok)
    row_pos = (t // GRID_W).astype(jnp.float32)
    col_pos = (t % GRID_W).astype(jnp.float32)
    n_freq = DA_HEAD_DIM // 4
    inv_freq = ROPE_THETA ** (-jnp.arange(n_freq, dtype=jnp.float32) / n_freq)
    ang = jnp.concatenate([row_pos[:, None] * inv_freq, col_pos[:, None] * inv_freq], axis=-1)
    return jnp.cos(ang).astype(dtype), jnp.sin(ang).astype(dtype)


def apply_rope(t, cos, sin):
    half = DA_HEAD_DIM // 2
    t1, t2 = t[..., :half], t[..., half:]
    return jnp.concatenate([t1 * cos - t2 * sin, t2 * cos + t1 * sin], axis=-1)


def diff_attend(q, k, v, lam):
    s = jnp.einsum('bqhcd,bkhcd->bhcqk', q, k).astype(jnp.float32) * (DA_HEAD_DIM ** -0.5)
    p = jax.nn.softmax(s, axis=-1)
    w = p[:, :, 0] - lam * p[:, :, 1]
    return jnp.einsum('bhqk,bkhe->bqhe', w.astype(v.dtype), v)


def diff_attention(h_lat, h_ctx, cos, sin, w_qkv, lam_vecs, subln_g, w_o, lambda_init, ctx_out):
    def project(h):
        b, n, _ = h.shape
        q, k, v = jnp.split(h @ w_qkv, 3, axis=-1)
        return (q.reshape(b, n, DA_HEADS, 2, DA_HEAD_DIM),
                k.reshape(b, n, DA_HEADS, 2, DA_HEAD_DIM),
                v.reshape(b, n, DA_HEADS, 2 * DA_HEAD_DIM))

    lv = lam_vecs.astype(jnp.float32)
    lam = jnp.exp(jnp.sum(lv[0] * lv[1])) - jnp.exp(jnp.sum(lv[2] * lv[3])) + lambda_init

    def finish(o):
        b, n = o.shape[:2]
        o = rms_norm(o, subln_g) * (1.0 - lambda_init)
        return o.reshape(b, n, D_MODEL) @ w_o

    q_c, k_c, v_c = project(h_ctx)
    q_l, k_l, v_l = project(h_lat)
    rc, rs = cos[None, :, None, None, :], sin[None, :, None, None, :]
    q_l = apply_rope(q_l, rc, rs)
    k_l = apply_rope(k_l, rc, rs)
    k_all = jnp.concatenate([k_c, k_l], axis=1)
    v_all = jnp.concatenate([v_c, v_l], axis=1)

    b, n = h_lat.shape[:2]
    n_blk = n // Q_BLOCK
    q_blocks = jnp.moveaxis(q_l.reshape(b, n_blk, Q_BLOCK, DA_HEADS, 2, DA_HEAD_DIM), 1, 0)
    o_blocks = lax.map(lambda qb: diff_attend(qb, k_all, v_all, lam), q_blocks)
    o_l = jnp.moveaxis(o_blocks, 0, 1).reshape(b, n, DA_HEADS, 2 * DA_HEAD_DIM)
    out_lat = finish(o_l)
    out_ctx = finish(diff_attend(q_c, k_c, v_c, lam)) if ctx_out else None
    return out_lat, out_ctx


def wkv_scan(r, w, k, v, a, b, s0, reverse):
    emit = r is not None
    to_time = lambda t: jnp.moveaxis(t.astype(jnp.float32), 1, 0)
    xs = tuple(to_time(t) for t in (w, k, v, a, b)) + ((to_time(r),) if emit else ())

    def step(S, inp):
        w_t, k_t, v_t, a_t, b_t = inp[:5]
        Sa = jnp.einsum('bhvk,bhk->bhv', S, a_t)
        S = (S * w_t[:, :, None, :] + Sa[..., None] * b_t[:, :, None, :]
             + v_t[..., None] * k_t[:, :, None, :])
        y = jnp.einsum('bhvk,bhk->bhv', S, inp[5]) if emit else None
        return S, y

    S, ys = lax.scan(step, s0, xs, reverse=reverse)
    return S, (jnp.moveaxis(ys, 0, 1) if emit else None)


def group_norm_heads(y, g, b):
    yf = y.astype(jnp.float32)
    mu = jnp.mean(yf, axis=-1, keepdims=True)
    var = jnp.mean(jnp.square(yf - mu), axis=-1, keepdims=True)
    yn = (yf - mu) * lax.rsqrt(var + LNX_EPS)
    return (yn * g.astype(jnp.float32).reshape(RW_HEADS, RW_HEAD)
            + b.astype(jnp.float32).reshape(RW_HEADS, RW_HEAD))


def rwkv7_bidir(h, p, v_first, s0, emit):
    b, n, _ = h.shape
    heads = lambda t: t.reshape(b, n, RW_HEADS, RW_HEAD)
    xx_p = shift_prev(h) - h
    xx_n = shift_next(h) - h
    xr, xw, xk, xv, xa, xg = (h + xx_p * p['mix_prev'][m] + xx_n * p['mix_next'][m]
                              for m in range(N_SHIFT_MIX))
    k = xk @ p['w_k']
    v = xv @ p['w_v']
    if v_first is None:
        v_first = v
    else:
        v = v + (v_first - v) * jax.nn.sigmoid(p['v0'] + (xv @ p['v1']) @ p['v2'])
    kk = heads(k * p['k_k']).astype(jnp.float32)
    kk = kk / jnp.maximum(jnp.sqrt(jnp.sum(kk * kk, axis=-1, keepdims=True)), 1e-12)
    r = heads(xr @ p['w_r']) if emit else None
    states, ys, keys = [], [], []
    for d, reverse in enumerate((False, True)):
        w_log = -jax.nn.softplus(-(p['w0'][d] + jnp.tanh(xw @ p['w1'][d]) @ p['w2'][d]).astype(jnp.float32)) - 0.5
        decay = jnp.exp(-jnp.exp(w_log))
        a = jax.nn.sigmoid(p['a0'][d] + (xa @ p['a1'][d]) @ p['a2'][d])
        k_d = heads(k * (1 + (a - 1) * p['k_a']))
        S, y = wkv_scan(r, heads(decay), k_d, heads(v), -kk, kk * heads(a), s0[d], reverse)
        states.append(S)
        ys.append(y)
        keys.append(k_d)
    new_states = (states[0], states[1])
    if not emit:
        return None, new_states, v_first
    y = group_norm_heads(ys[0] + ys[1], p['lnx_g'], p['lnx_b']).astype(h.dtype)
    bonus = (jnp.sum(r * keys[0] * p['r_k'], axis=-1, keepdims=True)
             + jnp.sum(r * keys[1] * p['r_k'], axis=-1, keepdims=True)) * heads(v)
    g = jax.nn.sigmoid(xg @ p['g1']) @ p['g2']
    out = ((y + bonus).reshape(b, n, D_MODEL) * g) @ p['w_o']
    return out, new_states, v_first


def conv_glu(h, w_up, conv_w, conv_b, w_down):
    gate, val = jnp.split(h @ w_up, 2, axis=-1)
    gate = shift_prev(gate) * conv_w[0] + gate * conv_w[1] + shift_next(gate) * conv_w[2] + conv_b
    return (jax.nn.silu(gate) * val) @ w_down


def setup_inputs(seed: int = 0) -> dict:
    key = jax.random.key(seed)
    ks = iter(jax.random.split(key, 64))
    nrm = lambda shape, scale: jax.random.normal(next(ks), shape, jnp.float32) * scale
    unif = lambda shape, lo, hi: jax.random.uniform(next(ks), shape, jnp.float32, lo, hi)
    D, L, NA, NR = D_MODEL, DEPTH, N_ATTN_LAYERS, N_RWKV_LAYERS
    return {
        'x': nrm((BATCH, SEQ, D), 1.0),
        'c': nrm((BATCH, D), 1.0),
        'ctx': nrm((BATCH, CTX_LEN, D), 1.0),
        'c_ctx': nrm((D,), 1.0),
        'ada_w': nrm((L, D, 6 * D), 0.5 * D ** -0.5),
        'ada_b': nrm((L, 6 * D), 0.02),
        'norm_mix_g': 1.0 + nrm((L, D), 0.02),
        'norm_ffn_g': 1.0 + nrm((L, D), 0.02),
        'ffn_w_up': nrm((L, D, 2 * D_FF), D ** -0.5),
        'ffn_conv_w': nrm((L, 3, D_FF), 3 ** -0.5),
        'ffn_conv_b': nrm((L, D_FF), 0.02),
        'ffn_w_down': nrm((L, D_FF, D), D_FF ** -0.5),
        'da_w_qkv': nrm((NA, D, 3 * D), D ** -0.5),
        'da_lambda': nrm((NA, 4, DA_HEAD_DIM), 0.1),
        'da_subln_g': 1.0 + nrm((NA, 2 * DA_HEAD_DIM), 0.02),
        'da_w_o': nrm((NA, D, D), D ** -0.5),
        'rw_mix_prev': unif((NR, N_SHIFT_MIX, D), 0.0, 0.6),
        'rw_mix_next': unif((NR, N_SHIFT_MIX, D), 0.0, 0.6),
        'rw_w_r': nrm((NR, D, D), D ** -0.5),
        'rw_w_k': nrm((NR, D, D), D ** -0.5),
        'rw_w_v': nrm((NR, D, D), D ** -0.5),
        'rw_w0': unif((NR, 2, D), -6.0, 0.0),
        'rw_w1': nrm((NR, 2, D, LORA_W), D ** -0.5),
        'rw_w2': nrm((NR, 2, LORA_W, D), 0.5 * LORA_W ** -0.5),
        'rw_a0': nrm((NR, 2, D), 0.5),
        'rw_a1': nrm((NR, 2, D, LORA_A), D ** -0.5),
        'rw_a2': nrm((NR, 2, LORA_A, D), 0.5 * LORA_A ** -0.5),
        'rw_v0': nrm((NR - 1, D), 0.5),
        'rw_v1': nrm((NR - 1, D, LORA_V), D ** -0.5),
        'rw_v2': nrm((NR - 1, LORA_V, D), 0.5 * LORA_V ** -0.5),
        'rw_k_k': 0.85 + nrm((NR, D), 0.05),
        'rw_k_a': 1.0 + nrm((NR, D), 0.05),
        'rw_r_k': nrm((NR, RW_HEADS, RW_HEAD), 0.1),
        'rw_g1': nrm((NR, D, LORA_G), D ** -0.5),
        'rw_g2': nrm((NR, LORA_G, D), LORA_G ** -0.5),
        'rw_lnx_g': 1.0 + nrm((NR, D), 0.02),
        'rw_lnx_b': nrm((NR, D), 0.02),
        'rw_w_o': nrm((NR, D, D), D ** -0.5),
        'final_norm_g': 1.0 + nrm((D,), 0.02),
    }


def reference(x, c, ctx, c_ctx, ada_w, ada_b, norm_mix_g, norm_ffn_g,
              ffn_w_up, ffn_conv_w, ffn_conv_b, ffn_w_down,
              da_w_qkv, da_lambda, da_subln_g, da_w_o,
              rw_mix_prev, rw_mix_next, rw_w_r, rw_w_k, rw_w_v, rw_w0, rw_w1, rw_w2,
              rw_a0, rw_a1, rw_a2, rw_v0, rw_v1, rw_v2, rw_k_k, rw_k_a, rw_r_k,
              rw_g1, rw_g2, rw_lnx_g, rw_lnx_b, rw_w_o, final_norm_g):
    B, T, _ = x.shape
    ROWS = T // GRID_W
    cos, sin = axial_rope_tables(ROWS, x.dtype)
    silu_c = jax.nn.silu(c)
    silu_cc = jax.nn.silu(c_ctx)
    v_first_lat = None
    v_first_ctx = None
    for i in range(DEPTH):
        last = i == DEPTH - 1
        j = i // N_MIXERS
        mod_l = jnp.split((silu_c @ ada_w[i] + ada_b[i])[:, None, :], 6, axis=-1)
        mod_c = jnp.split(silu_cc @ ada_w[i] + ada_b[i], 6, axis=-1)
        hl = modulate(rms_norm(x, norm_mix_g[i]), mod_l[0], mod_l[1])
        hc = modulate(rms_norm(ctx, norm_mix_g[i]), mod_c[0], mod_c[1])
        if i % N_MIXERS == 0:
            lambda_init = 0.8 - 0.6 * math.exp(-0.3 * i)
            out_l, out_c = diff_attention(hl, hc, cos, sin, da_w_qkv[j], da_lambda[j], da_subln_g[j],
                                          da_w_o[j], lambda_init, not last)
        else:
            p = dict(mix_prev=rw_mix_prev[j], mix_next=rw_mix_next[j], w_r=rw_w_r[j], w_k=rw_w_k[j],
                     w_v=rw_w_v[j], w0=rw_w0[j], w1=rw_w1[j], w2=rw_w2[j], a0=rw_a0[j], a1=rw_a1[j],
                     a2=rw_a2[j], k_k=rw_k_k[j], k_a=rw_k_a[j], r_k=rw_r_k[j], g1=rw_g1[j], g2=rw_g2[j],
                     lnx_g=rw_lnx_g[j], lnx_b=rw_lnx_b[j], w_o=rw_w_o[j])
            if j > 0:
                p.update(v0=rw_v0[j - 1], v1=rw_v1[j - 1], v2=rw_v2[j - 1])
            zero = jnp.zeros((B, RW_HEADS, RW_HEAD, RW_HEAD), jnp.float32)
            out_c, s_ctx, v_first_ctx = rwkv7_bidir(hc, p, v_first_ctx, (zero, zero), not last)
            out_l, _, v_first_lat = rwkv7_bidir(hl, p, v_first_lat, s_ctx, True)
        x = x + mod_l[2] * out_l
        hf = modulate(rms_norm(x, norm_ffn_g[i]), mod_l[3], mod_l[4])
        x = x + mod_l[5] * conv_glu(hf, ffn_w_up[i], ffn_conv_w[i], ffn_conv_b[i], ffn_w_down[i])
        if not last:
            ctx = ctx + mod_c[2] * out_c
            hfc = modulate(rms_norm(ctx, norm_ffn_g[i]), mod_c[3], mod_c[4])
            ctx = ctx + mod_c[5] * conv_glu(hfc, ffn_w_up[i], ffn_conv_w[i], ffn_conv_b[i], ffn_w_down[i])
    return rms_norm(x, final_norm_g)
```

```python
import functools
import math

import jax
import jax.numpy as jnp
from jax import lax
from jax.experimental import pallas as pl
from jax.experimental.pallas import tpu as pltpu

D_MODEL = 1024
GRID_W = 64
DA_HEAD_DIM = 64
DA_HEADS = D_MODEL // (2 * DA_HEAD_DIM)
ROPE_THETA = 10000.0
RW_HEAD = 64
N_SHIFT_MIX = 6
LNX_EPS = 64e-5
EPS = 1e-6

LANES = 128
SUBLANES = 8
ROW_TILE = 256
WKV_CHUNK = 32
KV_TILE = 256
FF_TILE = 256
VMEM_LIMIT = 56 * 1024 * 1024

F32 = jnp.float32
BF16 = jnp.bfloat16


def _dot(a, b):
    return jnp.dot(a.astype(BF16), b.astype(BF16), preferred_element_type=F32)


def _dot_nt(a, b):
    return lax.dot_general(a.astype(BF16), b.astype(BF16), (((1,), (1,)), ((), ())),
                           preferred_element_type=F32)


def _dot_tn(a, b):
    return lax.dot_general(a.astype(BF16), b.astype(BF16), (((0,), (0,)), ((), ())),
                           preferred_element_type=F32)


def _rms(x):
    return x * lax.rsqrt(jnp.mean(x * x, axis=-1, keepdims=True) + EPS)


def _norm_mod(x, g, shift, scale):
    return _rms(x) * g * (1.0 + scale) + shift


def _sigmoid(x):
    return 1.0 / (1.0 + jnp.exp(-x))


def _softplus(x):
    return jnp.maximum(x, 0.0) + jnp.log1p(jnp.exp(-jnp.abs(x)))


def _lane_iota(shape):
    return lax.broadcasted_iota(jnp.int32, shape, len(shape) - 1)


def _row_iota(shape):
    return lax.broadcasted_iota(jnp.int32, shape, len(shape) - 2)


def _group_ones():
    r = lax.broadcasted_iota(jnp.int32, (LANES, LANES), 0) // RW_HEAD
    c = lax.broadcasted_iota(jnp.int32, (LANES, LANES), 1) // RW_HEAD
    return (r == c).astype(BF16)


def _group_sum(z):
    ones = _group_ones()
    parts = [jnp.dot(z[:, j:j + LANES].astype(BF16), ones, preferred_element_type=F32)
             for j in range(0, z.shape[1], LANES)]
    return jnp.concatenate(parts, axis=1)


def _shifted(h, halo_prev, halo_next, prev_ok, next_ok):
    rows = h.shape[0]
    ri = _row_iota(h.shape)
    first = jnp.where(prev_ok, halo_prev[SUBLANES - 1:SUBLANES, :], 0.0)
    last = jnp.where(next_ok, halo_next[0:1, :], 0.0)
    h_prev = jnp.where(ri == 0, first, pltpu.roll(h, 1, 0))
    h_next = jnp.where(ri == rows - 1, last, pltpu.roll(h, rows - 1, 0))
    return h_prev, h_next


def _edge_flags(t, n_tiles, ctx_tiles):
    prev_ok = jnp.logical_and(t != 0, t != ctx_tiles)
    next_ok = jnp.logical_and(t != n_tiles - 1, t != ctx_tiles - 1)
    return prev_ok, next_ok


def _const_spec(shape):
    zeros = (0,) * len(shape)
    return pl.BlockSpec(shape, lambda *_: zeros, pipeline_mode=pl.Buffered(1))


def _tile_spec(t0=0):
    return pl.BlockSpec((None, ROW_TILE, D_MODEL), lambda b, t: (b, t + t0, 0))


def _halo_specs(n_rows, t0=0):
    per = ROW_TILE // SUBLANES
    last = n_rows // SUBLANES - 1
    prev = pl.BlockSpec((None, SUBLANES, D_MODEL),
                        lambda b, t: (b, jnp.maximum((t + t0) * per - 1, 0), 0))
    nxt = pl.BlockSpec((None, SUBLANES, D_MODEL),
                       lambda b, t: (b, jnp.minimum((t + t0 + 1) * per, last), 0))
    return prev, nxt


def _mod_spec(ctx_tiles, t0=0):
    return pl.BlockSpec((None, None, SUBLANES, D_MODEL),
                        lambda b, t: (b, ((t + t0) >= ctx_tiles).astype(jnp.int32), 0, 0))


def _params():
    return pltpu.CompilerParams(vmem_limit_bytes=VMEM_LIMIT)


def _ada_kernel(c_ref, w_ref, b_ref, o_ref):
    c = c_ref[...]
    s = c * _sigmoid(c)
    o_ref[...] = jnp.dot(s, w_ref[...], precision=lax.Precision.HIGHEST,
                         preferred_element_type=F32) + b_ref[...]


def _ada_table(cc, ada_w, ada_b):
    n_layers, _, six_d = ada_w.shape
    rows = cc.shape[0]
    nb = 1536
    return pl.pallas_call(
        _ada_kernel,
        grid=(n_layers, six_d // nb),
        in_specs=[pl.BlockSpec((rows, D_MODEL), lambda l, n: (0, 0)),
                  pl.BlockSpec((None, D_MODEL, nb), lambda l, n: (l, 0, n)),
                  pl.BlockSpec((None, 1, nb), lambda l, n: (l, 0, n))],
        out_specs=pl.BlockSpec((None, rows, nb), lambda l, n: (l, 0, n)),
        out_shape=jax.ShapeDtypeStruct((n_layers, rows, six_d), F32),
        compiler_params=_params(),
        name="ada_table",
    )(cc, ada_w, ada_b.reshape(n_layers, 1, six_d))


def _rope(slab, cos, sin_signed):
    lane = _lane_iota(slab.shape)
    first = (lane % DA_HEAD_DIM) < (DA_HEAD_DIM // 2)
    partner = jnp.where(first, pltpu.roll(slab, LANES - DA_HEAD_DIM // 2, 1),
                        pltpu.roll(slab, DA_HEAD_DIM // 2, 1))
    return slab * cos + partner * sin_signed


def _qkv_kernel(x_ref, mod_ref, g_ref, w_ref, cos_ref, sin_ref, q_ref, k_ref, v_ref):
    mod = mod_ref[...]
    h = _norm_mod(x_ref[...], g_ref[...], mod[0:1], mod[1:2]).astype(BF16)
    cos = cos_ref[...]
    sin = sin_ref[...]
    wide = 2 * LANES
    for j in range(0, D_MODEL, wide):
        qa = jnp.dot(h, w_ref[:, j:j + wide], preferred_element_type=F32)
        ka = jnp.dot(h, w_ref[:, D_MODEL + j:D_MODEL + j + wide], preferred_element_type=F32)
        for i in range(0, wide, LANES):
            q_ref[:, j + i:j + i + LANES] = (
                _rope(qa[:, i:i + LANES], cos, sin) * (DA_HEAD_DIM ** -0.5)).astype(BF16)
            k_ref[:, j + i:j + i + LANES] = _rope(ka[:, i:i + LANES], cos, sin).astype(BF16)
        v_ref[:, j:j + wide] = jnp.dot(
            h, w_ref[:, 2 * D_MODEL + j:2 * D_MODEL + j + wide],
            preferred_element_type=F32).astype(BF16)


def _qkv(x, mod, g, w_qkv, cos, sin, ctx_tiles):
    bsz, n_rows, _ = x.shape
    out = jax.ShapeDtypeStruct((bsz, n_rows, D_MODEL), BF16)
    tab = pl.BlockSpec((ROW_TILE, LANES), lambda b, t: (t, 0))
    return pl.pallas_call(
        _qkv_kernel,
        grid=(bsz, n_rows // ROW_TILE),
        in_specs=[_tile_spec(), _mod_spec(ctx_tiles), _const_spec((1, D_MODEL)),
                  _const_spec((D_MODEL, 3 * D_MODEL)), tab, tab],
        out_specs=[_tile_spec(), _tile_spec(), _tile_spec()],
        out_shape=[out, out, out],
        compiler_params=_params(),
        name="attn_qkv",
    )(x, mod, g, w_qkv, cos, sin)


def _flash_kernel(lam_ref, g_ref, q_ref, k_ref, v_ref, o_ref, m_ref, l_ref, acc_ref,
                  *, ctx_tiles, t0, ctx_kv, all_kv, lambda_init):
    t = pl.program_id(2) + t0
    q = q_ref[...]
    lo = _lane_iota(q.shape) < DA_HEAD_DIM
    zero = jnp.zeros_like(q)
    qs = (jnp.where(lo, q, zero), jnp.where(lo, zero, q))

    m_ref[...] = jnp.full(m_ref.shape, -jnp.inf, F32)
    l_ref[...] = jnp.zeros(l_ref.shape, F32)
    acc_ref[...] = jnp.zeros(acc_ref.shape, F32)

    def body(j, carry):
        start = pl.multiple_of(j * KV_TILE, KV_TILE)
        kc = k_ref[pl.ds(start, KV_TILE), :]
        vc = v_ref[pl.ds(start, KV_TILE), :]
        for c in range(2):
            s = lax.dot_general(qs[c], kc, (((1,), (1,)), ((), ())), preferred_element_type=F32)
            m_prev = m_ref[c]
            m_new = jnp.maximum(m_prev, jnp.max(s, axis=-1, keepdims=True))
            alpha = jnp.exp(m_prev - m_new)
            p = jnp.exp(s - m_new)
            l_ref[c] = alpha * l_ref[c] + jnp.sum(p, axis=-1, keepdims=True)
            acc_ref[c] = alpha * acc_ref[c] + jnp.dot(p.astype(BF16), vc,
                                                      preferred_element_type=F32)
            m_ref[c] = m_new
        return carry

    n_kv = jnp.where(t < ctx_tiles, ctx_kv, all_kv)
    lax.fori_loop(0, n_kv, body, 0)

    lv = lam_ref[...]
    lam = (jnp.exp(jnp.sum(lv[0:1] * lv[1:2], axis=-1, keepdims=True))
           - jnp.exp(jnp.sum(lv[2:3] * lv[3:4], axis=-1, keepdims=True)) + lambda_init)
    o = acc_ref[0] / l_ref[0] - lam * (acc_ref[1] / l_ref[1])
    o_ref[...] = (_rms(o) * g_ref[...] * (1.0 - lambda_init)).astype(BF16)


def _flash(q, k, v, lam_vecs, subln_g, ctx_tiles, t0, lambda_init):
    bsz, n_rows, _ = q.shape
    n_tiles = n_rows // ROW_TILE
    kv_spec = pl.BlockSpec((None, n_rows, LANES), lambda b, h, t: (b, 0, h))
    q_spec = pl.BlockSpec((None, ROW_TILE, LANES), lambda b, h, t: (b, t + t0, h))
    kern = functools.partial(
        _flash_kernel, ctx_tiles=ctx_tiles, t0=t0, ctx_kv=ctx_tiles * ROW_TILE // KV_TILE,
        all_kv=n_rows // KV_TILE, lambda_init=lambda_init)
    return pl.pallas_call(
        kern,
        grid=(bsz, DA_HEADS, n_tiles - t0),
        in_specs=[pl.BlockSpec(lam_vecs.shape, lambda b, h, t: (0, 0)),
                  pl.BlockSpec((1, LANES), lambda b, h, t: (0, 0)),
                  q_spec, kv_spec, kv_spec],
        out_specs=q_spec,
        out_shape=jax.ShapeDtypeStruct((bsz, n_rows, D_MODEL), BF16),
        scratch_shapes=[pltpu.VMEM((2, ROW_TILE, 1), F32), pltpu.VMEM((2, ROW_TILE, 1), F32),
                        pltpu.VMEM((2, ROW_TILE, LANES), F32)],
        compiler_params=_params(),
        name="diff_flash",
    )(lam_vecs, subln_g.reshape(1, LANES), q, k, v)


def _proj_residual_kernel(x_ref, z_ref, mod_ref, w_ref, o_ref):
    out = jnp.dot(z_ref[...], w_ref[...], preferred_element_type=F32)
    o_ref[...] = x_ref[...] + mod_ref[2:3, :] * out


def _proj_residual(x, z, mod, w, ctx_tiles, t0):
    bsz, n_rows, _ = x.shape
    n_tiles = n_rows // ROW_TILE - t0
    return pl.pallas_call(
        _proj_residual_kernel,
        grid=(bsz, n_tiles),
        in_specs=[_tile_spec(t0), _tile_spec(t0), _mod_spec(ctx_tiles, t0),
                  _const_spec((D_MODEL, D_MODEL))],
        out_specs=_tile_spec(),
        out_shape=jax.ShapeDtypeStruct((bsz, n_tiles * ROW_TILE, D_MODEL), F32),
        compiler_params=_params(),
        name="proj_residual",
    )(x, z, mod, w)


def _ffn_kernel(x_ref, xp_ref, xn_ref, mod_ref, g_ref, wup_ref, cw_ref, cb_ref, wdn_ref,
                fin_ref, o_ref, *, n_tiles, ctx_tiles, d_ff, final):
    t = pl.program_id(1)
    prev_ok, next_ok = _edge_flags(t, n_tiles, ctx_tiles)
    mod = mod_ref[...]
    g = g_ref[...]
    x = x_ref[...]
    h = _norm_mod(x, g, mod[3:4], mod[4:5]).astype(BF16)
    hp = _norm_mod(xp_ref[...], g, mod[3:4], mod[4:5]).astype(BF16)
    hn = _norm_mod(xn_ref[...], g, mod[3:4], mod[4:5]).astype(BF16)
    acc = jnp.zeros((ROW_TILE, D_MODEL), F32)
    for f in range(0, d_ff, FF_TILE):
        wg = wup_ref[:, f:f + FF_TILE]
        gate = jnp.dot(h, wg, preferred_element_type=F32)
        gate_p = jnp.dot(hp, wg, preferred_element_type=F32)
        gate_n = jnp.dot(hn, wg, preferred_element_type=F32)
        val = jnp.dot(h, wup_ref[:, d_ff + f:d_ff + f + FF_TILE], preferred_element_type=F32)
        g_prev, g_next = _shifted(gate, gate_p, gate_n, prev_ok, next_ok)
        cw = cw_ref[:, f:f + FF_TILE]
        conv = g_prev * cw[0:1] + gate * cw[1:2] + g_next * cw[2:3] + cb_ref[:, f:f + FF_TILE]
        act = conv * _sigmoid(conv) * val
        acc = acc + jnp.dot(act.astype(BF16), wdn_ref[f:f + FF_TILE, :],
                            preferred_element_type=F32)
    y = x + mod[5:6] * acc
    if final:
        y = _rms(y) * fin_ref[...]
    o_ref[...] = y


def _ffn(x, mod, g, w_up, conv_w, conv_b, w_down, final_g, ctx_tiles, final):
    bsz, n_rows, _ = x.shape
    n_tiles = n_rows // ROW_TILE
    d_ff = w_down.shape[0]
    prev, nxt = _halo_specs(n_rows)
    kern = functools.partial(_ffn_kernel, n_tiles=n_tiles, ctx_tiles=ctx_tiles, d_ff=d_ff,
                             final=final)
    return pl.pallas_call(
        kern,
        grid=(bsz, n_tiles),
        in_specs=[_tile_spec(), prev, nxt, _mod_spec(ctx_tiles), _const_spec((1, D_MODEL)),
                  _const_spec((D_MODEL, 2 * d_ff)), _const_spec((3, d_ff)),
                  _const_spec((1, d_ff)), _const_spec((d_ff, D_MODEL)),
                  _const_spec((1, D_MODEL))],
        out_specs=_tile_spec(),
        out_shape=jax.ShapeDtypeStruct((bsz, n_rows, D_MODEL), F32),
        compiler_params=_params(),
        name="conv_glu",
    )(x, x, x, mod, g, w_up, conv_w, conv_b, w_down, final_g)


def _rwkv_proj_kernel(*refs, n_tiles, ctx_tiles, has_vfirst):
    (x_ref, xp_ref, xn_ref, mod_ref, g_ref, mp_ref, mn_ref, wr_ref, wk_ref, wv_ref,
     w0_ref, w1_ref, w2_ref, a0_ref, a1_ref, a2_ref, kk_ref, ka_ref, rk_ref,
     g1_ref, g2_ref) = refs[:21]
    refs = refs[21:]
    if has_vfirst:
        vf_ref, v0_ref, v1_ref, v2_ref = refs[:4]
        refs = refs[4:]
    r_ref, k_ref, v_ref, kkn_ref, bonus_ref, gate_ref, lw_ref, a_ref = refs

    t = pl.program_id(1)
    prev_ok, next_ok = _edge_flags(t, n_tiles, ctx_tiles)
    mod = mod_ref[...]
    g = g_ref[...]
    h = _norm_mod(x_ref[...], g, mod[0:1], mod[1:2])
    hp = _norm_mod(xp_ref[...], g, mod[0:1], mod[1:2])
    hn = _norm_mod(xn_ref[...], g, mod[0:1], mod[1:2])
    h_prev, h_next = _shifted(h, hp, hn, prev_ok, next_ok)
    xx_p = h_prev - h
    xx_n = h_next - h
    mp = mp_ref[...]
    mn = mn_ref[...]

    def mix(m):
        return (h + xx_p * mp[m:m + 1] + xx_n * mn[m:m + 1]).astype(BF16)

    lo = _lane_iota((ROW_TILE, LANES)) < RW_HEAD

    def halves(z):
        return jnp.where(lo, z, 0.0), jnp.where(lo, 0.0, z)

    r = jnp.dot(mix(0), wr_ref[...], preferred_element_type=F32)
    r_ref[...] = r

    lw = halves(jnp.tanh(jnp.dot(mix(1), w1_ref[...], preferred_element_type=F32)))
    la = halves(jnp.dot(mix(4), a1_ref[...], preferred_element_type=F32))
    a_gate = []
    for d in range(2):
        wl = w0_ref[d:d + 1, :] + _dot(lw[d], w2_ref[...])
        lw_ref[d] = -jnp.exp(-_softplus(-wl) - 0.5)
        a_d = _sigmoid(a0_ref[d:d + 1, :] + _dot(la[d], a2_ref[...]))
        a_ref[d] = a_d
        a_gate.append(a_d)

    k = jnp.dot(mix(2), wk_ref[...], preferred_element_type=F32)
    k_ref[...] = k
    kk = k * kk_ref[...]
    kkn_ref[...] = kk / jnp.maximum(jnp.sqrt(_group_sum(kk * kk)), 1e-12)

    xv = mix(3)
    v = jnp.dot(xv, wv_ref[...], preferred_element_type=F32)
    if has_vfirst:
        lv = jnp.dot(xv, v1_ref[...], preferred_element_type=F32)
        v = v + (vf_ref[...] - v) * _sigmoid(v0_ref[...] + _dot(lv, v2_ref[...]))
    v_ref[...] = v

    ka = ka_ref[...]
    k_sum = k * (1.0 + (a_gate[0] - 1.0) * ka) + k * (1.0 + (a_gate[1] - 1.0) * ka)
    bonus_ref[...] = _group_sum(r * k_sum * rk_ref[...]) * v

    gl = _sigmoid(jnp.dot(mix(5), g1_ref[...], preferred_element_type=F32))
    gate_ref[...] = _dot(gl, g2_ref[...])


def _rwkv_proj(x, mod, g, p, v_first, ctx_tiles):
    bsz, n_rows, _ = x.shape
    n_tiles = n_rows // ROW_TILE
    prev, nxt = _halo_specs(n_rows)
    has_vfirst = v_first is not None
    args = [x, x, x, mod, g, p['mix_prev'], p['mix_next'], p['w_r'], p['w_k'], p['w_v'],
            p['w0'], p['w1'], p['w2'], p['a0'], p['a1'], p['a2'], p['k_k'], p['k_a'], p['r_k'],
            p['g1'], p['g2']]
    specs = [_tile_spec(), prev, nxt, _mod_spec(ctx_tiles)] + [
        _const_spec(a.shape) for a in args[4:]]
    if has_vfirst:
        args += [v_first, p['v0'], p['v1'], p['v2']]
        specs += [_tile_spec()] + [_const_spec(a.shape) for a in args[-3:]]
    one = jax.ShapeDtypeStruct((bsz, n_rows, D_MODEL), F32)
    two = jax.ShapeDtypeStruct((2, bsz, n_rows, D_MODEL), F32)
    two_spec = pl.BlockSpec((2, None, ROW_TILE, D_MODEL), lambda b, t: (0, b, t, 0))
    kern = functools.partial(_rwkv_proj_kernel, n_tiles=n_tiles, ctx_tiles=ctx_tiles,
                             has_vfirst=has_vfirst)
    return pl.pallas_call(
        kern,
        grid=(bsz, n_tiles),
        in_specs=specs,
        out_specs=[_tile_spec()] * 6 + [two_spec, two_spec],
        out_shape=[one] * 6 + [two, two],
        compiler_params=_params(),
        name="rwkv_proj",
    )(*args)


def _wkv_prep_kernel(r_ref, k_ref, v_ref, kk_ref, lw_ref, a_ref, ka_ref,
                     rh_ref, yl_ref, mt_ref, gt_ref, *, reverse):
    n = WKV_CHUNK
    ri = lax.broadcasted_iota(jnp.int32, (n, n), 0)
    ci = lax.broadcasted_iota(jnp.int32, (n, n), 1)
    if reverse:
        strict, incl = ci > ri, ci >= ri
    else:
        strict, incl = ci < ri, ci <= ri
    tri = incl.astype(F32)
    eye = (ri == ci).astype(F32)
    lo = _lane_iota((n, LANES)) < RW_HEAD
    lo_st = _lane_iota((RW_HEAD, LANES)) < RW_HEAD
    diag = (_lane_iota((RW_HEAD, LANES)) % RW_HEAD) == _row_iota((RW_HEAD, LANES))
    last = 0 if reverse else n - 1

    def chunk(c, carry):
        rows = pl.ds(pl.multiple_of(c * n, n), n)
        for hp in range(D_MODEL // LANES):
            col = slice(hp * LANES, (hp + 1) * LANES)
            ka = ka_ref[:, col]
            lw = lw_ref[rows, col]
            gate = a_ref[rows, col]
            kk = kk_ref[rows, col]
            v = v_ref[rows, col]
            kd = k_ref[rows, col] * (1.0 + (gate - 1.0) * ka)
            b = kk * gate
            cum = jnp.dot(tri, lw, precision=lax.Precision.HIGHEST, preferred_element_type=F32)
            cum_end = cum[last:last + 1, :]
            e_neg = jnp.exp(-cum)
            e_rem = jnp.exp(cum_end - cum)
            at = -kk * jnp.exp(cum - lw)
            rt = r_ref[rows, col] * jnp.exp(cum)
            kt = kd * e_neg
            bt = b * e_neg
            kh = kd * e_rem
            bh = b * e_rem

            lhs = jnp.concatenate([jnp.where(lo, at, 0.0), jnp.where(lo, 0.0, at),
                                   jnp.where(lo, rt, 0.0), jnp.where(lo, 0.0, rt)], axis=0)
            a_k = _dot_nt(lhs, kt)
            a_b = _dot_nt(lhs, bt)
            rh = rt
            au = jnp.zeros((n, 2 * LANES), F32)
            heads = []
            for hd in range(2):
                own = lo if hd == 0 else jnp.logical_not(lo)
                aak = jnp.where(strict, a_k[hd * n:(hd + 1) * n], 0.0)
                aab = jnp.where(strict, a_b[hd * n:(hd + 1) * n], 0.0)
                ark = jnp.where(incl, a_k[(2 + hd) * n:(3 + hd) * n], 0.0)
                arb = jnp.where(incl, a_b[(2 + hd) * n:(3 + hd) * n], 0.0)
                inv = eye + aab
                pw = aab
                steps = 1
                while 2 * steps < n:
                    pw = _dot(pw, pw)
                    inv = inv + _dot(inv, pw)
                    steps *= 2
                rhs = jnp.concatenate([jnp.where(own, at, 0.0),
                                       jnp.where(own, _dot(aak, v), 0.0)], axis=1)
                au = au + _dot(inv, rhs)
                heads.append((own, ark, arb))
            yl = jnp.zeros((n, LANES), F32)
            for own, ark, arb in heads:
                x = _dot(arb, au)
                rh = rh + jnp.where(own, x[:, :LANES], 0.0)
                yl = yl + jnp.where(own, x[:, LANES:] + _dot(ark, v), 0.0)
            rh_ref[rows, col] = rh
            yl_ref[rows, col] = yl
            pm = _dot_tn(bh, au)
            gfull = _dot_tn(kh, v) + pm[:, LANES:]
            mt = jnp.where(lo_st, pm[:RW_HEAD, :LANES], pm[RW_HEAD:, :LANES])
            mt_ref[c, :, col] = mt + jnp.where(diag, jnp.exp(cum_end), 0.0)
            gt_ref[c, :, col] = jnp.where(lo_st, gfull[:RW_HEAD], gfull[RW_HEAD:])
        return carry

    lax.fori_loop(0, ROW_TILE // n, chunk, 0)


def _wkv_prep(r, k, v, kk, lw, a, k_a, reverse):
    bsz, n_rows, _ = r.shape
    n_tiles = n_rows // ROW_TILE
    per = ROW_TILE // WKV_CHUNK
    dirn = 1 if reverse else 0
    dir_spec = pl.BlockSpec((None, None, ROW_TILE, D_MODEL), lambda b, t: (dirn, b, t, 0))
    st_spec = pl.BlockSpec((None, per, RW_HEAD, D_MODEL), lambda b, t: (b, t, 0, 0))
    one = jax.ShapeDtypeStruct((bsz, n_rows, D_MODEL), F32)
    st = jax.ShapeDtypeStruct((bsz, n_rows // WKV_CHUNK, RW_HEAD, D_MODEL), F32)
    return pl.pallas_call(
        functools.partial(_wkv_prep_kernel, reverse=reverse),
        grid=(bsz, n_tiles),
        in_specs=[_tile_spec()] * 4 + [dir_spec, dir_spec, _const_spec((1, D_MODEL))],
        out_specs=[_tile_spec(), _tile_spec(), st_spec, st_spec],
        out_shape=[one, one, st, st],
        compiler_params=_params(),
        name="wkv_prep_bwd" if reverse else "wkv_prep_fwd",
    )(r, k, v, kk, lw, a, k_a)


def _wkv_scan_kernel(rh_ref, yl_ref, mt_ref, gt_ref, y_ref, state_ref, *, reverse):
    n = WKV_CHUNK
    per = ROW_TILE // n

    @pl.when(pl.program_id(1) == 0)
    def _():
        state_ref[...] = jnp.zeros(state_ref.shape, F32)

    lo = _lane_iota((RW_HEAD, LANES)) < RW_HEAD
    for i in range(per):
        c = per - 1 - i if reverse else i
        for hp in range(D_MODEL // LANES):
            col = slice(hp * LANES, (hp + 1) * LANES)
            lhs = jnp.concatenate([rh_ref[c * n:(c + 1) * n, col], mt_ref[c, :, col]], axis=0)
            out = _dot(lhs, state_ref[hp])
            y_ref[c * n:(c + 1) * n, col] = out[:n] + yl_ref[c * n:(c + 1) * n, col]
            st = out[n:] + gt_ref[c, :, col]
            state_ref[hp] = jnp.concatenate([jnp.where(lo, st, 0.0), jnp.where(lo, 0.0, st)],
                                            axis=0)


def _wkv_scan(rh, yl, mt, gt, ctx_tiles, reverse):
    bsz, n_rows, _ = rh.shape
    n_tiles = n_rows // ROW_TILE
    per = ROW_TILE // WKV_CHUNK

    def tile(t):
        if not reverse:
            return t
        return jnp.where(t < ctx_tiles, ctx_tiles - 1 - t, n_tiles - 1 - (t - ctx_tiles))

    row_spec = pl.BlockSpec((None, ROW_TILE, D_MODEL), lambda b, t: (b, tile(t), 0))
    st_spec = pl.BlockSpec((None, per, RW_HEAD, D_MODEL), lambda b, t: (b, tile(t), 0, 0))
    return pl.pallas_call(
        functools.partial(_wkv_scan_kernel, reverse=reverse),
        grid=(bsz, n_tiles),
        in_specs=[row_spec, row_spec, st_spec, st_spec],
        out_specs=row_spec,
        out_shape=jax.ShapeDtypeStruct((bsz, n_rows, D_MODEL), F32),
        scratch_shapes=[pltpu.VMEM((D_MODEL // LANES, LANES, LANES), F32)],
        compiler_params=pltpu.CompilerParams(dimension_semantics=("arbitrary", "arbitrary"),
                                             vmem_limit_bytes=VMEM_LIMIT),
        name="wkv_scan_bwd" if reverse else "wkv_scan_fwd",
    )(rh, yl, mt, gt)


def _rwkv_out_kernel(x_ref, yf_ref, yb_ref, bonus_ref, gate_ref, mod_ref, lg_ref, lb_ref,
                     w_ref, o_ref):
    y = yf_ref[...] + yb_ref[...]
    mu = _group_sum(y) * (1.0 / RW_HEAD)
    yc = y - mu
    var = _group_sum(yc * yc) * (1.0 / RW_HEAD)
    yn = yc * lax.rsqrt(var + LNX_EPS) * lg_ref[...] + lb_ref[...]
    z = ((yn + bonus_ref[...]) * gate_ref[...]).astype(BF16)
    out = jnp.dot(z, w_ref[...], preferred_element_type=F32)
    o_ref[...] = x_ref[...] + mod_ref[2:3, :] * out


def _rwkv_out(x, yf, yb, bonus, gate, mod, lnx_g, lnx_b, w_o, ctx_tiles, t0):
    bsz, n_rows, _ = x.shape
    n_tiles = n_rows // ROW_TILE - t0
    return pl.pallas_call(
        _rwkv_out_kernel,
        grid=(bsz, n_tiles),
        in_specs=[_tile_spec(t0)] * 5 + [_mod_spec(ctx_tiles, t0), _const_spec((1, D_MODEL)),
                                          _const_spec((1, D_MODEL)),
                                          _const_spec((D_MODEL, D_MODEL))],
        out_specs=_tile_spec(),
        out_shape=jax.ShapeDtypeStruct((bsz, n_tiles * ROW_TILE, D_MODEL), F32),
        compiler_params=_params(),
        name="rwkv_out",
    )(x, yf, yb, bonus, gate, mod, lnx_g, lnx_b, w_o)


def _rope_tables(n_ctx, n_lat):
    t = jnp.arange(n_lat)
    row_pos = (t // GRID_W).astype(F32)
    col_pos = (t % GRID_W).astype(F32)
    n_freq = DA_HEAD_DIM // 4
    inv_freq = ROPE_THETA ** (-jnp.arange(n_freq, dtype=F32) / n_freq)
    ang = jnp.concatenate([row_pos[:, None] * inv_freq, col_pos[:, None] * inv_freq], axis=-1)
    cos, sin = jnp.cos(ang), jnp.sin(ang)
    reps = LANES // DA_HEAD_DIM
    cos = jnp.tile(jnp.concatenate([cos, cos], axis=-1), (1, reps))
    sin = jnp.tile(jnp.concatenate([-sin, sin], axis=-1), (1, reps))
    cos = jnp.concatenate([jnp.ones((n_ctx, LANES), F32), cos], axis=0)
    sin = jnp.concatenate([jnp.zeros((n_ctx, LANES), F32), sin], axis=0)
    return cos, sin


def kernel(x, c, ctx, c_ctx, ada_w, ada_b, norm_mix_g, norm_ffn_g, ffn_w_up, ffn_conv_w, ffn_conv_b, ffn_w_down, da_w_qkv, da_lambda, da_subln_g, da_w_o, rw_mix_prev, rw_mix_next, rw_w_r, rw_w_k, rw_w_v, rw_w0, rw_w1, rw_w2, rw_a0, rw_a1, rw_a2, rw_v0, rw_v1, rw_v2, rw_k_k, rw_k_a, rw_r_k, rw_g1, rw_g2, rw_lnx_g, rw_lnx_b, rw_w_o, final_norm_g):
    bsz, n_lat, d = x.shape
    n_ctx = ctx.shape[1]
    depth = ada_w.shape[0]
    assert d == D_MODEL and n_lat % ROW_TILE == 0 and n_ctx % ROW_TILE == 0
    assert n_lat % GRID_W == 0
    ctx_tiles = n_ctx // ROW_TILE
    row = lambda a: a.reshape(1, -1)
    cat = lambda a: jnp.concatenate([a[0], a[1]], axis=0 if a.shape[1] == RW_HEAD else 1)

    pad = (-(bsz + 1)) % SUBLANES
    cc = jnp.concatenate([c, c_ctx[None, :], jnp.zeros((pad, d), F32)], axis=0)
    table = _ada_table(cc, ada_w, ada_b)
    mod_l = table[:, :bsz].reshape(depth, bsz, 6, d)
    mod_c = jnp.broadcast_to(table[:, bsz].reshape(depth, 1, 6, d), (depth, bsz, 6, d))
    mods = jnp.stack([mod_c, mod_l], axis=2)
    mods = jnp.pad(mods, ((0, 0), (0, 0), (0, 0), (0, SUBLANES - 6), (0, 0)))

    cos, sin = _rope_tables(n_ctx, n_lat)
    stream = jnp.concatenate([ctx, x], axis=1)
    v_first = None
    no_final = jnp.ones((1, d), F32)

    for i in range(depth):
        last = i == depth - 1
        j = i // 2
        t0 = ctx_tiles if last else 0
        mod = mods[i]
        g_mix = row(norm_mix_g[i])
        if i % 2 == 0:
            lambda_init = 0.8 - 0.6 * math.exp(-0.3 * i)
            q, k, v = _qkv(stream, mod, g_mix, da_w_qkv[j].astype(BF16), cos, sin, ctx_tiles)
            o = _flash(q, k, v, da_lambda[j], da_subln_g[j], ctx_tiles, t0, lambda_init)
            stream = _proj_residual(stream, o, mod, da_w_o[j].astype(BF16), ctx_tiles, t0)
        else:
            p = dict(mix_prev=rw_mix_prev[j], mix_next=rw_mix_next[j],
                     w_r=rw_w_r[j].astype(BF16), w_k=rw_w_k[j].astype(BF16),
                     w_v=rw_w_v[j].astype(BF16), w0=rw_w0[j], w1=cat(rw_w1[j]).astype(BF16),
                     w2=cat(rw_w2[j]).astype(BF16), a0=rw_a0[j], a1=cat(rw_a1[j]).astype(BF16),
                     a2=cat(rw_a2[j]).astype(BF16), k_k=row(rw_k_k[j]), k_a=row(rw_k_a[j]),
                     r_k=row(rw_r_k[j]), g1=rw_g1[j].astype(BF16), g2=rw_g2[j].astype(BF16))
            if j > 0:
                n_v = rw_v1.shape[-1]
                p.update(v0=row(rw_v0[j - 1]),
                         v1=jnp.pad(rw_v1[j - 1], ((0, 0), (0, LANES - n_v))).astype(BF16),
                         v2=jnp.pad(rw_v2[j - 1], ((0, LANES - n_v), (0, 0))).astype(BF16))
            r, k, v, kk, bonus, gate, lw, a = _rwkv_proj(stream, mod, g_mix, p, v_first,
                                                         ctx_tiles)
            if v_first is None:
                v_first = v
            ys = []
            for reverse in (False, True):
                rh, yl, mt, gt = _wkv_prep(r, k, v, kk, lw, a, p['k_a'], reverse)
                ys.append(_wkv_scan(rh, yl, mt, gt, ctx_tiles, reverse))
            stream = _rwkv_out(stream, ys[0], ys[1], bonus, gate, mod, row(rw_lnx_g[j]),
                               row(rw_lnx_b[j]), rw_w_o[j].astype(BF16), ctx_tiles, t0)
        stream = _ffn(stream, mod, row(norm_ffn_g[i]), ffn_w_up[i].astype(BF16),
                      ffn_conv_w[i], row(ffn_conv_b[i]), ffn_w_down[i].astype(BF16),
                      row(final_norm_g) if last else no_final, 0 if last else ctx_tiles, last)
    return stream
```

```python
import functools
import math

import jax
import jax.numpy as jnp
from jax import lax
from jax.experimental import pallas as pl
from jax.experimental.pallas import tpu as pltpu

D_MODEL = 1024
GRID_W = 64
DA_HEAD_DIM = 64
DA_HEADS = D_MODEL // (2 * DA_HEAD_DIM)
ROPE_THETA = 10000.0
RW_HEAD = 64
N_SHIFT_MIX = 6
LNX_EPS = 64e-5
EPS = 1e-6

LANES = 128
SUBLANES = 8
ROW_TILE = 256
WKV_CHUNK = 32
FF_TILE = 256
PREP_CHUNKS = 4
VMEM_LIMIT = 56 * 1024 * 1024

F32 = jnp.float32
BF16 = jnp.bfloat16

Q_SCALE = DA_HEAD_DIM ** -0.5 * math.log2(math.e)


def _dot(a, b):
    return jnp.dot(a.astype(BF16), b.astype(BF16), preferred_element_type=F32)


def _dot_nt(a, b):
    return lax.dot_general(a.astype(BF16), b.astype(BF16), (((1,), (1,)), ((), ())),
                           preferred_element_type=F32)


def _dot_tn(a, b):
    return lax.dot_general(a.astype(BF16), b.astype(BF16), (((0,), (0,)), ((), ())),
                           preferred_element_type=F32)


def _rms(x):
    return x * lax.rsqrt(jnp.mean(x * x, axis=-1, keepdims=True) + EPS)


def _norm_mod(x, g, shift, scale):
    return _rms(x) * g * (1.0 + scale) + shift


def _sigmoid(x):
    return 1.0 / (1.0 + jnp.exp(-x))


def _softplus(x):
    return jnp.maximum(x, 0.0) + jnp.log1p(jnp.exp(-jnp.abs(x)))


def _lane_iota(shape):
    return lax.broadcasted_iota(jnp.int32, shape, len(shape) - 1)


def _row_iota(shape):
    return lax.broadcasted_iota(jnp.int32, shape, len(shape) - 2)


def _group_ones():
    r = lax.broadcasted_iota(jnp.int32, (LANES, LANES), 0) // RW_HEAD
    c = lax.broadcasted_iota(jnp.int32, (LANES, LANES), 1) // RW_HEAD
    return (r == c).astype(BF16)


def _group_sum(z):
    ones = _group_ones()
    parts = [jnp.dot(z[:, j:j + LANES].astype(BF16), ones, preferred_element_type=F32)
             for j in range(0, z.shape[1], LANES)]
    return jnp.concatenate(parts, axis=1)


def _shifted(h, halo_prev, halo_next, prev_ok, next_ok):
    rows = h.shape[0]
    ri = _row_iota(h.shape)
    first = jnp.where(prev_ok, halo_prev[SUBLANES - 1:SUBLANES, :], 0.0)
    last = jnp.where(next_ok, halo_next[0:1, :], 0.0)
    h_prev = jnp.where(ri == 0, first, pltpu.roll(h, 1, 0))
    h_next = jnp.where(ri == rows - 1, last, pltpu.roll(h, rows - 1, 0))
    return h_prev, h_next


def _edge_flags(t, n_tiles, ctx_tiles):
    prev_ok = jnp.logical_and(t != 0, t != ctx_tiles)
    next_ok = jnp.logical_and(t != n_tiles - 1, t != ctx_tiles - 1)
    return prev_ok, next_ok


def _const_spec(shape):
    zeros = (0,) * len(shape)
    return pl.BlockSpec(shape, lambda *_: zeros, pipeline_mode=pl.Buffered(1))


def _tile_spec(t0=0):
    return pl.BlockSpec((None, ROW_TILE, D_MODEL), lambda b, t: (b, t + t0, 0))


def _halo_specs(n_rows, t0=0):
    per = ROW_TILE // SUBLANES
    last = n_rows // SUBLANES - 1
    prev = pl.BlockSpec((None, SUBLANES, D_MODEL),
                        lambda b, t: (b, jnp.maximum((t + t0) * per - 1, 0), 0))
    nxt = pl.BlockSpec((None, SUBLANES, D_MODEL),
                       lambda b, t: (b, jnp.minimum((t + t0 + 1) * per, last), 0))
    return prev, nxt


def _mod_spec(ctx_tiles, t0=0):
    return pl.BlockSpec((None, None, SUBLANES, D_MODEL),
                        lambda b, t: (b, ((t + t0) >= ctx_tiles).astype(jnp.int32), 0, 0))


def _params():
    return pltpu.CompilerParams(vmem_limit_bytes=VMEM_LIMIT)


def _ada_kernel(c_ref, w_ref, b_ref, o_ref):
    c = c_ref[...]
    s = c * _sigmoid(c)
    o_ref[...] = jnp.dot(s, w_ref[...], precision=lax.Precision.HIGHEST,
                         preferred_element_type=F32) + b_ref[...]


def _ada_table(cc, ada_w, ada_b):
    n_layers, _, six_d = ada_w.shape
    rows = cc.shape[0]
    nb = 1536
    return pl.pallas_call(
        _ada_kernel,
        grid=(n_layers, six_d // nb),
        in_specs=[pl.BlockSpec((rows, D_MODEL), lambda l, n: (0, 0)),
                  pl.BlockSpec((None, D_MODEL, nb), lambda l, n: (l, 0, n)),
                  pl.BlockSpec((None, 1, nb), lambda l, n: (l, 0, n))],
        out_specs=pl.BlockSpec((None, rows, nb), lambda l, n: (l, 0, n)),
        out_shape=jax.ShapeDtypeStruct((n_layers, rows, six_d), F32),
        compiler_params=_params(),
        name="ada_table",
    )(cc, ada_w, ada_b.reshape(n_layers, 1, six_d))


def _rope(slab, cos, sin_signed):
    lane = _lane_iota(slab.shape)
    first = (lane % DA_HEAD_DIM) < (DA_HEAD_DIM // 2)
    partner = jnp.where(first, pltpu.roll(slab, LANES - DA_HEAD_DIM // 2, 1),
                        pltpu.roll(slab, DA_HEAD_DIM // 2, 1))
    return slab * cos + partner * sin_signed


def _qkv_kernel(x_ref, mod_ref, g_ref, w_ref, cos_ref, sin_ref, q_ref, k_ref, v_ref):
    mod = mod_ref[...]
    h = _norm_mod(x_ref[...], g_ref[...], mod[0:1], mod[1:2]).astype(BF16)
    cos = cos_ref[...]
    sin = sin_ref[...]
    wide = 2 * LANES
    for j in range(0, D_MODEL, wide):
        qa = jnp.dot(h, w_ref[:, j:j + wide], preferred_element_type=F32)
        ka = jnp.dot(h, w_ref[:, D_MODEL + j:D_MODEL + j + wide], preferred_element_type=F32)
        for i in range(0, wide, LANES):
            q_ref[:, j + i:j + i + LANES] = (
                _rope(qa[:, i:i + LANES], cos, sin) * Q_SCALE).astype(BF16)
            k_ref[:, j + i:j + i + LANES] = _rope(ka[:, i:i + LANES], cos, sin).astype(BF16)
        v_ref[:, j:j + wide] = jnp.dot(
            h, w_ref[:, 2 * D_MODEL + j:2 * D_MODEL + j + wide],
            preferred_element_type=F32).astype(BF16)


def _qkv(x, mod, g, w_qkv, cos, sin, ctx_tiles):
    bsz, n_rows, _ = x.shape
    out = jax.ShapeDtypeStruct((bsz, n_rows, D_MODEL), BF16)
    tab = pl.BlockSpec((ROW_TILE, LANES), lambda b, t: (t, 0))
    return pl.pallas_call(
        _qkv_kernel,
        grid=(bsz, n_rows // ROW_TILE),
        in_specs=[_tile_spec(), _mod_spec(ctx_tiles), _const_spec((1, D_MODEL)),
                  _const_spec((D_MODEL, 3 * D_MODEL)), tab, tab],
        out_specs=[_tile_spec(), _tile_spec(), _tile_spec()],
        out_shape=[out, out, out],
        compiler_params=_params(),
        name="attn_qkv",
    )(x, mod, g, w_qkv, cos, sin)


def _flash_kernel(lam_ref, g_ref, q_ref, k_ref, v_ref, o_ref,
                  *, ctx_tiles, t0, ctx_rows, lambda_init):
    t = pl.program_id(2) + t0
    q = q_ref[...]
    lo = _lane_iota(q.shape) < DA_HEAD_DIM
    zero = jnp.zeros_like(q)
    qs = (jnp.where(lo, q, zero), jnp.where(lo, zero, q))
    lv = lam_ref[...]
    lam = (jnp.exp(jnp.sum(lv[0:1] * lv[1:2], axis=-1, keepdims=True))
           - jnp.exp(jnp.sum(lv[2:3] * lv[3:4], axis=-1, keepdims=True)) + lambda_init)

    def attend(n_keys):
        k = k_ref[0:n_keys, :]
        v = v_ref[0:n_keys, :]
        outs = []
        for c in range(2):
            s = lax.dot_general(qs[c], k, (((1,), (1,)), ((), ())), preferred_element_type=F32)
            p = jnp.exp2(s - jnp.max(s, axis=-1, keepdims=True))
            norm = jnp.sum(p, axis=-1, keepdims=True)
            outs.append(jnp.dot(p.astype(BF16), v, preferred_element_type=F32) / norm)
        o = outs[0] - lam * outs[1]
        o_ref[...] = (_rms(o) * g_ref[...] * (1.0 - lambda_init)).astype(BF16)

    @pl.when(t < ctx_tiles)
    def _():
        attend(ctx_rows)

    @pl.when(t >= ctx_tiles)
    def _():
        attend(k_ref.shape[0])


def _flash(q, k, v, lam_vecs, subln_g, ctx_tiles, t0, lambda_init):
    bsz, n_rows, _ = q.shape
    n_tiles = n_rows // ROW_TILE
    kv_spec = pl.BlockSpec((None, n_rows, LANES), lambda b, h, t: (b, 0, h))
    q_spec = pl.BlockSpec((None, ROW_TILE, LANES), lambda b, h, t: (b, t + t0, h))
    kern = functools.partial(_flash_kernel, ctx_tiles=ctx_tiles, t0=t0,
                             ctx_rows=ctx_tiles * ROW_TILE, lambda_init=lambda_init)
    return pl.pallas_call(
        kern,
        grid=(bsz, DA_HEADS, n_tiles - t0),
        in_specs=[pl.BlockSpec(lam_vecs.shape, lambda b, h, t: (0, 0)),
                  pl.BlockSpec((1, LANES), lambda b, h, t: (0, 0)),
                  q_spec, kv_spec, kv_spec],
        out_specs=q_spec,
        out_shape=jax.ShapeDtypeStruct((bsz, n_rows, D_MODEL), BF16),
        compiler_params=_params(),
        name="diff_flash",
    )(lam_vecs, subln_g.reshape(1, LANES), q, k, v)


def _proj_residual_kernel(x_ref, z_ref, mod_ref, w_ref, o_ref):
    out = jnp.dot(z_ref[...], w_ref[...], preferred_element_type=F32)
    o_ref[...] = x_ref[...] + mod_ref[2:3, :] * out


def _proj_residual(x, z, mod, w, ctx_tiles, t0):
    bsz, n_rows, _ = x.shape
    n_tiles = n_rows // ROW_TILE - t0
    return pl.pallas_call(
        _proj_residual_kernel,
        grid=(bsz, n_tiles),
        in_specs=[_tile_spec(t0), _tile_spec(t0), _mod_spec(ctx_tiles, t0),
                  _const_spec((D_MODEL, D_MODEL))],
        out_specs=_tile_spec(),
        out_shape=jax.ShapeDtypeStruct((bsz, n_tiles * ROW_TILE, D_MODEL), F32),
        compiler_params=_params(),
        name="proj_residual",
    )(x, z, mod, w)


def _ffn_kernel(x_ref, xp_ref, xn_ref, mod_ref, g_ref, wup_ref, cw_ref, cb_ref, wdn_ref,
                fin_ref, o_ref, *, n_tiles, ctx_tiles, d_ff, final):
    t = pl.program_id(1)
    prev_ok, next_ok = _edge_flags(t, n_tiles, ctx_tiles)
    mod = mod_ref[...]
    g = g_ref[...]
    x = x_ref[...]
    h = _norm_mod(x, g, mod[3:4], mod[4:5]).astype(BF16)
    hp = _norm_mod(xp_ref[...], g, mod[3:4], mod[4:5]).astype(BF16)
    hn = _norm_mod(xn_ref[...], g, mod[3:4], mod[4:5]).astype(BF16)
    acc = jnp.zeros((ROW_TILE, D_MODEL), F32)
    for f in range(0, d_ff, FF_TILE):
        wg = wup_ref[:, f:f + FF_TILE]
        gate = jnp.dot(h, wg, preferred_element_type=F32)
        gate_p = jnp.dot(hp, wg, preferred_element_type=F32)
        gate_n = jnp.dot(hn, wg, preferred_element_type=F32)
        val = jnp.dot(h, wup_ref[:, d_ff + f:d_ff + f + FF_TILE], preferred_element_type=F32)
        g_prev, g_next = _shifted(gate, gate_p, gate_n, prev_ok, next_ok)
        cw = cw_ref[:, f:f + FF_TILE]
        conv = g_prev * cw[0:1] + gate * cw[1:2] + g_next * cw[2:3] + cb_ref[:, f:f + FF_TILE]
        act = conv * _sigmoid(conv) * val
        acc = acc + jnp.dot(act.astype(BF16), wdn_ref[f:f + FF_TILE, :],
                            preferred_element_type=F32)
    y = x + mod[5:6] * acc
    if final:
        y = _rms(y) * fin_ref[...]
    o_ref[...] = y


def _ffn(x, mod, g, w_up, conv_w, conv_b, w_down, final_g, ctx_tiles, final):
    bsz, n_rows, _ = x.shape
    n_tiles = n_rows // ROW_TILE
    d_ff = w_down.shape[0]
    prev, nxt = _halo_specs(n_rows)
    kern = functools.partial(_ffn_kernel, n_tiles=n_tiles, ctx_tiles=ctx_tiles, d_ff=d_ff,
                             final=final)
    return pl.pallas_call(
        kern,
        grid=(bsz, n_tiles),
        in_specs=[_tile_spec(), prev, nxt, _mod_spec(ctx_tiles), _const_spec((1, D_MODEL)),
                  _const_spec((D_MODEL, 2 * d_ff)), _const_spec((3, d_ff)),
                  _const_spec((1, d_ff)), _const_spec((d_ff, D_MODEL)),
                  _const_spec((1, D_MODEL))],
        out_specs=_tile_spec(),
        out_shape=jax.ShapeDtypeStruct((bsz, n_rows, D_MODEL), F32),
        compiler_params=_params(),
        name="conv_glu",
    )(x, x, x, mod, g, w_up, conv_w, conv_b, w_down, final_g)


def _rwkv_proj_kernel(*refs, n_tiles, ctx_tiles, has_vfirst):
    (x_ref, xp_ref, xn_ref, mod_ref, g_ref, mp_ref, mn_ref, wr_ref, wk_ref, wv_ref,
     w0_ref, w1_ref, w2_ref, a0_ref, a1_ref, a2_ref, kk_ref, ka_ref, rk_ref,
     g1_ref, g2_ref) = refs[:21]
    refs = refs[21:]
    if has_vfirst:
        vf_ref, v0_ref, v1_ref, v2_ref = refs[:4]
        refs = refs[4:]
    r_ref, k_ref, v_ref, kkn_ref, bonus_ref, gate_ref, lw_ref, a_ref = refs

    t = pl.program_id(1)
    prev_ok, next_ok = _edge_flags(t, n_tiles, ctx_tiles)
    mod = mod_ref[...]
    g = g_ref[...]
    h = _norm_mod(x_ref[...], g, mod[0:1], mod[1:2])
    hp = _norm_mod(xp_ref[...], g, mod[0:1], mod[1:2])
    hn = _norm_mod(xn_ref[...], g, mod[0:1], mod[1:2])
    h_prev, h_next = _shifted(h, hp, hn, prev_ok, next_ok)
    xx_p = h_prev - h
    xx_n = h_next - h
    mp = mp_ref[...]
    mn = mn_ref[...]

    def mix(m):
        return (h + xx_p * mp[m:m + 1] + xx_n * mn[m:m + 1]).astype(BF16)

    lo = _lane_iota((ROW_TILE, LANES)) < RW_HEAD

    def halves(z):
        return jnp.where(lo, z, 0.0), jnp.where(lo, 0.0, z)

    r = jnp.dot(mix(0), wr_ref[...], preferred_element_type=F32)
    r_ref[...] = r

    lw = halves(jnp.tanh(jnp.dot(mix(1), w1_ref[...], preferred_element_type=F32)))
    la = halves(jnp.dot(mix(4), a1_ref[...], preferred_element_type=F32))
    a_gate = []
    for d in range(2):
        wl = w0_ref[d:d + 1, :] + _dot(lw[d], w2_ref[...])
        lw_ref[d] = -jnp.exp(-_softplus(-wl) - 0.5)
        a_d = _sigmoid(a0_ref[d:d + 1, :] + _dot(la[d], a2_ref[...]))
        a_ref[d] = a_d
        a_gate.append(a_d)

    k = jnp.dot(mix(2), wk_ref[...], preferred_element_type=F32)
    k_ref[...] = k
    kk = k * kk_ref[...]
    kkn_ref[...] = kk / jnp.maximum(jnp.sqrt(_group_sum(kk * kk)), 1e-12)

    xv = mix(3)
    v = jnp.dot(xv, wv_ref[...], preferred_element_type=F32)
    if has_vfirst:
        lv = jnp.dot(xv, v1_ref[...], preferred_element_type=F32)
        v = v + (vf_ref[...] - v) * _sigmoid(v0_ref[...] + _dot(lv, v2_ref[...]))
    v_ref[...] = v

    ka = ka_ref[...]
    k_sum = k * (1.0 + (a_gate[0] - 1.0) * ka) + k * (1.0 + (a_gate[1] - 1.0) * ka)
    bonus_ref[...] = _group_sum(r * k_sum * rk_ref[...]) * v

    gl = _sigmoid(jnp.dot(mix(5), g1_ref[...], preferred_element_type=F32))
    gate_ref[...] = _dot(gl, g2_ref[...])


def _rwkv_proj(x, mod, g, p, v_first, ctx_tiles):
    bsz, n_rows, _ = x.shape
    n_tiles = n_rows // ROW_TILE
    prev, nxt = _halo_specs(n_rows)
    has_vfirst = v_first is not None
    args = [x, x, x, mod, g, p['mix_prev'], p['mix_next'], p['w_r'], p['w_k'], p['w_v'],
            p['w0'], p['w1'], p['w2'], p['a0'], p['a1'], p['a2'], p['k_k'], p['k_a'], p['r_k'],
            p['g1'], p['g2']]
    specs = [_tile_spec(), prev, nxt, _mod_spec(ctx_tiles)] + [
        _const_spec(a.shape) for a in args[4:]]
    if has_vfirst:
        args += [v_first, p['v0'], p['v1'], p['v2']]
        specs += [_tile_spec()] + [_const_spec(a.shape) for a in args[-3:]]
    one = jax.ShapeDtypeStruct((bsz, n_rows, D_MODEL), F32)
    two = jax.ShapeDtypeStruct((2, bsz, n_rows, D_MODEL), F32)
    two_spec = pl.BlockSpec((2, None, ROW_TILE, D_MODEL), lambda b, t: (0, b, t, 0))
    kern = functools.partial(_rwkv_proj_kernel, n_tiles=n_tiles, ctx_tiles=ctx_tiles,
                             has_vfirst=has_vfirst)
    return pl.pallas_call(
        kern,
        grid=(bsz, n_tiles),
        in_specs=specs,
        out_specs=[_tile_spec()] * 6 + [two_spec, two_spec],
        out_shape=[one] * 6 + [two, two],
        compiler_params=_params(),
        name="rwkv_proj",
    )(*args)


def _mask_stack(x, n_blocks, width):
    blk = _lane_iota(x.shape) // width
    zero = jnp.zeros_like(x)
    return jnp.concatenate([jnp.where(blk == i, x, zero) for i in range(n_blocks)], axis=0)


def _wkv_prep_kernel(r_ref, k_ref, v_ref, kk_ref, lw_ref, a_ref, ka_ref,
                     rh_ref, yl_ref, mt_ref, gt_ref, *, reverse):
    n = WKV_CHUNK
    hpg = LANES // n
    gw = hpg * RW_HEAD
    groups = D_MODEL // gw
    ri = lax.broadcasted_iota(jnp.int32, (n, n), 0)
    ci = lax.broadcasted_iota(jnp.int32, (n, n), 1)
    tri = (ci >= ri if reverse else ci <= ri).astype(F32)
    row = _row_iota((n, LANES))
    src = _lane_iota((n, LANES)) % n
    if reverse:
        strict, incl = src > row, src >= row
    else:
        strict, incl = src < row, src <= row
    eye = (src == row).astype(F32)
    lo_st = _lane_iota((RW_HEAD, LANES)) < RW_HEAD
    diag = (_lane_iota((RW_HEAD, LANES)) % RW_HEAD) == _row_iota((RW_HEAD, LANES))
    last = 0 if reverse else n - 1
    ka = ka_ref[...]
    bf = lambda z: z.astype(BF16)

    def body(it, carry):
        items = []
        for cc in range(PREP_CHUNKS):
            c = it * PREP_CHUNKS + cc
            rows = pl.ds(pl.multiple_of(c * n, n), n)
            lw = lw_ref[rows, :]
            gate = a_ref[rows, :]
            kk = kk_ref[rows, :]
            kd = k_ref[rows, :] * (1.0 + (gate - 1.0) * ka)
            b = kk * gate
            cum = jnp.dot(tri, lw, precision=lax.Precision.HIGHEST, preferred_element_type=F32)
            cum_end = cum[last:last + 1, :]
            e_neg = jnp.exp(-cum)
            e_rem = jnp.exp(cum_end - cum)
            rt = r_ref[rows, :] * jnp.exp(cum)
            full = dict(c=c, rows=rows, rt=rt, w_end=jnp.exp(cum_end), v=bf(v_ref[rows, :]),
                        at=bf(-kk * jnp.exp(cum - lw)), rtb=bf(rt), kt=bf(kd * e_neg),
                        bt=bf(b * e_neg), kh=bf(kd * e_rem), bh=bf(b * e_rem))
            for g in range(groups):
                sl = slice(g * gw, (g + 1) * gw)
                items.append(dict(full=full, sl=sl))

        for it_ in items:
            f, sl = it_['full'], it_['sl']
            lhs = jnp.concatenate([f['at'][:, sl], f['rtb'][:, sl]], axis=0)
            rhs = jnp.concatenate([_mask_stack(f['kt'][:, sl], hpg, RW_HEAD),
                                   _mask_stack(f['bt'][:, sl], hpg, RW_HEAD)], axis=0)
            a = lax.dot_general(lhs, rhs, (((1,), (1,)), ((), ())), preferred_element_type=F32)
            it_['aak'] = jnp.where(strict, a[:n, :LANES], 0.0)
            it_['ark'] = jnp.where(incl, a[n:, :LANES], 0.0)
            it_['arb'] = bf(jnp.where(incl, a[n:, LANES:], 0.0))
            pw = jnp.where(strict, a[:n, LANES:], 0.0)
            it_['inv'] = eye + pw
            it_['pw'] = bf(pw)
            it_['bd'] = _mask_stack(it_['pw'], hpg, n)

        steps = 1
        while 2 * steps < n:
            for it_ in items:
                it_['pw'] = bf(jnp.dot(it_['pw'], it_['bd'], preferred_element_type=F32))
                it_['bd'] = _mask_stack(it_['pw'], hpg, n)
            for it_ in items:
                it_['inv'] = it_['inv'] + jnp.dot(bf(it_['inv']), it_['bd'],
                                                  preferred_element_type=F32)
            steps *= 2

        for it_ in items:
            f, sl = it_['full'], it_['sl']
            vms = _mask_stack(f['v'][:, sl], hpg, RW_HEAD)
            x = jnp.dot(bf(jnp.concatenate([it_['aak'], it_['ark']], axis=0)), vms,
                        preferred_element_type=F32)
            it_['arkv'] = x[n:]
            it_['rhs'] = jnp.concatenate([_mask_stack(f['at'][:, sl], hpg, RW_HEAD),
                                          _mask_stack(bf(x[:n]), hpg, RW_HEAD)], axis=1)
        for it_ in items:
            au = bf(jnp.dot(bf(it_['inv']), it_['rhs'], preferred_element_type=F32))
            it_['ah'] = au[:, :gw]
            it_['ul'] = au[:, gw:]
            it_['rhs'] = jnp.concatenate([_mask_stack(it_['ah'], hpg, RW_HEAD),
                                          _mask_stack(it_['ul'], hpg, RW_HEAD)], axis=1)
        for it_ in items:
            f, sl = it_['full'], it_['sl']
            x = jnp.dot(it_['arb'], it_['rhs'], preferred_element_type=F32)
            rh_ref[f['rows'], sl] = f['rt'][:, sl] + x[:, :gw]
            yl_ref[f['rows'], sl] = it_['arkv'] + x[:, gw:]
        for it_ in items:
            f, sl = it_['full'], it_['sl']
            for j in range(0, gw, LANES):
                loc = slice(j, j + LANES)
                hs = slice(sl.start + j, sl.start + j + LANES)
                bh = f['bh'][:, hs]
                pm = lax.dot_general(bh, it_['ah'][:, loc], (((0,), (0,)), ((), ())),
                                     preferred_element_type=F32)
                gf = lax.dot_general(
                    jnp.concatenate([f['kh'][:, hs], bh], axis=0),
                    jnp.concatenate([f['v'][:, hs], it_['ul'][:, loc]], axis=0),
                    (((0,), (0,)), ((), ())), preferred_element_type=F32)
                mt = jnp.where(lo_st, pm[:RW_HEAD], pm[RW_HEAD:])
                mt_ref[f['c'], :, hs] = mt + jnp.where(diag, f['w_end'][:, hs], 0.0)
                gt_ref[f['c'], :, hs] = jnp.where(lo_st, gf[:RW_HEAD], gf[RW_HEAD:])
        return carry

    lax.fori_loop(0, ROW_TILE // (n * PREP_CHUNKS), body, 0)


def _wkv_prep(r, k, v, kk, lw, a, k_a, reverse):
    bsz, n_rows, _ = r.shape
    n_tiles = n_rows // ROW_TILE
    per = ROW_TILE // WKV_CHUNK
    dirn = 1 if reverse else 0
    dir_spec = pl.BlockSpec((None, None, ROW_TILE, D_MODEL), lambda b, t: (dirn, b, t, 0))
    st_spec = pl.BlockSpec((None, per, RW_HEAD, D_MODEL), lambda b, t: (b, t, 0, 0))
    one = jax.ShapeDtypeStruct((bsz, n_rows, D_MODEL), F32)
    st = jax.ShapeDtypeStruct((bsz, n_rows // WKV_CHUNK, RW_HEAD, D_MODEL), F32)
    return pl.pallas_call(
        functools.partial(_wkv_prep_kernel, reverse=reverse),
        grid=(bsz, n_tiles),
        in_specs=[_tile_spec()] * 4 + [dir_spec, dir_spec, _const_spec((1, D_MODEL))],
        out_specs=[_tile_spec(), _tile_spec(), st_spec, st_spec],
        out_shape=[one, one, st, st],
        compiler_params=_params(),
        name="wkv_prep_bwd" if reverse else "wkv_prep_fwd",
    )(r, k, v, kk, lw, a, k_a)


def _wkv_scan_kernel(rh_ref, yl_ref, mt_ref, gt_ref, y_ref, state_ref, *, reverse):
    n = WKV_CHUNK
    per = ROW_TILE // n

    @pl.when(pl.program_id(1) == 0)
    def _():
        state_ref[...] = jnp.zeros(state_ref.shape, F32)

    lo = _lane_iota((RW_HEAD, LANES)) < RW_HEAD
    for i in range(per):
        c = per - 1 - i if reverse else i
        for hp in range(D_MODEL // LANES):
            col = slice(hp * LANES, (hp + 1) * LANES)
            lhs = jnp.concatenate([rh_ref[c * n:(c + 1) * n, col], mt_ref[c, :, col]], axis=0)
            out = _dot(lhs, state_ref[hp])
            y_ref[c * n:(c + 1) * n, col] = out[:n] + yl_ref[c * n:(c + 1) * n, col]
            st = out[n:] + gt_ref[c, :, col]
            state_ref[hp] = jnp.concatenate([jnp.where(lo, st, 0.0), jnp.where(lo, 0.0, st)],
                                            axis=0)


def _wkv_scan(rh, yl, mt, gt, ctx_tiles, reverse):
    bsz, n_rows, _ = rh.shape
    n_tiles = n_rows // ROW_TILE
    per = ROW_TILE // WKV_CHUNK

    def tile(t):
        if not reverse:
            return t
        return jnp.where(t < ctx_tiles, ctx_tiles - 1 - t, n_tiles - 1 - (t - ctx_tiles))

    row_spec = pl.BlockSpec((None, ROW_TILE, D_MODEL), lambda b, t: (b, tile(t), 0))
    st_spec = pl.BlockSpec((None, per, RW_HEAD, D_MODEL), lambda b, t: (b, tile(t), 0, 0))
    return pl.pallas_call(
        functools.partial(_wkv_scan_kernel, reverse=reverse),
        grid=(bsz, n_tiles),
        in_specs=[row_spec, row_spec, st_spec, st_spec],
        out_specs=row_spec,
        out_shape=jax.ShapeDtypeStruct((bsz, n_rows, D_MODEL), F32),
        scratch_shapes=[pltpu.VMEM((D_MODEL // LANES, LANES, LANES), F32)],
        compiler_params=pltpu.CompilerParams(dimension_semantics=("arbitrary", "arbitrary"),
                                             vmem_limit_bytes=VMEM_LIMIT),
        name="wkv_scan_bwd" if reverse else "wkv_scan_fwd",
    )(rh, yl, mt, gt)


def _rwkv_out_kernel(x_ref, yf_ref, yb_ref, bonus_ref, gate_ref, mod_ref, lg_ref, lb_ref,
                     w_ref, o_ref):
    y = yf_ref[...] + yb_ref[...]
    mu = _group_sum(y) * (1.0 / RW_HEAD)
    yc = y - mu
    var = _group_sum(yc * yc) * (1.0 / RW_HEAD)
    yn = yc * lax.rsqrt(var + LNX_EPS) * lg_ref[...] + lb_ref[...]
    z = ((yn + bonus_ref[...]) * gate_ref[...]).astype(BF16)
    out = jnp.dot(z, w_ref[...], preferred_element_type=F32)
    o_ref[...] = x_ref[...] + mod_ref[2:3, :] * out


def _rwkv_out(x, yf, yb, bonus, gate, mod, lnx_g, lnx_b, w_o, ctx_tiles, t0):
    bsz, n_rows, _ = x.shape
    n_tiles = n_rows // ROW_TILE - t0
    return pl.pallas_call(
        _rwkv_out_kernel,
        grid=(bsz, n_tiles),
        in_specs=[_tile_spec(t0)] * 5 + [_mod_spec(ctx_tiles, t0), _const_spec((1, D_MODEL)),
                                          _const_spec((1, D_MODEL)),
                                          _const_spec((D_MODEL, D_MODEL))],
        out_specs=_tile_spec(),
        out_shape=jax.ShapeDtypeStruct((bsz, n_tiles * ROW_TILE, D_MODEL), F32),
        compiler_params=_params(),
        name="rwkv_out",
    )(x, yf, yb, bonus, gate, mod, lnx_g, lnx_b, w_o)


def _rope_tables(n_ctx, n_lat):
    t = jnp.arange(n_lat)
    row_pos = (t // GRID_W).astype(F32)
    col_pos = (t % GRID_W).astype(F32)
    n_freq = DA_HEAD_DIM // 4
    inv_freq = ROPE_THETA ** (-jnp.arange(n_freq, dtype=F32) / n_freq)
    ang = jnp.concatenate([row_pos[:, None] * inv_freq, col_pos[:, None] * inv_freq], axis=-1)
    cos, sin = jnp.cos(ang), jnp.sin(ang)
    reps = LANES // DA_HEAD_DIM
    cos = jnp.tile(jnp.concatenate([cos, cos], axis=-1), (1, reps))
    sin = jnp.tile(jnp.concatenate([-sin, sin], axis=-1), (1, reps))
    cos = jnp.concatenate([jnp.ones((n_ctx, LANES), F32), cos], axis=0)
    sin = jnp.concatenate([jnp.zeros((n_ctx, LANES), F32), sin], axis=0)
    return cos, sin


def kernel(x, c, ctx, c_ctx, ada_w, ada_b, norm_mix_g, norm_ffn_g, ffn_w_up, ffn_conv_w, ffn_conv_b, ffn_w_down, da_w_qkv, da_lambda, da_subln_g, da_w_o, rw_mix_prev, rw_mix_next, rw_w_r, rw_w_k, rw_w_v, rw_w0, rw_w1, rw_w2, rw_a0, rw_a1, rw_a2, rw_v0, rw_v1, rw_v2, rw_k_k, rw_k_a, rw_r_k, rw_g1, rw_g2, rw_lnx_g, rw_lnx_b, rw_w_o, final_norm_g):
    bsz, n_lat, d = x.shape
    n_ctx = ctx.shape[1]
    depth = ada_w.shape[0]
    assert d == D_MODEL and n_lat % ROW_TILE == 0 and n_ctx % ROW_TILE == 0
    assert n_lat % GRID_W == 0
    ctx_tiles = n_ctx // ROW_TILE
    row = lambda a: a.reshape(1, -1)
    cat = lambda a: jnp.concatenate([a[0], a[1]], axis=0 if a.shape[1] == RW_HEAD else 1)

    pad = (-(bsz + 1)) % SUBLANES
    cc = jnp.concatenate([c, c_ctx[None, :], jnp.zeros((pad, d), F32)], axis=0)
    table = _ada_table(cc, ada_w, ada_b)
    mod_l = table[:, :bsz].reshape(depth, bsz, 6, d)
    mod_c = jnp.broadcast_to(table[:, bsz].reshape(depth, 1, 6, d), (depth, bsz, 6, d))
    mods = jnp.stack([mod_c, mod_l], axis=2)
    mods = jnp.pad(mods, ((0, 0), (0, 0), (0, 0), (0, SUBLANES - 6), (0, 0)))

    cos, sin = _rope_tables(n_ctx, n_lat)
    stream = jnp.concatenate([ctx, x], axis=1)
    v_first = None
    no_final = jnp.ones((1, d), F32)

    for i in range(depth):
        last = i == depth - 1
        j = i // 2
        t0 = ctx_tiles if last else 0
        mod = mods[i]
        g_mix = row(norm_mix_g[i])
        if i % 2 == 0:
            lambda_init = 0.8 - 0.6 * math.exp(-0.3 * i)
            q, k, v = _qkv(stream, mod, g_mix, da_w_qkv[j].astype(BF16), cos, sin, ctx_tiles)
            o = _flash(q, k, v, da_lambda[j], da_subln_g[j], ctx_tiles, t0, lambda_init)
            stream = _proj_residual(stream, o, mod, da_w_o[j].astype(BF16), ctx_tiles, t0)
        else:
            p = dict(mix_prev=rw_mix_prev[j], mix_next=rw_mix_next[j],
                     w_r=rw_w_r[j].astype(BF16), w_k=rw_w_k[j].astype(BF16),
                     w_v=rw_w_v[j].astype(BF16), w0=rw_w0[j], w1=cat(rw_w1[j]).astype(BF16),
                     w2=cat(rw_w2[j]).astype(BF16), a0=rw_a0[j], a1=cat(rw_a1[j]).astype(BF16),
                     a2=cat(rw_a2[j]).astype(BF16), k_k=row(rw_k_k[j]), k_a=row(rw_k_a[j]),
                     r_k=row(rw_r_k[j]), g1=rw_g1[j].astype(BF16), g2=rw_g2[j].astype(BF16))
            if j > 0:
                n_v = rw_v1.shape[-1]
                p.update(v0=row(rw_v0[j - 1]),
                         v1=jnp.pad(rw_v1[j - 1], ((0, 0), (0, LANES - n_v))).astype(BF16),
                         v2=jnp.pad(rw_v2[j - 1], ((0, LANES - n_v), (0, 0))).astype(BF16))
            r, k, v, kk, bonus, gate, lw, a = _rwkv_proj(stream, mod, g_mix, p, v_first,
                                                         ctx_tiles)
            if v_first is None:
                v_first = v
            ys = []
            for reverse in (False, True):
                rh, yl, mt, gt = _wkv_prep(r, k, v, kk, lw, a, p['k_a'], reverse)
                ys.append(_wkv_scan(rh, yl, mt, gt, ctx_tiles, reverse))
            stream = _rwkv_out(stream, ys[0], ys[1], bonus, gate, mod, row(rw_lnx_g[j]),
                               row(rw_lnx_b[j]), rw_w_o[j].astype(BF16), ctx_tiles, t0)
        stream = _ffn(stream, mod, row(norm_ffn_g[i]), ffn_w_up[i].astype(BF16),
                      ffn_conv_w[i], row(ffn_conv_b[i]), ffn_w_down[i].astype(BF16),
                      row(final_norm_g) if last else no_final, 0 if last else ctx_tiles, last)
    return stream
```

```python
import functools
import math

import jax
import jax.numpy as jnp
from jax import lax
from jax.experimental import pallas as pl
from jax.experimental.pallas import tpu as pltpu

D_MODEL = 1024
GRID_W = 64
DA_HEAD_DIM = 64
DA_HEADS = D_MODEL // (2 * DA_HEAD_DIM)
ROPE_THETA = 10000.0
RW_HEAD = 64
N_SHIFT_MIX = 6
LNX_EPS = 64e-5
EPS = 1e-6

LANES = 128
SUBLANES = 8
ROW_TILE = 256
WKV_CHUNK = 32
FF_TILE = 256
FLASH_UNITS = 4
PREP_CHUNKS = 4
VMEM_LIMIT = 56 * 1024 * 1024

F32 = jnp.float32
BF16 = jnp.bfloat16

Q_SCALE = DA_HEAD_DIM ** -0.5 * math.log2(math.e)


def _dot(a, b):
    return jnp.dot(a.astype(BF16), b.astype(BF16), preferred_element_type=F32)


def _dot_nt(a, b):
    return lax.dot_general(a.astype(BF16), b.astype(BF16), (((1,), (1,)), ((), ())),
                           preferred_element_type=F32)


def _dot_tn(a, b):
    return lax.dot_general(a.astype(BF16), b.astype(BF16), (((0,), (0,)), ((), ())),
                           preferred_element_type=F32)


def _rms(x):
    return x * lax.rsqrt(jnp.mean(x * x, axis=-1, keepdims=True) + EPS)


def _norm_mod(x, g, shift, scale):
    return _rms(x) * g * (1.0 + scale) + shift


def _sigmoid(x):
    return 1.0 / (1.0 + jnp.exp(-x))


def _softplus(x):
    return jnp.maximum(x, 0.0) + jnp.log1p(jnp.exp(-jnp.abs(x)))


def _lane_iota(shape):
    return lax.broadcasted_iota(jnp.int32, shape, len(shape) - 1)


def _row_iota(shape):
    return lax.broadcasted_iota(jnp.int32, shape, len(shape) - 2)


def _group_ones():
    r = lax.broadcasted_iota(jnp.int32, (LANES, LANES), 0) // RW_HEAD
    c = lax.broadcasted_iota(jnp.int32, (LANES, LANES), 1) // RW_HEAD
    return (r == c).astype(BF16)


def _group_sum(z):
    ones = _group_ones()
    parts = [jnp.dot(z[:, j:j + LANES].astype(BF16), ones, preferred_element_type=F32)
             for j in range(0, z.shape[1], LANES)]
    return jnp.concatenate(parts, axis=1)


def _shifted(h, halo_prev, halo_next, prev_ok, next_ok):
    rows = h.shape[0]
    ri = _row_iota(h.shape)
    first = jnp.where(prev_ok, halo_prev[SUBLANES - 1:SUBLANES, :], 0.0)
    last = jnp.where(next_ok, halo_next[0:1, :], 0.0)
    h_prev = jnp.where(ri == 0, first, pltpu.roll(h, 1, 0))
    h_next = jnp.where(ri == rows - 1, last, pltpu.roll(h, rows - 1, 0))
    return h_prev, h_next


def _edge_flags(t, n_tiles, ctx_tiles):
    prev_ok = jnp.logical_and(t != 0, t != ctx_tiles)
    next_ok = jnp.logical_and(t != n_tiles - 1, t != ctx_tiles - 1)
    return prev_ok, next_ok


def _const_spec(shape):
    zeros = (0,) * len(shape)
    return pl.BlockSpec(shape, lambda *_: zeros, pipeline_mode=pl.Buffered(1))


def _tile_spec(t0=0):
    return pl.BlockSpec((None, ROW_TILE, D_MODEL), lambda b, t: (b, t + t0, 0))


def _halo_specs(n_rows, t0=0):
    per = ROW_TILE // SUBLANES
    last = n_rows // SUBLANES - 1
    prev = pl.BlockSpec((None, SUBLANES, D_MODEL),
                        lambda b, t: (b, jnp.maximum((t + t0) * per - 1, 0), 0))
    nxt = pl.BlockSpec((None, SUBLANES, D_MODEL),
                       lambda b, t: (b, jnp.minimum((t + t0 + 1) * per, last), 0))
    return prev, nxt


def _mod_spec(ctx_tiles, t0=0):
    return pl.BlockSpec((None, None, SUBLANES, D_MODEL),
                        lambda b, t: (b, ((t + t0) >= ctx_tiles).astype(jnp.int32), 0, 0))


def _params():
    return pltpu.CompilerParams(vmem_limit_bytes=VMEM_LIMIT)


def _ada_kernel(c_ref, w_ref, b_ref, o_ref):
    c = c_ref[...]
    s = c * _sigmoid(c)
    o_ref[...] = jnp.dot(s, w_ref[...], precision=lax.Precision.HIGHEST,
                         preferred_element_type=F32) + b_ref[...]


def _ada_table(cc, ada_w, ada_b):
    n_layers, _, six_d = ada_w.shape
    rows = cc.shape[0]
    nb = 1536
    return pl.pallas_call(
        _ada_kernel,
        grid=(n_layers, six_d // nb),
        in_specs=[pl.BlockSpec((rows, D_MODEL), lambda l, n: (0, 0)),
                  pl.BlockSpec((None, D_MODEL, nb), lambda l, n: (l, 0, n)),
                  pl.BlockSpec((None, 1, nb), lambda l, n: (l, 0, n))],
        out_specs=pl.BlockSpec((None, rows, nb), lambda l, n: (l, 0, n)),
        out_shape=jax.ShapeDtypeStruct((n_layers, rows, six_d), F32),
        compiler_params=_params(),
        name="ada_table",
    )(cc, ada_w, ada_b.reshape(n_layers, 1, six_d))


def _rope(slab, cos, sin_signed):
    lane = _lane_iota(slab.shape)
    first = (lane % DA_HEAD_DIM) < (DA_HEAD_DIM // 2)
    partner = jnp.where(first, pltpu.roll(slab, LANES - DA_HEAD_DIM // 2, 1),
                        pltpu.roll(slab, DA_HEAD_DIM // 2, 1))
    return slab * cos + partner * sin_signed


def _qkv_kernel(x_ref, mod_ref, g_ref, w_ref, cos_ref, sin_ref, q_ref, k_ref, v_ref):
    mod = mod_ref[...]
    h = _norm_mod(x_ref[...], g_ref[...], mod[0:1], mod[1:2]).astype(BF16)
    cos = cos_ref[...]
    sin = sin_ref[...]
    wide = 2 * LANES
    for j in range(0, D_MODEL, wide):
        qa = jnp.dot(h, w_ref[:, j:j + wide], preferred_element_type=F32)
        ka = jnp.dot(h, w_ref[:, D_MODEL + j:D_MODEL + j + wide], preferred_element_type=F32)
        for i in range(0, wide, LANES):
            q_ref[:, j + i:j + i + LANES] = (
                _rope(qa[:, i:i + LANES], cos, sin) * Q_SCALE).astype(BF16)
            k_ref[:, j + i:j + i + LANES] = _rope(ka[:, i:i + LANES], cos, sin).astype(BF16)
        v_ref[:, j:j + wide] = jnp.dot(
            h, w_ref[:, 2 * D_MODEL + j:2 * D_MODEL + j + wide],
            preferred_element_type=F32).astype(BF16)


def _qkv(x, mod, g, w_qkv, cos, sin, ctx_tiles):
    bsz, n_rows, _ = x.shape
    out = jax.ShapeDtypeStruct((bsz, n_rows, D_MODEL), BF16)
    tab = pl.BlockSpec((ROW_TILE, LANES), lambda b, t: (t, 0))
    return pl.pallas_call(
        _qkv_kernel,
        grid=(bsz, n_rows // ROW_TILE),
        in_specs=[_tile_spec(), _mod_spec(ctx_tiles), _const_spec((1, D_MODEL)),
                  _const_spec((D_MODEL, 3 * D_MODEL)), tab, tab],
        out_specs=[_tile_spec(), _tile_spec(), _tile_spec()],
        out_shape=[out, out, out],
        compiler_params=_params(),
        name="attn_qkv",
    )(x, mod, g, w_qkv, cos, sin)


def _key_splits(n_tiles):
    units = min(FLASH_UNITS, n_tiles)
    base, extra = divmod(n_tiles, units)
    bounds, start = [], 0
    for u in range(units):
        stop = start + base + (1 if u < extra else 0)
        bounds.append((start, stop))
        start = stop
    return bounds


def _flash_kernel(lam_ref, g_ref, q_ref, k_ref, v_ref, o_ref, kbd_ref, vbd_ref,
                  *, ctx_tiles, t0, lambda_init):
    t = pl.program_id(2) + t0
    n_keys = k_ref.shape[0]
    wide = 2 * LANES

    @pl.when(pl.program_id(2) == 0)
    def _():
        zero = jnp.zeros((LANES, LANES), BF16)

        def fill(j, carry):
            src = pl.ds(pl.multiple_of(j * LANES, LANES), LANES)
            first = pl.ds(pl.multiple_of(j * wide, wide), LANES)
            second = pl.ds(pl.multiple_of(j * wide + LANES, LANES), LANES)
            for ref, dst in ((k_ref, kbd_ref), (v_ref, vbd_ref)):
                tile = ref[src, :]
                dst[first, :] = jnp.concatenate([tile, zero], axis=1)
                dst[second, :] = jnp.concatenate([zero, tile], axis=1)
            return carry

        lax.fori_loop(0, n_keys // LANES, fill, 0)

    q = q_ref[...]
    lo = _lane_iota(q.shape) < DA_HEAD_DIM
    zero = jnp.zeros_like(q)
    lhs = jnp.concatenate([jnp.where(lo, q, zero), jnp.where(lo, zero, q)], axis=1)
    lv = lam_ref[...]
    lam = (jnp.exp(jnp.sum(lv[0:1] * lv[1:2], axis=-1, keepdims=True))
           - jnp.exp(jnp.sum(lv[2:3] * lv[3:4], axis=-1, keepdims=True)) + lambda_init)

    def halves(z, reduce):
        a = jnp.broadcast_to(reduce(z[:, :LANES], axis=-1, keepdims=True), (z.shape[0], LANES))
        b = jnp.broadcast_to(reduce(z[:, LANES:], axis=-1, keepdims=True), (z.shape[0], LANES))
        return jnp.concatenate([a, b], axis=1)

    def scores(bounds):
        start, stop = bounds
        return lax.dot_general(lhs, kbd_ref[start * wide:stop * wide, :],
                               (((1,), (1,)), ((), ())), preferred_element_type=F32)

    def finish(bounds, s):
        start, stop = bounds
        groups = [s[:, j:j + wide] for j in range(0, s.shape[1], wide)]
        m = halves(functools.reduce(jnp.maximum, groups), jnp.max)
        probs = [jnp.exp2(gr - m) for gr in groups]
        norm = halves(functools.reduce(jnp.add, probs), jnp.sum)
        p = jnp.concatenate([pr.astype(BF16) for pr in probs], axis=1)
        pv = jnp.dot(p, vbd_ref[start * wide:stop * wide, :], preferred_element_type=F32)
        return m, norm, pv

    def attend(n_tiles):
        splits = _key_splits(n_tiles)
        parts = []
        s_next = scores(splits[0])
        for u, bounds in enumerate(splits):
            s_cur = s_next
            if u + 1 < len(splits):
                s_next = scores(splits[u + 1])
            parts.append(finish(bounds, s_cur))
        m_all = functools.reduce(jnp.maximum, [m for m, _, _ in parts])
        norm = jnp.zeros_like(m_all)
        acc = jnp.zeros_like(m_all)
        for m, n_u, pv in parts:
            w = jnp.exp2(m - m_all)
            norm = norm + n_u * w
            acc = acc + pv * w
        o = acc / norm
        o = o[:, :LANES] - lam * o[:, LANES:]
        o_ref[...] = (_rms(o) * g_ref[...] * (1.0 - lambda_init)).astype(BF16)

    @pl.when(t < ctx_tiles)
    def _():
        attend(ctx_tiles * ROW_TILE // LANES)

    @pl.when(t >= ctx_tiles)
    def _():
        attend(n_keys // LANES)


def _flash(q, k, v, lam_vecs, subln_g, ctx_tiles, t0, lambda_init):
    bsz, n_rows, _ = q.shape
    n_tiles = n_rows // ROW_TILE
    kv_spec = pl.BlockSpec((None, n_rows, LANES), lambda b, h, t: (b, 0, h))
    q_spec = pl.BlockSpec((None, ROW_TILE, LANES), lambda b, h, t: (b, t + t0, h))
    kern = functools.partial(_flash_kernel, ctx_tiles=ctx_tiles, t0=t0, lambda_init=lambda_init)
    return pl.pallas_call(
        kern,
        grid=(bsz, DA_HEADS, n_tiles - t0),
        in_specs=[pl.BlockSpec(lam_vecs.shape, lambda b, h, t: (0, 0)),
                  pl.BlockSpec((1, LANES), lambda b, h, t: (0, 0)),
                  q_spec, kv_spec, kv_spec],
        out_specs=q_spec,
        out_shape=jax.ShapeDtypeStruct((bsz, n_rows, D_MODEL), BF16),
        scratch_shapes=[pltpu.VMEM((2 * n_rows, 2 * LANES), BF16),
                        pltpu.VMEM((2 * n_rows, 2 * LANES), BF16)],
        compiler_params=pltpu.CompilerParams(
            dimension_semantics=("arbitrary", "arbitrary", "arbitrary"),
            vmem_limit_bytes=VMEM_LIMIT),
        name="diff_flash",
    )(lam_vecs, subln_g.reshape(1, LANES), q, k, v)


def _proj_residual_kernel(x_ref, z_ref, mod_ref, w_ref, o_ref):
    out = jnp.dot(z_ref[...], w_ref[...], preferred_element_type=F32)
    o_ref[...] = x_ref[...] + mod_ref[2:3, :] * out


def _proj_residual(x, z, mod, w, ctx_tiles, t0):
    bsz, n_rows, _ = x.shape
    n_tiles = n_rows // ROW_TILE - t0
    return pl.pallas_call(
        _proj_residual_kernel,
        grid=(bsz, n_tiles),
        in_specs=[_tile_spec(t0), _tile_spec(t0), _mod_spec(ctx_tiles, t0),
                  _const_spec((D_MODEL, D_MODEL))],
        out_specs=_tile_spec(),
        out_shape=jax.ShapeDtypeStruct((bsz, n_tiles * ROW_TILE, D_MODEL), F32),
        compiler_params=_params(),
        name="proj_residual",
    )(x, z, mod, w)


def _ffn_kernel(x_ref, xp_ref, xn_ref, mod_ref, g_ref, wup_ref, cw_ref, cb_ref, wdn_ref,
                fin_ref, o_ref, *, n_tiles, ctx_tiles, d_ff, final):
    t = pl.program_id(1)
    prev_ok, next_ok = _edge_flags(t, n_tiles, ctx_tiles)
    mod = mod_ref[...]
    g = g_ref[...]
    x = x_ref[...]
    h = _norm_mod(x, g, mod[3:4], mod[4:5]).astype(BF16)
    hp = _norm_mod(xp_ref[...], g, mod[3:4], mod[4:5]).astype(BF16)
    hn = _norm_mod(xn_ref[...], g, mod[3:4], mod[4:5]).astype(BF16)
    hcat = jnp.concatenate([h, hp, hn], axis=0)
    tiles = list(range(0, d_ff, FF_TILE))

    def up(f):
        gate = jnp.dot(hcat, wup_ref[:, f:f + FF_TILE], preferred_element_type=F32)
        val = jnp.dot(h, wup_ref[:, d_ff + f:d_ff + f + FF_TILE], preferred_element_type=F32)
        return gate, val

    acc = jnp.zeros((ROW_TILE, D_MODEL), F32)
    nxt = up(tiles[0])
    for i, f in enumerate(tiles):
        gate_all, val = nxt
        if i + 1 < len(tiles):
            nxt = up(tiles[i + 1])
        gate = gate_all[:ROW_TILE]
        g_prev, g_next = _shifted(gate, gate_all[ROW_TILE:ROW_TILE + SUBLANES],
                                  gate_all[ROW_TILE + SUBLANES:], prev_ok, next_ok)
        cw = cw_ref[:, f:f + FF_TILE]
        conv = g_prev * cw[0:1] + gate * cw[1:2] + g_next * cw[2:3] + cb_ref[:, f:f + FF_TILE]
        act = conv * _sigmoid(conv) * val
        acc = acc + jnp.dot(act.astype(BF16), wdn_ref[f:f + FF_TILE, :],
                            preferred_element_type=F32)
    y = x + mod[5:6] * acc
    if final:
        y = _rms(y) * fin_ref[...]
    o_ref[...] = y


def _ffn(x, mod, g, w_up, conv_w, conv_b, w_down, final_g, ctx_tiles, final):
    bsz, n_rows, _ = x.shape
    n_tiles = n_rows // ROW_TILE
    d_ff = w_down.shape[0]
    prev, nxt = _halo_specs(n_rows)
    kern = functools.partial(_ffn_kernel, n_tiles=n_tiles, ctx_tiles=ctx_tiles, d_ff=d_ff,
                             final=final)
    return pl.pallas_call(
        kern,
        grid=(bsz, n_tiles),
        in_specs=[_tile_spec(), prev, nxt, _mod_spec(ctx_tiles), _const_spec((1, D_MODEL)),
                  _const_spec((D_MODEL, 2 * d_ff)), _const_spec((3, d_ff)),
                  _const_spec((1, d_ff)), _const_spec((d_ff, D_MODEL)),
                  _const_spec((1, D_MODEL))],
        out_specs=_tile_spec(),
        out_shape=jax.ShapeDtypeStruct((bsz, n_rows, D_MODEL), F32),
        compiler_params=_params(),
        name="conv_glu",
    )(x, x, x, mod, g, w_up, conv_w, conv_b, w_down, final_g)


def _rwkv_proj_kernel(*refs, n_tiles, ctx_tiles, has_vfirst):
    (x_ref, xp_ref, xn_ref, mod_ref, g_ref, mp_ref, mn_ref, wr_ref, wk_ref, wv_ref,
     w0_ref, w1_ref, w2_ref, a0_ref, a1_ref, a2_ref, kk_ref, ka_ref, rk_ref,
     g1_ref, g2_ref) = refs[:21]
    refs = refs[21:]
    if has_vfirst:
        vf_ref, v0_ref, v1_ref, v2_ref = refs[:4]
        refs = refs[4:]
    r_ref, k_ref, v_ref, kkn_ref, bonus_ref, gate_ref, lw_ref, a_ref = refs

    t = pl.program_id(1)
    prev_ok, next_ok = _edge_flags(t, n_tiles, ctx_tiles)
    mod = mod_ref[...]
    g = g_ref[...]
    h = _norm_mod(x_ref[...], g, mod[0:1], mod[1:2])
    hp = _norm_mod(xp_ref[...], g, mod[0:1], mod[1:2])
    hn = _norm_mod(xn_ref[...], g, mod[0:1], mod[1:2])
    h_prev, h_next = _shifted(h, hp, hn, prev_ok, next_ok)
    xx_p = h_prev - h
    xx_n = h_next - h
    mp = mp_ref[...]
    mn = mn_ref[...]

    def mix(m):
        return (h + xx_p * mp[m:m + 1] + xx_n * mn[m:m + 1]).astype(BF16)

    lo = _lane_iota((ROW_TILE, LANES)) < RW_HEAD

    def halves(z):
        return jnp.where(lo, z, 0.0), jnp.where(lo, 0.0, z)

    r = jnp.dot(mix(0), wr_ref[...], preferred_element_type=F32)
    r_ref[...] = r

    lw = halves(jnp.tanh(jnp.dot(mix(1), w1_ref[...], preferred_element_type=F32)))
    la = halves(jnp.dot(mix(4), a1_ref[...], preferred_element_type=F32))
    a_gate = []
    for d in range(2):
        wl = w0_ref[d:d + 1, :] + _dot(lw[d], w2_ref[...])
        lw_ref[d] = -jnp.exp(-_softplus(-wl) - 0.5)
        a_d = _sigmoid(a0_ref[d:d + 1, :] + _dot(la[d], a2_ref[...]))
        a_ref[d] = a_d
        a_gate.append(a_d)

    k = jnp.dot(mix(2), wk_ref[...], preferred_element_type=F32)
    k_ref[...] = k
    kk = k * kk_ref[...]
    kkn_ref[...] = kk / jnp.maximum(jnp.sqrt(_group_sum(kk * kk)), 1e-12)

    xv = mix(3)
    v = jnp.dot(xv, wv_ref[...], preferred_element_type=F32)
    if has_vfirst:
        lv = jnp.dot(xv, v1_ref[...], preferred_element_type=F32)
        v = v + (vf_ref[...] - v) * _sigmoid(v0_ref[...] + _dot(lv, v2_ref[...]))
    v_ref[...] = v

    ka = ka_ref[...]
    k_sum = k * (1.0 + (a_gate[0] - 1.0) * ka) + k * (1.0 + (a_gate[1] - 1.0) * ka)
    bonus_ref[...] = _group_sum(r * k_sum * rk_ref[...]) * v

    gl = _sigmoid(jnp.dot(mix(5), g1_ref[...], preferred_element_type=F32))
    gate_ref[...] = _dot(gl, g2_ref[...])


def _rwkv_proj(x, mod, g, p, v_first, ctx_tiles):
    bsz, n_rows, _ = x.shape
    n_tiles = n_rows // ROW_TILE
    prev, nxt = _halo_specs(n_rows)
    has_vfirst = v_first is not None
    args = [x, x, x, mod, g, p['mix_prev'], p['mix_next'], p['w_r'], p['w_k'], p['w_v'],
            p['w0'], p['w1'], p['w2'], p['a0'], p['a1'], p['a2'], p['k_k'], p['k_a'], p['r_k'],
            p['g1'], p['g2']]
    specs = [_tile_spec(), prev, nxt, _mod_spec(ctx_tiles)] + [
        _const_spec(a.shape) for a in args[4:]]
    if has_vfirst:
        args += [v_first, p['v0'], p['v1'], p['v2']]
        specs += [_tile_spec()] + [_const_spec(a.shape) for a in args[-3:]]
    one = jax.ShapeDtypeStruct((bsz, n_rows, D_MODEL), F32)
    two = jax.ShapeDtypeStruct((2, bsz, n_rows, D_MODEL), F32)
    two_spec = pl.BlockSpec((2, None, ROW_TILE, D_MODEL), lambda b, t: (0, b, t, 0))
    kern = functools.partial(_rwkv_proj_kernel, n_tiles=n_tiles, ctx_tiles=ctx_tiles,
                             has_vfirst=has_vfirst)
    return pl.pallas_call(
        kern,
        grid=(bsz, n_tiles),
        in_specs=specs,
        out_specs=[_tile_spec()] * 6 + [two_spec, two_spec],
        out_shape=[one] * 6 + [two, two],
        compiler_params=_params(),
        name="rwkv_proj",
    )(*args)


def _mask_stack(x, n_blocks, width):
    blk = _lane_iota(x.shape) // width
    zero = jnp.zeros_like(x)
    return jnp.concatenate([jnp.where(blk == i, x, zero) for i in range(n_blocks)], axis=0)


def _wkv_prep_kernel(r_ref, k_ref, v_ref, kk_ref, lw_ref, a_ref, ka_ref,
                     rh_ref, yl_ref, mt_ref, gt_ref, *, reverse):
    n = WKV_CHUNK
    hpg = LANES // n
    gw = hpg * RW_HEAD
    groups = D_MODEL // gw
    ri = lax.broadcasted_iota(jnp.int32, (n, n), 0)
    ci = lax.broadcasted_iota(jnp.int32, (n, n), 1)
    tri = (ci >= ri if reverse else ci <= ri).astype(F32)
    row = _row_iota((n, LANES))
    src = _lane_iota((n, LANES)) % n
    if reverse:
        strict, incl = src > row, src >= row
    else:
        strict, incl = src < row, src <= row
    eye = (src == row).astype(F32)
    lo_st = _lane_iota((RW_HEAD, LANES)) < RW_HEAD
    diag = (_lane_iota((RW_HEAD, LANES)) % RW_HEAD) == _row_iota((RW_HEAD, LANES))
    last = 0 if reverse else n - 1
    ka = ka_ref[...]
    bf = lambda z: z.astype(BF16)

    def body(it, carry):
        items = []
        for cc in range(PREP_CHUNKS):
            c = it * PREP_CHUNKS + cc
            rows = pl.ds(pl.multiple_of(c * n, n), n)
            lw = lw_ref[rows, :]
            gate = a_ref[rows, :]
            kk = kk_ref[rows, :]
            kd = k_ref[rows, :] * (1.0 + (gate - 1.0) * ka)
            b = kk * gate
            cum = jnp.dot(tri, lw, precision=lax.Precision.HIGHEST, preferred_element_type=F32)
            cum_end = cum[last:last + 1, :]
            e_neg = jnp.exp(-cum)
            e_rem = jnp.exp(cum_end - cum)
            rt = r_ref[rows, :] * jnp.exp(cum)
            full = dict(c=c, rows=rows, rt=rt, w_end=jnp.exp(cum_end), v=bf(v_ref[rows, :]),
                        at=bf(-kk * jnp.exp(cum - lw)), rtb=bf(rt), kt=bf(kd * e_neg),
                        bt=bf(b * e_neg), kh=bf(kd * e_rem), bh=bf(b * e_rem))
            for g in range(groups):
                sl = slice(g * gw, (g + 1) * gw)
                items.append(dict(full=full, sl=sl))

        for it_ in items:
            f, sl = it_['full'], it_['sl']
            lhs = jnp.concatenate([f['at'][:, sl], f['rtb'][:, sl]], axis=0)
            rhs = jnp.concatenate([_mask_stack(f['kt'][:, sl], hpg, RW_HEAD),
                                   _mask_stack(f['bt'][:, sl], hpg, RW_HEAD)], axis=0)
            a = lax.dot_general(lhs, rhs, (((1,), (1,)), ((), ())), preferred_element_type=F32)
            it_['aak'] = jnp.where(strict, a[:n, :LANES], 0.0)
            it_['ark'] = jnp.where(incl, a[n:, :LANES], 0.0)
            it_['arb'] = bf(jnp.where(incl, a[n:, LANES:], 0.0))
            pw = jnp.where(strict, a[:n, LANES:], 0.0)
            it_['inv'] = eye + pw
            it_['pw'] = bf(pw)
            it_['bd'] = _mask_stack(it_['pw'], hpg, n)

        steps = 1
        while 2 * steps < n:
            for it_ in items:
                it_['pw'] = bf(jnp.dot(it_['pw'], it_['bd'], preferred_element_type=F32))
                it_['bd'] = _mask_stack(it_['pw'], hpg, n)
            for it_ in items:
                it_['inv'] = it_['inv'] + jnp.dot(bf(it_['inv']), it_['bd'],
                                                  preferred_element_type=F32)
            steps *= 2

        for it_ in items:
            f, sl = it_['full'], it_['sl']
            vms = _mask_stack(f['v'][:, sl], hpg, RW_HEAD)
            x = jnp.dot(bf(jnp.concatenate([it_['aak'], it_['ark']], axis=0)), vms,
                        preferred_element_type=F32)
            it_['arkv'] = x[n:]
            it_['rhs'] = jnp.concatenate([_mask_stack(f['at'][:, sl], hpg, RW_HEAD),
                                          _mask_stack(bf(x[:n]), hpg, RW_HEAD)], axis=1)
        for it_ in items:
            au = bf(jnp.dot(bf(it_['inv']), it_['rhs'], preferred_element_type=F32))
            it_['ah'] = au[:, :gw]
            it_['ul'] = au[:, gw:]
            it_['rhs'] = jnp.concatenate([_mask_stack(it_['ah'], hpg, RW_HEAD),
                                          _mask_stack(it_['ul'], hpg, RW_HEAD)], axis=1)
        for it_ in items:
            f, sl = it_['full'], it_['sl']
            x = jnp.dot(it_['arb'], it_['rhs'], preferred_element_type=F32)
            rh_ref[f['rows'], sl] = f['rt'][:, sl] + x[:, :gw]
            yl_ref[f['rows'], sl] = it_['arkv'] + x[:, gw:]
        for it_ in items:
            f, sl = it_['full'], it_['sl']
            for j in range(0, gw, LANES):
                loc = slice(j, j + LANES)
                hs = slice(sl.start + j, sl.start + j + LANES)
                bh = f['bh'][:, hs]
                pm = lax.dot_general(bh, it_['ah'][:, loc], (((0,), (0,)), ((), ())),
                                     preferred_element_type=F32)
                gf = lax.dot_general(
                    jnp.concatenate([f['kh'][:, hs], bh], axis=0),
                    jnp.concatenate([f['v'][:, hs], it_['ul'][:, loc]], axis=0),
                    (((0,), (0,)), ((), ())), preferred_element_type=F32)
                mt = jnp.where(lo_st, pm[:RW_HEAD], pm[RW_HEAD:])
                mt_ref[f['c'], :, hs] = mt + jnp.where(diag, f['w_end'][:, hs], 0.0)
                gt_ref[f['c'], :, hs] = jnp.where(lo_st, gf[:RW_HEAD], gf[RW_HEAD:])
        return carry

    lax.fori_loop(0, ROW_TILE // (n * PREP_CHUNKS), body, 0)


def _wkv_prep(r, k, v, kk, lw, a, k_a, reverse):
    bsz, n_rows, _ = r.shape
    n_tiles = n_rows // ROW_TILE
    per = ROW_TILE // WKV_CHUNK
    dirn = 1 if reverse else 0
    dir_spec = pl.BlockSpec((None, None, ROW_TILE, D_MODEL), lambda b, t: (dirn, b, t, 0))
    st_spec = pl.BlockSpec((None, per, RW_HEAD, D_MODEL), lambda b, t: (b, t, 0, 0))
    one = jax.ShapeDtypeStruct((bsz, n_rows, D_MODEL), F32)
    st = jax.ShapeDtypeStruct((bsz, n_rows // WKV_CHUNK, RW_HEAD, D_MODEL), F32)
    return pl.pallas_call(
        functools.partial(_wkv_prep_kernel, reverse=reverse),
        grid=(bsz, n_tiles),
        in_specs=[_tile_spec()] * 4 + [dir_spec, dir_spec, _const_spec((1, D_MODEL))],
        out_specs=[_tile_spec(), _tile_spec(), st_spec, st_spec],
        out_shape=[one, one, st, st],
        compiler_params=_params(),
        name="wkv_prep_bwd" if reverse else "wkv_prep_fwd",
    )(r, k, v, kk, lw, a, k_a)


def _wkv_scan_kernel(rh_ref, yl_ref, mt_ref, gt_ref, y_ref, state_ref, *, reverse):
    n = WKV_CHUNK
    per = ROW_TILE // n

    @pl.when(pl.program_id(1) == 0)
    def _():
        state_ref[...] = jnp.zeros(state_ref.shape, F32)

    lo = _lane_iota((RW_HEAD, LANES)) < RW_HEAD
    for i in range(per):
        c = per - 1 - i if reverse else i
        for hp in range(D_MODEL // LANES):
            col = slice(hp * LANES, (hp + 1) * LANES)
            lhs = jnp.concatenate([rh_ref[c * n:(c + 1) * n, col], mt_ref[c, :, col]], axis=0)
            out = _dot(lhs, state_ref[hp])
            y_ref[c * n:(c + 1) * n, col] = out[:n] + yl_ref[c * n:(c + 1) * n, col]
            st = out[n:] + gt_ref[c, :, col]
            state_ref[hp] = jnp.concatenate([jnp.where(lo, st, 0.0), jnp.where(lo, 0.0, st)],
                                            axis=0)


def _wkv_scan(rh, yl, mt, gt, ctx_tiles, reverse):
    bsz, n_rows, _ = rh.shape
    n_tiles = n_rows // ROW_TILE
    per = ROW_TILE // WKV_CHUNK

    def tile(t):
        if not reverse:
            return t
        return jnp.where(t < ctx_tiles, ctx_tiles - 1 - t, n_tiles - 1 - (t - ctx_tiles))

    row_spec = pl.BlockSpec((None, ROW_TILE, D_MODEL), lambda b, t: (b, tile(t), 0))
    st_spec = pl.BlockSpec((None, per, RW_HEAD, D_MODEL), lambda b, t: (b, tile(t), 0, 0))
    return pl.pallas_call(
        functools.partial(_wkv_scan_kernel, reverse=reverse),
        grid=(bsz, n_tiles),
        in_specs=[row_spec, row_spec, st_spec, st_spec],
        out_specs=row_spec,
        out_shape=jax.ShapeDtypeStruct((bsz, n_rows, D_MODEL), F32),
        scratch_shapes=[pltpu.VMEM((D_MODEL // LANES, LANES, LANES), F32)],
        compiler_params=pltpu.CompilerParams(dimension_semantics=("arbitrary", "arbitrary"),
                                             vmem_limit_bytes=VMEM_LIMIT),
        name="wkv_scan_bwd" if reverse else "wkv_scan_fwd",
    )(rh, yl, mt, gt)


def _rwkv_out_kernel(x_ref, yf_ref, yb_ref, bonus_ref, gate_ref, mod_ref, lg_ref, lb_ref,
                     w_ref, o_ref):
    y = yf_ref[...] + yb_ref[...]
    mu = _group_sum(y) * (1.0 / RW_HEAD)
    yc = y - mu
    var = _group_sum(yc * yc) * (1.0 / RW_HEAD)
    yn = yc * lax.rsqrt(var + LNX_EPS) * lg_ref[...] + lb_ref[...]
    z = ((yn + bonus_ref[...]) * gate_ref[...]).astype(BF16)
    out = jnp.dot(z, w_ref[...], preferred_element_type=F32)
    o_ref[...] = x_ref[...] + mod_ref[2:3, :] * out


def _rwkv_out(x, yf, yb, bonus, gate, mod, lnx_g, lnx_b, w_o, ctx_tiles, t0):
    bsz, n_rows, _ = x.shape
    n_tiles = n_rows // ROW_TILE - t0
    return pl.pallas_call(
        _rwkv_out_kernel,
        grid=(bsz, n_tiles),
        in_specs=[_tile_spec(t0)] * 5 + [_mod_spec(ctx_tiles, t0), _const_spec((1, D_MODEL)),
                                          _const_spec((1, D_MODEL)),
                                          _const_spec((D_MODEL, D_MODEL))],
        out_specs=_tile_spec(),
        out_shape=jax.ShapeDtypeStruct((bsz, n_tiles * ROW_TILE, D_MODEL), F32),
        compiler_params=_params(),
        name="rwkv_out",
    )(x, yf, yb, bonus, gate, mod, lnx_g, lnx_b, w_o)


def _rope_tables(n_ctx, n_lat):
    t = jnp.arange(n_lat)
    row_pos = (t // GRID_W).astype(F32)
    col_pos = (t % GRID_W).astype(F32)
    n_freq = DA_HEAD_DIM // 4
    inv_freq = ROPE_THETA ** (-jnp.arange(n_freq, dtype=F32) / n_freq)
    ang = jnp.concatenate([row_pos[:, None] * inv_freq, col_pos[:, None] * inv_freq], axis=-1)
    cos, sin = jnp.cos(ang), jnp.sin(ang)
    reps = LANES // DA_HEAD_DIM
    cos = jnp.tile(jnp.concatenate([cos, cos], axis=-1), (1, reps))
    sin = jnp.tile(jnp.concatenate([-sin, sin], axis=-1), (1, reps))
    cos = jnp.concatenate([jnp.ones((n_ctx, LANES), F32), cos], axis=0)
    sin = jnp.concatenate([jnp.zeros((n_ctx, LANES), F32), sin], axis=0)
    return cos, sin


def kernel(x, c, ctx, c_ctx, ada_w, ada_b, norm_mix_g, norm_ffn_g, ffn_w_up, ffn_conv_w, ffn_conv_b, ffn_w_down, da_w_qkv, da_lambda, da_subln_g, da_w_o, rw_mix_prev, rw_mix_next, rw_w_r, rw_w_k, rw_w_v, rw_w0, rw_w1, rw_w2, rw_a0, rw_a1, rw_a2, rw_v0, rw_v1, rw_v2, rw_k_k, rw_k_a, rw_r_k, rw_g1, rw_g2, rw_lnx_g, rw_lnx_b, rw_w_o, final_norm_g):
    bsz, n_lat, d = x.shape
    n_ctx = ctx.shape[1]
    depth = ada_w.shape[0]
    assert d == D_MODEL and n_lat % ROW_TILE == 0 and n_ctx % ROW_TILE == 0
    assert n_lat % GRID_W == 0
    ctx_tiles = n_ctx // ROW_TILE
    row = lambda a: a.reshape(1, -1)
    cat = lambda a: jnp.concatenate([a[0], a[1]], axis=0 if a.shape[1] == RW_HEAD else 1)

    pad = (-(bsz + 1)) % SUBLANES
    cc = jnp.concatenate([c, c_ctx[None, :], jnp.zeros((pad, d), F32)], axis=0)
    table = _ada_table(cc, ada_w, ada_b)
    mod_l = table[:, :bsz].reshape(depth, bsz, 6, d)
    mod_c = jnp.broadcast_to(table[:, bsz].reshape(depth, 1, 6, d), (depth, bsz, 6, d))
    mods = jnp.stack([mod_c, mod_l], axis=2)
    mods = jnp.pad(mods, ((0, 0), (0, 0), (0, 0), (0, SUBLANES - 6), (0, 0)))

    cos, sin = _rope_tables(n_ctx, n_lat)
    stream = jnp.concatenate([ctx, x], axis=1)
    v_first = None
    no_final = jnp.ones((1, d), F32)

    for i in range(depth):
        last = i == depth - 1
        j = i // 2
        t0 = ctx_tiles if last else 0
        mod = mods[i]
        g_mix = row(norm_mix_g[i])
        if i % 2 == 0:
            lambda_init = 0.8 - 0.6 * math.exp(-0.3 * i)
            q, k, v = _qkv(stream, mod, g_mix, da_w_qkv[j].astype(BF16), cos, sin, ctx_tiles)
            o = _flash(q, k, v, da_lambda[j], da_subln_g[j], ctx_tiles, t0, lambda_init)
            stream = _proj_residual(stream, o, mod, da_w_o[j].astype(BF16), ctx_tiles, t0)
        else:
            p = dict(mix_prev=rw_mix_prev[j], mix_next=rw_mix_next[j],
                     w_r=rw_w_r[j].astype(BF16), w_k=rw_w_k[j].astype(BF16),
                     w_v=rw_w_v[j].astype(BF16), w0=rw_w0[j], w1=cat(rw_w1[j]).astype(BF16),
                     w2=cat(rw_w2[j]).astype(BF16), a0=rw_a0[j], a1=cat(rw_a1[j]).astype(BF16),
                     a2=cat(rw_a2[j]).astype(BF16), k_k=row(rw_k_k[j]), k_a=row(rw_k_a[j]),
                     r_k=row(rw_r_k[j]), g1=rw_g1[j].astype(BF16), g2=rw_g2[j].astype(BF16))
            if j > 0:
                n_v = rw_v1.shape[-1]
                p.update(v0=row(rw_v0[j - 1]),
                         v1=jnp.pad(rw_v1[j - 1], ((0, 0), (0, LANES - n_v))).astype(BF16),
                         v2=jnp.pad(rw_v2[j - 1], ((0, LANES - n_v), (0, 0))).astype(BF16))
            r, k, v, kk, bonus, gate, lw, a = _rwkv_proj(stream, mod, g_mix, p, v_first,
                                                         ctx_tiles)
            if v_first is None:
                v_first = v
            ys = []
            for reverse in (False, True):
                rh, yl, mt, gt = _wkv_prep(r, k, v, kk, lw, a, p['k_a'], reverse)
                ys.append(_wkv_scan(rh, yl, mt, gt, ctx_tiles, reverse))
            stream = _rwkv_out(stream, ys[0], ys[1], bonus, gate, mod, row(rw_lnx_g[j]),
                               row(rw_lnx_b[j]), rw_w_o[j].astype(BF16), ctx_tiles, t0)
        stream = _ffn(stream, mod, row(norm_ffn_g[i]), ffn_w_up[i].astype(BF16),
                      ffn_conv_w[i], row(ffn_conv_b[i]), ffn_w_down[i].astype(BF16),
                      row(final_norm_g) if last else no_final, 0 if last else ctx_tiles, last)
    return stream
```

```python
import functools
import math

import jax
import jax.numpy as jnp
from jax import lax
from jax.experimental import pallas as pl
from jax.experimental.pallas import tpu as pltpu

D_MODEL = 1024
GRID_W = 64
DA_HEAD_DIM = 64
DA_HEADS = D_MODEL // (2 * DA_HEAD_DIM)
ROPE_THETA = 10000.0
RW_HEAD = 64
N_SHIFT_MIX = 6
LNX_EPS = 64e-5
EPS = 1e-6

LANES = 128
SUBLANES = 8
ROW_TILE = 256
WKV_CHUNK = 32
FF_TILE = 256
FLASH_UNITS = 4
PREP_CHUNKS = 4
VMEM_LIMIT = 56 * 1024 * 1024

F32 = jnp.float32
BF16 = jnp.bfloat16

Q_SCALE = DA_HEAD_DIM ** -0.5 * math.log2(math.e)


def _dot(a, b):
    return jnp.dot(a.astype(BF16), b.astype(BF16), preferred_element_type=F32)


def _dot_nt(a, b):
    return lax.dot_general(a.astype(BF16), b.astype(BF16), (((1,), (1,)), ((), ())),
                           preferred_element_type=F32)


def _dot_tn(a, b):
    return lax.dot_general(a.astype(BF16), b.astype(BF16), (((0,), (0,)), ((), ())),
                           preferred_element_type=F32)


def _rms(x):
    return x * lax.rsqrt(jnp.mean(x * x, axis=-1, keepdims=True) + EPS)


def _norm_mod(x, g, shift, scale):
    return _rms(x) * g * (1.0 + scale) + shift


def _sigmoid(x):
    return 1.0 / (1.0 + jnp.exp(-x))


def _softplus(x):
    return jnp.maximum(x, 0.0) + jnp.log1p(jnp.exp(-jnp.abs(x)))


def _lane_iota(shape):
    return lax.broadcasted_iota(jnp.int32, shape, len(shape) - 1)


def _row_iota(shape):
    return lax.broadcasted_iota(jnp.int32, shape, len(shape) - 2)


def _group_ones():
    r = lax.broadcasted_iota(jnp.int32, (LANES, LANES), 0) // RW_HEAD
    c = lax.broadcasted_iota(jnp.int32, (LANES, LANES), 1) // RW_HEAD
    return (r == c).astype(BF16)


def _group_sum(z):
    ones = _group_ones()
    parts = [jnp.dot(z[:, j:j + LANES].astype(BF16), ones, preferred_element_type=F32)
             for j in range(0, z.shape[1], LANES)]
    return jnp.concatenate(parts, axis=1)


def _shifted(h, halo_prev, halo_next, prev_ok, next_ok):
    rows = h.shape[0]
    ri = _row_iota(h.shape)
    first = jnp.where(prev_ok, halo_prev[SUBLANES - 1:SUBLANES, :], 0.0)
    last = jnp.where(next_ok, halo_next[0:1, :], 0.0)
    h_prev = jnp.where(ri == 0, first, pltpu.roll(h, 1, 0))
    h_next = jnp.where(ri == rows - 1, last, pltpu.roll(h, rows - 1, 0))
    return h_prev, h_next


def _edge_flags(t, n_tiles, ctx_tiles):
    prev_ok = jnp.logical_and(t != 0, t != ctx_tiles)
    next_ok = jnp.logical_and(t != n_tiles - 1, t != ctx_tiles - 1)
    return prev_ok, next_ok


def _const_spec(shape):
    zeros = (0,) * len(shape)
    return pl.BlockSpec(shape, lambda *_: zeros, pipeline_mode=pl.Buffered(1))


def _tile_spec(t0=0):
    return pl.BlockSpec((None, ROW_TILE, D_MODEL), lambda b, t: (b, t + t0, 0))


def _halo_specs(n_rows, t0=0):
    per = ROW_TILE // SUBLANES
    last = n_rows // SUBLANES - 1
    prev = pl.BlockSpec((None, SUBLANES, D_MODEL),
                        lambda b, t: (b, jnp.maximum((t + t0) * per - 1, 0), 0))
    nxt = pl.BlockSpec((None, SUBLANES, D_MODEL),
                       lambda b, t: (b, jnp.minimum((t + t0 + 1) * per, last), 0))
    return prev, nxt


def _mod_spec(ctx_tiles, t0=0):
    return pl.BlockSpec((None, None, SUBLANES, D_MODEL),
                        lambda b, t: (b, ((t + t0) >= ctx_tiles).astype(jnp.int32), 0, 0))


def _params():
    return pltpu.CompilerParams(vmem_limit_bytes=VMEM_LIMIT)


def _ada_kernel(c_ref, w_ref, b_ref, o_ref):
    c = c_ref[...]
    s = c * _sigmoid(c)
    o_ref[...] = jnp.dot(s, w_ref[...], precision=lax.Precision.HIGHEST,
                         preferred_element_type=F32) + b_ref[...]


def _ada_table(cc, ada_w, ada_b):
    n_layers, _, six_d = ada_w.shape
    rows = cc.shape[0]
    nb = 1536
    return pl.pallas_call(
        _ada_kernel,
        grid=(n_layers, six_d // nb),
        in_specs=[pl.BlockSpec((rows, D_MODEL), lambda l, n: (0, 0)),
                  pl.BlockSpec((None, D_MODEL, nb), lambda l, n: (l, 0, n)),
                  pl.BlockSpec((None, 1, nb), lambda l, n: (l, 0, n))],
        out_specs=pl.BlockSpec((None, rows, nb), lambda l, n: (l, 0, n)),
        out_shape=jax.ShapeDtypeStruct((n_layers, rows, six_d), F32),
        compiler_params=_params(),
        name="ada_table",
    )(cc, ada_w, ada_b.reshape(n_layers, 1, six_d))


def _rope(slab, cos, sin_signed):
    lane = _lane_iota(slab.shape)
    first = (lane % DA_HEAD_DIM) < (DA_HEAD_DIM // 2)
    partner = jnp.where(first, pltpu.roll(slab, LANES - DA_HEAD_DIM // 2, 1),
                        pltpu.roll(slab, DA_HEAD_DIM // 2, 1))
    return slab * cos + partner * sin_signed


def _qkv_kernel(x_ref, mod_ref, g_ref, w_ref, cos_ref, sin_ref, q_ref, k_ref, v_ref):
    mod = mod_ref[...]
    h = _norm_mod(x_ref[...], g_ref[...], mod[0:1], mod[1:2]).astype(BF16)
    cos = cos_ref[...]
    sin = sin_ref[...]
    wide = 2 * LANES
    for j in range(0, D_MODEL, wide):
        qa = jnp.dot(h, w_ref[:, j:j + wide], preferred_element_type=F32)
        ka = jnp.dot(h, w_ref[:, D_MODEL + j:D_MODEL + j + wide], preferred_element_type=F32)
        for i in range(0, wide, LANES):
            q_ref[:, j + i:j + i + LANES] = (
                _rope(qa[:, i:i + LANES], cos, sin) * Q_SCALE).astype(BF16)
            k_ref[:, j + i:j + i + LANES] = _rope(ka[:, i:i + LANES], cos, sin).astype(BF16)
        v_ref[:, j:j + wide] = jnp.dot(
            h, w_ref[:, 2 * D_MODEL + j:2 * D_MODEL + j + wide],
            preferred_element_type=F32).astype(BF16)


def _qkv(x, mod, g, w_qkv, cos, sin, ctx_tiles):
    bsz, n_rows, _ = x.shape
    out = jax.ShapeDtypeStruct((bsz, n_rows, D_MODEL), BF16)
    tab = pl.BlockSpec((ROW_TILE, LANES), lambda b, t: (t, 0))
    return pl.pallas_call(
        _qkv_kernel,
        grid=(bsz, n_rows // ROW_TILE),
        in_specs=[_tile_spec(), _mod_spec(ctx_tiles), _const_spec((1, D_MODEL)),
                  _const_spec((D_MODEL, 3 * D_MODEL)), tab, tab],
        out_specs=[_tile_spec(), _tile_spec(), _tile_spec()],
        out_shape=[out, out, out],
        compiler_params=_params(),
        name="attn_qkv",
    )(x, mod, g, w_qkv, cos, sin)


def _key_splits(n_tiles):
    units = min(FLASH_UNITS, n_tiles)
    base, extra = divmod(n_tiles, units)
    bounds, start = [], 0
    for u in range(units):
        stop = start + base + (1 if u < extra else 0)
        bounds.append((start, stop))
        start = stop
    return bounds


def _flash_kernel(lam_ref, g_ref, q_ref, k_ref, v_ref, o_ref, kbd_ref, vbd_ref,
                  *, ctx_tiles, t0, lambda_init):
    t = pl.program_id(2) + t0
    n_keys = k_ref.shape[0]
    wide = 2 * LANES

    @pl.when(pl.program_id(2) == 0)
    def _():
        zero = jnp.zeros((LANES, LANES), BF16)

        def fill(j, carry):
            src = pl.ds(pl.multiple_of(j * LANES, LANES), LANES)
            first = pl.ds(pl.multiple_of(j * wide, wide), LANES)
            second = pl.ds(pl.multiple_of(j * wide + LANES, LANES), LANES)
            for ref, dst in ((k_ref, kbd_ref), (v_ref, vbd_ref)):
                tile = ref[src, :]
                dst[first, :] = jnp.concatenate([tile, zero], axis=1)
                dst[second, :] = jnp.concatenate([zero, tile], axis=1)
            return carry

        lax.fori_loop(0, n_keys // LANES, fill, 0)

    q = q_ref[...]
    lo = _lane_iota(q.shape) < DA_HEAD_DIM
    zero = jnp.zeros_like(q)
    lhs = jnp.concatenate([jnp.where(lo, q, zero), jnp.where(lo, zero, q)], axis=1)
    lv = lam_ref[...]
    lam = (jnp.exp(jnp.sum(lv[0:1] * lv[1:2], axis=-1, keepdims=True))
           - jnp.exp(jnp.sum(lv[2:3] * lv[3:4], axis=-1, keepdims=True)) + lambda_init)

    def halves(z, reduce):
        a = jnp.broadcast_to(reduce(z[:, :LANES], axis=-1, keepdims=True), (z.shape[0], LANES))
        b = jnp.broadcast_to(reduce(z[:, LANES:], axis=-1, keepdims=True), (z.shape[0], LANES))
        return jnp.concatenate([a, b], axis=1)

    def scores(bounds):
        start, stop = bounds
        return lax.dot_general(lhs, kbd_ref[start * wide:stop * wide, :],
                               (((1,), (1,)), ((), ())), preferred_element_type=F32)

    def finish(bounds, s):
        start, stop = bounds
        groups = [s[:, j:j + wide] for j in range(0, s.shape[1], wide)]
        m = halves(functools.reduce(jnp.maximum, groups), jnp.max)
        probs = [jnp.exp2(gr - m) for gr in groups]
        norm = halves(functools.reduce(jnp.add, probs), jnp.sum)
        p = jnp.concatenate([pr.astype(BF16) for pr in probs], axis=1)
        pv = jnp.dot(p, vbd_ref[start * wide:stop * wide, :], preferred_element_type=F32)
        return m, norm, pv

    def attend(n_tiles):
        splits = _key_splits(n_tiles)
        parts = []
        s_next = scores(splits[0])
        for u, bounds in enumerate(splits):
            s_cur = s_next
            if u + 1 < len(splits):
                s_next = scores(splits[u + 1])
            parts.append(finish(bounds, s_cur))
        m_all = functools.reduce(jnp.maximum, [m for m, _, _ in parts])
        norm = jnp.zeros_like(m_all)
        acc = jnp.zeros_like(m_all)
        for m, n_u, pv in parts:
            w = jnp.exp2(m - m_all)
            norm = norm + n_u * w
            acc = acc + pv * w
        o = acc / norm
        o = o[:, :LANES] - lam * o[:, LANES:]
        o_ref[...] = (_rms(o) * g_ref[...] * (1.0 - lambda_init)).astype(BF16)

    @pl.when(t < ctx_tiles)
    def _():
        attend(ctx_tiles * ROW_TILE // LANES)

    @pl.when(t >= ctx_tiles)
    def _():
        attend(n_keys // LANES)


def _flash(q, k, v, lam_vecs, subln_g, ctx_tiles, t0, lambda_init):
    bsz, n_rows, _ = q.shape
    n_tiles = n_rows // ROW_TILE
    kv_spec = pl.BlockSpec((None, n_rows, LANES), lambda b, h, t: (b, 0, h))
    q_spec = pl.BlockSpec((None, ROW_TILE, LANES), lambda b, h, t: (b, t + t0, h))
    kern = functools.partial(_flash_kernel, ctx_tiles=ctx_tiles, t0=t0, lambda_init=lambda_init)
    return pl.pallas_call(
        kern,
        grid=(bsz, DA_HEADS, n_tiles - t0),
        in_specs=[pl.BlockSpec(lam_vecs.shape, lambda b, h, t: (0, 0)),
                  pl.BlockSpec((1, LANES), lambda b, h, t: (0, 0)),
                  q_spec, kv_spec, kv_spec],
        out_specs=q_spec,
        out_shape=jax.ShapeDtypeStruct((bsz, n_rows, D_MODEL), BF16),
        scratch_shapes=[pltpu.VMEM((2 * n_rows, 2 * LANES), BF16),
                        pltpu.VMEM((2 * n_rows, 2 * LANES), BF16)],
        compiler_params=pltpu.CompilerParams(
            dimension_semantics=("arbitrary", "arbitrary", "arbitrary"),
            vmem_limit_bytes=VMEM_LIMIT),
        name="diff_flash",
    )(lam_vecs, subln_g.reshape(1, LANES), q, k, v)


def _proj_residual_kernel(x_ref, z_ref, mod_ref, w_ref, o_ref):
    out = jnp.dot(z_ref[...], w_ref[...], preferred_element_type=F32)
    o_ref[...] = x_ref[...] + mod_ref[2:3, :] * out


def _proj_residual(x, z, mod, w, ctx_tiles, t0):
    bsz, n_rows, _ = x.shape
    n_tiles = n_rows // ROW_TILE - t0
    return pl.pallas_call(
        _proj_residual_kernel,
        grid=(bsz, n_tiles),
        in_specs=[_tile_spec(t0), _tile_spec(t0), _mod_spec(ctx_tiles, t0),
                  _const_spec((D_MODEL, D_MODEL))],
        out_specs=_tile_spec(),
        out_shape=jax.ShapeDtypeStruct((bsz, n_tiles * ROW_TILE, D_MODEL), F32),
        compiler_params=_params(),
        name="proj_residual",
    )(x, z, mod, w)


def _ffn_kernel(x_ref, xp_ref, xn_ref, mod_ref, g_ref, wup_ref, cw_ref, cb_ref, wdn_ref,
                fin_ref, o_ref, *, n_tiles, ctx_tiles, d_ff, final):
    t = pl.program_id(1)
    prev_ok, next_ok = _edge_flags(t, n_tiles, ctx_tiles)
    mod = mod_ref[...]
    g = g_ref[...]
    x = x_ref[...]
    h = _norm_mod(x, g, mod[3:4], mod[4:5]).astype(BF16)
    hp = _norm_mod(xp_ref[...], g, mod[3:4], mod[4:5]).astype(BF16)
    hn = _norm_mod(xn_ref[...], g, mod[3:4], mod[4:5]).astype(BF16)
    hcat = jnp.concatenate([h, hp, hn], axis=0)
    tiles = list(range(0, d_ff, FF_TILE))

    def up(f):
        gate = jnp.dot(hcat, wup_ref[:, f:f + FF_TILE], preferred_element_type=F32)
        val = jnp.dot(h, wup_ref[:, d_ff + f:d_ff + f + FF_TILE], preferred_element_type=F32)
        return gate, val

    acc = jnp.zeros((ROW_TILE, D_MODEL), F32)
    nxt = up(tiles[0])
    for i, f in enumerate(tiles):
        gate_all, val = nxt
        if i + 1 < len(tiles):
            nxt = up(tiles[i + 1])
        gate = gate_all[:ROW_TILE]
        g_prev, g_next = _shifted(gate, gate_all[ROW_TILE:ROW_TILE + SUBLANES],
                                  gate_all[ROW_TILE + SUBLANES:], prev_ok, next_ok)
        cw = cw_ref[:, f:f + FF_TILE]
        conv = g_prev * cw[0:1] + gate * cw[1:2] + g_next * cw[2:3] + cb_ref[:, f:f + FF_TILE]
        act = conv * _sigmoid(conv) * val
        acc = acc + jnp.dot(act.astype(BF16), wdn_ref[f:f + FF_TILE, :],
                            preferred_element_type=F32)
    y = x + mod[5:6] * acc
    if final:
        y = _rms(y) * fin_ref[...]
    o_ref[...] = y


def _ffn(x, mod, g, w_up, conv_w, conv_b, w_down, final_g, ctx_tiles, final):
    bsz, n_rows, _ = x.shape
    n_tiles = n_rows // ROW_TILE
    d_ff = w_down.shape[0]
    prev, nxt = _halo_specs(n_rows)
    kern = functools.partial(_ffn_kernel, n_tiles=n_tiles, ctx_tiles=ctx_tiles, d_ff=d_ff,
                             final=final)
    return pl.pallas_call(
        kern,
        grid=(bsz, n_tiles),
        in_specs=[_tile_spec(), prev, nxt, _mod_spec(ctx_tiles), _const_spec((1, D_MODEL)),
                  _const_spec((D_MODEL, 2 * d_ff)), _const_spec((3, d_ff)),
                  _const_spec((1, d_ff)), _const_spec((d_ff, D_MODEL)),
                  _const_spec((1, D_MODEL))],
        out_specs=_tile_spec(),
        out_shape=jax.ShapeDtypeStruct((bsz, n_rows, D_MODEL), F32),
        compiler_params=_params(),
        name="conv_glu",
    )(x, x, x, mod, g, w_up, conv_w, conv_b, w_down, final_g)


def _rwkv_proj_kernel(*refs, n_tiles, ctx_tiles, has_vfirst):
    (x_ref, xp_ref, xn_ref, mod_ref, g_ref, mp_ref, mn_ref, wr_ref, wk_ref, wv_ref,
     w0_ref, w1_ref, w2_ref, a0_ref, a1_ref, a2_ref, kk_ref, ka_ref, rk_ref,
     g1_ref, g2_ref) = refs[:21]
    refs = refs[21:]
    if has_vfirst:
        vf_ref, v0_ref, v1_ref, v2_ref = refs[:4]
        refs = refs[4:]
    r_ref, k_ref, v_ref, kkn_ref, bonus_ref, gate_ref, lw_ref, a_ref = refs

    t = pl.program_id(1)
    prev_ok, next_ok = _edge_flags(t, n_tiles, ctx_tiles)
    mod = mod_ref[...]
    g = g_ref[...]
    h = _norm_mod(x_ref[...], g, mod[0:1], mod[1:2])
    hp = _norm_mod(xp_ref[...], g, mod[0:1], mod[1:2])
    hn = _norm_mod(xn_ref[...], g, mod[0:1], mod[1:2])
    h_prev, h_next = _shifted(h, hp, hn, prev_ok, next_ok)
    xx_p = h_prev - h
    xx_n = h_next - h
    mp = mp_ref[...]
    mn = mn_ref[...]

    def mix(m):
        return (h + xx_p * mp[m:m + 1] + xx_n * mn[m:m + 1]).astype(BF16)

    lo = _lane_iota((ROW_TILE, LANES)) < RW_HEAD

    def halves(z):
        return jnp.where(lo, z, 0.0), jnp.where(lo, 0.0, z)

    r = jnp.dot(mix(0), wr_ref[...], preferred_element_type=F32)
    r_ref[...] = r

    lw = halves(jnp.tanh(jnp.dot(mix(1), w1_ref[...], preferred_element_type=F32)))
    la = halves(jnp.dot(mix(4), a1_ref[...], preferred_element_type=F32))
    a_gate = []
    for d in range(2):
        wl = w0_ref[d:d + 1, :] + _dot(lw[d], w2_ref[...])
        lw_ref[d] = -jnp.exp(-_softplus(-wl) - 0.5)
        a_d = _sigmoid(a0_ref[d:d + 1, :] + _dot(la[d], a2_ref[...]))
        a_ref[d] = a_d
        a_gate.append(a_d)

    k = jnp.dot(mix(2), wk_ref[...], preferred_element_type=F32)
    k_ref[...] = k
    kk = k * kk_ref[...]
    kkn_ref[...] = kk / jnp.maximum(jnp.sqrt(_group_sum(kk * kk)), 1e-12)

    xv = mix(3)
    v = jnp.dot(xv, wv_ref[...], preferred_element_type=F32)
    if has_vfirst:
        lv = jnp.dot(xv, v1_ref[...], preferred_element_type=F32)
        v = v + (vf_ref[...] - v) * _sigmoid(v0_ref[...] + _dot(lv, v2_ref[...]))
    v_ref[...] = v

    ka = ka_ref[...]
    k_sum = k * (1.0 + (a_gate[0] - 1.0) * ka) + k * (1.0 + (a_gate[1] - 1.0) * ka)
    bonus_ref[...] = _group_sum(r * k_sum * rk_ref[...]) * v

    gl = _sigmoid(jnp.dot(mix(5), g1_ref[...], preferred_element_type=F32))
    gate_ref[...] = _dot(gl, g2_ref[...])


def _rwkv_proj(x, mod, g, p, v_first, ctx_tiles):
    bsz, n_rows, _ = x.shape
    n_tiles = n_rows // ROW_TILE
    prev, nxt = _halo_specs(n_rows)
    has_vfirst = v_first is not None
    args = [x, x, x, mod, g, p['mix_prev'], p['mix_next'], p['w_r'], p['w_k'], p['w_v'],
            p['w0'], p['w1'], p['w2'], p['a0'], p['a1'], p['a2'], p['k_k'], p['k_a'], p['r_k'],
            p['g1'], p['g2']]
    specs = [_tile_spec(), prev, nxt, _mod_spec(ctx_tiles)] + [
        _const_spec(a.shape) for a in args[4:]]
    if has_vfirst:
        args += [v_first, p['v0'], p['v1'], p['v2']]
        specs += [_tile_spec()] + [_const_spec(a.shape) for a in args[-3:]]
    one = jax.ShapeDtypeStruct((bsz, n_rows, D_MODEL), F32)
    two = jax.ShapeDtypeStruct((2, bsz, n_rows, D_MODEL), F32)
    two_spec = pl.BlockSpec((2, None, ROW_TILE, D_MODEL), lambda b, t: (0, b, t, 0))
    kern = functools.partial(_rwkv_proj_kernel, n_tiles=n_tiles, ctx_tiles=ctx_tiles,
                             has_vfirst=has_vfirst)
    return pl.pallas_call(
        kern,
        grid=(bsz, n_tiles),
        in_specs=specs,
        out_specs=[_tile_spec()] * 6 + [two_spec, two_spec],
        out_shape=[one] * 6 + [two, two],
        compiler_params=_params(),
        name="rwkv_proj",
    )(*args)


def _mask_stack(x, n_blocks, width):
    blk = _lane_iota(x.shape) // width
    zero = jnp.zeros_like(x)
    return jnp.concatenate([jnp.where(blk == i, x, zero) for i in range(n_blocks)], axis=0)


def _wkv_prep_kernel(r_ref, k_ref, v_ref, kk_ref, lw_ref, a_ref, ka_ref,
                     rh_ref, yl_ref, mt_ref, gt_ref):
    n = WKV_CHUNK
    hpg = LANES // n
    gw = hpg * RW_HEAD
    groups = D_MODEL // gw
    ri = lax.broadcasted_iota(jnp.int32, (n, n), 0)
    ci = lax.broadcasted_iota(jnp.int32, (n, n), 1)
    row = _row_iota((n, LANES))
    src = _lane_iota((n, LANES)) % n
    eye = (src == row).astype(F32)
    order = ((ci <= ri).astype(F32), src < row, src <= row, n - 1), \
            ((ci >= ri).astype(F32), src > row, src >= row, 0)
    lo_st = _lane_iota((RW_HEAD, LANES)) < RW_HEAD
    diag = (_lane_iota((RW_HEAD, LANES)) % RW_HEAD) == _row_iota((RW_HEAD, LANES))
    ka = ka_ref[...]
    bf = lambda z: z.astype(BF16)

    def body(it, carry):
        items = []
        for cc in range(PREP_CHUNKS):
            c = it * PREP_CHUNKS + cc
            rows = pl.ds(pl.multiple_of(c * n, n), n)
            kk = kk_ref[rows, :]
            k = k_ref[rows, :]
            r = r_ref[rows, :]
            v = bf(v_ref[rows, :])
            for d in range(2):
                tri, strict, incl, last = order[d]
                lw = lw_ref[d, rows, :]
                gate = a_ref[d, rows, :]
                kd = k * (1.0 + (gate - 1.0) * ka)
                b = kk * gate
                cum = jnp.dot(tri, lw, precision=lax.Precision.HIGHEST,
                              preferred_element_type=F32)
                cum_end = cum[last:last + 1, :]
                e_neg = jnp.exp(-cum)
                e_rem = jnp.exp(cum_end - cum)
                rt = r * jnp.exp(cum)
                full = dict(c=c, d=d, rows=rows, rt=rt, w_end=jnp.exp(cum_end), v=v,
                            at=bf(-kk * jnp.exp(cum - lw)), rtb=bf(rt), kt=bf(kd * e_neg),
                            bt=bf(b * e_neg), kh=bf(kd * e_rem), bh=bf(b * e_rem),
                            strict=strict, incl=incl)
                for g in range(groups):
                    items.append(dict(full=full, sl=slice(g * gw, (g + 1) * gw)))

        for it_ in items:
            f, sl = it_['full'], it_['sl']
            lhs = jnp.concatenate([f['at'][:, sl], f['rtb'][:, sl]], axis=0)
            rhs = jnp.concatenate([_mask_stack(f['kt'][:, sl], hpg, RW_HEAD),
                                   _mask_stack(f['bt'][:, sl], hpg, RW_HEAD)], axis=0)
            a = lax.dot_general(lhs, rhs, (((1,), (1,)), ((), ())), preferred_element_type=F32)
            it_['aak'] = jnp.where(f['strict'], a[:n, :LANES], 0.0)
            it_['ark'] = jnp.where(f['incl'], a[n:, :LANES], 0.0)
            it_['arb'] = bf(jnp.where(f['incl'], a[n:, LANES:], 0.0))
            pw = jnp.where(f['strict'], a[:n, LANES:], 0.0)
            it_['inv'] = eye + pw
            it_['pw'] = bf(pw)
            it_['bd'] = _mask_stack(it_['pw'], hpg, n)

        steps = 1
        while 2 * steps < n:
            for it_ in items:
                it_['pw'] = bf(jnp.dot(it_['pw'], it_['bd'], preferred_element_type=F32))
                it_['bd'] = _mask_stack(it_['pw'], hpg, n)
            for it_ in items:
                it_['inv'] = it_['inv'] + jnp.dot(bf(it_['inv']), it_['bd'],
                                                  preferred_element_type=F32)
            steps *= 2

        for it_ in items:
            f, sl = it_['full'], it_['sl']
            vms = _mask_stack(f['v'][:, sl], hpg, RW_HEAD)
            x = jnp.dot(bf(jnp.concatenate([it_['aak'], it_['ark']], axis=0)), vms,
                        preferred_element_type=F32)
            it_['arkv'] = x[n:]
            it_['rhs'] = jnp.concatenate([_mask_stack(f['at'][:, sl], hpg, RW_HEAD),
                                          _mask_stack(bf(x[:n]), hpg, RW_HEAD)], axis=1)
        for it_ in items:
            au = bf(jnp.dot(bf(it_['inv']), it_['rhs'], preferred_element_type=F32))
            it_['ah'] = au[:, :gw]
            it_['ul'] = au[:, gw:]
            it_['rhs'] = jnp.concatenate([_mask_stack(it_['ah'], hpg, RW_HEAD),
                                          _mask_stack(it_['ul'], hpg, RW_HEAD)], axis=1)
        for it_ in items:
            f, sl = it_['full'], it_['sl']
            x = jnp.dot(it_['arb'], it_['rhs'], preferred_element_type=F32)
            rh_ref[f['d'], f['rows'], sl] = bf(f['rt'][:, sl] + x[:, :gw])
            yl_ref[f['d'], f['rows'], sl] = it_['arkv'] + x[:, gw:]
        for it_ in items:
            f, sl = it_['full'], it_['sl']
            for j in range(0, gw, LANES):
                loc = slice(j, j + LANES)
                hs = slice(sl.start + j, sl.start + j + LANES)
                bh = f['bh'][:, hs]
                pm = lax.dot_general(bh, it_['ah'][:, loc], (((0,), (0,)), ((), ())),
                                     preferred_element_type=F32)
                gf = lax.dot_general(
                    jnp.concatenate([f['kh'][:, hs], bh], axis=0),
                    jnp.concatenate([f['v'][:, hs], it_['ul'][:, loc]], axis=0),
                    (((0,), (0,)), ((), ())), preferred_element_type=F32)
                mt = jnp.where(lo_st, pm[:RW_HEAD], pm[RW_HEAD:])
                mt_ref[f['d'], f['c'], :, hs] = bf(mt + jnp.where(diag, f['w_end'][:, hs], 0.0))
                gt_ref[f['d'], f['c'], :, hs] = jnp.where(lo_st, gf[:RW_HEAD], gf[RW_HEAD:])
        return carry

    lax.fori_loop(0, ROW_TILE // (n * PREP_CHUNKS), body, 0)


def _wkv_prep(r, k, v, kk, lw, a, k_a):
    bsz, n_rows, _ = r.shape
    n_tiles = n_rows // ROW_TILE
    per = ROW_TILE // WKV_CHUNK
    dir_spec = pl.BlockSpec((2, None, ROW_TILE, D_MODEL), lambda b, t: (0, b, t, 0))
    st_spec = pl.BlockSpec((2, None, per, RW_HEAD, D_MODEL), lambda b, t: (0, b, t, 0, 0))
    rows = (2, bsz, n_rows, D_MODEL)
    st = (2, bsz, n_rows // WKV_CHUNK, RW_HEAD, D_MODEL)
    return pl.pallas_call(
        _wkv_prep_kernel,
        grid=(bsz, n_tiles),
        in_specs=[_tile_spec()] * 4 + [dir_spec, dir_spec, _const_spec((1, D_MODEL))],
        out_specs=[dir_spec, dir_spec, st_spec, st_spec],
        out_shape=[jax.ShapeDtypeStruct(rows, BF16), jax.ShapeDtypeStruct(rows, F32),
                   jax.ShapeDtypeStruct(st, BF16), jax.ShapeDtypeStruct(st, F32)],
        compiler_params=_params(),
        name="wkv_prep",
    )(r, k, v, kk, lw, a, k_a)


def _wkv_scan_kernel(rhf_ref, ylf_ref, mtf_ref, gtf_ref, rhb_ref, ylb_ref, mtb_ref, gtb_ref,
                     yf_ref, yb_ref, state_ref):
    n = WKV_CHUNK
    per = ROW_TILE // n

    @pl.when(pl.program_id(1) == 0)
    def _():
        state_ref[...] = jnp.zeros(state_ref.shape, F32)

    lo = _lane_iota((RW_HEAD, LANES)) < RW_HEAD
    dirs = ((rhf_ref, ylf_ref, mtf_ref, gtf_ref, yf_ref), (rhb_ref, ylb_ref, mtb_ref, gtb_ref, yb_ref))
    for i in range(per):
        for d, (rh_ref, yl_ref, mt_ref, gt_ref, y_ref) in enumerate(dirs):
            c = per - 1 - i if d == 1 else i
            for hp in range(D_MODEL // LANES):
                col = slice(hp * LANES, (hp + 1) * LANES)
                lhs = jnp.concatenate([rh_ref[c * n:(c + 1) * n, col], mt_ref[c, :, col]],
                                      axis=0)
                out = jnp.dot(lhs, state_ref[d, hp].astype(BF16), preferred_element_type=F32)
                y_ref[c * n:(c + 1) * n, col] = out[:n] + yl_ref[c * n:(c + 1) * n, col]
                st = out[n:] + gt_ref[c, :, col]
                state_ref[d, hp] = jnp.concatenate(
                    [jnp.where(lo, st, 0.0), jnp.where(lo, 0.0, st)], axis=0)


def _wkv_scan(rh, yl, mt, gt, ctx_tiles):
    _, bsz, n_rows, _ = rh.shape
    n_tiles = n_rows // ROW_TILE
    per = ROW_TILE // WKV_CHUNK

    def back(t):
        return jnp.where(t < ctx_tiles, ctx_tiles - 1 - t, n_tiles - 1 - (t - ctx_tiles))

    def specs(d, tile):
        row = pl.BlockSpec((None, None, ROW_TILE, D_MODEL), lambda b, t: (d, b, tile(t), 0))
        st = pl.BlockSpec((None, None, per, RW_HEAD, D_MODEL),
                          lambda b, t: (d, b, tile(t), 0, 0))
        return [row, row, st, st]

    fwd = lambda t: t
    y_f = pl.BlockSpec((None, ROW_TILE, D_MODEL), lambda b, t: (b, t, 0))
    y_b = pl.BlockSpec((None, ROW_TILE, D_MODEL), lambda b, t: (b, back(t), 0))
    out = jax.ShapeDtypeStruct((bsz, n_rows, D_MODEL), F32)
    return pl.pallas_call(
        _wkv_scan_kernel,
        grid=(bsz, n_tiles),
        in_specs=specs(0, fwd) + specs(1, back),
        out_specs=[y_f, y_b],
        out_shape=[out, out],
        scratch_shapes=[pltpu.VMEM((2, D_MODEL // LANES, LANES, LANES), F32)],
        compiler_params=pltpu.CompilerParams(dimension_semantics=("arbitrary", "arbitrary"),
                                             vmem_limit_bytes=VMEM_LIMIT),
        name="wkv_scan",
    )(rh, yl, mt, gt, rh, yl, mt, gt)


def _rwkv_out_kernel(x_ref, yf_ref, yb_ref, bonus_ref, gate_ref, mod_ref, lg_ref, lb_ref,
                     w_ref, o_ref):
    y = yf_ref[...] + yb_ref[...]
    mu = _group_sum(y) * (1.0 / RW_HEAD)
    yc = y - mu
    var = _group_sum(yc * yc) * (1.0 / RW_HEAD)
    yn = yc * lax.rsqrt(var + LNX_EPS) * lg_ref[...] + lb_ref[...]
    z = ((yn + bonus_ref[...]) * gate_ref[...]).astype(BF16)
    out = jnp.dot(z, w_ref[...], preferred_element_type=F32)
    o_ref[...] = x_ref[...] + mod_ref[2:3, :] * out


def _rwkv_out(x, yf, yb, bonus, gate, mod, lnx_g, lnx_b, w_o, ctx_tiles, t0):
    bsz, n_rows, _ = x.shape
    n_tiles = n_rows // ROW_TILE - t0
    return pl.pallas_call(
        _rwkv_out_kernel,
        grid=(bsz, n_tiles),
        in_specs=[_tile_spec(t0)] * 5 + [_mod_spec(ctx_tiles, t0), _const_spec((1, D_MODEL)),
                                          _const_spec((1, D_MODEL)),
                                          _const_spec((D_MODEL, D_MODEL))],
        out_specs=_tile_spec(),
        out_shape=jax.ShapeDtypeStruct((bsz, n_tiles * ROW_TILE, D_MODEL), F32),
        compiler_params=_params(),
        name="rwkv_out",
    )(x, yf, yb, bonus, gate, mod, lnx_g, lnx_b, w_o)


def _rope_tables(n_ctx, n_lat):
    t = jnp.arange(n_lat)
    row_pos = (t // GRID_W).astype(F32)
    col_pos = (t % GRID_W).astype(F32)
    n_freq = DA_HEAD_DIM // 4
    inv_freq = ROPE_THETA ** (-jnp.arange(n_freq, dtype=F32) / n_freq)
    ang = jnp.concatenate([row_pos[:, None] * inv_freq, col_pos[:, None] * inv_freq], axis=-1)
    cos, sin = jnp.cos(ang), jnp.sin(ang)
    reps = LANES // DA_HEAD_DIM
    cos = jnp.tile(jnp.concatenate([cos, cos], axis=-1), (1, reps))
    sin = jnp.tile(jnp.concatenate([-sin, sin], axis=-1), (1, reps))
    cos = jnp.concatenate([jnp.ones((n_ctx, LANES), F32), cos], axis=0)
    sin = jnp.concatenate([jnp.zeros((n_ctx, LANES), F32), sin], axis=0)
    return cos, sin


def kernel(x, c, ctx, c_ctx, ada_w, ada_b, norm_mix_g, norm_ffn_g, ffn_w_up, ffn_conv_w, ffn_conv_b, ffn_w_down, da_w_qkv, da_lambda, da_subln_g, da_w_o, rw_mix_prev, rw_mix_next, rw_w_r, rw_w_k, rw_w_v, rw_w0, rw_w1, rw_w2, rw_a0, rw_a1, rw_a2, rw_v0, rw_v1, rw_v2, rw_k_k, rw_k_a, rw_r_k, rw_g1, rw_g2, rw_lnx_g, rw_lnx_b, rw_w_o, final_norm_g):
    bsz, n_lat, d = x.shape
    n_ctx = ctx.shape[1]
    depth = ada_w.shape[0]
    assert d == D_MODEL and n_lat % ROW_TILE == 0 and n_ctx % ROW_TILE == 0
    assert n_lat % GRID_W == 0
    ctx_tiles = n_ctx // ROW_TILE
    row = lambda a: a.reshape(1, -1)
    cat = lambda a: jnp.concatenate([a[0], a[1]], axis=0 if a.shape[1] == RW_HEAD else 1)

    pad = (-(bsz + 1)) % SUBLANES
    cc = jnp.concatenate([c, c_ctx[None, :], jnp.zeros((pad, d), F32)], axis=0)
    table = _ada_table(cc, ada_w, ada_b)
    mod_l = table[:, :bsz].reshape(depth, bsz, 6, d)
    mod_c = jnp.broadcast_to(table[:, bsz].reshape(depth, 1, 6, d), (depth, bsz, 6, d))
    mods = jnp.stack([mod_c, mod_l], axis=2)
    mods = jnp.pad(mods, ((0, 0), (0, 0), (0, 0), (0, SUBLANES - 6), (0, 0)))

    cos, sin = _rope_tables(n_ctx, n_lat)
    stream = jnp.concatenate([ctx, x], axis=1)
    v_first = None
    no_final = jnp.ones((1, d), F32)

    for i in range(depth):
        last = i == depth - 1
        j = i // 2
        t0 = ctx_tiles if last else 0
        mod = mods[i]
        g_mix = row(norm_mix_g[i])
        if i % 2 == 0:
            lambda_init = 0.8 - 0.6 * math.exp(-0.3 * i)
            q, k, v = _qkv(stream, mod, g_mix, da_w_qkv[j].astype(BF16), cos, sin, ctx_tiles)
            o = _flash(q, k, v, da_lambda[j], da_subln_g[j], ctx_tiles, t0, lambda_init)
            stream = _proj_residual(stream, o, mod, da_w_o[j].astype(BF16), ctx_tiles, t0)
        else:
            p = dict(mix_prev=rw_mix_prev[j], mix_next=rw_mix_next[j],
                     w_r=rw_w_r[j].astype(BF16), w_k=rw_w_k[j].astype(BF16),
                     w_v=rw_w_v[j].astype(BF16), w0=rw_w0[j], w1=cat(rw_w1[j]).astype(BF16),
                     w2=cat(rw_w2[j]).astype(BF16), a0=rw_a0[j], a1=cat(rw_a1[j]).astype(BF16),
                     a2=cat(rw_a2[j]).astype(BF16), k_k=row(rw_k_k[j]), k_a=row(rw_k_a[j]),
                     r_k=row(rw_r_k[j]), g1=rw_g1[j].astype(BF16), g2=rw_g2[j].astype(BF16))
            if j > 0:
                n_v = rw_v1.shape[-1]
                p.update(v0=row(rw_v0[j - 1]),
                         v1=jnp.pad(rw_v1[j - 1], ((0, 0), (0, LANES - n_v))).astype(BF16),
                         v2=jnp.pad(rw_v2[j - 1], ((0, LANES - n_v), (0, 0))).astype(BF16))
            r, k, v, kk, bonus, gate, lw, a = _rwkv_proj(stream, mod, g_mix, p, v_first,
                                                         ctx_tiles)
            if v_first is None:
                v_first = v
            rh, yl, mt, gt = _wkv_prep(r, k, v, kk, lw, a, p['k_a'])
            ys = _wkv_scan(rh, yl, mt, gt, ctx_tiles)
            stream = _rwkv_out(stream, ys[0], ys[1], bonus, gate, mod, row(rw_lnx_g[j]),
                               row(rw_lnx_b[j]), rw_w_o[j].astype(BF16), ctx_tiles, t0)
        stream = _ffn(stream, mod, row(norm_ffn_g[i]), ffn_w_up[i].astype(BF16),
                      ffn_conv_w[i], row(ffn_conv_b[i]), ffn_w_down[i].astype(BF16),
                      row(final_norm_g) if last else no_final, 0 if last else ctx_tiles, last)
    return stream
```

```python
import functools
import math

import jax
import jax.numpy as jnp
from jax import lax
from jax.experimental import pallas as pl
from jax.experimental.pallas import tpu as pltpu

D_MODEL = 1024
GRID_W = 64
DA_HEAD_DIM = 64
DA_HEADS = D_MODEL // (2 * DA_HEAD_DIM)
ROPE_THETA = 10000.0
RW_HEAD = 64
N_SHIFT_MIX = 6
LNX_EPS = 64e-5
EPS = 1e-6

LANES = 128
SUBLANES = 8
ROW_TILE = 256
WKV_CHUNK = 32
FF_TILE = 256
PREP_CHUNKS = 4
VMEM_LIMIT = 56 * 1024 * 1024

F32 = jnp.float32
BF16 = jnp.bfloat16

Q_SCALE = DA_HEAD_DIM ** -0.5 * math.log2(math.e)


def _dot(a, b):
    return jnp.dot(a.astype(BF16), b.astype(BF16), preferred_element_type=F32)


def _dot_nt(a, b):
    return lax.dot_general(a.astype(BF16), b.astype(BF16), (((1,), (1,)), ((), ())),
                           preferred_element_type=F32)


def _dot_tn(a, b):
    return lax.dot_general(a.astype(BF16), b.astype(BF16), (((0,), (0,)), ((), ())),
                           preferred_element_type=F32)


def _rms(x):
    return x * lax.rsqrt(jnp.mean(x * x, axis=-1, keepdims=True) + EPS)


def _norm_mod(x, g, shift, scale):
    return _rms(x) * g * (1.0 + scale) + shift


def _sigmoid(x):
    return 1.0 / (1.0 + jnp.exp(-x))


def _lane_iota(shape):
    return lax.broadcasted_iota(jnp.int32, shape, len(shape) - 1)


def _row_iota(shape):
    return lax.broadcasted_iota(jnp.int32, shape, len(shape) - 2)


def _group_ones():
    r = lax.broadcasted_iota(jnp.int32, (LANES, LANES), 0) // RW_HEAD
    c = lax.broadcasted_iota(jnp.int32, (LANES, LANES), 1) // RW_HEAD
    return (r == c).astype(BF16)


def _group_sum(z):
    ones = _group_ones()
    parts = [jnp.dot(z[:, j:j + LANES].astype(BF16), ones, preferred_element_type=F32)
             for j in range(0, z.shape[1], LANES)]
    return jnp.concatenate(parts, axis=1)


def _shifted(h, halo_prev, halo_next, prev_ok, next_ok):
    rows = h.shape[0]
    ri = _row_iota(h.shape)
    first = jnp.where(prev_ok, halo_prev[SUBLANES - 1:SUBLANES, :], 0.0)
    last = jnp.where(next_ok, halo_next[0:1, :], 0.0)
    h_prev = jnp.where(ri == 0, first, pltpu.roll(h, 1, 0))
    h_next = jnp.where(ri == rows - 1, last, pltpu.roll(h, rows - 1, 0))
    return h_prev, h_next


def _edge_flags(t, n_tiles, ctx_tiles):
    prev_ok = jnp.logical_and(t != 0, t != ctx_tiles)
    next_ok = jnp.logical_and(t != n_tiles - 1, t != ctx_tiles - 1)
    return prev_ok, next_ok


def _const_spec(shape):
    zeros = (0,) * len(shape)
    return pl.BlockSpec(shape, lambda *_: zeros, pipeline_mode=pl.Buffered(1))


def _tile_spec(t0=0):
    return pl.BlockSpec((None, ROW_TILE, D_MODEL), lambda b, t: (b, t + t0, 0))


def _halo_specs(n_rows, t0=0):
    per = ROW_TILE // SUBLANES
    last = n_rows // SUBLANES - 1
    prev = pl.BlockSpec((None, SUBLANES, D_MODEL),
                        lambda b, t: (b, jnp.maximum((t + t0) * per - 1, 0), 0))
    nxt = pl.BlockSpec((None, SUBLANES, D_MODEL),
                       lambda b, t: (b, jnp.minimum((t + t0 + 1) * per, last), 0))
    return prev, nxt


def _mod_spec(ctx_tiles, t0=0):
    return pl.BlockSpec((None, None, SUBLANES, D_MODEL),
                        lambda b, t: (b, ((t + t0) >= ctx_tiles).astype(jnp.int32), 0, 0))


def _params():
    return pltpu.CompilerParams(vmem_limit_bytes=VMEM_LIMIT)


def _ada_kernel(c_ref, w_ref, b_ref, o_ref):
    c = c_ref[...]
    s = c * _sigmoid(c)
    o_ref[...] = jnp.dot(s, w_ref[...], precision=lax.Precision.HIGHEST,
                         preferred_element_type=F32) + b_ref[...]


def _ada_table(cc, ada_w, ada_b):
    n_layers, _, six_d = ada_w.shape
    rows = cc.shape[0]
    nb = 1536
    return pl.pallas_call(
        _ada_kernel,
        grid=(n_layers, six_d // nb),
        in_specs=[pl.BlockSpec((rows, D_MODEL), lambda l, n: (0, 0)),
                  pl.BlockSpec((None, D_MODEL, nb), lambda l, n: (l, 0, n)),
                  pl.BlockSpec((None, 1, nb), lambda l, n: (l, 0, n))],
        out_specs=pl.BlockSpec((None, rows, nb), lambda l, n: (l, 0, n)),
        out_shape=jax.ShapeDtypeStruct((n_layers, rows, six_d), F32),
        compiler_params=_params(),
        name="ada_table",
    )(cc, ada_w, ada_b.reshape(n_layers, 1, six_d))


def _rope(slab, cos, sin_signed):
    lane = _lane_iota(slab.shape)
    first = (lane % DA_HEAD_DIM) < (DA_HEAD_DIM // 2)
    partner = jnp.where(first, pltpu.roll(slab, LANES - DA_HEAD_DIM // 2, 1),
                        pltpu.roll(slab, DA_HEAD_DIM // 2, 1))
    return slab * cos + partner * sin_signed


def _qkv_kernel(x_ref, mod_ref, g_ref, w_ref, cos_ref, sin_ref, q_ref, k_ref, v_ref):
    mod = mod_ref[...]
    h = _norm_mod(x_ref[...], g_ref[...], mod[0:1], mod[1:2]).astype(BF16)
    cos = cos_ref[...]
    sin = sin_ref[...]
    wide = 2 * LANES
    for j in range(0, D_MODEL, wide):
        qa = jnp.dot(h, w_ref[:, j:j + wide], preferred_element_type=F32)
        ka = jnp.dot(h, w_ref[:, D_MODEL + j:D_MODEL + j + wide], preferred_element_type=F32)
        for i in range(0, wide, LANES):
            q_ref[:, j + i:j + i + LANES] = (
                _rope(qa[:, i:i + LANES], cos, sin) * Q_SCALE).astype(BF16)
            k_ref[:, j + i:j + i + LANES] = _rope(ka[:, i:i + LANES], cos, sin).astype(BF16)
        v_ref[:, j:j + wide] = jnp.dot(
            h, w_ref[:, 2 * D_MODEL + j:2 * D_MODEL + j + wide],
            preferred_element_type=F32).astype(BF16)


def _qkv(x, mod, g, w_qkv, cos, sin, ctx_tiles):
    bsz, n_rows, _ = x.shape
    out = jax.ShapeDtypeStruct((bsz, n_rows, D_MODEL), BF16)
    tab = pl.BlockSpec((ROW_TILE, LANES), lambda b, t: (t, 0))
    return pl.pallas_call(
        _qkv_kernel,
        grid=(bsz, n_rows // ROW_TILE),
        in_specs=[_tile_spec(), _mod_spec(ctx_tiles), _const_spec((1, D_MODEL)),
                  _const_spec((D_MODEL, 3 * D_MODEL)), tab, tab],
        out_specs=[_tile_spec(), _tile_spec(), _tile_spec()],
        out_shape=[out, out, out],
        compiler_params=_params(),
        name="attn_qkv",
    )(x, mod, g, w_qkv, cos, sin)


def _flash_kernel(lam_ref, g_ref, q_ref, qn_ref, k_ref, v_ref, o_ref,
                  sa_ref, sb_ref, ma_ref, mb_ref, *, ctx_tiles, lambda_init):
    t = pl.program_id(2)
    n_keys = k_ref.shape[0]
    ctx_keys = ctx_tiles * ROW_TILE
    lv = lam_ref[...]
    lam = (jnp.exp(jnp.sum(lv[0:1] * lv[1:2], axis=-1, keepdims=True))
           - jnp.exp(jnp.sum(lv[2:3] * lv[3:4], axis=-1, keepdims=True)) + lambda_init)

    def scores(q, n):
        lo = _lane_iota(q.shape) < DA_HEAD_DIM
        zero = jnp.zeros_like(q)
        k = k_ref[0:n, :]
        return [lax.dot_general(qc, k, (((1,), (1,)), ((), ())), preferred_element_type=F32)
                for qc in (jnp.where(lo, q, zero), jnp.where(lo, zero, q))]

    def stash(q, s_ref, m_ref):
        for c, s in enumerate(scores(q, n_keys)):
            s_ref[c] = s
            m_ref[c] = jnp.max(s, axis=-1, keepdims=True)

    def combine(p1, n1, p2, n2, n):
        w = p1 - p2 * (lam * n1 / n2)
        o = jnp.dot(w.astype(BF16), v_ref[0:n, :], preferred_element_type=F32) / n1
        o_ref[...] = (_rms(o) * g_ref[...] * (1.0 - lambda_init)).astype(BF16)

    def finish(s_ref, m_ref):
        norms = []
        for c in range(2):
            p = jnp.exp2(s_ref[c] - m_ref[c])
            s_ref[c] = p
            norms.append(jnp.sum(p, axis=-1, keepdims=True))
        combine(s_ref[0], norms[0], s_ref[1], norms[1], n_keys)

    @pl.when(t < ctx_tiles)
    def _():
        ps = [jnp.exp2(s - jnp.max(s, axis=-1, keepdims=True))
              for s in scores(q_ref[...], ctx_keys)]
        combine(ps[0], jnp.sum(ps[0], axis=-1, keepdims=True),
                ps[1], jnp.sum(ps[1], axis=-1, keepdims=True), ctx_keys)

    slots = ((sa_ref, ma_ref), (sb_ref, mb_ref))

    @pl.when(t == ctx_tiles)
    def _():
        stash(q_ref[...], *slots[ctx_tiles % 2])

    for parity in range(2):
        @pl.when(jnp.logical_and(t >= ctx_tiles, t % 2 == parity))
        def _():
            stash(qn_ref[...], *slots[1 - parity])
            finish(*slots[parity])


def _flash(q, k, v, lam_vecs, subln_g, ctx_tiles, lambda_init):
    bsz, n_rows, _ = q.shape
    n_tiles = n_rows // ROW_TILE
    kv_spec = pl.BlockSpec((None, n_rows, LANES), lambda b, h, t: (b, 0, h))
    q_spec = pl.BlockSpec((None, ROW_TILE, LANES), lambda b, h, t: (b, t, h))
    q_next = pl.BlockSpec((None, ROW_TILE, LANES),
                          lambda b, h, t: (b, jnp.minimum(t + 1, n_tiles - 1), h))
    kern = functools.partial(_flash_kernel, ctx_tiles=ctx_tiles, lambda_init=lambda_init)
    score = pltpu.VMEM((2, ROW_TILE, n_rows), F32)
    rowmax = pltpu.VMEM((2, ROW_TILE, 1), F32)
    return pl.pallas_call(
        kern,
        grid=(bsz, DA_HEADS, n_tiles),
        in_specs=[pl.BlockSpec(lam_vecs.shape, lambda b, h, t: (0, 0)),
                  pl.BlockSpec((1, LANES), lambda b, h, t: (0, 0)),
                  q_spec, q_next, kv_spec, kv_spec],
        out_specs=q_spec,
        out_shape=jax.ShapeDtypeStruct((bsz, n_rows, D_MODEL), BF16),
        scratch_shapes=[score, score, rowmax, rowmax],
        compiler_params=pltpu.CompilerParams(
            dimension_semantics=("arbitrary", "arbitrary", "arbitrary"),
            vmem_limit_bytes=VMEM_LIMIT),
        name="diff_flash",
    )(lam_vecs, subln_g.reshape(1, LANES), q, q, k, v)


def _proj_residual_kernel(x_ref, z_ref, mod_ref, w_ref, o_ref):
    out = jnp.dot(z_ref[...], w_ref[...], preferred_element_type=F32)
    o_ref[...] = x_ref[...] + mod_ref[2:3, :] * out


def _proj_residual(x, z, mod, w, ctx_tiles, t0):
    bsz, n_rows, _ = x.shape
    n_tiles = n_rows // ROW_TILE - t0
    return pl.pallas_call(
        _proj_residual_kernel,
        grid=(bsz, n_tiles),
        in_specs=[_tile_spec(t0), _tile_spec(t0), _mod_spec(ctx_tiles, t0),
                  _const_spec((D_MODEL, D_MODEL))],
        out_specs=_tile_spec(),
        out_shape=jax.ShapeDtypeStruct((bsz, n_tiles * ROW_TILE, D_MODEL), F32),
        compiler_params=_params(),
        name="proj_residual",
    )(x, z, mod, w)


def _ffn_kernel(x_ref, xp_ref, xn_ref, mod_ref, g_ref, wup_ref, cw_ref, cb_ref, wdn_ref,
                fin_ref, o_ref, *, n_tiles, ctx_tiles, d_ff, final):
    t = pl.program_id(1)
    prev_ok, next_ok = _edge_flags(t, n_tiles, ctx_tiles)
    mod = mod_ref[...]
    g = g_ref[...]
    x = x_ref[...]
    h = _norm_mod(x, g, mod[3:4], mod[4:5]).astype(BF16)
    hp = _norm_mod(xp_ref[...], g, mod[3:4], mod[4:5]).astype(BF16)
    hn = _norm_mod(xn_ref[...], g, mod[3:4], mod[4:5]).astype(BF16)
    hcat = jnp.concatenate([h, hp, hn], axis=0)
    tiles = list(range(0, d_ff, FF_TILE))

    def up(f):
        gate = jnp.dot(hcat, wup_ref[:, f:f + FF_TILE], preferred_element_type=F32)
        val = jnp.dot(h, wup_ref[:, d_ff + f:d_ff + f + FF_TILE], preferred_element_type=F32)
        return gate, val

    acc = jnp.zeros((ROW_TILE, D_MODEL), F32)
    nxt = up(tiles[0])
    for i, f in enumerate(tiles):
        gate_all, val = nxt
        if i + 1 < len(tiles):
            nxt = up(tiles[i + 1])
        gate = gate_all[:ROW_TILE]
        g_prev, g_next = _shifted(gate, gate_all[ROW_TILE:ROW_TILE + SUBLANES],
                                  gate_all[ROW_TILE + SUBLANES:], prev_ok, next_ok)
        cw = cw_ref[:, f:f + FF_TILE]
        conv = g_prev * cw[0:1] + gate * cw[1:2] + g_next * cw[2:3] + cb_ref[:, f:f + FF_TILE]
        act = conv * _sigmoid(conv) * val
        acc = acc + jnp.dot(act.astype(BF16), wdn_ref[f:f + FF_TILE, :],
                            preferred_element_type=F32)
    y = x + mod[5:6] * acc
    if final:
        y = _rms(y) * fin_ref[...]
    o_ref[...] = y


def _ffn(x, mod, g, w_up, conv_w, conv_b, w_down, final_g, ctx_tiles, final):
    bsz, n_rows, _ = x.shape
    n_tiles = n_rows // ROW_TILE
    d_ff = w_down.shape[0]
    prev, nxt = _halo_specs(n_rows)
    kern = functools.partial(_ffn_kernel, n_tiles=n_tiles, ctx_tiles=ctx_tiles, d_ff=d_ff,
                             final=final)
    return pl.pallas_call(
        kern,
        grid=(bsz, n_tiles),
        in_specs=[_tile_spec(), prev, nxt, _mod_spec(ctx_tiles), _const_spec((1, D_MODEL)),
                  _const_spec((D_MODEL, 2 * d_ff)), _const_spec((3, d_ff)),
                  _const_spec((1, d_ff)), _const_spec((d_ff, D_MODEL)),
                  _const_spec((1, D_MODEL))],
        out_specs=_tile_spec(),
        out_shape=jax.ShapeDtypeStruct((bsz, n_rows, D_MODEL), F32),
        compiler_params=_params(),
        name="conv_glu",
    )(x, x, x, mod, g, w_up, conv_w, conv_b, w_down, final_g)


def _rwkv_proj_kernel(*refs, n_tiles, ctx_tiles, has_vfirst):
    (x_ref, xp_ref, xn_ref, mod_ref, g_ref, mp_ref, mn_ref, wr_ref, wk_ref, wv_ref,
     w0_ref, w1_ref, w2_ref, a0_ref, a1_ref, a2_ref, kk_ref, ka_ref, rk_ref,
     g1_ref, g2_ref) = refs[:21]
    refs = refs[21:]
    if has_vfirst:
        vf_ref, v0_ref, v1_ref, v2_ref = refs[:4]
        refs = refs[4:]
    r_ref, k_ref, v_ref, kkn_ref, bonus_ref, gate_ref, lw_ref, a_ref = refs

    t = pl.program_id(1)
    prev_ok, next_ok = _edge_flags(t, n_tiles, ctx_tiles)
    mod = mod_ref[...]
    g = g_ref[...]
    h = _norm_mod(x_ref[...], g, mod[0:1], mod[1:2])
    hp = _norm_mod(xp_ref[...], g, mod[0:1], mod[1:2])
    hn = _norm_mod(xn_ref[...], g, mod[0:1], mod[1:2])
    h_prev, h_next = _shifted(h, hp, hn, prev_ok, next_ok)
    hb = h.astype(BF16)
    xx_p = (h_prev - h).astype(BF16)
    xx_n = (h_next - h).astype(BF16)
    mp = mp_ref[...].astype(BF16)
    mn = mn_ref[...].astype(BF16)

    def mix(m):
        return hb + xx_p * mp[m:m + 1] + xx_n * mn[m:m + 1]

    lo = _lane_iota((ROW_TILE, LANES)) < RW_HEAD

    def halves(z):
        return jnp.where(lo, z, 0.0), jnp.where(lo, 0.0, z)

    r = jnp.dot(mix(0), wr_ref[...], preferred_element_type=F32)
    r_ref[...] = r

    lw = halves(jnp.tanh(jnp.dot(mix(1), w1_ref[...], preferred_element_type=F32)))
    la = halves(jnp.dot(mix(4), a1_ref[...], preferred_element_type=F32))
    a_gate = []
    for d in range(2):
        wl = w0_ref[d:d + 1, :] + _dot(lw[d], w2_ref[...])
        lw_ref[d] = _sigmoid(wl) * (-math.exp(-0.5))
        a_d = _sigmoid(a0_ref[d:d + 1, :] + _dot(la[d], a2_ref[...]))
        a_ref[d] = a_d
        a_gate.append(a_d)

    k = jnp.dot(mix(2), wk_ref[...], preferred_element_type=F32)
    k_ref[...] = k
    kk = k * kk_ref[...]
    kkn_ref[...] = kk / jnp.maximum(jnp.sqrt(_group_sum(kk * kk)), 1e-12)

    xv = mix(3)
    v = jnp.dot(xv, wv_ref[...], preferred_element_type=F32)
    if has_vfirst:
        lv = jnp.dot(xv, v1_ref[...], preferred_element_type=F32)
        v = v + (vf_ref[...] - v) * _sigmoid(v0_ref[...] + _dot(lv, v2_ref[...]))
    v_ref[...] = v

    ka = ka_ref[...]
    k_sum = k * (1.0 + (a_gate[0] - 1.0) * ka) + k * (1.0 + (a_gate[1] - 1.0) * ka)
    bonus_ref[...] = _group_sum(r * k_sum * rk_ref[...]) * v

    gl = _sigmoid(jnp.dot(mix(5), g1_ref[...], preferred_element_type=F32))
    gate_ref[...] = _dot(gl, g2_ref[...])


def _rwkv_proj(x, mod, g, p, v_first, ctx_tiles):
    bsz, n_rows, _ = x.shape
    n_tiles = n_rows // ROW_TILE
    prev, nxt = _halo_specs(n_rows)
    has_vfirst = v_first is not None
    args = [x, x, x, mod, g, p['mix_prev'], p['mix_next'], p['w_r'], p['w_k'], p['w_v'],
            p['w0'], p['w1'], p['w2'], p['a0'], p['a1'], p['a2'], p['k_k'], p['k_a'], p['r_k'],
            p['g1'], p['g2']]
    specs = [_tile_spec(), prev, nxt, _mod_spec(ctx_tiles)] + [
        _const_spec(a.shape) for a in args[4:]]
    if has_vfirst:
        args += [v_first, p['v0'], p['v1'], p['v2']]
        specs += [_tile_spec()] + [_const_spec(a.shape) for a in args[-3:]]
    one = jax.ShapeDtypeStruct((bsz, n_rows, D_MODEL), F32)
    two = jax.ShapeDtypeStruct((2, bsz, n_rows, D_MODEL), F32)
    two_spec = pl.BlockSpec((2, None, ROW_TILE, D_MODEL), lambda b, t: (0, b, t, 0))
    kern = functools.partial(_rwkv_proj_kernel, n_tiles=n_tiles, ctx_tiles=ctx_tiles,
                             has_vfirst=has_vfirst)
    return pl.pallas_call(
        kern,
        grid=(bsz, n_tiles),
        in_specs=specs,
        out_specs=[_tile_spec()] * 6 + [two_spec, two_spec],
        out_shape=[one] * 6 + [two, two],
        compiler_params=_params(),
        name="rwkv_proj",
    )(*args)


def _mask_stack(x, n_blocks, width):
    blk = _lane_iota(x.shape) // width
    zero = jnp.zeros_like(x)
    return jnp.concatenate([jnp.where(blk == i, x, zero) for i in range(n_blocks)], axis=0)


def _wkv_prep_kernel(r_ref, k_ref, v_ref, kk_ref, lw_ref, a_ref, ka_ref,
                     rh_ref, yl_ref, mt_ref, gt_ref):
    n = WKV_CHUNK
    hpg = LANES // n
    gw = hpg * RW_HEAD
    groups = D_MODEL // gw
    ri = lax.broadcasted_iota(jnp.int32, (n, n), 0)
    ci = lax.broadcasted_iota(jnp.int32, (n, n), 1)
    row = _row_iota((n, LANES))
    src = _lane_iota((n, LANES)) % n
    eye = (src == row).astype(F32)
    order = ((ci <= ri).astype(F32), src < row, src <= row, n - 1), \
            ((ci >= ri).astype(F32), src > row, src >= row, 0)
    lo_st = _lane_iota((RW_HEAD, LANES)) < RW_HEAD
    diag = (_lane_iota((RW_HEAD, LANES)) % RW_HEAD) == _row_iota((RW_HEAD, LANES))
    ka = ka_ref[...]
    bf = lambda z: z.astype(BF16)

    def body(it, carry):
        items = []
        for cc in range(PREP_CHUNKS):
            c = it * PREP_CHUNKS + cc
            rows = pl.ds(pl.multiple_of(c * n, n), n)
            kk = kk_ref[rows, :]
            k = k_ref[rows, :]
            r = r_ref[rows, :]
            v = bf(v_ref[rows, :])
            for d in range(2):
                tri, strict, incl, last = order[d]
                lw = lw_ref[d, rows, :]
                gate = a_ref[d, rows, :]
                kd = k * (1.0 + (gate - 1.0) * ka)
                b = kk * gate
                cum = jnp.dot(tri, lw, precision=lax.Precision.HIGHEST,
                              preferred_element_type=F32)
                cum_end = cum[last:last + 1, :]
                e_neg = jnp.exp(-cum)
                e_rem = jnp.exp(cum_end - cum)
                rt = r * jnp.exp(cum)
                full = dict(c=c, d=d, rows=rows, rt=rt, w_end=jnp.exp(cum_end), v=v,
                            at=bf(-kk * jnp.exp(cum - lw)), rtb=bf(rt), kt=bf(kd * e_neg),
                            bt=bf(b * e_neg), kh=bf(kd * e_rem), bh=bf(b * e_rem),
                            strict=strict, incl=incl)
                for g in range(groups):
                    items.append(dict(full=full, sl=slice(g * gw, (g + 1) * gw)))

        for it_ in items:
            f, sl = it_['full'], it_['sl']
            lhs = jnp.concatenate([f['at'][:, sl], f['rtb'][:, sl]], axis=0)
            rhs = jnp.concatenate([_mask_stack(f['kt'][:, sl], hpg, RW_HEAD),
                                   _mask_stack(f['bt'][:, sl], hpg, RW_HEAD)], axis=0)
            a = lax.dot_general(lhs, rhs, (((1,), (1,)), ((), ())), preferred_element_type=F32)
            it_['aak'] = jnp.where(f['strict'], a[:n, :LANES], 0.0)
            it_['ark'] = jnp.where(f['incl'], a[n:, :LANES], 0.0)
            it_['arb'] = bf(jnp.where(f['incl'], a[n:, LANES:], 0.0))
            pw = jnp.where(f['strict'], a[:n, LANES:], 0.0)
            it_['inv'] = eye + pw
            it_['pw'] = bf(pw)

        for it_ in items:
            it_['pw'] = bf(jnp.dot(it_['pw'], _mask_stack(it_['pw'], hpg, n),
                                   preferred_element_type=F32))
        steps = 2
        while steps < n:
            final = 2 * steps >= n
            for it_ in items:
                rhs = _mask_stack(bf(it_['inv']), hpg, n)
                if not final:
                    rhs = jnp.concatenate([_mask_stack(it_['pw'], hpg, n), rhs], axis=1)
                x = jnp.dot(it_['pw'], rhs, preferred_element_type=F32)
                it_['inv'] = it_['inv'] + x[:, -LANES:]
                if not final:
                    it_['pw'] = bf(x[:, :LANES])
            steps *= 2

        for it_ in items:
            f, sl = it_['full'], it_['sl']
            vms = _mask_stack(f['v'][:, sl], hpg, RW_HEAD)
            x = jnp.dot(bf(jnp.concatenate([it_['aak'], it_['ark']], axis=0)), vms,
                        preferred_element_type=F32)
            it_['arkv'] = x[n:]
            it_['rhs'] = jnp.concatenate([_mask_stack(f['at'][:, sl], hpg, RW_HEAD),
                                          _mask_stack(bf(x[:n]), hpg, RW_HEAD)], axis=1)
        for it_ in items:
            au = bf(jnp.dot(bf(it_['inv']), it_['rhs'], preferred_element_type=F32))
            it_['ah'] = au[:, :gw]
            it_['ul'] = au[:, gw:]
            it_['rhs'] = jnp.concatenate([_mask_stack(it_['ah'], hpg, RW_HEAD),
                                          _mask_stack(it_['ul'], hpg, RW_HEAD)], axis=1)
        for it_ in items:
            f, sl = it_['full'], it_['sl']
            x = jnp.dot(it_['arb'], it_['rhs'], preferred_element_type=F32)
            rh_ref[f['d'], f['rows'], sl] = bf(f['rt'][:, sl] + x[:, :gw])
            yl_ref[f['d'], f['rows'], sl] = it_['arkv'] + x[:, gw:]
        for it_ in items:
            f, sl = it_['full'], it_['sl']
            for j in range(0, gw, LANES):
                loc = slice(j, j + LANES)
                hs = slice(sl.start + j, sl.start + j + LANES)
                bh = f['bh'][:, hs]
                none = jnp.zeros((n, LANES), BF16)
                x = lax.dot_general(
                    jnp.concatenate([f['kh'][:, hs], bh], axis=0),
                    jnp.concatenate(
                        [jnp.concatenate([f['v'][:, hs], it_['ul'][:, loc]], axis=0),
                         jnp.concatenate([none, it_['ah'][:, loc]], axis=0)], axis=1),
                    (((0,), (0,)), ((), ())), preferred_element_type=F32)
                gf = x[:, :LANES]
                pm = x[:, LANES:]
                mt = jnp.where(lo_st, pm[:RW_HEAD], pm[RW_HEAD:])
                mt_ref[f['d'], f['c'], :, hs] = bf(mt + jnp.where(diag, f['w_end'][:, hs], 0.0))
                gt_ref[f['d'], f['c'], :, hs] = jnp.where(lo_st, gf[:RW_HEAD], gf[RW_HEAD:])
        return carry

    lax.fori_loop(0, ROW_TILE // (n * PREP_CHUNKS), body, 0)


def _wkv_prep(r, k, v, kk, lw, a, k_a):
    bsz, n_rows, _ = r.shape
    n_tiles = n_rows // ROW_TILE
    per = ROW_TILE // WKV_CHUNK
    dir_spec = pl.BlockSpec((2, None, ROW_TILE, D_MODEL), lambda b, t: (0, b, t, 0))
    st_spec = pl.BlockSpec((2, None, per, RW_HEAD, D_MODEL), lambda b, t: (0, b, t, 0, 0))
    rows = (2, bsz, n_rows, D_MODEL)
    st = (2, bsz, n_rows // WKV_CHUNK, RW_HEAD, D_MODEL)
    return pl.pallas_call(
        _wkv_prep_kernel,
        grid=(bsz, n_tiles),
        in_specs=[_tile_spec()] * 4 + [dir_spec, dir_spec, _const_spec((1, D_MODEL))],
        out_specs=[dir_spec, dir_spec, st_spec, st_spec],
        out_shape=[jax.ShapeDtypeStruct(rows, BF16), jax.ShapeDtypeStruct(rows, F32),
                   jax.ShapeDtypeStruct(st, BF16), jax.ShapeDtypeStruct(st, F32)],
        compiler_params=_params(),
        name="wkv_prep",
    )(r, k, v, kk, lw, a, k_a)


def _wkv_scan_kernel(rhf_ref, ylf_ref, mtf_ref, gtf_ref, rhb_ref, ylb_ref, mtb_ref, gtb_ref,
                     yf_ref, yb_ref, state_ref):
    n = WKV_CHUNK
    per = ROW_TILE // n

    @pl.when(pl.program_id(1) == 0)
    def _():
        state_ref[...] = jnp.zeros(state_ref.shape, F32)

    lo = _lane_iota((RW_HEAD, LANES)) < RW_HEAD
    dirs = ((rhf_ref, ylf_ref, mtf_ref, gtf_ref, yf_ref), (rhb_ref, ylb_ref, mtb_ref, gtb_ref, yb_ref))
    for i in range(per):
        for d, (rh_ref, yl_ref, mt_ref, gt_ref, y_ref) in enumerate(dirs):
            c = per - 1 - i if d == 1 else i
            for hp in range(D_MODEL // LANES):
                col = slice(hp * LANES, (hp + 1) * LANES)
                lhs = jnp.concatenate([rh_ref[c * n:(c + 1) * n, col], mt_ref[c, :, col]],
                                      axis=0)
                out = jnp.dot(lhs, state_ref[d, hp].astype(BF16), preferred_element_type=F32)
                y_ref[c * n:(c + 1) * n, col] = out[:n] + yl_ref[c * n:(c + 1) * n, col]
                st = out[n:] + gt_ref[c, :, col]
                state_ref[d, hp] = jnp.concatenate(
                    [jnp.where(lo, st, 0.0), jnp.where(lo, 0.0, st)], axis=0)


def _wkv_scan(rh, yl, mt, gt, ctx_tiles):
    _, bsz, n_rows, _ = rh.shape
    n_tiles = n_rows // ROW_TILE
    per = ROW_TILE // WKV_CHUNK

    def back(t):
        return jnp.where(t < ctx_tiles, ctx_tiles - 1 - t, n_tiles - 1 - (t - ctx_tiles))

    def specs(d, tile):
        row = pl.BlockSpec((None, None, ROW_TILE, D_MODEL), lambda b, t: (d, b, tile(t), 0))
        st = pl.BlockSpec((None, None, per, RW_HEAD, D_MODEL),
                          lambda b, t: (d, b, tile(t), 0, 0))
        return [row, row, st, st]

    fwd = lambda t: t
    y_f = pl.BlockSpec((None, ROW_TILE, D_MODEL), lambda b, t: (b, t, 0))
    y_b = pl.BlockSpec((None, ROW_TILE, D_MODEL), lambda b, t: (b, back(t), 0))
    out = jax.ShapeDtypeStruct((bsz, n_rows, D_MODEL), F32)
    return pl.pallas_call(
        _wkv_scan_kernel,
        grid=(bsz, n_tiles),
        in_specs=specs(0, fwd) + specs(1, back),
        out_specs=[y_f, y_b],
        out_shape=[out, out],
        scratch_shapes=[pltpu.VMEM((2, D_MODEL // LANES, LANES, LANES), F32)],
        compiler_params=pltpu.CompilerParams(dimension_semantics=("arbitrary", "arbitrary"),
                                             vmem_limit_bytes=VMEM_LIMIT),
        name="wkv_scan",
    )(rh, yl, mt, gt, rh, yl, mt, gt)


def _rwkv_out_kernel(x_ref, yf_ref, yb_ref, bonus_ref, gate_ref, mod_ref, lg_ref, lb_ref,
                     w_ref, o_ref):
    y = yf_ref[...] + yb_ref[...]
    mu = _group_sum(y) * (1.0 / RW_HEAD)
    yc = y - mu
    var = _group_sum(yc * yc) * (1.0 / RW_HEAD)
    yn = yc * lax.rsqrt(var + LNX_EPS) * lg_ref[...] + lb_ref[...]
    z = ((yn + bonus_ref[...]) * gate_ref[...]).astype(BF16)
    out = jnp.dot(z, w_ref[...], preferred_element_type=F32)
    o_ref[...] = x_ref[...] + mod_ref[2:3, :] * out


def _rwkv_out(x, yf, yb, bonus, gate, mod, lnx_g, lnx_b, w_o, ctx_tiles, t0):
    bsz, n_rows, _ = x.shape
    n_tiles = n_rows // ROW_TILE - t0
    return pl.pallas_call(
        _rwkv_out_kernel,
        grid=(bsz, n_tiles),
        in_specs=[_tile_spec(t0)] * 5 + [_mod_spec(ctx_tiles, t0), _const_spec((1, D_MODEL)),
                                          _const_spec((1, D_MODEL)),
                                          _const_spec((D_MODEL, D_MODEL))],
        out_specs=_tile_spec(),
        out_shape=jax.ShapeDtypeStruct((bsz, n_tiles * ROW_TILE, D_MODEL), F32),
        compiler_params=_params(),
        name="rwkv_out",
    )(x, yf, yb, bonus, gate, mod, lnx_g, lnx_b, w_o)


def _rope_tables(n_ctx, n_lat):
    t = jnp.arange(n_lat)
    row_pos = (t // GRID_W).astype(F32)
    col_pos = (t % GRID_W).astype(F32)
    n_freq = DA_HEAD_DIM // 4
    inv_freq = ROPE_THETA ** (-jnp.arange(n_freq, dtype=F32) / n_freq)
    ang = jnp.concatenate([row_pos[:, None] * inv_freq, col_pos[:, None] * inv_freq], axis=-1)
    cos, sin = jnp.cos(ang), jnp.sin(ang)
    reps = LANES // DA_HEAD_DIM
    cos = jnp.tile(jnp.concatenate([cos, cos], axis=-1), (1, reps))
    sin = jnp.tile(jnp.concatenate([-sin, sin], axis=-1), (1, reps))
    cos = jnp.concatenate([jnp.ones((n_ctx, LANES), F32), cos], axis=0)
    sin = jnp.concatenate([jnp.zeros((n_ctx, LANES), F32), sin], axis=0)
    return cos, sin


def kernel(x, c, ctx, c_ctx, ada_w, ada_b, norm_mix_g, norm_ffn_g, ffn_w_up, ffn_conv_w, ffn_conv_b, ffn_w_down, da_w_qkv, da_lambda, da_subln_g, da_w_o, rw_mix_prev, rw_mix_next, rw_w_r, rw_w_k, rw_w_v, rw_w0, rw_w1, rw_w2, rw_a0, rw_a1, rw_a2, rw_v0, rw_v1, rw_v2, rw_k_k, rw_k_a, rw_r_k, rw_g1, rw_g2, rw_lnx_g, rw_lnx_b, rw_w_o, final_norm_g):
    bsz, n_lat, d = x.shape
    n_ctx = ctx.shape[1]
    depth = ada_w.shape[0]
    assert d == D_MODEL and n_lat % ROW_TILE == 0 and n_ctx % ROW_TILE == 0
    assert n_lat % GRID_W == 0
    ctx_tiles = n_ctx // ROW_TILE
    row = lambda a: a.reshape(1, -1)
    cat = lambda a: jnp.concatenate([a[0], a[1]], axis=0 if a.shape[1] == RW_HEAD else 1)

    pad = (-(bsz + 1)) % SUBLANES
    cc = jnp.concatenate([c, c_ctx[None, :], jnp.zeros((pad, d), F32)], axis=0)
    table = _ada_table(cc, ada_w, ada_b)
    mod_l = table[:, :bsz].reshape(depth, bsz, 6, d)
    mod_c = jnp.broadcast_to(table[:, bsz].reshape(depth, 1, 6, d), (depth, bsz, 6, d))
    mods = jnp.stack([mod_c, mod_l], axis=2)
    mods = jnp.pad(mods, ((0, 0), (0, 0), (0, 0), (0, SUBLANES - 6), (0, 0)))

    cos, sin = _rope_tables(n_ctx, n_lat)
    stream = jnp.concatenate([ctx, x], axis=1)
    v_first = None
    no_final = jnp.ones((1, d), F32)

    for i in range(depth):
        last = i == depth - 1
        j = i // 2
        t0 = ctx_tiles if last else 0
        mod = mods[i]
        g_mix = row(norm_mix_g[i])
        if i % 2 == 0:
            lambda_init = 0.8 - 0.6 * math.exp(-0.3 * i)
            q, k, v = _qkv(stream, mod, g_mix, da_w_qkv[j].astype(BF16), cos, sin, ctx_tiles)
            o = _flash(q, k, v, da_lambda[j], da_subln_g[j], ctx_tiles, lambda_init)
            stream = _proj_residual(stream, o, mod, da_w_o[j].astype(BF16), ctx_tiles, t0)
        else:
            p = dict(mix_prev=rw_mix_prev[j], mix_next=rw_mix_next[j],
                     w_r=rw_w_r[j].astype(BF16), w_k=rw_w_k[j].astype(BF16),
                     w_v=rw_w_v[j].astype(BF16), w0=rw_w0[j], w1=cat(rw_w1[j]).astype(BF16),
                     w2=cat(rw_w2[j]).astype(BF16), a0=rw_a0[j], a1=cat(rw_a1[j]).astype(BF16),
                     a2=cat(rw_a2[j]).astype(BF16), k_k=row(rw_k_k[j]), k_a=row(rw_k_a[j]),
                     r_k=row(rw_r_k[j]), g1=rw_g1[j].astype(BF16), g2=rw_g2[j].astype(BF16))
            if j > 0:
                n_v = rw_v1.shape[-1]
                p.update(v0=row(rw_v0[j - 1]),
                         v1=jnp.pad(rw_v1[j - 1], ((0, 0), (0, LANES - n_v))).astype(BF16),
                         v2=jnp.pad(rw_v2[j - 1], ((0, LANES - n_v), (0, 0))).astype(BF16))
            r, k, v, kk, bonus, gate, lw, a = _rwkv_proj(stream, mod, g_mix, p, v_first,
                                                         ctx_tiles)
            if v_first is None:
                v_first = v
            rh, yl, mt, gt = _wkv_prep(r, k, v, kk, lw, a, p['k_a'])
            ys = _wkv_scan(rh, yl, mt, gt, ctx_tiles)
            stream = _rwkv_out(stream, ys[0], ys[1], bonus, gate, mod, row(rw_lnx_g[j]),
                               row(rw_lnx_b[j]), rw_w_o[j].astype(BF16), ctx_tiles, t0)
        stream = _ffn(stream, mod, row(norm_ffn_g[i]), ffn_w_up[i].astype(BF16),
                      ffn_conv_w[i], row(ffn_conv_b[i]), ffn_w_down[i].astype(BF16),
                      row(final_norm_g) if last else no_final, 0 if last else ctx_tiles, last)
    return stream
```

```python
import functools
import math

import jax
import jax.numpy as jnp
from jax import lax
from jax.experimental import pallas as pl
from jax.experimental.pallas import tpu as pltpu

D_MODEL = 1024
GRID_W = 64
DA_HEAD_DIM = 64
DA_HEADS = D_MODEL // (2 * DA_HEAD_DIM)
ROPE_THETA = 10000.0
RW_HEAD = 64
N_SHIFT_MIX = 6
LNX_EPS = 64e-5
EPS = 1e-6

LANES = 128
SUBLANES = 8
ROW_TILE = 256
WKV_CHUNK = 32
FF_TILE = 256
PREP_CHUNKS = 4
VMEM_LIMIT = 56 * 1024 * 1024

F32 = jnp.float32
BF16 = jnp.bfloat16

Q_SCALE = DA_HEAD_DIM ** -0.5 * math.log2(math.e)


def _dot(a, b):
    return jnp.dot(a.astype(BF16), b.astype(BF16), preferred_element_type=F32)


def _dot_nt(a, b):
    return lax.dot_general(a.astype(BF16), b.astype(BF16), (((1,), (1,)), ((), ())),
                           preferred_element_type=F32)


def _dot_tn(a, b):
    return lax.dot_general(a.astype(BF16), b.astype(BF16), (((0,), (0,)), ((), ())),
                           preferred_element_type=F32)


def _rms(x):
    return x * lax.rsqrt(jnp.mean(x * x, axis=-1, keepdims=True) + EPS)


def _norm_mod(x, g, shift, scale):
    return _rms(x) * g * (1.0 + scale) + shift


def _sigmoid(x):
    return 1.0 / (1.0 + jnp.exp(-x))


def _lane_iota(shape):
    return lax.broadcasted_iota(jnp.int32, shape, len(shape) - 1)


def _row_iota(shape):
    return lax.broadcasted_iota(jnp.int32, shape, len(shape) - 2)


def _group_ones():
    r = lax.broadcasted_iota(jnp.int32, (LANES, LANES), 0) // RW_HEAD
    c = lax.broadcasted_iota(jnp.int32, (LANES, LANES), 1) // RW_HEAD
    return (r == c).astype(BF16)


def _group_sum(z):
    ones = _group_ones()
    parts = [jnp.dot(z[:, j:j + LANES].astype(BF16), ones, preferred_element_type=F32)
             for j in range(0, z.shape[1], LANES)]
    return jnp.concatenate(parts, axis=1)


def _shifted(h, halo_prev, halo_next, prev_ok, next_ok):
    rows = h.shape[0]
    ri = _row_iota(h.shape)
    first = jnp.where(prev_ok, halo_prev[SUBLANES - 1:SUBLANES, :], 0.0)
    last = jnp.where(next_ok, halo_next[0:1, :], 0.0)
    h_prev = jnp.where(ri == 0, first, pltpu.roll(h, 1, 0))
    h_next = jnp.where(ri == rows - 1, last, pltpu.roll(h, rows - 1, 0))
    return h_prev, h_next


def _edge_flags(t, n_tiles, ctx_tiles):
    prev_ok = jnp.logical_and(t != 0, t != ctx_tiles)
    next_ok = jnp.logical_and(t != n_tiles - 1, t != ctx_tiles - 1)
    return prev_ok, next_ok


def _const_spec(shape):
    zeros = (0,) * len(shape)
    return pl.BlockSpec(shape, lambda *_: zeros, pipeline_mode=pl.Buffered(1))


def _tile_spec(t0=0):
    return pl.BlockSpec((None, ROW_TILE, D_MODEL), lambda b, t: (b, t + t0, 0))


def _halo_specs(n_rows, t0=0):
    per = ROW_TILE // SUBLANES
    last = n_rows // SUBLANES - 1
    prev = pl.BlockSpec((None, SUBLANES, D_MODEL),
                        lambda b, t: (b, jnp.maximum((t + t0) * per - 1, 0), 0))
    nxt = pl.BlockSpec((None, SUBLANES, D_MODEL),
                       lambda b, t: (b, jnp.minimum((t + t0 + 1) * per, last), 0))
    return prev, nxt


def _mod_spec(ctx_tiles, t0=0):
    return pl.BlockSpec((None, None, SUBLANES, D_MODEL),
                        lambda b, t: (b, ((t + t0) >= ctx_tiles).astype(jnp.int32), 0, 0))


def _params():
    return pltpu.CompilerParams(vmem_limit_bytes=VMEM_LIMIT)


def _ada_kernel(c_ref, w_ref, b_ref, o_ref):
    c = c_ref[...]
    s = c * _sigmoid(c)
    o_ref[...] = jnp.dot(s, w_ref[...], precision=lax.Precision.HIGHEST,
                         preferred_element_type=F32) + b_ref[...]


def _ada_table(cc, ada_w, ada_b):
    n_layers, _, six_d = ada_w.shape
    rows = cc.shape[0]
    nb = 1536
    return pl.pallas_call(
        _ada_kernel,
        grid=(n_layers, six_d // nb),
        in_specs=[pl.BlockSpec((rows, D_MODEL), lambda l, n: (0, 0)),
                  pl.BlockSpec((None, D_MODEL, nb), lambda l, n: (l, 0, n)),
                  pl.BlockSpec((None, 1, nb), lambda l, n: (l, 0, n))],
        out_specs=pl.BlockSpec((None, rows, nb), lambda l, n: (l, 0, n)),
        out_shape=jax.ShapeDtypeStruct((n_layers, rows, six_d), F32),
        compiler_params=_params(),
        name="ada_table",
    )(cc, ada_w, ada_b.reshape(n_layers, 1, six_d))


def _rope(slab, cos, sin_signed):
    lane = _lane_iota(slab.shape)
    first = (lane % DA_HEAD_DIM) < (DA_HEAD_DIM // 2)
    partner = jnp.where(first, pltpu.roll(slab, LANES - DA_HEAD_DIM // 2, 1),
                        pltpu.roll(slab, DA_HEAD_DIM // 2, 1))
    return slab * cos + partner * sin_signed


def _qkv_kernel(x_ref, mod_ref, g_ref, w_ref, cos_ref, sin_ref, q_ref, k_ref, v_ref):
    mod = mod_ref[...]
    h = _norm_mod(x_ref[...], g_ref[...], mod[0:1], mod[1:2]).astype(BF16)
    cos = cos_ref[...]
    sin = sin_ref[...]
    wide = 2 * LANES
    for j in range(0, D_MODEL, wide):
        qa = jnp.dot(h, w_ref[:, j:j + wide], preferred_element_type=F32)
        ka = jnp.dot(h, w_ref[:, D_MODEL + j:D_MODEL + j + wide], preferred_element_type=F32)
        for i in range(0, wide, LANES):
            q_ref[:, j + i:j + i + LANES] = (
                _rope(qa[:, i:i + LANES], cos, sin) * Q_SCALE).astype(BF16)
            k_ref[:, j + i:j + i + LANES] = _rope(ka[:, i:i + LANES], cos, sin).astype(BF16)
        v_ref[:, j:j + wide] = jnp.dot(
            h, w_ref[:, 2 * D_MODEL + j:2 * D_MODEL + j + wide],
            preferred_element_type=F32).astype(BF16)


def _qkv(x, mod, g, w_qkv, cos, sin, ctx_tiles):
    bsz, n_rows, _ = x.shape
    out = jax.ShapeDtypeStruct((bsz, n_rows, D_MODEL), BF16)
    tab = pl.BlockSpec((ROW_TILE, LANES), lambda b, t: (t, 0))
    return pl.pallas_call(
        _qkv_kernel,
        grid=(bsz, n_rows // ROW_TILE),
        in_specs=[_tile_spec(), _mod_spec(ctx_tiles), _const_spec((1, D_MODEL)),
                  _const_spec((D_MODEL, 3 * D_MODEL)), tab, tab],
        out_specs=[_tile_spec(), _tile_spec(), _tile_spec()],
        out_shape=[out, out, out],
        compiler_params=_params(),
        name="attn_qkv",
    )(x, mod, g, w_qkv, cos, sin)


def _flash_kernel(lam_ref, g_ref, q_ref, qn_ref, k_ref, v_ref, o_ref,
                  sa_ref, sb_ref, ma_ref, mb_ref, vx_ref, *, ctx_tiles, lambda_init):
    t = pl.program_id(2)
    n_keys = k_ref.shape[0]
    ctx_keys = ctx_tiles * ROW_TILE
    lv = lam_ref[...]
    lam = (jnp.exp(jnp.sum(lv[0:1] * lv[1:2], axis=-1, keepdims=True))
           - jnp.exp(jnp.sum(lv[2:3] * lv[3:4], axis=-1, keepdims=True)) + lambda_init)

    @pl.when(t == 0)
    def _():
        vx_ref[:, :LANES] = v_ref[...]
        vx_ref[:, LANES:] = (_lane_iota((n_keys, LANES)) == 0).astype(BF16)

    def scores(q, n):
        lo = _lane_iota(q.shape) < DA_HEAD_DIM
        zero = jnp.zeros_like(q)
        k = k_ref[0:n, :]
        return [lax.dot_general(qc, k, (((1,), (1,)), ((), ())), preferred_element_type=F32)
                for qc in (jnp.where(lo, q, zero), jnp.where(lo, zero, q))]

    def stash(q, s_ref, m_ref):
        for c, s in enumerate(scores(q, n_keys)):
            s_ref[c] = s
            m_ref[c] = jnp.max(s, axis=-1, keepdims=True)

    def attend(shifted, n):
        outs = []
        for x in shifted:
            p = jnp.exp2(x.astype(BF16))
            acc = jnp.dot(p, vx_ref[0:n, :], preferred_element_type=F32)
            outs.append(acc[:, :LANES] / acc[:, LANES:LANES + 1])
        o = outs[0] - lam * outs[1]
        o_ref[...] = (_rms(o) * g_ref[...] * (1.0 - lambda_init)).astype(BF16)

    @pl.when(t < ctx_tiles)
    def _():
        attend([s - jnp.max(s, axis=-1, keepdims=True) for s in scores(q_ref[...], ctx_keys)],
               ctx_keys)

    slots = ((sa_ref, ma_ref), (sb_ref, mb_ref))

    @pl.when(t == ctx_tiles)
    def _():
        stash(q_ref[...], *slots[ctx_tiles % 2])

    for parity in range(2):
        @pl.when(jnp.logical_and(t >= ctx_tiles, t % 2 == parity))
        def _():
            stash(qn_ref[...], *slots[1 - parity])
            s_ref, m_ref = slots[parity]
            attend([s_ref[c] - m_ref[c] for c in range(2)], n_keys)


def _flash(q, k, v, lam_vecs, subln_g, ctx_tiles, lambda_init):
    bsz, n_rows, _ = q.shape
    n_tiles = n_rows // ROW_TILE
    kv_spec = pl.BlockSpec((None, n_rows, LANES), lambda b, h, t: (b, 0, h))
    q_spec = pl.BlockSpec((None, ROW_TILE, LANES), lambda b, h, t: (b, t, h))
    q_next = pl.BlockSpec((None, ROW_TILE, LANES),
                          lambda b, h, t: (b, jnp.minimum(t + 1, n_tiles - 1), h))
    kern = functools.partial(_flash_kernel, ctx_tiles=ctx_tiles, lambda_init=lambda_init)
    score = pltpu.VMEM((2, ROW_TILE, n_rows), F32)
    rowmax = pltpu.VMEM((2, ROW_TILE, 1), F32)
    return pl.pallas_call(
        kern,
        grid=(bsz, DA_HEADS, n_tiles),
        in_specs=[pl.BlockSpec(lam_vecs.shape, lambda b, h, t: (0, 0)),
                  pl.BlockSpec((1, LANES), lambda b, h, t: (0, 0)),
                  q_spec, q_next, kv_spec, kv_spec],
        out_specs=q_spec,
        out_shape=jax.ShapeDtypeStruct((bsz, n_rows, D_MODEL), BF16),
        scratch_shapes=[score, score, rowmax, rowmax,
                        pltpu.VMEM((n_rows, 2 * LANES), BF16)],
        compiler_params=pltpu.CompilerParams(
            dimension_semantics=("arbitrary", "arbitrary", "arbitrary"),
            vmem_limit_bytes=VMEM_LIMIT),
        name="diff_flash",
    )(lam_vecs, subln_g.reshape(1, LANES), q, q, k, v)


def _proj_residual_kernel(x_ref, z_ref, mod_ref, w_ref, o_ref):
    out = jnp.dot(z_ref[...], w_ref[...], preferred_element_type=F32)
    o_ref[...] = x_ref[...] + mod_ref[2:3, :] * out


def _proj_residual(x, z, mod, w, ctx_tiles, t0):
    bsz, n_rows, _ = x.shape
    n_tiles = n_rows // ROW_TILE - t0
    return pl.pallas_call(
        _proj_residual_kernel,
        grid=(bsz, n_tiles),
        in_specs=[_tile_spec(t0), _tile_spec(t0), _mod_spec(ctx_tiles, t0),
                  _const_spec((D_MODEL, D_MODEL))],
        out_specs=_tile_spec(),
        out_shape=jax.ShapeDtypeStruct((bsz, n_tiles * ROW_TILE, D_MODEL), F32),
        compiler_params=_params(),
        name="proj_residual",
    )(x, z, mod, w)


def _ffn_kernel(x_ref, xp_ref, xn_ref, mod_ref, g_ref, wup_ref, cw_ref, cb_ref, wdn_ref,
                fin_ref, o_ref, *, n_tiles, ctx_tiles, d_ff, final):
    t = pl.program_id(1)
    prev_ok, next_ok = _edge_flags(t, n_tiles, ctx_tiles)
    mod = mod_ref[...]
    g = g_ref[...]
    x = x_ref[...]
    h = _norm_mod(x, g, mod[3:4], mod[4:5]).astype(BF16)
    hp = _norm_mod(xp_ref[...], g, mod[3:4], mod[4:5]).astype(BF16)
    hn = _norm_mod(xn_ref[...], g, mod[3:4], mod[4:5]).astype(BF16)
    hcat = jnp.concatenate([h, hp, hn], axis=0)
    tiles = list(range(0, d_ff, FF_TILE))

    def up(f):
        gate = jnp.dot(hcat, wup_ref[:, f:f + FF_TILE], preferred_element_type=F32)
        val = jnp.dot(h, wup_ref[:, d_ff + f:d_ff + f + FF_TILE], preferred_element_type=F32)
        return gate, val

    acc = jnp.zeros((ROW_TILE, D_MODEL), F32)
    nxt = up(tiles[0])
    for i, f in enumerate(tiles):
        gate_all, val = nxt
        if i + 1 < len(tiles):
            nxt = up(tiles[i + 1])
        gate = gate_all[:ROW_TILE]
        g_prev, g_next = _shifted(gate, gate_all[ROW_TILE:ROW_TILE + SUBLANES],
                                  gate_all[ROW_TILE + SUBLANES:], prev_ok, next_ok)
        cw = cw_ref[:, f:f + FF_TILE]
        conv = g_prev * cw[0:1] + gate * cw[1:2] + g_next * cw[2:3] + cb_ref[:, f:f + FF_TILE]
        act = conv * _sigmoid(conv) * val
        acc = acc + jnp.dot(act.astype(BF16), wdn_ref[f:f + FF_TILE, :],
                            preferred_element_type=F32)
    y = x + mod[5:6] * acc
    if final:
        y = _rms(y) * fin_ref[...]
    o_ref[...] = y


def _ffn(x, mod, g, w_up, conv_w, conv_b, w_down, final_g, ctx_tiles, final):
    bsz, n_rows, _ = x.shape
    n_tiles = n_rows // ROW_TILE
    d_ff = w_down.shape[0]
    prev, nxt = _halo_specs(n_rows)
    kern = functools.partial(_ffn_kernel, n_tiles=n_tiles, ctx_tiles=ctx_tiles, d_ff=d_ff,
                             final=final)
    return pl.pallas_call(
        kern,
        grid=(bsz, n_tiles),
        in_specs=[_tile_spec(), prev, nxt, _mod_spec(ctx_tiles), _const_spec((1, D_MODEL)),
                  _const_spec((D_MODEL, 2 * d_ff)), _const_spec((3, d_ff)),
                  _const_spec((1, d_ff)), _const_spec((d_ff, D_MODEL)),
                  _const_spec((1, D_MODEL))],
        out_specs=_tile_spec(),
        out_shape=jax.ShapeDtypeStruct((bsz, n_rows, D_MODEL), F32),
        compiler_params=_params(),
        name="conv_glu",
    )(x, x, x, mod, g, w_up, conv_w, conv_b, w_down, final_g)


def _rwkv_proj_kernel(*refs, n_tiles, ctx_tiles, has_vfirst):
    (x_ref, xp_ref, xn_ref, mod_ref, g_ref, mp_ref, mn_ref, wr_ref, wk_ref, wv_ref,
     w0_ref, w1_ref, w2_ref, a0_ref, a1_ref, a2_ref, kk_ref, ka_ref, rk_ref,
     g1_ref, g2_ref) = refs[:21]
    refs = refs[21:]
    if has_vfirst:
        vf_ref, v0_ref, v1_ref, v2_ref = refs[:4]
        refs = refs[4:]
    r_ref, k_ref, v_ref, kkn_ref, bonus_ref, gate_ref, lw_ref, a_ref = refs

    t = pl.program_id(1)
    prev_ok, next_ok = _edge_flags(t, n_tiles, ctx_tiles)
    mod = mod_ref[...]
    g = g_ref[...]
    h = _norm_mod(x_ref[...], g, mod[0:1], mod[1:2])
    hp = _norm_mod(xp_ref[...], g, mod[0:1], mod[1:2])
    hn = _norm_mod(xn_ref[...], g, mod[0:1], mod[1:2])
    h_prev, h_next = _shifted(h, hp, hn, prev_ok, next_ok)
    hb = h.astype(BF16)
    xx_p = (h_prev - h).astype(BF16)
    xx_n = (h_next - h).astype(BF16)
    mp = mp_ref[...].astype(BF16)
    mn = mn_ref[...].astype(BF16)

    def mix(m):
        return hb + xx_p * mp[m:m + 1] + xx_n * mn[m:m + 1]

    lo = _lane_iota((ROW_TILE, LANES)) < RW_HEAD

    def halves(z):
        return jnp.where(lo, z, 0.0), jnp.where(lo, 0.0, z)

    r = jnp.dot(mix(0), wr_ref[...], preferred_element_type=F32)
    r_ref[...] = r

    lw = halves(jnp.tanh(jnp.dot(mix(1), w1_ref[...], preferred_element_type=F32)))
    la = halves(jnp.dot(mix(4), a1_ref[...], preferred_element_type=F32))
    a_gate = []
    for d in range(2):
        wl = w0_ref[d:d + 1, :] + _dot(lw[d], w2_ref[...])
        lw_ref[d] = _sigmoid(wl) * (-math.exp(-0.5))
        a_d = _sigmoid(a0_ref[d:d + 1, :] + _dot(la[d], a2_ref[...]))
        a_ref[d] = a_d
        a_gate.append(a_d)

    k = jnp.dot(mix(2), wk_ref[...], preferred_element_type=F32)
    k_ref[...] = k
    kk = k * kk_ref[...]
    kkn_ref[...] = kk / jnp.maximum(jnp.sqrt(_group_sum(kk * kk)), 1e-12)

    xv = mix(3)
    v = jnp.dot(xv, wv_ref[...], preferred_element_type=F32)
    if has_vfirst:
        lv = jnp.dot(xv, v1_ref[...], preferred_element_type=F32)
        v = v + (vf_ref[...] - v) * _sigmoid(v0_ref[...] + _dot(lv, v2_ref[...]))
    v_ref[...] = v

    ka = ka_ref[...]
    k_sum = k * (1.0 + (a_gate[0] - 1.0) * ka) + k * (1.0 + (a_gate[1] - 1.0) * ka)
    bonus_ref[...] = _group_sum(r * k_sum * rk_ref[...]) * v

    gl = _sigmoid(jnp.dot(mix(5), g1_ref[...], preferred_element_type=F32))
    gate_ref[...] = _dot(gl, g2_ref[...])


def _rwkv_proj(x, mod, g, p, v_first, ctx_tiles):
    bsz, n_rows, _ = x.shape
    n_tiles = n_rows // ROW_TILE
    prev, nxt = _halo_specs(n_rows)
    has_vfirst = v_first is not None
    args = [x, x, x, mod, g, p['mix_prev'], p['mix_next'], p['w_r'], p['w_k'], p['w_v'],
            p['w0'], p['w1'], p['w2'], p['a0'], p['a1'], p['a2'], p['k_k'], p['k_a'], p['r_k'],
            p['g1'], p['g2']]
    specs = [_tile_spec(), prev, nxt, _mod_spec(ctx_tiles)] + [
        _const_spec(a.shape) for a in args[4:]]
    if has_vfirst:
        args += [v_first, p['v0'], p['v1'], p['v2']]
        specs += [_tile_spec()] + [_const_spec(a.shape) for a in args[-3:]]
    one = jax.ShapeDtypeStruct((bsz, n_rows, D_MODEL), F32)
    two = jax.ShapeDtypeStruct((2, bsz, n_rows, D_MODEL), F32)
    two_spec = pl.BlockSpec((2, None, ROW_TILE, D_MODEL), lambda b, t: (0, b, t, 0))
    kern = functools.partial(_rwkv_proj_kernel, n_tiles=n_tiles, ctx_tiles=ctx_tiles,
                             has_vfirst=has_vfirst)
    return pl.pallas_call(
        kern,
        grid=(bsz, n_tiles),
        in_specs=specs,
        out_specs=[_tile_spec()] * 6 + [two_spec, two_spec],
        out_shape=[one] * 6 + [two, two],
        compiler_params=_params(),
        name="rwkv_proj",
    )(*args)


def _mask_stack(x, n_blocks, width):
    blk = _lane_iota(x.shape) // width
    zero = jnp.zeros_like(x)
    return jnp.concatenate([jnp.where(blk == i, x, zero) for i in range(n_blocks)], axis=0)


def _wkv_prep_kernel(r_ref, k_ref, v_ref, kk_ref, lw_ref, a_ref, ka_ref,
                     rh_ref, yl_ref, mt_ref, gt_ref):
    n = WKV_CHUNK
    hpg = LANES // n
    gw = hpg * RW_HEAD
    groups = D_MODEL // gw
    ri = lax.broadcasted_iota(jnp.int32, (n, n), 0)
    ci = lax.broadcasted_iota(jnp.int32, (n, n), 1)
    row = _row_iota((n, LANES))
    src = _lane_iota((n, LANES)) % n
    eye = (src == row).astype(F32)
    order = ((ci <= ri).astype(F32), src < row, src <= row, n - 1), \
            ((ci >= ri).astype(F32), src > row, src >= row, 0)
    lo_st = _lane_iota((RW_HEAD, LANES)) < RW_HEAD
    diag = (_lane_iota((RW_HEAD, LANES)) % RW_HEAD) == _row_iota((RW_HEAD, LANES))
    ka = ka_ref[...]
    bf = lambda z: z.astype(BF16)

    def body(it, carry):
        items = []
        for cc in range(PREP_CHUNKS):
            c = it * PREP_CHUNKS + cc
            rows = pl.ds(pl.multiple_of(c * n, n), n)
            kk = kk_ref[rows, :]
            k = k_ref[rows, :]
            r = r_ref[rows, :]
            v = bf(v_ref[rows, :])
            for d in range(2):
                tri, strict, incl, last = order[d]
                lw = lw_ref[d, rows, :]
                gate = a_ref[d, rows, :]
                kd = k * (1.0 + (gate - 1.0) * ka)
                b = kk * gate
                cum = jnp.dot(tri, lw, precision=lax.Precision.HIGHEST,
                              preferred_element_type=F32)
                cum_end = cum[last:last + 1, :]
                e_neg = jnp.exp(-cum)
                e_rem = jnp.exp(cum_end - cum)
                rt = r * jnp.exp(cum)
                full = dict(c=c, d=d, rows=rows, rt=rt, w_end=jnp.exp(cum_end), v=v,
                            at=bf(-kk * jnp.exp(cum - lw)), rtb=bf(rt), kt=bf(kd * e_neg),
                            bt=bf(b * e_neg), kh=bf(kd * e_rem), bh=bf(b * e_rem),
                            strict=strict, incl=incl)
                for g in range(groups):
                    items.append(dict(full=full, sl=slice(g * gw, (g + 1) * gw)))

        for it_ in items:
            f, sl = it_['full'], it_['sl']
            lhs = jnp.concatenate([f['at'][:, sl], f['rtb'][:, sl]], axis=0)
            rhs = jnp.concatenate([_mask_stack(f['kt'][:, sl], hpg, RW_HEAD),
                                   _mask_stack(f['bt'][:, sl], hpg, RW_HEAD)], axis=0)
            a = lax.dot_general(lhs, rhs, (((1,), (1,)), ((), ())), preferred_element_type=F32)
            it_['aak'] = jnp.where(f['strict'], a[:n, :LANES], 0.0)
            it_['ark'] = jnp.where(f['incl'], a[n:, :LANES], 0.0)
            it_['arb'] = bf(jnp.where(f['incl'], a[n:, LANES:], 0.0))
            pw = jnp.where(f['strict'], a[:n, LANES:], 0.0)
            it_['inv'] = eye + pw
            it_['pw'] = bf(pw)

        for it_ in items:
            it_['pw'] = bf(jnp.dot(it_['pw'], _mask_stack(it_['pw'], hpg, n),
                                   preferred_element_type=F32))
        steps = 2
        while steps < n:
            final = 2 * steps >= n
            for it_ in items:
                rhs = _mask_stack(bf(it_['inv']), hpg, n)
                if not final:
                    rhs = jnp.concatenate([_mask_stack(it_['pw'], hpg, n), rhs], axis=1)
                x = jnp.dot(it_['pw'], rhs, preferred_element_type=F32)
                it_['inv'] = it_['inv'] + x[:, -LANES:]
                if not final:
                    it_['pw'] = bf(x[:, :LANES])
            steps *= 2

        for it_ in items:
            f, sl = it_['full'], it_['sl']
            vms = _mask_stack(f['v'][:, sl], hpg, RW_HEAD)
            x = jnp.dot(bf(jnp.concatenate([it_['aak'], it_['ark']], axis=0)), vms,
                        preferred_element_type=F32)
            it_['arkv'] = x[n:]
            it_['rhs'] = jnp.concatenate([_mask_stack(f['at'][:, sl], hpg, RW_HEAD),
                                          _mask_stack(bf(x[:n]), hpg, RW_HEAD)], axis=1)
        for it_ in items:
            au = bf(jnp.dot(bf(it_['inv']), it_['rhs'], preferred_element_type=F32))
            it_['ah'] = au[:, :gw]
            it_['ul'] = au[:, gw:]
            it_['rhs'] = jnp.concatenate([_mask_stack(it_['ah'], hpg, RW_HEAD),
                                          _mask_stack(it_['ul'], hpg, RW_HEAD)], axis=1)
        for it_ in items:
            f, sl = it_['full'], it_['sl']
            x = jnp.dot(it_['arb'], it_['rhs'], preferred_element_type=F32)
            rh_ref[f['d'], f['rows'], sl] = bf(f['rt'][:, sl] + x[:, :gw])
            yl_ref[f['d'], f['rows'], sl] = it_['arkv'] + x[:, gw:]
        for it_ in items:
            f, sl = it_['full'], it_['sl']
            for j in range(0, gw, LANES):
                loc = slice(j, j + LANES)
                hs = slice(sl.start + j, sl.start + j + LANES)
                bh = f['bh'][:, hs]
                none = jnp.zeros((n, LANES), BF16)
                x = lax.dot_general(
                    jnp.concatenate([f['kh'][:, hs], bh], axis=0),
                    jnp.concatenate(
                        [jnp.concatenate([f['v'][:, hs], it_['ul'][:, loc]], axis=0),
                         jnp.concatenate([none, it_['ah'][:, loc]], axis=0)], axis=1),
                    (((0,), (0,)), ((), ())), preferred_element_type=F32)
                gf = x[:, :LANES]
                pm = x[:, LANES:]
                mt = jnp.where(lo_st, pm[:RW_HEAD], pm[RW_HEAD:])
                mt_ref[f['d'], f['c'], :, hs] = bf(mt + jnp.where(diag, f['w_end'][:, hs], 0.0))
                gt_ref[f['d'], f['c'], :, hs] = jnp.where(lo_st, gf[:RW_HEAD], gf[RW_HEAD:])
        return carry

    lax.fori_loop(0, ROW_TILE // (n * PREP_CHUNKS), body, 0)


def _wkv_prep(r, k, v, kk, lw, a, k_a):
    bsz, n_rows, _ = r.shape
    n_tiles = n_rows // ROW_TILE
    per = ROW_TILE // WKV_CHUNK
    dir_spec = pl.BlockSpec((2, None, ROW_TILE, D_MODEL), lambda b, t: (0, b, t, 0))
    st_spec = pl.BlockSpec((2, None, per, RW_HEAD, D_MODEL), lambda b, t: (0, b, t, 0, 0))
    rows = (2, bsz, n_rows, D_MODEL)
    st = (2, bsz, n_rows // WKV_CHUNK, RW_HEAD, D_MODEL)
    return pl.pallas_call(
        _wkv_prep_kernel,
        grid=(bsz, n_tiles),
        in_specs=[_tile_spec()] * 4 + [dir_spec, dir_spec, _const_spec((1, D_MODEL))],
        out_specs=[dir_spec, dir_spec, st_spec, st_spec],
        out_shape=[jax.ShapeDtypeStruct(rows, BF16), jax.ShapeDtypeStruct(rows, F32),
                   jax.ShapeDtypeStruct(st, BF16), jax.ShapeDtypeStruct(st, F32)],
        compiler_params=_params(),
        name="wkv_prep",
    )(r, k, v, kk, lw, a, k_a)


def _wkv_scan_kernel(rhf_ref, ylf_ref, mtf_ref, gtf_ref, rhb_ref, ylb_ref, mtb_ref, gtb_ref,
                     yf_ref, yb_ref, state_ref):
    n = WKV_CHUNK
    per = ROW_TILE // n

    @pl.when(pl.program_id(1) == 0)
    def _():
        state_ref[...] = jnp.zeros(state_ref.shape, F32)

    lo = _lane_iota((RW_HEAD, LANES)) < RW_HEAD
    dirs = ((rhf_ref, ylf_ref, mtf_ref, gtf_ref, yf_ref), (rhb_ref, ylb_ref, mtb_ref, gtb_ref, yb_ref))
    for i in range(per):
        for d, (rh_ref, yl_ref, mt_ref, gt_ref, y_ref) in enumerate(dirs):
            c = per - 1 - i if d == 1 else i
            for hp in range(D_MODEL // LANES):
                col = slice(hp * LANES, (hp + 1) * LANES)
                lhs = jnp.concatenate([rh_ref[c * n:(c + 1) * n, col], mt_ref[c, :, col]],
                                      axis=0)
                out = jnp.dot(lhs, state_ref[d, hp].astype(BF16), preferred_element_type=F32)
                y_ref[c * n:(c + 1) * n, col] = out[:n] + yl_ref[c * n:(c + 1) * n, col]
                st = out[n:] + gt_ref[c, :, col]
                state_ref[d, hp] = jnp.concatenate(
                    [jnp.where(lo, st, 0.0), jnp.where(lo, 0.0, st)], axis=0)


def _wkv_scan(rh, yl, mt, gt, ctx_tiles):
    _, bsz, n_rows, _ = rh.shape
    n_tiles = n_rows // ROW_TILE
    per = ROW_TILE // WKV_CHUNK

    def back(t):
        return jnp.where(t < ctx_tiles, ctx_tiles - 1 - t, n_tiles - 1 - (t - ctx_tiles))

    def specs(d, tile):
        row = pl.BlockSpec((None, None, ROW_TILE, D_MODEL), lambda b, t: (d, b, tile(t), 0))
        st = pl.BlockSpec((None, None, per, RW_HEAD, D_MODEL),
                          lambda b, t: (d, b, tile(t), 0, 0))
        return [row, row, st, st]

    fwd = lambda t: t
    y_f = pl.BlockSpec((None, ROW_TILE, D_MODEL), lambda b, t: (b, t, 0))
    y_b = pl.BlockSpec((None, ROW_TILE, D_MODEL), lambda b, t: (b, back(t), 0))
    out = jax.ShapeDtypeStruct((bsz, n_rows, D_MODEL), F32)
    return pl.pallas_call(
        _wkv_scan_kernel,
        grid=(bsz, n_tiles),
        in_specs=specs(0, fwd) + specs(1, back),
        out_specs=[y_f, y_b],
        out_shape=[out, out],
        scratch_shapes=[pltpu.VMEM((2, D_MODEL // LANES, LANES, LANES), F32)],
        compiler_params=pltpu.CompilerParams(dimension_semantics=("arbitrary", "arbitrary"),
                                             vmem_limit_bytes=VMEM_LIMIT),
        name="wkv_scan",
    )(rh, yl, mt, gt, rh, yl, mt, gt)


def _rwkv_out_kernel(x_ref, yf_ref, yb_ref, bonus_ref, gate_ref, mod_ref, lg_ref, lb_ref,
                     w_ref, o_ref):
    y = yf_ref[...] + yb_ref[...]
    mu = _group_sum(y) * (1.0 / RW_HEAD)
    yc = y - mu
    var = _group_sum(yc * yc) * (1.0 / RW_HEAD)
    yn = yc * lax.rsqrt(var + LNX_EPS) * lg_ref[...] + lb_ref[...]
    z = ((yn + bonus_ref[...]) * gate_ref[...]).astype(BF16)
    out = jnp.dot(z, w_ref[...], preferred_element_type=F32)
    o_ref[...] = x_ref[...] + mod_ref[2:3, :] * out


def _rwkv_out(x, yf, yb, bonus, gate, mod, lnx_g, lnx_b, w_o, ctx_tiles, t0):
    bsz, n_rows, _ = x.shape
    n_tiles = n_rows // ROW_TILE - t0
    return pl.pallas_call(
        _rwkv_out_kernel,
        grid=(bsz, n_tiles),
        in_specs=[_tile_spec(t0)] * 5 + [_mod_spec(ctx_tiles, t0), _const_spec((1, D_MODEL)),
                                          _const_spec((1, D_MODEL)),
                                          _const_spec((D_MODEL, D_MODEL))],
        out_specs=_tile_spec(),
        out_shape=jax.ShapeDtypeStruct((bsz, n_tiles * ROW_TILE, D_MODEL), F32),
        compiler_params=_params(),
        name="rwkv_out",
    )(x, yf, yb, bonus, gate, mod, lnx_g, lnx_b, w_o)


def _rope_tables(n_ctx, n_lat):
    t = jnp.arange(n_lat)
    row_pos = (t // GRID_W).astype(F32)
    col_pos = (t % GRID_W).astype(F32)
    n_freq = DA_HEAD_DIM // 4
    inv_freq = ROPE_THETA ** (-jnp.arange(n_freq, dtype=F32) / n_freq)
    ang = jnp.concatenate([row_pos[:, None] * inv_freq, col_pos[:, None] * inv_freq], axis=-1)
    cos, sin = jnp.cos(ang), jnp.sin(ang)
    reps = LANES // DA_HEAD_DIM
    cos = jnp.tile(jnp.concatenate([cos, cos], axis=-1), (1, reps))
    sin = jnp.tile(jnp.concatenate([-sin, sin], axis=-1), (1, reps))
    cos = jnp.concatenate([jnp.ones((n_ctx, LANES), F32), cos], axis=0)
    sin = jnp.concatenate([jnp.zeros((n_ctx, LANES), F32), sin], axis=0)
    return cos, sin


def kernel(x, c, ctx, c_ctx, ada_w, ada_b, norm_mix_g, norm_ffn_g, ffn_w_up, ffn_conv_w, ffn_conv_b, ffn_w_down, da_w_qkv, da_lambda, da_subln_g, da_w_o, rw_mix_prev, rw_mix_next, rw_w_r, rw_w_k, rw_w_v, rw_w0, rw_w1, rw_w2, rw_a0, rw_a1, rw_a2, rw_v0, rw_v1, rw_v2, rw_k_k, rw_k_a, rw_r_k, rw_g1, rw_g2, rw_lnx_g, rw_lnx_b, rw_w_o, final_norm_g):
    bsz, n_lat, d = x.shape
    n_ctx = ctx.shape[1]
    depth = ada_w.shape[0]
    assert d == D_MODEL and n_lat % ROW_TILE == 0 and n_ctx % ROW_TILE == 0
    assert n_lat % GRID_W == 0
    ctx_tiles = n_ctx // ROW_TILE
    row = lambda a: a.reshape(1, -1)
    cat = lambda a: jnp.concatenate([a[0], a[1]], axis=0 if a.shape[1] == RW_HEAD else 1)

    pad = (-(bsz + 1)) % SUBLANES
    cc = jnp.concatenate([c, c_ctx[None, :], jnp.zeros((pad, d), F32)], axis=0)
    table = _ada_table(cc, ada_w, ada_b)
    mod_l = table[:, :bsz].reshape(depth, bsz, 6, d)
    mod_c = jnp.broadcast_to(table[:, bsz].reshape(depth, 1, 6, d), (depth, bsz, 6, d))
    mods = jnp.stack([mod_c, mod_l], axis=2)
    mods = jnp.pad(mods, ((0, 0), (0, 0), (0, 0), (0, SUBLANES - 6), (0, 0)))

    cos, sin = _rope_tables(n_ctx, n_lat)
    stream = jnp.concatenate([ctx, x], axis=1)
    v_first = None
    no_final = jnp.ones((1, d), F32)

    for i in range(depth):
        last = i == depth - 1
        j = i // 2
        t0 = ctx_tiles if last else 0
        mod = mods[i]
        g_mix = row(norm_mix_g[i])
        if i % 2 == 0:
            lambda_init = 0.8 - 0.6 * math.exp(-0.3 * i)
            q, k, v = _qkv(stream, mod, g_mix, da_w_qkv[j].astype(BF16), cos, sin, ctx_tiles)
            o = _flash(q, k, v, da_lambda[j], da_subln_g[j], ctx_tiles, lambda_init)
            stream = _proj_residual(stream, o, mod, da_w_o[j].astype(BF16), ctx_tiles, t0)
        else:
            p = dict(mix_prev=rw_mix_prev[j], mix_next=rw_mix_next[j],
                     w_r=rw_w_r[j].astype(BF16), w_k=rw_w_k[j].astype(BF16),
                     w_v=rw_w_v[j].astype(BF16), w0=rw_w0[j], w1=cat(rw_w1[j]).astype(BF16),
                     w2=cat(rw_w2[j]).astype(BF16), a0=rw_a0[j], a1=cat(rw_a1[j]).astype(BF16),
                     a2=cat(rw_a2[j]).astype(BF16), k_k=row(rw_k_k[j]), k_a=row(rw_k_a[j]),
                     r_k=row(rw_r_k[j]), g1=rw_g1[j].astype(BF16), g2=rw_g2[j].astype(BF16))
            if j > 0:
                n_v = rw_v1.shape[-1]
                p.update(v0=row(rw_v0[j - 1]),
                         v1=jnp.pad(rw_v1[j - 1], ((0, 0), (0, LANES - n_v))).astype(BF16),
                         v2=jnp.pad(rw_v2[j - 1], ((0, LANES - n_v), (0, 0))).astype(BF16))
            r, k, v, kk, bonus, gate, lw, a = _rwkv_proj(stream, mod, g_mix, p, v_first,
                                                         ctx_tiles)
            if v_first is None:
                v_first = v
            rh, yl, mt, gt = _wkv_prep(r, k, v, kk, lw, a, p['k_a'])
            ys = _wkv_scan(rh, yl, mt, gt, ctx_tiles)
            stream = _rwkv_out(stream, ys[0], ys[1], bonus, gate, mod, row(rw_lnx_g[j]),
                               row(rw_lnx_b[j]), rw_w_o[j].astype(BF16), ctx_tiles, t0)
        stream = _ffn(stream, mod, row(norm_ffn_g[i]), ffn_w_up[i].astype(BF16),
                      ffn_conv_w[i], row(ffn_conv_b[i]), ffn_w_down[i].astype(BF16),
                      row(final_norm_g) if last else no_final, 0 if last else ctx_tiles, last)
    return stream
```

```python
import functools
import math

import jax
import jax.numpy as jnp
from jax import lax
from jax.experimental import pallas as pl
from jax.experimental.pallas import tpu as pltpu

D_MODEL = 1024
GRID_W = 64
DA_HEAD_DIM = 64
DA_HEADS = D_MODEL // (2 * DA_HEAD_DIM)
ROPE_THETA = 10000.0
RW_HEAD = 64
N_SHIFT_MIX = 6
LNX_EPS = 64e-5
EPS = 1e-6

LANES = 128
SUBLANES = 8
ROW_TILE = 256
WKV_CHUNK = 32
FF_TILE = 256
PREP_CHUNKS = 4
VMEM_LIMIT = 56 * 1024 * 1024

F32 = jnp.float32
BF16 = jnp.bfloat16

Q_SCALE = DA_HEAD_DIM ** -0.5 * math.log2(math.e)


def _dot(a, b):
    return jnp.dot(a.astype(BF16), b.astype(BF16), preferred_element_type=F32)


def _dot_nt(a, b):
    return lax.dot_general(a.astype(BF16), b.astype(BF16), (((1,), (1,)), ((), ())),
                           preferred_element_type=F32)


def _dot_tn(a, b):
    return lax.dot_general(a.astype(BF16), b.astype(BF16), (((0,), (0,)), ((), ())),
                           preferred_element_type=F32)


def _rms(x):
    return x * lax.rsqrt(jnp.mean(x * x, axis=-1, keepdims=True) + EPS)


def _norm_mod(x, g, shift, scale):
    return _rms(x) * g * (1.0 + scale) + shift


def _sigmoid(x):
    return 1.0 / (1.0 + jnp.exp(-x))


def _lane_iota(shape):
    return lax.broadcasted_iota(jnp.int32, shape, len(shape) - 1)


def _row_iota(shape):
    return lax.broadcasted_iota(jnp.int32, shape, len(shape) - 2)


def _group_ones():
    r = lax.broadcasted_iota(jnp.int32, (LANES, LANES), 0) // RW_HEAD
    c = lax.broadcasted_iota(jnp.int32, (LANES, LANES), 1) // RW_HEAD
    return (r == c).astype(BF16)


def _group_sum(z):
    ones = _group_ones()
    parts = [jnp.dot(z[:, j:j + LANES].astype(BF16), ones, preferred_element_type=F32)
             for j in range(0, z.shape[1], LANES)]
    return jnp.concatenate(parts, axis=1)


def _shifted(h, halo_prev, halo_next, prev_ok, next_ok):
    rows = h.shape[0]
    ri = _row_iota(h.shape)
    first = jnp.where(prev_ok, halo_prev[SUBLANES - 1:SUBLANES, :], 0.0)
    last = jnp.where(next_ok, halo_next[0:1, :], 0.0)
    h_prev = jnp.where(ri == 0, first, pltpu.roll(h, 1, 0))
    h_next = jnp.where(ri == rows - 1, last, pltpu.roll(h, rows - 1, 0))
    return h_prev, h_next


def _edge_flags(t, n_tiles, ctx_tiles):
    prev_ok = jnp.logical_and(t != 0, t != ctx_tiles)
    next_ok = jnp.logical_and(t != n_tiles - 1, t != ctx_tiles - 1)
    return prev_ok, next_ok


def _const_spec(shape):
    zeros = (0,) * len(shape)
    return pl.BlockSpec(shape, lambda *_: zeros, pipeline_mode=pl.Buffered(1))


def _tile_spec(t0=0):
    return pl.BlockSpec((None, ROW_TILE, D_MODEL), lambda b, t: (b, t + t0, 0))


def _halo_specs(n_rows, t0=0):
    per = ROW_TILE // SUBLANES
    last = n_rows // SUBLANES - 1
    prev = pl.BlockSpec((None, SUBLANES, D_MODEL),
                        lambda b, t: (b, jnp.maximum((t + t0) * per - 1, 0), 0))
    nxt = pl.BlockSpec((None, SUBLANES, D_MODEL),
                       lambda b, t: (b, jnp.minimum((t + t0 + 1) * per, last), 0))
    return prev, nxt


def _mod_spec(ctx_tiles, t0=0):
    return pl.BlockSpec((None, None, SUBLANES, D_MODEL),
                        lambda b, t: (b, ((t + t0) >= ctx_tiles).astype(jnp.int32), 0, 0))


def _params():
    return pltpu.CompilerParams(vmem_limit_bytes=VMEM_LIMIT)


def _ada_kernel(c_ref, w_ref, b_ref, o_ref):
    c = c_ref[...]
    s = c * _sigmoid(c)
    o_ref[...] = jnp.dot(s, w_ref[...], precision=lax.Precision.HIGHEST,
                         preferred_element_type=F32) + b_ref[...]


def _ada_table(cc, ada_w, ada_b):
    n_layers, _, six_d = ada_w.shape
    rows = cc.shape[0]
    nb = 1536
    return pl.pallas_call(
        _ada_kernel,
        grid=(n_layers, six_d // nb),
        in_specs=[pl.BlockSpec((rows, D_MODEL), lambda l, n: (0, 0)),
                  pl.BlockSpec((None, D_MODEL, nb), lambda l, n: (l, 0, n)),
                  pl.BlockSpec((None, 1, nb), lambda l, n: (l, 0, n))],
        out_specs=pl.BlockSpec((None, rows, nb), lambda l, n: (l, 0, n)),
        out_shape=jax.ShapeDtypeStruct((n_layers, rows, six_d), F32),
        compiler_params=_params(),
        name="ada_table",
    )(cc, ada_w, ada_b.reshape(n_layers, 1, six_d))


def _rope(slab, cos, sin_signed):
    lane = _lane_iota(slab.shape)
    first = (lane % DA_HEAD_DIM) < (DA_HEAD_DIM // 2)
    partner = jnp.where(first, pltpu.roll(slab, LANES - DA_HEAD_DIM // 2, 1),
                        pltpu.roll(slab, DA_HEAD_DIM // 2, 1))
    return slab * cos + partner * sin_signed


def _qkv_kernel(x_ref, mod_ref, g_ref, w_ref, cos_ref, sin_ref, q_ref, k_ref, v_ref):
    mod = mod_ref[...]
    h = _norm_mod(x_ref[...], g_ref[...], mod[0:1], mod[1:2]).astype(BF16)
    cos = cos_ref[...]
    sin = sin_ref[...]
    wide = 2 * LANES
    for j in range(0, D_MODEL, wide):
        qa = jnp.dot(h, w_ref[:, j:j + wide], preferred_element_type=F32)
        ka = jnp.dot(h, w_ref[:, D_MODEL + j:D_MODEL + j + wide], preferred_element_type=F32)
        for i in range(0, wide, LANES):
            q_ref[:, j + i:j + i + LANES] = (
                _rope(qa[:, i:i + LANES], cos, sin) * Q_SCALE).astype(BF16)
            k_ref[:, j + i:j + i + LANES] = _rope(ka[:, i:i + LANES], cos, sin).astype(BF16)
        v_ref[:, j:j + wide] = jnp.dot(
            h, w_ref[:, 2 * D_MODEL + j:2 * D_MODEL + j + wide],
            preferred_element_type=F32).astype(BF16)


def _qkv(x, mod, g, w_qkv, cos, sin, ctx_tiles):
    bsz, n_rows, _ = x.shape
    out = jax.ShapeDtypeStruct((bsz, n_rows, D_MODEL), BF16)
    tab = pl.BlockSpec((ROW_TILE, LANES), lambda b, t: (t, 0))
    return pl.pallas_call(
        _qkv_kernel,
        grid=(bsz, n_rows // ROW_TILE),
        in_specs=[_tile_spec(), _mod_spec(ctx_tiles), _const_spec((1, D_MODEL)),
                  _const_spec((D_MODEL, 3 * D_MODEL)), tab, tab],
        out_specs=[_tile_spec(), _tile_spec(), _tile_spec()],
        out_shape=[out, out, out],
        compiler_params=_params(),
        name="attn_qkv",
    )(x, mod, g, w_qkv, cos, sin)


def _flash_kernel(lam_ref, g_ref, q_ref, qn_ref, k_ref, v_ref, o_ref,
                  sa_ref, sb_ref, ma_ref, mb_ref, vx_ref, *, ctx_tiles, lambda_init):
    t = pl.program_id(2)
    n_keys = k_ref.shape[0]
    ctx_keys = ctx_tiles * ROW_TILE
    lv = lam_ref[...]
    lam = (jnp.exp(jnp.sum(lv[0:1] * lv[1:2], axis=-1, keepdims=True))
           - jnp.exp(jnp.sum(lv[2:3] * lv[3:4], axis=-1, keepdims=True)) + lambda_init)

    @pl.when(t == 0)
    def _():
        vx_ref[:, :LANES] = v_ref[...]
        vx_ref[:, LANES:] = (_lane_iota((n_keys, LANES)) == 0).astype(BF16)

    def scores(q, n):
        lo = _lane_iota(q.shape) < DA_HEAD_DIM
        zero = jnp.zeros_like(q)
        k = k_ref[0:n, :]
        return [lax.dot_general(qc, k, (((1,), (1,)), ((), ())), preferred_element_type=F32)
                for qc in (jnp.where(lo, q, zero), jnp.where(lo, zero, q))]

    def stash(q, s_ref, m_ref):
        for c, s in enumerate(scores(q, n_keys)):
            s_ref[c] = s
            m_ref[c] = jnp.max(s, axis=-1, keepdims=True)

    def attend(shifted, n):
        outs = []
        for x in shifted:
            p = jnp.exp2(x.astype(BF16))
            acc = jnp.dot(p, vx_ref[0:n, :], preferred_element_type=F32)
            outs.append(acc[:, :LANES] / acc[:, LANES:LANES + 1])
        o = outs[0] - lam * outs[1]
        o_ref[...] = (_rms(o) * g_ref[...] * (1.0 - lambda_init)).astype(BF16)

    @pl.when(t < ctx_tiles)
    def _():
        attend([s - jnp.max(s, axis=-1, keepdims=True) for s in scores(q_ref[...], ctx_keys)],
               ctx_keys)

    slots = ((sa_ref, ma_ref), (sb_ref, mb_ref))

    @pl.when(t == ctx_tiles)
    def _():
        stash(q_ref[...], *slots[ctx_tiles % 2])

    for parity in range(2):
        @pl.when(jnp.logical_and(t >= ctx_tiles, t % 2 == parity))
        def _():
            stash(qn_ref[...], *slots[1 - parity])
            s_ref, m_ref = slots[parity]
            attend([s_ref[c] - m_ref[c] for c in range(2)], n_keys)


def _flash(q, k, v, lam_vecs, subln_g, ctx_tiles, lambda_init):
    bsz, n_rows, _ = q.shape
    n_tiles = n_rows // ROW_TILE
    kv_spec = pl.BlockSpec((None, n_rows, LANES), lambda b, h, t: (b, 0, h))
    q_spec = pl.BlockSpec((None, ROW_TILE, LANES), lambda b, h, t: (b, t, h))
    q_next = pl.BlockSpec((None, ROW_TILE, LANES),
                          lambda b, h, t: (b, jnp.minimum(t + 1, n_tiles - 1), h))
    kern = functools.partial(_flash_kernel, ctx_tiles=ctx_tiles, lambda_init=lambda_init)
    score = pltpu.VMEM((2, ROW_TILE, n_rows), F32)
    rowmax = pltpu.VMEM((2, ROW_TILE, 1), F32)
    return pl.pallas_call(
        kern,
        grid=(bsz, DA_HEADS, n_tiles),
        in_specs=[pl.BlockSpec(lam_vecs.shape, lambda b, h, t: (0, 0)),
                  pl.BlockSpec((1, LANES), lambda b, h, t: (0, 0)),
                  q_spec, q_next, kv_spec, kv_spec],
        out_specs=q_spec,
        out_shape=jax.ShapeDtypeStruct((bsz, n_rows, D_MODEL), BF16),
        scratch_shapes=[score, score, rowmax, rowmax,
                        pltpu.VMEM((n_rows, 2 * LANES), BF16)],
        compiler_params=pltpu.CompilerParams(
            dimension_semantics=("arbitrary", "arbitrary", "arbitrary"),
            vmem_limit_bytes=VMEM_LIMIT),
        name="diff_flash",
    )(lam_vecs, subln_g.reshape(1, LANES), q, q, k, v)


def _proj_residual_kernel(x_ref, z_ref, mod_ref, w_ref, o_ref):
    out = jnp.dot(z_ref[...], w_ref[...], preferred_element_type=F32)
    o_ref[...] = x_ref[...] + mod_ref[2:3, :] * out


def _proj_residual(x, z, mod, w, ctx_tiles, t0):
    bsz, n_rows, _ = x.shape
    n_tiles = n_rows // ROW_TILE - t0
    return pl.pallas_call(
        _proj_residual_kernel,
        grid=(bsz, n_tiles),
        in_specs=[_tile_spec(t0), _tile_spec(t0), _mod_spec(ctx_tiles, t0),
                  _const_spec((D_MODEL, D_MODEL))],
        out_specs=_tile_spec(),
        out_shape=jax.ShapeDtypeStruct((bsz, n_tiles * ROW_TILE, D_MODEL), F32),
        compiler_params=_params(),
        name="proj_residual",
    )(x, z, mod, w)


def _ffn_kernel(x_ref, xp_ref, xn_ref, mod_ref, g_ref, wup_ref, cw_ref, cb_ref, wdn_ref,
                fin_ref, o_ref, *, n_tiles, ctx_tiles, d_ff, final):
    t = pl.program_id(1)
    prev_ok, next_ok = _edge_flags(t, n_tiles, ctx_tiles)
    mod = mod_ref[...]
    g = g_ref[...]
    x = x_ref[...]
    h = _norm_mod(x, g, mod[3:4], mod[4:5]).astype(BF16)
    hp = _norm_mod(xp_ref[...], g, mod[3:4], mod[4:5]).astype(BF16)
    hn = _norm_mod(xn_ref[...], g, mod[3:4], mod[4:5]).astype(BF16)
    hcat = jnp.concatenate([h, hp, hn], axis=0)
    tiles = list(range(0, d_ff, FF_TILE))

    def up(f):
        gate = jnp.dot(hcat, wup_ref[:, f:f + FF_TILE], preferred_element_type=F32)
        val = jnp.dot(h, wup_ref[:, d_ff + f:d_ff + f + FF_TILE], preferred_element_type=F32)
        return gate, val

    acc = jnp.zeros((ROW_TILE, D_MODEL), F32)
    nxt = up(tiles[0])
    for i, f in enumerate(tiles):
        gate_all, val = nxt
        if i + 1 < len(tiles):
            nxt = up(tiles[i + 1])
        gate = gate_all[:ROW_TILE]
        g_prev, g_next = _shifted(gate, gate_all[ROW_TILE:ROW_TILE + SUBLANES],
                                  gate_all[ROW_TILE + SUBLANES:], prev_ok, next_ok)
        cw = cw_ref[:, f:f + FF_TILE]
        conv = g_prev * cw[0:1] + gate * cw[1:2] + g_next * cw[2:3] + cb_ref[:, f:f + FF_TILE]
        act = conv * _sigmoid(conv) * val
        acc = acc + jnp.dot(act.astype(BF16), wdn_ref[f:f + FF_TILE, :],
                            preferred_element_type=F32)
    y = x + mod[5:6] * acc
    if final:
        y = _rms(y) * fin_ref[...]
    o_ref[...] = y


def _ffn(x, mod, g, w_up, conv_w, conv_b, w_down, final_g, ctx_tiles, final):
    bsz, n_rows, _ = x.shape
    n_tiles = n_rows // ROW_TILE
    d_ff = w_down.shape[0]
    prev, nxt = _halo_specs(n_rows)
    kern = functools.partial(_ffn_kernel, n_tiles=n_tiles, ctx_tiles=ctx_tiles, d_ff=d_ff,
                             final=final)
    return pl.pallas_call(
        kern,
        grid=(bsz, n_tiles),
        in_specs=[_tile_spec(), prev, nxt, _mod_spec(ctx_tiles), _const_spec((1, D_MODEL)),
                  _const_spec((D_MODEL, 2 * d_ff)), _const_spec((3, d_ff)),
                  _const_spec((1, d_ff)), _const_spec((d_ff, D_MODEL)),
                  _const_spec((1, D_MODEL))],
        out_specs=_tile_spec(),
        out_shape=jax.ShapeDtypeStruct((bsz, n_rows, D_MODEL), F32),
        compiler_params=_params(),
        name="conv_glu",
    )(x, x, x, mod, g, w_up, conv_w, conv_b, w_down, final_g)


def _rwkv_proj_kernel(*refs, n_tiles, ctx_tiles, has_vfirst):
    (x_ref, xp_ref, xn_ref, mod_ref, g_ref, mp_ref, mn_ref, wr_ref, wk_ref, wv_ref,
     w0_ref, w1_ref, w2_ref, a0_ref, a1_ref, a2_ref, kk_ref, ka_ref, rk_ref,
     g1_ref, g2_ref) = refs[:21]
    refs = refs[21:]
    if has_vfirst:
        vf_ref, v0_ref, v1_ref, v2_ref = refs[:4]
        refs = refs[4:]
    r_ref, k_ref, v_ref, kkn_ref, bonus_ref, gate_ref, lw_ref, a_ref = refs

    t = pl.program_id(1)
    prev_ok, next_ok = _edge_flags(t, n_tiles, ctx_tiles)
    mod = mod_ref[...]
    g = g_ref[...]
    h = _norm_mod(x_ref[...], g, mod[0:1], mod[1:2])
    hp = _norm_mod(xp_ref[...], g, mod[0:1], mod[1:2])
    hn = _norm_mod(xn_ref[...], g, mod[0:1], mod[1:2])
    h_prev, h_next = _shifted(h, hp, hn, prev_ok, next_ok)
    hb = h.astype(BF16)
    xx_p = (h_prev - h).astype(BF16)
    xx_n = (h_next - h).astype(BF16)
    mp = mp_ref[...].astype(BF16)
    mn = mn_ref[...].astype(BF16)

    def mix(m):
        return hb + xx_p * mp[m:m + 1] + xx_n * mn[m:m + 1]

    lo = _lane_iota((ROW_TILE, LANES)) < RW_HEAD

    def halves(z):
        return jnp.where(lo, z, 0.0), jnp.where(lo, 0.0, z)

    r = jnp.dot(mix(0), wr_ref[...], preferred_element_type=F32)
    r_ref[...] = r

    lw = halves(jnp.tanh(jnp.dot(mix(1), w1_ref[...], preferred_element_type=F32)))
    la = halves(jnp.dot(mix(4), a1_ref[...], preferred_element_type=F32))
    a_gate = []
    for d in range(2):
        wl = w0_ref[d:d + 1, :] + _dot(lw[d], w2_ref[...])
        lw_ref[d] = _sigmoid(wl) * (-math.exp(-0.5))
        a_d = _sigmoid(a0_ref[d:d + 1, :] + _dot(la[d], a2_ref[...]))
        a_ref[d] = a_d
        a_gate.append(a_d)

    k = jnp.dot(mix(2), wk_ref[...], preferred_element_type=F32)
    k_ref[...] = k
    kk = k * kk_ref[...]
    kkn_ref[...] = kk / jnp.maximum(jnp.sqrt(_group_sum(kk * kk)), 1e-12)

    xv = mix(3)
    v = jnp.dot(xv, wv_ref[...], preferred_element_type=F32)
    if has_vfirst:
        lv = jnp.dot(xv, v1_ref[...], preferred_element_type=F32)
        v = v + (vf_ref[...] - v) * _sigmoid(v0_ref[...] + _dot(lv, v2_ref[...]))
    v_ref[...] = v

    ka = ka_ref[...]
    k_sum = k * (1.0 + (a_gate[0] - 1.0) * ka) + k * (1.0 + (a_gate[1] - 1.0) * ka)
    bonus_ref[...] = (_group_sum(r * k_sum * rk_ref[...]) * v).astype(BF16)

    gl = _sigmoid(jnp.dot(mix(5), g1_ref[...], preferred_element_type=F32))
    gate_ref[...] = _dot(gl, g2_ref[...]).astype(BF16)


def _rwkv_proj(x, mod, g, p, v_first, ctx_tiles):
    bsz, n_rows, _ = x.shape
    n_tiles = n_rows // ROW_TILE
    prev, nxt = _halo_specs(n_rows)
    has_vfirst = v_first is not None
    args = [x, x, x, mod, g, p['mix_prev'], p['mix_next'], p['w_r'], p['w_k'], p['w_v'],
            p['w0'], p['w1'], p['w2'], p['a0'], p['a1'], p['a2'], p['k_k'], p['k_a'], p['r_k'],
            p['g1'], p['g2']]
    specs = [_tile_spec(), prev, nxt, _mod_spec(ctx_tiles)] + [
        _const_spec(a.shape) for a in args[4:]]
    if has_vfirst:
        args += [v_first, p['v0'], p['v1'], p['v2']]
        specs += [_tile_spec()] + [_const_spec(a.shape) for a in args[-3:]]
    one = jax.ShapeDtypeStruct((bsz, n_rows, D_MODEL), F32)
    half = jax.ShapeDtypeStruct((bsz, n_rows, D_MODEL), BF16)
    two = jax.ShapeDtypeStruct((2, bsz, n_rows, D_MODEL), F32)
    two_spec = pl.BlockSpec((2, None, ROW_TILE, D_MODEL), lambda b, t: (0, b, t, 0))
    kern = functools.partial(_rwkv_proj_kernel, n_tiles=n_tiles, ctx_tiles=ctx_tiles,
                             has_vfirst=has_vfirst)
    return pl.pallas_call(
        kern,
        grid=(bsz, n_tiles),
        in_specs=specs,
        out_specs=[_tile_spec()] * 6 + [two_spec, two_spec],
        out_shape=[one] * 4 + [half, half, two, two],
        compiler_params=_params(),
        name="rwkv_proj",
    )(*args)


def _mask_stack(x, n_blocks, width):
    blk = _lane_iota(x.shape) // width
    zero = jnp.zeros_like(x)
    return jnp.concatenate([jnp.where(blk == i, x, zero) for i in range(n_blocks)], axis=0)


def _wkv_prep_kernel(r_ref, k_ref, v_ref, kk_ref, lw_ref, a_ref, ka_ref,
                     rh_ref, yl_ref, mt_ref, gt_ref):
    n = WKV_CHUNK
    hpg = LANES // n
    gw = hpg * RW_HEAD
    groups = D_MODEL // gw
    ri = lax.broadcasted_iota(jnp.int32, (n, n), 0)
    ci = lax.broadcasted_iota(jnp.int32, (n, n), 1)
    row = _row_iota((n, LANES))
    src = _lane_iota((n, LANES)) % n
    eye = (src == row).astype(F32)
    order = ((ci <= ri).astype(F32), src < row, src <= row, n - 1), \
            ((ci >= ri).astype(F32), src > row, src >= row, 0)
    lo_st = _lane_iota((RW_HEAD, LANES)) < RW_HEAD
    diag = (_lane_iota((RW_HEAD, LANES)) % RW_HEAD) == _row_iota((RW_HEAD, LANES))
    ka = ka_ref[...]
    bf = lambda z: z.astype(BF16)

    def body(it, carry):
        items = []
        for cc in range(PREP_CHUNKS):
            c = it * PREP_CHUNKS + cc
            rows = pl.ds(pl.multiple_of(c * n, n), n)
            kk = kk_ref[rows, :]
            k = k_ref[rows, :]
            r = r_ref[rows, :]
            v = bf(v_ref[rows, :])
            for d in range(2):
                tri, strict, incl, last = order[d]
                lw = lw_ref[d, rows, :]
                gate = a_ref[d, rows, :]
                kd = k * (1.0 + (gate - 1.0) * ka)
                b = kk * gate
                cum = jnp.dot(tri, lw, precision=lax.Precision.HIGHEST,
                              preferred_element_type=F32)
                cum_end = cum[last:last + 1, :]
                e_neg = jnp.exp(-cum)
                e_rem = jnp.exp(cum_end - cum)
                rt = r * jnp.exp(cum)
                full = dict(c=c, d=d, rows=rows, rt=rt, w_end=jnp.exp(cum_end), v=v,
                            at=bf(-kk * jnp.exp(cum - lw)), rtb=bf(rt), kt=bf(kd * e_neg),
                            bt=bf(b * e_neg), kh=bf(kd * e_rem), bh=bf(b * e_rem),
                            strict=strict, incl=incl)
                for g in range(groups):
                    items.append(dict(full=full, sl=slice(g * gw, (g + 1) * gw)))

        for it_ in items:
            f, sl = it_['full'], it_['sl']
            lhs = jnp.concatenate([f['at'][:, sl], f['rtb'][:, sl]], axis=0)
            rhs = jnp.concatenate([_mask_stack(f['kt'][:, sl], hpg, RW_HEAD),
                                   _mask_stack(f['bt'][:, sl], hpg, RW_HEAD)], axis=0)
            a = lax.dot_general(lhs, rhs, (((1,), (1,)), ((), ())), preferred_element_type=F32)
            it_['aak'] = jnp.where(f['strict'], a[:n, :LANES], 0.0)
            it_['ark'] = jnp.where(f['incl'], a[n:, :LANES], 0.0)
            it_['arb'] = bf(jnp.where(f['incl'], a[n:, LANES:], 0.0))
            pw = jnp.where(f['strict'], a[:n, LANES:], 0.0)
            it_['inv'] = eye + pw
            it_['pw'] = bf(pw)

        for it_ in items:
            it_['pw'] = bf(jnp.dot(it_['pw'], _mask_stack(it_['pw'], hpg, n),
                                   preferred_element_type=F32))
        steps = 2
        while steps < n:
            final = 2 * steps >= n
            for it_ in items:
                rhs = _mask_stack(bf(it_['inv']), hpg, n)
                if not final:
                    rhs = jnp.concatenate([_mask_stack(it_['pw'], hpg, n), rhs], axis=1)
                x = jnp.dot(it_['pw'], rhs, preferred_element_type=F32)
                it_['inv'] = it_['inv'] + x[:, -LANES:]
                if not final:
                    it_['pw'] = bf(x[:, :LANES])
            steps *= 2

        for it_ in items:
            f, sl = it_['full'], it_['sl']
            vms = _mask_stack(f['v'][:, sl], hpg, RW_HEAD)
            x = jnp.dot(bf(jnp.concatenate([it_['aak'], it_['ark']], axis=0)), vms,
                        preferred_element_type=F32)
            it_['arkv'] = x[n:]
            it_['rhs'] = jnp.concatenate([_mask_stack(f['at'][:, sl], hpg, RW_HEAD),
                                          _mask_stack(bf(x[:n]), hpg, RW_HEAD)], axis=1)
        for it_ in items:
            au = bf(jnp.dot(bf(it_['inv']), it_['rhs'], preferred_element_type=F32))
            it_['ah'] = au[:, :gw]
            it_['ul'] = au[:, gw:]
            it_['rhs'] = jnp.concatenate([_mask_stack(it_['ah'], hpg, RW_HEAD),
                                          _mask_stack(it_['ul'], hpg, RW_HEAD)], axis=1)
        for it_ in items:
            f, sl = it_['full'], it_['sl']
            x = jnp.dot(it_['arb'], it_['rhs'], preferred_element_type=F32)
            rh_ref[f['d'], f['rows'], sl] = bf(f['rt'][:, sl] + x[:, :gw])
            yl_ref[f['d'], f['rows'], sl] = it_['arkv'] + x[:, gw:]
        for it_ in items:
            f, sl = it_['full'], it_['sl']
            for j in range(0, gw, LANES):
                loc = slice(j, j + LANES)
                hs = slice(sl.start + j, sl.start + j + LANES)
                bh = f['bh'][:, hs]
                none = jnp.zeros((n, LANES), BF16)
                x = lax.dot_general(
                    jnp.concatenate([f['kh'][:, hs], bh], axis=0),
                    jnp.concatenate(
                        [jnp.concatenate([f['v'][:, hs], it_['ul'][:, loc]], axis=0),
                         jnp.concatenate([none, it_['ah'][:, loc]], axis=0)], axis=1),
                    (((0,), (0,)), ((), ())), preferred_element_type=F32)
                gf = x[:, :LANES]
                pm = x[:, LANES:]
                mt = jnp.where(lo_st, pm[:RW_HEAD], pm[RW_HEAD:])
                mt_ref[f['d'], f['c'], :, hs] = bf(mt + jnp.where(diag, f['w_end'][:, hs], 0.0))
                gt_ref[f['d'], f['c'], :, hs] = bf(jnp.where(lo_st, gf[:RW_HEAD], gf[RW_HEAD:]))
        return carry

    lax.fori_loop(0, ROW_TILE // (n * PREP_CHUNKS), body, 0)


def _wkv_prep(r, k, v, kk, lw, a, k_a):
    bsz, n_rows, _ = r.shape
    n_tiles = n_rows // ROW_TILE
    per = ROW_TILE // WKV_CHUNK
    dir_spec = pl.BlockSpec((2, None, ROW_TILE, D_MODEL), lambda b, t: (0, b, t, 0))
    st_spec = pl.BlockSpec((2, None, per, RW_HEAD, D_MODEL), lambda b, t: (0, b, t, 0, 0))
    rows = (2, bsz, n_rows, D_MODEL)
    st = (2, bsz, n_rows // WKV_CHUNK, RW_HEAD, D_MODEL)
    return pl.pallas_call(
        _wkv_prep_kernel,
        grid=(bsz, n_tiles),
        in_specs=[_tile_spec()] * 4 + [dir_spec, dir_spec, _const_spec((1, D_MODEL))],
        out_specs=[dir_spec, dir_spec, st_spec, st_spec],
        out_shape=[jax.ShapeDtypeStruct(rows, BF16), jax.ShapeDtypeStruct(rows, F32),
                   jax.ShapeDtypeStruct(st, BF16), jax.ShapeDtypeStruct(st, BF16)],
        compiler_params=_params(),
        name="wkv_prep",
    )(r, k, v, kk, lw, a, k_a)


def _wkv_scan_kernel(rhf_ref, ylf_ref, mtf_ref, gtf_ref, rhb_ref, ylb_ref, mtb_ref, gtb_ref,
                     yf_ref, yb_ref, state_ref):
    n = WKV_CHUNK
    per = ROW_TILE // n

    @pl.when(pl.program_id(1) == 0)
    def _():
        state_ref[...] = jnp.zeros(state_ref.shape, F32)

    lo = _lane_iota((RW_HEAD, LANES)) < RW_HEAD
    dirs = ((rhf_ref, ylf_ref, mtf_ref, gtf_ref, yf_ref), (rhb_ref, ylb_ref, mtb_ref, gtb_ref, yb_ref))
    for i in range(per):
        for d, (rh_ref, yl_ref, mt_ref, gt_ref, y_ref) in enumerate(dirs):
            c = per - 1 - i if d == 1 else i
            for hp in range(D_MODEL // LANES):
                col = slice(hp * LANES, (hp + 1) * LANES)
                lhs = jnp.concatenate([rh_ref[c * n:(c + 1) * n, col], mt_ref[c, :, col]],
                                      axis=0)
                out = jnp.dot(lhs, state_ref[d, hp].astype(BF16), preferred_element_type=F32)
                y_ref[c * n:(c + 1) * n, col] = out[:n] + yl_ref[c * n:(c + 1) * n, col]
                st = out[n:] + gt_ref[c, :, col]
                state_ref[d, hp] = jnp.concatenate(
                    [jnp.where(lo, st, 0.0), jnp.where(lo, 0.0, st)], axis=0)


def _wkv_scan(rh, yl, mt, gt, ctx_tiles):
    _, bsz, n_rows, _ = rh.shape
    n_tiles = n_rows // ROW_TILE
    per = ROW_TILE // WKV_CHUNK

    def back(t):
        return jnp.where(t < ctx_tiles, ctx_tiles - 1 - t, n_tiles - 1 - (t - ctx_tiles))

    def specs(d, tile):
        row = pl.BlockSpec((None, None, ROW_TILE, D_MODEL), lambda b, t: (d, b, tile(t), 0))
        st = pl.BlockSpec((None, None, per, RW_HEAD, D_MODEL),
                          lambda b, t: (d, b, tile(t), 0, 0))
        return [row, row, st, st]

    fwd = lambda t: t
    y_f = pl.BlockSpec((None, ROW_TILE, D_MODEL), lambda b, t: (b, t, 0))
    y_b = pl.BlockSpec((None, ROW_TILE, D_MODEL), lambda b, t: (b, back(t), 0))
    out = jax.ShapeDtypeStruct((bsz, n_rows, D_MODEL), F32)
    return pl.pallas_call(
        _wkv_scan_kernel,
        grid=(bsz, n_tiles),
        in_specs=specs(0, fwd) + specs(1, back),
        out_specs=[y_f, y_b],
        out_shape=[out, out],
        scratch_shapes=[pltpu.VMEM((2, D_MODEL // LANES, LANES, LANES), F32)],
        compiler_params=pltpu.CompilerParams(dimension_semantics=("arbitrary", "arbitrary"),
                                             vmem_limit_bytes=VMEM_LIMIT),
        name="wkv_scan",
    )(rh, yl, mt, gt, rh, yl, mt, gt)


def _rwkv_out_kernel(x_ref, yf_ref, yb_ref, bonus_ref, gate_ref, mod_ref, lg_ref, lb_ref,
                     w_ref, o_ref):
    y = yf_ref[...] + yb_ref[...]
    mu = _group_sum(y) * (1.0 / RW_HEAD)
    yc = y - mu
    var = _group_sum(yc * yc) * (1.0 / RW_HEAD)
    yn = yc * lax.rsqrt(var + LNX_EPS) * lg_ref[...] + lb_ref[...]
    z = ((yn + bonus_ref[...]) * gate_ref[...]).astype(BF16)
    out = jnp.dot(z, w_ref[...], preferred_element_type=F32)
    o_ref[...] = x_ref[...] + mod_ref[2:3, :] * out


def _rwkv_out(x, yf, yb, bonus, gate, mod, lnx_g, lnx_b, w_o, ctx_tiles, t0):
    bsz, n_rows, _ = x.shape
    n_tiles = n_rows // ROW_TILE - t0
    return pl.pallas_call(
        _rwkv_out_kernel,
        grid=(bsz, n_tiles),
        in_specs=[_tile_spec(t0)] * 5 + [_mod_spec(ctx_tiles, t0), _const_spec((1, D_MODEL)),
                                          _const_spec((1, D_MODEL)),
                                          _const_spec((D_MODEL, D_MODEL))],
        out_specs=_tile_spec(),
        out_shape=jax.ShapeDtypeStruct((bsz, n_tiles * ROW_TILE, D_MODEL), F32),
        compiler_params=_params(),
        name="rwkv_out",
    )(x, yf, yb, bonus, gate, mod, lnx_g, lnx_b, w_o)


def _rope_tables(n_ctx, n_lat):
    t = jnp.arange(n_lat)
    row_pos = (t // GRID_W).astype(F32)
    col_pos = (t % GRID_W).astype(F32)
    n_freq = DA_HEAD_DIM // 4
    inv_freq = ROPE_THETA ** (-jnp.arange(n_freq, dtype=F32) / n_freq)
    ang = jnp.concatenate([row_pos[:, None] * inv_freq, col_pos[:, None] * inv_freq], axis=-1)
    cos, sin = jnp.cos(ang), jnp.sin(ang)
    reps = LANES // DA_HEAD_DIM
    cos = jnp.tile(jnp.concatenate([cos, cos], axis=-1), (1, reps))
    sin = jnp.tile(jnp.concatenate([-sin, sin], axis=-1), (1, reps))
    cos = jnp.concatenate([jnp.ones((n_ctx, LANES), F32), cos], axis=0)
    sin = jnp.concatenate([jnp.zeros((n_ctx, LANES), F32), sin], axis=0)
    return cos, sin


def kernel(x, c, ctx, c_ctx, ada_w, ada_b, norm_mix_g, norm_ffn_g, ffn_w_up, ffn_conv_w, ffn_conv_b, ffn_w_down, da_w_qkv, da_lambda, da_subln_g, da_w_o, rw_mix_prev, rw_mix_next, rw_w_r, rw_w_k, rw_w_v, rw_w0, rw_w1, rw_w2, rw_a0, rw_a1, rw_a2, rw_v0, rw_v1, rw_v2, rw_k_k, rw_k_a, rw_r_k, rw_g1, rw_g2, rw_lnx_g, rw_lnx_b, rw_w_o, final_norm_g):
    bsz, n_lat, d = x.shape
    n_ctx = ctx.shape[1]
    depth = ada_w.shape[0]
    assert d == D_MODEL and n_lat % ROW_TILE == 0 and n_ctx % ROW_TILE == 0
    assert n_lat % GRID_W == 0
    ctx_tiles = n_ctx // ROW_TILE
    row = lambda a: a.reshape(1, -1)
    cat = lambda a: jnp.concatenate([a[0], a[1]], axis=0 if a.shape[1] == RW_HEAD else 1)

    pad = (-(bsz + 1)) % SUBLANES
    cc = jnp.concatenate([c, c_ctx[None, :], jnp.zeros((pad, d), F32)], axis=0)
    table = _ada_table(cc, ada_w, ada_b)
    mod_l = table[:, :bsz].reshape(depth, bsz, 6, d)
    mod_c = jnp.broadcast_to(table[:, bsz].reshape(depth, 1, 6, d), (depth, bsz, 6, d))
    mods = jnp.stack([mod_c, mod_l], axis=2)
    mods = jnp.pad(mods, ((0, 0), (0, 0), (0, 0), (0, SUBLANES - 6), (0, 0)))

    cos, sin = _rope_tables(n_ctx, n_lat)
    stream = jnp.concatenate([ctx, x], axis=1)
    v_first = None
    no_final = jnp.ones((1, d), F32)

    for i in range(depth):
        last = i == depth - 1
        j = i // 2
        t0 = ctx_tiles if last else 0
        mod = mods[i]
        g_mix = row(norm_mix_g[i])
        if i % 2 == 0:
            lambda_init = 0.8 - 0.6 * math.exp(-0.3 * i)
            q, k, v = _qkv(stream, mod, g_mix, da_w_qkv[j].astype(BF16), cos, sin, ctx_tiles)
            o = _flash(q, k, v, da_lambda[j], da_subln_g[j], ctx_tiles, lambda_init)
            stream = _proj_residual(stream, o, mod, da_w_o[j].astype(BF16), ctx_tiles, t0)
        else:
            p = dict(mix_prev=rw_mix_prev[j], mix_next=rw_mix_next[j],
                     w_r=rw_w_r[j].astype(BF16), w_k=rw_w_k[j].astype(BF16),
                     w_v=rw_w_v[j].astype(BF16), w0=rw_w0[j], w1=cat(rw_w1[j]).astype(BF16),
                     w2=cat(rw_w2[j]).astype(BF16), a0=rw_a0[j], a1=cat(rw_a1[j]).astype(BF16),
                     a2=cat(rw_a2[j]).astype(BF16), k_k=row(rw_k_k[j]), k_a=row(rw_k_a[j]),
                     r_k=row(rw_r_k[j]), g1=rw_g1[j].astype(BF16), g2=rw_g2[j].astype(BF16))
            if j > 0:
                n_v = rw_v1.shape[-1]
                p.update(v0=row(rw_v0[j - 1]),
                         v1=jnp.pad(rw_v1[j - 1], ((0, 0), (0, LANES - n_v))).astype(BF16),
                         v2=jnp.pad(rw_v2[j - 1], ((0, LANES - n_v), (0, 0))).astype(BF16))
            r, k, v, kk, bonus, gate, lw, a = _rwkv_proj(stream, mod, g_mix, p, v_first,
                                                         ctx_tiles)
            if v_first is None:
                v_first = v
            rh, yl, mt, gt = _wkv_prep(r, k, v, kk, lw, a, p['k_a'])
            ys = _wkv_scan(rh, yl, mt, gt, ctx_tiles)
            stream = _rwkv_out(stream, ys[0], ys[1], bonus, gate, mod, row(rw_lnx_g[j]),
                               row(rw_lnx_b[j]), rw_w_o[j].astype(BF16), ctx_tiles, t0)
        stream = _ffn(stream, mod, row(norm_ffn_g[i]), ffn_w_up[i].astype(BF16),
                      ffn_conv_w[i], row(ffn_conv_b[i]), ffn_w_down[i].astype(BF16),
                      row(final_norm_g) if last else no_final, 0 if last else ctx_tiles, last)
    return stream
```

```python
import functools
import math

import jax
import jax.numpy as jnp
from jax import lax
from jax.experimental import pallas as pl
from jax.experimental.pallas import tpu as pltpu

D_MODEL = 1024
GRID_W = 64
DA_HEAD_DIM = 64
DA_HEADS = D_MODEL // (2 * DA_HEAD_DIM)
ROPE_THETA = 10000.0
RW_HEAD = 64
N_SHIFT_MIX = 6
LNX_EPS = 64e-5
EPS = 1e-6

LANES = 128
SUBLANES = 8
ROW_TILE = 256
WKV_CHUNK = 32
FF_TILE = 256
PREP_CHUNKS = 4
VMEM_LIMIT = 56 * 1024 * 1024

F32 = jnp.float32
BF16 = jnp.bfloat16

Q_SCALE = DA_HEAD_DIM ** -0.5 * math.log2(math.e)


def _dot(a, b):
    return jnp.dot(a.astype(BF16), b.astype(BF16), preferred_element_type=F32)


def _dot_nt(a, b):
    return lax.dot_general(a.astype(BF16), b.astype(BF16), (((1,), (1,)), ((), ())),
                           preferred_element_type=F32)


def _dot_tn(a, b):
    return lax.dot_general(a.astype(BF16), b.astype(BF16), (((0,), (0,)), ((), ())),
                           preferred_element_type=F32)


def _rms(x):
    return x * lax.rsqrt(jnp.mean(x * x, axis=-1, keepdims=True) + EPS)


def _norm_mod(x, g, shift, scale):
    return _rms(x) * g * (1.0 + scale) + shift


def _sigmoid(x):
    return 1.0 / (1.0 + jnp.exp(-x))


def _lane_iota(shape):
    return lax.broadcasted_iota(jnp.int32, shape, len(shape) - 1)


def _row_iota(shape):
    return lax.broadcasted_iota(jnp.int32, shape, len(shape) - 2)


def _group_ones():
    r = lax.broadcasted_iota(jnp.int32, (LANES, LANES), 0) // RW_HEAD
    c = lax.broadcasted_iota(jnp.int32, (LANES, LANES), 1) // RW_HEAD
    return (r == c).astype(BF16)


def _group_sum(z):
    ones = _group_ones()
    parts = [jnp.dot(z[:, j:j + LANES].astype(BF16), ones, preferred_element_type=F32)
             for j in range(0, z.shape[1], LANES)]
    return jnp.concatenate(parts, axis=1)


def _shifted(h, halo_prev, halo_next, prev_ok, next_ok):
    rows = h.shape[0]
    ri = _row_iota(h.shape)
    first = jnp.where(prev_ok, halo_prev[SUBLANES - 1:SUBLANES, :], 0.0)
    last = jnp.where(next_ok, halo_next[0:1, :], 0.0)
    h_prev = jnp.where(ri == 0, first, pltpu.roll(h, 1, 0))
    h_next = jnp.where(ri == rows - 1, last, pltpu.roll(h, rows - 1, 0))
    return h_prev, h_next


def _edge_flags(t, n_tiles, ctx_tiles):
    prev_ok = jnp.logical_and(t != 0, t != ctx_tiles)
    next_ok = jnp.logical_and(t != n_tiles - 1, t != ctx_tiles - 1)
    return prev_ok, next_ok


def _const_spec(shape):
    zeros = (0,) * len(shape)
    return pl.BlockSpec(shape, lambda *_: zeros, pipeline_mode=pl.Buffered(1))


def _tile_spec(t0=0):
    return pl.BlockSpec((None, ROW_TILE, D_MODEL), lambda b, t: (b, t + t0, 0))


def _halo_specs(n_rows, t0=0):
    per = ROW_TILE // SUBLANES
    last = n_rows // SUBLANES - 1
    prev = pl.BlockSpec((None, SUBLANES, D_MODEL),
                        lambda b, t: (b, jnp.maximum((t + t0) * per - 1, 0), 0))
    nxt = pl.BlockSpec((None, SUBLANES, D_MODEL),
                       lambda b, t: (b, jnp.minimum((t + t0 + 1) * per, last), 0))
    return prev, nxt


def _mod_spec(ctx_tiles, t0=0):
    return pl.BlockSpec((None, None, SUBLANES, D_MODEL),
                        lambda b, t: (b, ((t + t0) >= ctx_tiles).astype(jnp.int32), 0, 0))


def _params():
    return pltpu.CompilerParams(vmem_limit_bytes=VMEM_LIMIT)


def _ada_kernel(c_ref, w_ref, b_ref, o_ref):
    c = c_ref[...]
    s = c * _sigmoid(c)
    o_ref[...] = jnp.dot(s, w_ref[...], precision=lax.Precision.HIGHEST,
                         preferred_element_type=F32) + b_ref[...]


def _ada_table(cc, ada_w, ada_b):
    n_layers, _, six_d = ada_w.shape
    rows = cc.shape[0]
    nb = 1536
    return pl.pallas_call(
        _ada_kernel,
        grid=(n_layers, six_d // nb),
        in_specs=[pl.BlockSpec((rows, D_MODEL), lambda l, n: (0, 0)),
                  pl.BlockSpec((None, D_MODEL, nb), lambda l, n: (l, 0, n)),
                  pl.BlockSpec((None, 1, nb), lambda l, n: (l, 0, n))],
        out_specs=pl.BlockSpec((None, rows, nb), lambda l, n: (l, 0, n)),
        out_shape=jax.ShapeDtypeStruct((n_layers, rows, six_d), F32),
        compiler_params=_params(),
        name="ada_table",
    )(cc, ada_w, ada_b.reshape(n_layers, 1, six_d))


def _rope(slab, cos, sin_signed):
    lane = _lane_iota(slab.shape)
    first = (lane % DA_HEAD_DIM) < (DA_HEAD_DIM // 2)
    partner = jnp.where(first, pltpu.roll(slab, LANES - DA_HEAD_DIM // 2, 1),
                        pltpu.roll(slab, DA_HEAD_DIM // 2, 1))
    return slab * cos + partner * sin_signed


def _qkv_kernel(x_ref, mod_ref, g_ref, w_ref, cos_ref, sin_ref, q_ref, k_ref, v_ref):
    mod = mod_ref[...]
    h = _norm_mod(x_ref[...], g_ref[...], mod[0:1], mod[1:2]).astype(BF16)
    cos = cos_ref[...]
    sin = sin_ref[...]
    wide = 2 * LANES
    for j in range(0, D_MODEL, wide):
        qa = jnp.dot(h, w_ref[:, j:j + wide], preferred_element_type=F32)
        ka = jnp.dot(h, w_ref[:, D_MODEL + j:D_MODEL + j + wide], preferred_element_type=F32)
        for i in range(0, wide, LANES):
            q_ref[:, j + i:j + i + LANES] = (
                _rope(qa[:, i:i + LANES], cos, sin) * Q_SCALE).astype(BF16)
            k_ref[:, j + i:j + i + LANES] = _rope(ka[:, i:i + LANES], cos, sin).astype(BF16)
        v_ref[:, j:j + wide] = jnp.dot(
            h, w_ref[:, 2 * D_MODEL + j:2 * D_MODEL + j + wide],
            preferred_element_type=F32).astype(BF16)


def _qkv(x, mod, g, w_qkv, cos, sin, ctx_tiles):
    bsz, n_rows, _ = x.shape
    out = jax.ShapeDtypeStruct((bsz, n_rows, D_MODEL), BF16)
    tab = pl.BlockSpec((ROW_TILE, LANES), lambda b, t: (t, 0))
    return pl.pallas_call(
        _qkv_kernel,
        grid=(bsz, n_rows // ROW_TILE),
        in_specs=[_tile_spec(), _mod_spec(ctx_tiles), _const_spec((1, D_MODEL)),
                  _const_spec((D_MODEL, 3 * D_MODEL)), tab, tab],
        out_specs=[_tile_spec(), _tile_spec(), _tile_spec()],
        out_shape=[out, out, out],
        compiler_params=_params(),
        name="attn_qkv",
    )(x, mod, g, w_qkv, cos, sin)


def _flash_kernel(lam_ref, g_ref, q_ref, qn_ref, k_ref, v_ref, o_ref,
                  sa_ref, sb_ref, ma_ref, mb_ref, vx_ref, *, ctx_tiles, lambda_init):
    t = pl.program_id(2)
    n_keys = k_ref.shape[0]
    ctx_keys = ctx_tiles * ROW_TILE
    lv = lam_ref[...]
    lam = (jnp.exp(jnp.sum(lv[0:1] * lv[1:2], axis=-1, keepdims=True))
           - jnp.exp(jnp.sum(lv[2:3] * lv[3:4], axis=-1, keepdims=True)) + lambda_init)

    @pl.when(t == 0)
    def _():
        vx_ref[:, :LANES] = v_ref[...]
        vx_ref[:, LANES:] = (_lane_iota((n_keys, LANES)) == 0).astype(BF16)

    def scores(q, n):
        lo = _lane_iota(q.shape) < DA_HEAD_DIM
        zero = jnp.zeros_like(q)
        k = k_ref[0:n, :]
        return [lax.dot_general(qc, k, (((1,), (1,)), ((), ())), preferred_element_type=F32)
                for qc in (jnp.where(lo, q, zero), jnp.where(lo, zero, q))]

    def stash(q, s_ref, m_ref):
        for c, s in enumerate(scores(q, n_keys)):
            s_ref[c] = s
            m_ref[c] = jnp.max(s, axis=-1, keepdims=True)

    def attend(shifted, n):
        outs = []
        for x in shifted:
            p = jnp.exp2(x.astype(BF16))
            acc = jnp.dot(p, vx_ref[0:n, :], preferred_element_type=F32)
            outs.append(acc[:, :LANES] / acc[:, LANES:LANES + 1])
        o = outs[0] - lam * outs[1]
        o_ref[...] = (_rms(o) * g_ref[...] * (1.0 - lambda_init)).astype(BF16)

    @pl.when(t < ctx_tiles)
    def _():
        attend([s - jnp.max(s, axis=-1, keepdims=True) for s in scores(q_ref[...], ctx_keys)],
               ctx_keys)

    slots = ((sa_ref, ma_ref), (sb_ref, mb_ref))

    @pl.when(t == ctx_tiles)
    def _():
        stash(q_ref[...], *slots[ctx_tiles % 2])

    def shifted(slot):
        s_ref, m_ref = slots[slot]
        return [s_ref[c] - m_ref[c] for c in range(2)]

    last = pl.num_programs(2) - 1
    for parity in range(2):
        @pl.when(jnp.logical_and(jnp.logical_and(t >= ctx_tiles, t < last), t % 2 == parity))
        def _():
            stash(qn_ref[...], *slots[1 - parity])
            attend(shifted(parity), n_keys)

        @pl.when(jnp.logical_and(jnp.logical_and(t >= ctx_tiles, t == last), t % 2 == parity))
        def _():
            attend(shifted(parity), n_keys)


def _flash(q, k, v, lam_vecs, subln_g, ctx_tiles, lambda_init):
    bsz, n_rows, _ = q.shape
    n_tiles = n_rows // ROW_TILE
    kv_spec = pl.BlockSpec((None, n_rows, LANES), lambda b, h, t: (b, 0, h))
    q_spec = pl.BlockSpec((None, ROW_TILE, LANES), lambda b, h, t: (b, t, h))
    q_next = pl.BlockSpec((None, ROW_TILE, LANES),
                          lambda b, h, t: (b, jnp.minimum(t + 1, n_tiles - 1), h))
    kern = functools.partial(_flash_kernel, ctx_tiles=ctx_tiles, lambda_init=lambda_init)
    score = pltpu.VMEM((2, ROW_TILE, n_rows), F32)
    rowmax = pltpu.VMEM((2, ROW_TILE, 1), F32)
    return pl.pallas_call(
        kern,
        grid=(bsz, DA_HEADS, n_tiles),
        in_specs=[pl.BlockSpec(lam_vecs.shape, lambda b, h, t: (0, 0)),
                  pl.BlockSpec((1, LANES), lambda b, h, t: (0, 0)),
                  q_spec, q_next, kv_spec, kv_spec],
        out_specs=q_spec,
        out_shape=jax.ShapeDtypeStruct((bsz, n_rows, D_MODEL), BF16),
        scratch_shapes=[score, score, rowmax, rowmax,
                        pltpu.VMEM((n_rows, 2 * LANES), BF16)],
        compiler_params=pltpu.CompilerParams(
            dimension_semantics=("arbitrary", "arbitrary", "arbitrary"),
            vmem_limit_bytes=VMEM_LIMIT),
        name="diff_flash",
    )(lam_vecs, subln_g.reshape(1, LANES), q, q, k, v)


def _proj_residual_kernel(x_ref, z_ref, mod_ref, w_ref, o_ref):
    out = jnp.dot(z_ref[...], w_ref[...], preferred_element_type=F32)
    o_ref[...] = x_ref[...] + mod_ref[2:3, :] * out


def _proj_residual(x, z, mod, w, ctx_tiles, t0):
    bsz, n_rows, _ = x.shape
    n_tiles = n_rows // ROW_TILE - t0
    return pl.pallas_call(
        _proj_residual_kernel,
        grid=(bsz, n_tiles),
        in_specs=[_tile_spec(t0), _tile_spec(t0), _mod_spec(ctx_tiles, t0),
                  _const_spec((D_MODEL, D_MODEL))],
        out_specs=_tile_spec(),
        out_shape=jax.ShapeDtypeStruct((bsz, n_tiles * ROW_TILE, D_MODEL), F32),
        compiler_params=_params(),
        name="proj_residual",
    )(x, z, mod, w)


def _ffn_kernel(x_ref, xp_ref, xn_ref, mod_ref, g_ref, wup_ref, cw_ref, cb_ref, wdn_ref,
                fin_ref, o_ref, *, n_tiles, ctx_tiles, d_ff, final):
    t = pl.program_id(1)
    prev_ok, next_ok = _edge_flags(t, n_tiles, ctx_tiles)
    mod = mod_ref[...]
    g = g_ref[...]
    x = x_ref[...]
    h = _norm_mod(x, g, mod[3:4], mod[4:5]).astype(BF16)
    hp = _norm_mod(xp_ref[...], g, mod[3:4], mod[4:5]).astype(BF16)
    hn = _norm_mod(xn_ref[...], g, mod[3:4], mod[4:5]).astype(BF16)
    hcat = jnp.concatenate([h, hp, hn], axis=0)
    tiles = list(range(0, d_ff, FF_TILE))

    def up(f):
        gate = jnp.dot(hcat, wup_ref[:, f:f + FF_TILE], preferred_element_type=F32)
        val = jnp.dot(h, wup_ref[:, d_ff + f:d_ff + f + FF_TILE], preferred_element_type=F32)
        return gate, val

    acc = jnp.zeros((ROW_TILE, D_MODEL), F32)
    nxt = up(tiles[0])
    for i, f in enumerate(tiles):
        gate_all, val = nxt
        if i + 1 < len(tiles):
            nxt = up(tiles[i + 1])
        gate = gate_all[:ROW_TILE]
        g_prev, g_next = _shifted(gate, gate_all[ROW_TILE:ROW_TILE + SUBLANES],
                                  gate_all[ROW_TILE + SUBLANES:], prev_ok, next_ok)
        cw = cw_ref[:, f:f + FF_TILE]
        conv = g_prev * cw[0:1] + gate * cw[1:2] + g_next * cw[2:3] + cb_ref[:, f:f + FF_TILE]
        act = conv * _sigmoid(conv) * val
        acc = acc + jnp.dot(act.astype(BF16), wdn_ref[f:f + FF_TILE, :],
                            preferred_element_type=F32)
    y = x + mod[5:6] * acc
    if final:
        y = _rms(y) * fin_ref[...]
    o_ref[...] = y


def _ffn(x, mod, g, w_up, conv_w, conv_b, w_down, final_g, ctx_tiles, final):
    bsz, n_rows, _ = x.shape
    n_tiles = n_rows // ROW_TILE
    d_ff = w_down.shape[0]
    prev, nxt = _halo_specs(n_rows)
    kern = functools.partial(_ffn_kernel, n_tiles=n_tiles, ctx_tiles=ctx_tiles, d_ff=d_ff,
                             final=final)
    return pl.pallas_call(
        kern,
        grid=(bsz, n_tiles),
        in_specs=[_tile_spec(), prev, nxt, _mod_spec(ctx_tiles), _const_spec((1, D_MODEL)),
                  _const_spec((D_MODEL, 2 * d_ff)), _const_spec((3, d_ff)),
                  _const_spec((1, d_ff)), _const_spec((d_ff, D_MODEL)),
                  _const_spec((1, D_MODEL))],
        out_specs=_tile_spec(),
        out_shape=jax.ShapeDtypeStruct((bsz, n_rows, D_MODEL), F32),
        compiler_params=_params(),
        name="conv_glu",
    )(x, x, x, mod, g, w_up, conv_w, conv_b, w_down, final_g)


def _rwkv_proj_kernel(*refs, n_tiles, ctx_tiles, has_vfirst):
    (x_ref, xp_ref, xn_ref, mod_ref, g_ref, mp_ref, mn_ref, wr_ref, wk_ref, wv_ref,
     w0_ref, w1_ref, w2_ref, a0_ref, a1_ref, a2_ref, kk_ref, ka_ref, rk_ref,
     g1_ref, g2_ref) = refs[:21]
    refs = refs[21:]
    if has_vfirst:
        vf_ref, v0_ref, v1_ref, v2_ref = refs[:4]
        refs = refs[4:]
    r_ref, k_ref, v_ref, kkn_ref, bonus_ref, gate_ref, lw_ref, a_ref = refs

    t = pl.program_id(1)
    prev_ok, next_ok = _edge_flags(t, n_tiles, ctx_tiles)
    mod = mod_ref[...]
    g = g_ref[...]
    h = _norm_mod(x_ref[...], g, mod[0:1], mod[1:2])
    hp = _norm_mod(xp_ref[...], g, mod[0:1], mod[1:2])
    hn = _norm_mod(xn_ref[...], g, mod[0:1], mod[1:2])
    h_prev, h_next = _shifted(h, hp, hn, prev_ok, next_ok)
    hb = h.astype(BF16)
    xx_p = (h_prev - h).astype(BF16)
    xx_n = (h_next - h).astype(BF16)
    mp = mp_ref[...].astype(BF16)
    mn = mn_ref[...].astype(BF16)

    def mix(m):
        return hb + xx_p * mp[m:m + 1] + xx_n * mn[m:m + 1]

    lo = _lane_iota((ROW_TILE, LANES)) < RW_HEAD

    def halves(z):
        return jnp.where(lo, z, 0.0), jnp.where(lo, 0.0, z)

    r = jnp.dot(mix(0), wr_ref[...], preferred_element_type=F32)
    r_ref[...] = r

    lw = halves(jnp.tanh(jnp.dot(mix(1), w1_ref[...], preferred_element_type=F32)))
    la = halves(jnp.dot(mix(4), a1_ref[...], preferred_element_type=F32))
    a_gate = []
    for d in range(2):
        wl = w0_ref[d:d + 1, :] + _dot(lw[d], w2_ref[...])
        lw_ref[d] = _sigmoid(wl) * (-math.exp(-0.5))
        a_d = _sigmoid(a0_ref[d:d + 1, :] + _dot(la[d], a2_ref[...]))
        a_ref[d] = a_d
        a_gate.append(a_d)

    k = jnp.dot(mix(2), wk_ref[...], preferred_element_type=F32)
    k_ref[...] = k
    kk = k * kk_ref[...]
    kkn_ref[...] = kk * lax.rsqrt(jnp.maximum(_group_sum(kk * kk), 1e-24))

    xv = mix(3)
    v = jnp.dot(xv, wv_ref[...], preferred_element_type=F32)
    if has_vfirst:
        lv = jnp.dot(xv, v1_ref[...], preferred_element_type=F32)
        v = v + (vf_ref[...] - v) * _sigmoid(v0_ref[...] + _dot(lv, v2_ref[...]))
    v_ref[...] = v

    ka = ka_ref[...]
    k_sum = k * (2.0 + (a_gate[0] + a_gate[1] - 2.0) * ka)
    bonus_ref[...] = (_group_sum(r * k_sum * rk_ref[...]) * v).astype(BF16)

    gl = _sigmoid(jnp.dot(mix(5), g1_ref[...], preferred_element_type=F32))
    gate_ref[...] = _dot(gl, g2_ref[...]).astype(BF16)


def _rwkv_proj(x, mod, g, p, v_first, ctx_tiles):
    bsz, n_rows, _ = x.shape
    n_tiles = n_rows // ROW_TILE
    prev, nxt = _halo_specs(n_rows)
    has_vfirst = v_first is not None
    args = [x, x, x, mod, g, p['mix_prev'], p['mix_next'], p['w_r'], p['w_k'], p['w_v'],
            p['w0'], p['w1'], p['w2'], p['a0'], p['a1'], p['a2'], p['k_k'], p['k_a'], p['r_k'],
            p['g1'], p['g2']]
    specs = [_tile_spec(), prev, nxt, _mod_spec(ctx_tiles)] + [
        _const_spec(a.shape) for a in args[4:]]
    if has_vfirst:
        args += [v_first, p['v0'], p['v1'], p['v2']]
        specs += [_tile_spec()] + [_const_spec(a.shape) for a in args[-3:]]
    one = jax.ShapeDtypeStruct((bsz, n_rows, D_MODEL), F32)
    half = jax.ShapeDtypeStruct((bsz, n_rows, D_MODEL), BF16)
    two = jax.ShapeDtypeStruct((2, bsz, n_rows, D_MODEL), F32)
    two_spec = pl.BlockSpec((2, None, ROW_TILE, D_MODEL), lambda b, t: (0, b, t, 0))
    kern = functools.partial(_rwkv_proj_kernel, n_tiles=n_tiles, ctx_tiles=ctx_tiles,
                             has_vfirst=has_vfirst)
    return pl.pallas_call(
        kern,
        grid=(bsz, n_tiles),
        in_specs=specs,
        out_specs=[_tile_spec()] * 6 + [two_spec, two_spec],
        out_shape=[one] * 4 + [half, half, two, two],
        compiler_params=_params(),
        name="rwkv_proj",
    )(*args)


def _mask_stack(x, n_blocks, width):
    blk = _lane_iota(x.shape) // width
    zero = jnp.zeros_like(x)
    return jnp.concatenate([jnp.where(blk == i, x, zero) for i in range(n_blocks)], axis=0)


def _wkv_prep_kernel(r_ref, k_ref, v_ref, kk_ref, lw_ref, a_ref, ka_ref,
                     rh_ref, yl_ref, mt_ref, gt_ref):
    n = WKV_CHUNK
    hpg = LANES // n
    gw = hpg * RW_HEAD
    groups = D_MODEL // gw
    ri = lax.broadcasted_iota(jnp.int32, (n, n), 0)
    ci = lax.broadcasted_iota(jnp.int32, (n, n), 1)
    row = _row_iota((n, LANES))
    src = _lane_iota((n, LANES)) % n
    eye = (src == row).astype(F32)
    order = ((ci <= ri).astype(BF16), src < row, src <= row, n - 1), \
            ((ci >= ri).astype(BF16), src > row, src >= row, 0)
    lo_st = _lane_iota((RW_HEAD, LANES)) < RW_HEAD
    diag = (_lane_iota((RW_HEAD, LANES)) % RW_HEAD) == _row_iota((RW_HEAD, LANES))
    ka = ka_ref[...]
    bf = lambda z: z.astype(BF16)

    def body(it, carry):
        items = []
        for cc in range(PREP_CHUNKS):
            c = it * PREP_CHUNKS + cc
            rows = pl.ds(pl.multiple_of(c * n, n), n)
            kk = kk_ref[rows, :]
            k = k_ref[rows, :]
            r = r_ref[rows, :]
            v = bf(v_ref[rows, :])
            for d in range(2):
                tri, strict, incl, last = order[d]
                lw = lw_ref[d, rows, :]
                gate = a_ref[d, rows, :]
                kd = k * (1.0 + (gate - 1.0) * ka)
                b = kk * gate
                lw_hi = bf(lw)
                lw_lo = bf(lw - lw_hi.astype(F32))
                cum = (jnp.dot(tri, lw_hi, preferred_element_type=F32)
                       + jnp.dot(tri, lw_lo, preferred_element_type=F32))
                cum_end = cum[last:last + 1, :]
                e_neg = jnp.exp(-cum)
                e_rem = jnp.exp(cum_end - cum)
                rt = r * jnp.exp(cum)
                full = dict(c=c, d=d, rows=rows, rt=rt, w_end=jnp.exp(cum_end), v=v,
                            at=bf(-kk * jnp.exp(cum - lw)), rtb=bf(rt), kt=bf(kd * e_neg),
                            bt=bf(b * e_neg), kh=bf(kd * e_rem), bh=bf(b * e_rem),
                            strict=strict, incl=incl)
                for g in range(groups):
                    items.append(dict(full=full, sl=slice(g * gw, (g + 1) * gw)))

        for it_ in items:
            f, sl = it_['full'], it_['sl']
            lhs = jnp.concatenate([f['at'][:, sl], f['rtb'][:, sl]], axis=0)
            rhs = jnp.concatenate([_mask_stack(f['kt'][:, sl], hpg, RW_HEAD),
                                   _mask_stack(f['bt'][:, sl], hpg, RW_HEAD)], axis=0)
            a = lax.dot_general(lhs, rhs, (((1,), (1,)), ((), ())), preferred_element_type=F32)
            it_['aak'] = jnp.where(f['strict'], a[:n, :LANES], 0.0)
            it_['ark'] = jnp.where(f['incl'], a[n:, :LANES], 0.0)
            it_['arb'] = bf(jnp.where(f['incl'], a[n:, LANES:], 0.0))
            pw = jnp.where(f['strict'], a[:n, LANES:], 0.0)
            it_['inv'] = eye + pw
            it_['pw'] = bf(pw)

        for it_ in items:
            it_['pw'] = bf(jnp.dot(it_['pw'], _mask_stack(it_['pw'], hpg, n),
                                   preferred_element_type=F32))
        steps = 2
        while steps < n:
            final = 2 * steps >= n
            for it_ in items:
                rhs = _mask_stack(bf(it_['inv']), hpg, n)
                if not final:
                    rhs = jnp.concatenate([_mask_stack(it_['pw'], hpg, n), rhs], axis=1)
                x = jnp.dot(it_['pw'], rhs, preferred_element_type=F32)
                it_['inv'] = it_['inv'] + x[:, -LANES:]
                if not final:
                    it_['pw'] = bf(x[:, :LANES])
            steps *= 2

        for it_ in items:
            f, sl = it_['full'], it_['sl']
            vms = _mask_stack(f['v'][:, sl], hpg, RW_HEAD)
            x = jnp.dot(bf(jnp.concatenate([it_['aak'], it_['ark']], axis=0)), vms,
                        preferred_element_type=F32)
            it_['arkv'] = x[n:]
            it_['rhs'] = jnp.concatenate([_mask_stack(f['at'][:, sl], hpg, RW_HEAD),
                                          _mask_stack(bf(x[:n]), hpg, RW_HEAD)], axis=1)
        for it_ in items:
            au = bf(jnp.dot(bf(it_['inv']), it_['rhs'], preferred_element_type=F32))
            it_['ah'] = au[:, :gw]
            it_['ul'] = au[:, gw:]
            it_['rhs'] = jnp.concatenate([_mask_stack(it_['ah'], hpg, RW_HEAD),
                                          _mask_stack(it_['ul'], hpg, RW_HEAD)], axis=1)
        for it_ in items:
            f, sl = it_['full'], it_['sl']
            x = jnp.dot(it_['arb'], it_['rhs'], preferred_element_type=F32)
            rh_ref[f['d'], f['rows'], sl] = bf(f['rt'][:, sl] + x[:, :gw])
            yl_ref[f['d'], f['rows'], sl] = it_['arkv'] + x[:, gw:]
        for it_ in items:
            f, sl = it_['full'], it_['sl']
            for j in range(0, gw, LANES):
                loc = slice(j, j + LANES)
                hs = slice(sl.start + j, sl.start + j + LANES)
                bh = f['bh'][:, hs]
                none = jnp.zeros((n, LANES), BF16)
                x = lax.dot_general(
                    jnp.concatenate([f['kh'][:, hs], bh], axis=0),
                    jnp.concatenate(
                        [jnp.concatenate([f['v'][:, hs], it_['ul'][:, loc]], axis=0),
                         jnp.concatenate([none, it_['ah'][:, loc]], axis=0)], axis=1),
                    (((0,), (0,)), ((), ())), preferred_element_type=F32)
                gf = x[:, :LANES]
                pm = x[:, LANES:]
                mt = jnp.where(lo_st, pm[:RW_HEAD], pm[RW_HEAD:])
                mt_ref[f['d'], f['c'], :, hs] = bf(mt + jnp.where(diag, f['w_end'][:, hs], 0.0))
                gt_ref[f['d'], f['c'], :, hs] = bf(jnp.where(lo_st, gf[:RW_HEAD], gf[RW_HEAD:]))
        return carry

    lax.fori_loop(0, ROW_TILE // (n * PREP_CHUNKS), body, 0)


def _wkv_prep(r, k, v, kk, lw, a, k_a):
    bsz, n_rows, _ = r.shape
    n_tiles = n_rows // ROW_TILE
    per = ROW_TILE // WKV_CHUNK
    dir_spec = pl.BlockSpec((2, None, ROW_TILE, D_MODEL), lambda b, t: (0, b, t, 0))
    st_spec = pl.BlockSpec((2, None, per, RW_HEAD, D_MODEL), lambda b, t: (0, b, t, 0, 0))
    rows = (2, bsz, n_rows, D_MODEL)
    st = (2, bsz, n_rows // WKV_CHUNK, RW_HEAD, D_MODEL)
    return pl.pallas_call(
        _wkv_prep_kernel,
        grid=(bsz, n_tiles),
        in_specs=[_tile_spec()] * 4 + [dir_spec, dir_spec, _const_spec((1, D_MODEL))],
        out_specs=[dir_spec, dir_spec, st_spec, st_spec],
        out_shape=[jax.ShapeDtypeStruct(rows, BF16), jax.ShapeDtypeStruct(rows, F32),
                   jax.ShapeDtypeStruct(st, BF16), jax.ShapeDtypeStruct(st, BF16)],
        compiler_params=_params(),
        name="wkv_prep",
    )(r, k, v, kk, lw, a, k_a)


def _wkv_scan_kernel(rhf_ref, ylf_ref, mtf_ref, gtf_ref, rhb_ref, ylb_ref, mtb_ref, gtb_ref,
                     yf_ref, yb_ref, state_ref):
    n = WKV_CHUNK
    per = ROW_TILE // n

    @pl.when(pl.program_id(1) == 0)
    def _():
        state_ref[...] = jnp.zeros(state_ref.shape, F32)

    lo = _lane_iota((RW_HEAD, LANES)) < RW_HEAD
    dirs = ((rhf_ref, ylf_ref, mtf_ref, gtf_ref, yf_ref), (rhb_ref, ylb_ref, mtb_ref, gtb_ref, yb_ref))
    for i in range(per):
        for d, (rh_ref, yl_ref, mt_ref, gt_ref, y_ref) in enumerate(dirs):
            c = per - 1 - i if d == 1 else i
            for hp in range(D_MODEL // LANES):
                col = slice(hp * LANES, (hp + 1) * LANES)
                lhs = jnp.concatenate([rh_ref[c * n:(c + 1) * n, col], mt_ref[c, :, col]],
                                      axis=0)
                out = jnp.dot(lhs, state_ref[d, hp].astype(BF16), preferred_element_type=F32)
                y_ref[c * n:(c + 1) * n, col] = out[:n] + yl_ref[c * n:(c + 1) * n, col]
                st = out[n:] + gt_ref[c, :, col]
                state_ref[d, hp] = jnp.concatenate(
                    [jnp.where(lo, st, 0.0), jnp.where(lo, 0.0, st)], axis=0)


def _wkv_scan(rh, yl, mt, gt, ctx_tiles):
    _, bsz, n_rows, _ = rh.shape
    n_tiles = n_rows // ROW_TILE
    per = ROW_TILE // WKV_CHUNK

    def back(t):
        return jnp.where(t < ctx_tiles, ctx_tiles - 1 - t, n_tiles - 1 - (t - ctx_tiles))

    def specs(d, tile):
        row = pl.BlockSpec((None, None, ROW_TILE, D_MODEL), lambda b, t: (d, b, tile(t), 0))
        st = pl.BlockSpec((None, None, per, RW_HEAD, D_MODEL),
                          lambda b, t: (d, b, tile(t), 0, 0))
        return [row, row, st, st]

    fwd = lambda t: t
    y_f = pl.BlockSpec((None, ROW_TILE, D_MODEL), lambda b, t: (b, t, 0))
    y_b = pl.BlockSpec((None, ROW_TILE, D_MODEL), lambda b, t: (b, back(t), 0))
    out = jax.ShapeDtypeStruct((bsz, n_rows, D_MODEL), F32)
    return pl.pallas_call(
        _wkv_scan_kernel,
        grid=(bsz, n_tiles),
        in_specs=specs(0, fwd) + specs(1, back),
        out_specs=[y_f, y_b],
        out_shape=[out, out],
        scratch_shapes=[pltpu.VMEM((2, D_MODEL // LANES, LANES, LANES), F32)],
        compiler_params=pltpu.CompilerParams(dimension_semantics=("arbitrary", "arbitrary"),
                                             vmem_limit_bytes=VMEM_LIMIT),
        name="wkv_scan",
    )(rh, yl, mt, gt, rh, yl, mt, gt)


def _rwkv_out_kernel(x_ref, yf_ref, yb_ref, bonus_ref, gate_ref, mod_ref, lg_ref, lb_ref,
                     w_ref, o_ref):
    y = yf_ref[...] + yb_ref[...]
    mu = _group_sum(y) * (1.0 / RW_HEAD)
    yc = y - mu
    var = _group_sum(yc * yc) * (1.0 / RW_HEAD)
    yn = yc * lax.rsqrt(var + LNX_EPS) * lg_ref[...] + lb_ref[...]
    z = ((yn + bonus_ref[...]) * gate_ref[...]).astype(BF16)
    out = jnp.dot(z, w_ref[...], preferred_element_type=F32)
    o_ref[...] = x_ref[...] + mod_ref[2:3, :] * out


def _rwkv_out(x, yf, yb, bonus, gate, mod, lnx_g, lnx_b, w_o, ctx_tiles, t0):
    bsz, n_rows, _ = x.shape
    n_tiles = n_rows // ROW_TILE - t0
    return pl.pallas_call(
        _rwkv_out_kernel,
        grid=(bsz, n_tiles),
        in_specs=[_tile_spec(t0)] * 5 + [_mod_spec(ctx_tiles, t0), _const_spec((1, D_MODEL)),
                                          _const_spec((1, D_MODEL)),
                                          _const_spec((D_MODEL, D_MODEL))],
        out_specs=_tile_spec(),
        out_shape=jax.ShapeDtypeStruct((bsz, n_tiles * ROW_TILE, D_MODEL), F32),
        compiler_params=_params(),
        name="rwkv_out",
    )(x, yf, yb, bonus, gate, mod, lnx_g, lnx_b, w_o)


def _rope_tables(n_ctx, n_lat):
    t = jnp.arange(n_lat)
    row_pos = (t // GRID_W).astype(F32)
    col_pos = (t % GRID_W).astype(F32)
    n_freq = DA_HEAD_DIM // 4
    inv_freq = ROPE_THETA ** (-jnp.arange(n_freq, dtype=F32) / n_freq)
    ang = jnp.concatenate([row_pos[:, None] * inv_freq, col_pos[:, None] * inv_freq], axis=-1)
    cos, sin = jnp.cos(ang), jnp.sin(ang)
    reps = LANES // DA_HEAD_DIM
    cos = jnp.tile(jnp.concatenate([cos, cos], axis=-1), (1, reps))
    sin = jnp.tile(jnp.concatenate([-sin, sin], axis=-1), (1, reps))
    cos = jnp.concatenate([jnp.ones((n_ctx, LANES), F32), cos], axis=0)
    sin = jnp.concatenate([jnp.zeros((n_ctx, LANES), F32), sin], axis=0)
    return cos, sin


def kernel(x, c, ctx, c_ctx, ada_w, ada_b, norm_mix_g, norm_ffn_g, ffn_w_up, ffn_conv_w, ffn_conv_b, ffn_w_down, da_w_qkv, da_lambda, da_subln_g, da_w_o, rw_mix_prev, rw_mix_next, rw_w_r, rw_w_k, rw_w_v, rw_w0, rw_w1, rw_w2, rw_a0, rw_a1, rw_a2, rw_v0, rw_v1, rw_v2, rw_k_k, rw_k_a, rw_r_k, rw_g1, rw_g2, rw_lnx_g, rw_lnx_b, rw_w_o, final_norm_g):
    bsz, n_lat, d = x.shape
    n_ctx = ctx.shape[1]
    depth = ada_w.shape[0]
    assert d == D_MODEL and n_lat % ROW_TILE == 0 and n_ctx % ROW_TILE == 0
    assert n_lat % GRID_W == 0
    ctx_tiles = n_ctx // ROW_TILE
    row = lambda a: a.reshape(1, -1)
    cat = lambda a: jnp.concatenate([a[0], a[1]], axis=0 if a.shape[1] == RW_HEAD else 1)

    pad = (-(bsz + 1)) % SUBLANES
    cc = jnp.concatenate([c, c_ctx[None, :], jnp.zeros((pad, d), F32)], axis=0)
    table = _ada_table(cc, ada_w, ada_b)
    mod_l = table[:, :bsz].reshape(depth, bsz, 6, d)
    mod_c = jnp.broadcast_to(table[:, bsz].reshape(depth, 1, 6, d), (depth, bsz, 6, d))
    mods = jnp.stack([mod_c, mod_l], axis=2)
    mods = jnp.pad(mods, ((0, 0), (0, 0), (0, 0), (0, SUBLANES - 6), (0, 0)))

    cos, sin = _rope_tables(n_ctx, n_lat)
    stream = jnp.concatenate([ctx, x], axis=1)
    v_first = None
    no_final = jnp.ones((1, d), F32)

    for i in range(depth):
        last = i == depth - 1
        j = i // 2
        t0 = ctx_tiles if last else 0
        mod = mods[i]
        g_mix = row(norm_mix_g[i])
        if i % 2 == 0:
            lambda_init = 0.8 - 0.6 * math.exp(-0.3 * i)
            q, k, v = _qkv(stream, mod, g_mix, da_w_qkv[j].astype(BF16), cos, sin, ctx_tiles)
            o = _flash(q, k, v, da_lambda[j], da_subln_g[j], ctx_tiles, lambda_init)
            stream = _proj_residual(stream, o, mod, da_w_o[j].astype(BF16), ctx_tiles, t0)
        else:
            p = dict(mix_prev=rw_mix_prev[j], mix_next=rw_mix_next[j],
                     w_r=rw_w_r[j].astype(BF16), w_k=rw_w_k[j].astype(BF16),
                     w_v=rw_w_v[j].astype(BF16), w0=rw_w0[j], w1=cat(rw_w1[j]).astype(BF16),
                     w2=cat(rw_w2[j]).astype(BF16), a0=rw_a0[j], a1=cat(rw_a1[j]).astype(BF16),
                     a2=cat(rw_a2[j]).astype(BF16), k_k=row(rw_k_k[j]), k_a=row(rw_k_a[j]),
                     r_k=row(rw_r_k[j]), g1=rw_g1[j].astype(BF16), g2=rw_g2[j].astype(BF16))
            if j > 0:
                n_v = rw_v1.shape[-1]
                p.update(v0=row(rw_v0[j - 1]),
                         v1=jnp.pad(rw_v1[j - 1], ((0, 0), (0, LANES - n_v))).astype(BF16),
                         v2=jnp.pad(rw_v2[j - 1], ((0, LANES - n_v), (0, 0))).astype(BF16))
            r, k, v, kk, bonus, gate, lw, a = _rwkv_proj(stream, mod, g_mix, p, v_first,
                                                         ctx_tiles)
            if v_first is None:
                v_first = v
            rh, yl, mt, gt = _wkv_prep(r, k, v, kk, lw, a, p['k_a'])
            ys = _wkv_scan(rh, yl, mt, gt, ctx_tiles)
            stream = _rwkv_out(stream, ys[0], ys[1], bonus, gate, mod, row(rw_lnx_g[j]),
                               row(rw_lnx_b[j]), rw_w_o[j].astype(BF16), ctx_tiles, t0)
        stream = _ffn(stream, mod, row(norm_ffn_g[i]), ffn_w_up[i].astype(BF16),
                      ffn_conv_w[i], row(ffn_conv_b[i]), ffn_w_down[i].astype(BF16),
                      row(final_norm_g) if last else no_final, 0 if last else ctx_tiles, last)
    return stream
```

```python
import functools
import math

import jax
import jax.numpy as jnp
from jax import lax
from jax.experimental import pallas as pl
from jax.experimental.pallas import tpu as pltpu

D_MODEL = 1024
GRID_W = 64
DA_HEAD_DIM = 64
DA_HEADS = D_MODEL // (2 * DA_HEAD_DIM)
ROPE_THETA = 10000.0
RW_HEAD = 64
N_SHIFT_MIX = 6
LNX_EPS = 64e-5
EPS = 1e-6

LANES = 128
SUBLANES = 8
ROW_TILE = 256
WKV_CHUNK = 32
FF_TILE = 256
PREP_CHUNKS = 4
VMEM_LIMIT = 56 * 1024 * 1024

F32 = jnp.float32
BF16 = jnp.bfloat16

Q_SCALE = DA_HEAD_DIM ** -0.5 * math.log2(math.e)


def _dot(a, b):
    return jnp.dot(a.astype(BF16), b.astype(BF16), preferred_element_type=F32)


def _dot_nt(a, b):
    return lax.dot_general(a.astype(BF16), b.astype(BF16), (((1,), (1,)), ((), ())),
                           preferred_element_type=F32)


def _dot_tn(a, b):
    return lax.dot_general(a.astype(BF16), b.astype(BF16), (((0,), (0,)), ((), ())),
                           preferred_element_type=F32)


def _rms(x):
    return x * lax.rsqrt(jnp.mean(x * x, axis=-1, keepdims=True) + EPS)


def _norm_mod(x, g, shift, scale):
    return _rms(x) * g * (1.0 + scale) + shift


def _sigmoid(x):
    return 0.5 * jnp.tanh(0.5 * x) + 0.5


def _lane_iota(shape):
    return lax.broadcasted_iota(jnp.int32, shape, len(shape) - 1)


def _row_iota(shape):
    return lax.broadcasted_iota(jnp.int32, shape, len(shape) - 2)


def _group_ones():
    r = lax.broadcasted_iota(jnp.int32, (LANES, LANES), 0) // RW_HEAD
    c = lax.broadcasted_iota(jnp.int32, (LANES, LANES), 1) // RW_HEAD
    return (r == c).astype(BF16)


def _group_sum(z):
    ones = _group_ones()
    parts = [jnp.dot(z[:, j:j + LANES].astype(BF16), ones, preferred_element_type=F32)
             for j in range(0, z.shape[1], LANES)]
    return jnp.concatenate(parts, axis=1)


def _shifted(h, halo_prev, halo_next, prev_ok, next_ok):
    rows = h.shape[0]
    ri = _row_iota(h.shape)
    first = jnp.where(prev_ok, halo_prev[SUBLANES - 1:SUBLANES, :], 0.0)
    last = jnp.where(next_ok, halo_next[0:1, :], 0.0)
    h_prev = jnp.where(ri == 0, first, pltpu.roll(h, 1, 0))
    h_next = jnp.where(ri == rows - 1, last, pltpu.roll(h, rows - 1, 0))
    return h_prev, h_next


def _edge_flags(t, n_tiles, ctx_tiles):
    prev_ok = jnp.logical_and(t != 0, t != ctx_tiles)
    next_ok = jnp.logical_and(t != n_tiles - 1, t != ctx_tiles - 1)
    return prev_ok, next_ok


def _const_spec(shape):
    zeros = (0,) * len(shape)
    return pl.BlockSpec(shape, lambda *_: zeros, pipeline_mode=pl.Buffered(1))


def _tile_spec(t0=0):
    return pl.BlockSpec((None, ROW_TILE, D_MODEL), lambda b, t: (b, t + t0, 0))


def _halo_specs(n_rows, t0=0):
    per = ROW_TILE // SUBLANES
    last = n_rows // SUBLANES - 1
    prev = pl.BlockSpec((None, SUBLANES, D_MODEL),
                        lambda b, t: (b, jnp.maximum((t + t0) * per - 1, 0), 0))
    nxt = pl.BlockSpec((None, SUBLANES, D_MODEL),
                       lambda b, t: (b, jnp.minimum((t + t0 + 1) * per, last), 0))
    return prev, nxt


def _mod_spec(ctx_tiles, t0=0):
    return pl.BlockSpec((None, None, SUBLANES, D_MODEL),
                        lambda b, t: (b, ((t + t0) >= ctx_tiles).astype(jnp.int32), 0, 0))


def _params():
    return pltpu.CompilerParams(vmem_limit_bytes=VMEM_LIMIT)


def _ada_kernel(c_ref, w_ref, b_ref, o_ref):
    c = c_ref[...]
    s = c * _sigmoid(c)
    o_ref[...] = jnp.dot(s, w_ref[...], precision=lax.Precision.HIGHEST,
                         preferred_element_type=F32) + b_ref[...]


def _ada_table(cc, ada_w, ada_b):
    n_layers, _, six_d = ada_w.shape
    rows = cc.shape[0]
    nb = 1536
    return pl.pallas_call(
        _ada_kernel,
        grid=(n_layers, six_d // nb),
        in_specs=[pl.BlockSpec((rows, D_MODEL), lambda l, n: (0, 0)),
                  pl.BlockSpec((None, D_MODEL, nb), lambda l, n: (l, 0, n)),
                  pl.BlockSpec((None, 1, nb), lambda l, n: (l, 0, n))],
        out_specs=pl.BlockSpec((None, rows, nb), lambda l, n: (l, 0, n)),
        out_shape=jax.ShapeDtypeStruct((n_layers, rows, six_d), F32),
        compiler_params=_params(),
        name="ada_table",
    )(cc, ada_w, ada_b.reshape(n_layers, 1, six_d))


def _rope(slab, cos, sin_signed):
    lane = _lane_iota(slab.shape)
    first = (lane % DA_HEAD_DIM) < (DA_HEAD_DIM // 2)
    partner = jnp.where(first, pltpu.roll(slab, LANES - DA_HEAD_DIM // 2, 1),
                        pltpu.roll(slab, DA_HEAD_DIM // 2, 1))
    return slab * cos + partner * sin_signed


def _qkv_kernel(x_ref, mod_ref, g_ref, w_ref, cos_ref, sin_ref, q_ref, k_ref, v_ref):
    mod = mod_ref[...]
    h = _norm_mod(x_ref[...], g_ref[...], mod[0:1], mod[1:2]).astype(BF16)
    cos = cos_ref[...]
    sin = sin_ref[...]
    wide = 2 * LANES
    for j in range(0, D_MODEL, wide):
        qa = jnp.dot(h, w_ref[:, j:j + wide], preferred_element_type=F32)
        ka = jnp.dot(h, w_ref[:, D_MODEL + j:D_MODEL + j + wide], preferred_element_type=F32)
        for i in range(0, wide, LANES):
            q_ref[:, j + i:j + i + LANES] = (
                _rope(qa[:, i:i + LANES], cos, sin) * Q_SCALE).astype(BF16)
            k_ref[:, j + i:j + i + LANES] = _rope(ka[:, i:i + LANES], cos, sin).astype(BF16)
        v_ref[:, j:j + wide] = jnp.dot(
            h, w_ref[:, 2 * D_MODEL + j:2 * D_MODEL + j + wide],
            preferred_element_type=F32).astype(BF16)


def _qkv(x, mod, g, w_qkv, cos, sin, ctx_tiles):
    bsz, n_rows, _ = x.shape
    out = jax.ShapeDtypeStruct((bsz, n_rows, D_MODEL), BF16)
    tab = pl.BlockSpec((ROW_TILE, LANES), lambda b, t: (t, 0))
    return pl.pallas_call(
        _qkv_kernel,
        grid=(bsz, n_rows // ROW_TILE),
        in_specs=[_tile_spec(), _mod_spec(ctx_tiles), _const_spec((1, D_MODEL)),
                  _const_spec((D_MODEL, 3 * D_MODEL)), tab, tab],
        out_specs=[_tile_spec(), _tile_spec(), _tile_spec()],
        out_shape=[out, out, out],
        compiler_params=_params(),
        name="attn_qkv",
    )(x, mod, g, w_qkv, cos, sin)


def _flash_kernel(lam_ref, g_ref, q_ref, qn_ref, k_ref, v_ref, o_ref,
                  sa_ref, sb_ref, ma_ref, mb_ref, vx_ref, *, ctx_tiles, lambda_init):
    t = pl.program_id(2)
    n_keys = k_ref.shape[0]
    ctx_keys = ctx_tiles * ROW_TILE
    lv = lam_ref[...]
    lam = (jnp.exp(jnp.sum(lv[0:1] * lv[1:2], axis=-1, keepdims=True))
           - jnp.exp(jnp.sum(lv[2:3] * lv[3:4], axis=-1, keepdims=True)) + lambda_init)

    @pl.when(t == 0)
    def _():
        vx_ref[:, :LANES] = v_ref[...]
        vx_ref[:, LANES:] = (_lane_iota((n_keys, LANES)) == 0).astype(BF16)

    def scores(q, n):
        lo = _lane_iota(q.shape) < DA_HEAD_DIM
        zero = jnp.zeros_like(q)
        k = k_ref[0:n, :]
        return [lax.dot_general(qc, k, (((1,), (1,)), ((), ())), preferred_element_type=F32)
                for qc in (jnp.where(lo, q, zero), jnp.where(lo, zero, q))]

    def stash(q, s_ref, m_ref, comps=(0, 1)):
        lo = _lane_iota(q.shape) < DA_HEAD_DIM
        zero = jnp.zeros_like(q)
        for c in comps:
            qc = jnp.where(lo, q, zero) if c == 0 else jnp.where(lo, zero, q)
            s = lax.dot_general(qc, k_ref[...], (((1,), (1,)), ((), ())),
                                preferred_element_type=F32)
            s_ref[c] = s
            m_ref[c] = jnp.max(s, axis=-1, keepdims=True)

    def attend(shifted, n):
        outs = []
        for x in shifted:
            p = jnp.exp2(x.astype(BF16))
            acc = jnp.dot(p, vx_ref[0:n, :], preferred_element_type=F32)
            outs.append(acc[:, :LANES] / acc[:, LANES:LANES + 1])
        o = outs[0] - lam * outs[1]
        o_ref[...] = (_rms(o) * g_ref[...] * (1.0 - lambda_init)).astype(BF16)

    @pl.when(t < ctx_tiles)
    def _():
        attend([s - jnp.max(s, axis=-1, keepdims=True) for s in scores(q_ref[...], ctx_keys)],
               ctx_keys)

    slots = ((sa_ref, ma_ref), (sb_ref, mb_ref))

    @pl.when(t == ctx_tiles)
    def _():
        stash(q_ref[...], *slots[ctx_tiles % 2])

    def shifted(slot):
        s_ref, m_ref = slots[slot]
        return [s_ref[c] - m_ref[c] for c in range(2)]

    last = pl.num_programs(2) - 1
    for parity in range(2):
        @pl.when(jnp.logical_and(jnp.logical_and(t >= ctx_tiles, t < last), t % 2 == parity))
        def _():
            stash(qn_ref[...], *slots[1 - parity], comps=(0,))
            attend(shifted(parity), n_keys)
            stash(qn_ref[...], *slots[1 - parity], comps=(1,))

        @pl.when(jnp.logical_and(jnp.logical_and(t >= ctx_tiles, t == last), t % 2 == parity))
        def _():
            attend(shifted(parity), n_keys)


def _flash(q, k, v, lam_vecs, subln_g, ctx_tiles, lambda_init):
    bsz, n_rows, _ = q.shape
    n_tiles = n_rows // ROW_TILE
    kv_spec = pl.BlockSpec((None, n_rows, LANES), lambda b, h, t: (b, 0, h))
    q_spec = pl.BlockSpec((None, ROW_TILE, LANES), lambda b, h, t: (b, t, h))
    q_next = pl.BlockSpec((None, ROW_TILE, LANES),
                          lambda b, h, t: (b, jnp.minimum(t + 1, n_tiles - 1), h))
    kern = functools.partial(_flash_kernel, ctx_tiles=ctx_tiles, lambda_init=lambda_init)
    score = pltpu.VMEM((2, ROW_TILE, n_rows), F32)
    rowmax = pltpu.VMEM((2, ROW_TILE, 1), F32)
    return pl.pallas_call(
        kern,
        grid=(bsz, DA_HEADS, n_tiles),
        in_specs=[pl.BlockSpec(lam_vecs.shape, lambda b, h, t: (0, 0)),
                  pl.BlockSpec((1, LANES), lambda b, h, t: (0, 0)),
                  q_spec, q_next, kv_spec, kv_spec],
        out_specs=q_spec,
        out_shape=jax.ShapeDtypeStruct((bsz, n_rows, D_MODEL), BF16),
        scratch_shapes=[score, score, rowmax, rowmax,
                        pltpu.VMEM((n_rows, 2 * LANES), BF16)],
        compiler_params=pltpu.CompilerParams(
            dimension_semantics=("arbitrary", "arbitrary", "arbitrary"),
            vmem_limit_bytes=VMEM_LIMIT),
        name="diff_flash",
    )(lam_vecs, subln_g.reshape(1, LANES), q, q, k, v)


def _proj_residual_kernel(x_ref, z_ref, mod_ref, w_ref, o_ref):
    out = jnp.dot(z_ref[...], w_ref[...], preferred_element_type=F32)
    o_ref[...] = x_ref[...] + mod_ref[2:3, :] * out


def _proj_residual(x, z, mod, w, ctx_tiles, t0):
    bsz, n_rows, _ = x.shape
    n_tiles = n_rows // ROW_TILE - t0
    return pl.pallas_call(
        _proj_residual_kernel,
        grid=(bsz, n_tiles),
        in_specs=[_tile_spec(t0), _tile_spec(t0), _mod_spec(ctx_tiles, t0),
                  _const_spec((D_MODEL, D_MODEL))],
        out_specs=_tile_spec(),
        out_shape=jax.ShapeDtypeStruct((bsz, n_tiles * ROW_TILE, D_MODEL), F32),
        compiler_params=_params(),
        name="proj_residual",
    )(x, z, mod, w)


def _ffn_kernel(x_ref, xp_ref, xn_ref, mod_ref, g_ref, wup_ref, cw_ref, cb_ref, wdn_ref,
                fin_ref, o_ref, *, n_tiles, ctx_tiles, d_ff, final):
    t = pl.program_id(1)
    prev_ok, next_ok = _edge_flags(t, n_tiles, ctx_tiles)
    mod = mod_ref[...]
    g = g_ref[...]
    x = x_ref[...]
    h = _norm_mod(x, g, mod[3:4], mod[4:5]).astype(BF16)
    hp = _norm_mod(xp_ref[...], g, mod[3:4], mod[4:5]).astype(BF16)
    hn = _norm_mod(xn_ref[...], g, mod[3:4], mod[4:5]).astype(BF16)
    hcat = jnp.concatenate([h, hp, hn], axis=0)
    tiles = list(range(0, d_ff, FF_TILE))

    def up(f):
        gate = jnp.dot(hcat, wup_ref[:, f:f + FF_TILE], preferred_element_type=F32)
        val = jnp.dot(h, wup_ref[:, d_ff + f:d_ff + f + FF_TILE], preferred_element_type=F32)
        return gate, val

    acc = jnp.zeros((ROW_TILE, D_MODEL), F32)
    nxt = up(tiles[0])
    for i, f in enumerate(tiles):
        gate_all, val = nxt
        if i + 1 < len(tiles):
            nxt = up(tiles[i + 1])
        gate = gate_all[:ROW_TILE]
        g_prev, g_next = _shifted(gate, gate_all[ROW_TILE:ROW_TILE + SUBLANES],
                                  gate_all[ROW_TILE + SUBLANES:], prev_ok, next_ok)
        cw = cw_ref[:, f:f + FF_TILE]
        conv = g_prev * cw[0:1] + gate * cw[1:2] + g_next * cw[2:3] + cb_ref[:, f:f + FF_TILE]
        act = conv * _sigmoid(conv) * val
        acc = acc + jnp.dot(act.astype(BF16), wdn_ref[f:f + FF_TILE, :],
                            preferred_element_type=F32)
    y = x + mod[5:6] * acc
    if final:
        y = _rms(y) * fin_ref[...]
    o_ref[...] = y


def _ffn(x, mod, g, w_up, conv_w, conv_b, w_down, final_g, ctx_tiles, final):
    bsz, n_rows, _ = x.shape
    n_tiles = n_rows // ROW_TILE
    d_ff = w_down.shape[0]
    prev, nxt = _halo_specs(n_rows)
    kern = functools.partial(_ffn_kernel, n_tiles=n_tiles, ctx_tiles=ctx_tiles, d_ff=d_ff,
                             final=final)
    return pl.pallas_call(
        kern,
        grid=(bsz, n_tiles),
        in_specs=[_tile_spec(), prev, nxt, _mod_spec(ctx_tiles), _const_spec((1, D_MODEL)),
                  _const_spec((D_MODEL, 2 * d_ff)), _const_spec((3, d_ff)),
                  _const_spec((1, d_ff)), _const_spec((d_ff, D_MODEL)),
                  _const_spec((1, D_MODEL))],
        out_specs=_tile_spec(),
        out_shape=jax.ShapeDtypeStruct((bsz, n_rows, D_MODEL), F32),
        compiler_params=_params(),
        name="conv_glu",
    )(x, x, x, mod, g, w_up, conv_w, conv_b, w_down, final_g)


def _rwkv_proj_kernel(*refs, n_tiles, ctx_tiles, has_vfirst):
    (x_ref, xp_ref, xn_ref, mod_ref, g_ref, mp_ref, mn_ref, wr_ref, wk_ref, wv_ref,
     w0_ref, w1_ref, w2_ref, a0_ref, a1_ref, a2_ref, kk_ref, ka_ref, rk_ref,
     g1_ref, g2_ref) = refs[:21]
    refs = refs[21:]
    if has_vfirst:
        vf_ref, v0_ref, v1_ref, v2_ref = refs[:4]
        refs = refs[4:]
    r_ref, k_ref, v_ref, kkn_ref, bonus_ref, gate_ref, lw_ref, a_ref = refs

    t = pl.program_id(1)
    prev_ok, next_ok = _edge_flags(t, n_tiles, ctx_tiles)
    mod = mod_ref[...]
    g = g_ref[...]
    h = _norm_mod(x_ref[...], g, mod[0:1], mod[1:2])
    hp = _norm_mod(xp_ref[...], g, mod[0:1], mod[1:2])
    hn = _norm_mod(xn_ref[...], g, mod[0:1], mod[1:2])
    h_prev, h_next = _shifted(h, hp, hn, prev_ok, next_ok)
    hb = h.astype(BF16)
    xx_p = (h_prev - h).astype(BF16)
    xx_n = (h_next - h).astype(BF16)
    mp = mp_ref[...].astype(BF16)
    mn = mn_ref[...].astype(BF16)

    def mix(m):
        return hb + xx_p * mp[m:m + 1] + xx_n * mn[m:m + 1]

    lo = _lane_iota((ROW_TILE, LANES)) < RW_HEAD

    def halves(z):
        return jnp.where(lo, z, 0.0), jnp.where(lo, 0.0, z)

    r = jnp.dot(mix(0), wr_ref[...], preferred_element_type=F32)
    r_ref[...] = r

    lw = halves(jnp.tanh(jnp.dot(mix(1), w1_ref[...], preferred_element_type=F32)))
    la = halves(jnp.dot(mix(4), a1_ref[...], preferred_element_type=F32))
    a_gate = []
    for d in range(2):
        wl = w0_ref[d:d + 1, :] + _dot(lw[d], w2_ref[...])
        lw_ref[d] = _sigmoid(wl) * (-math.exp(-0.5))
        a_d = _sigmoid(a0_ref[d:d + 1, :] + _dot(la[d], a2_ref[...]))
        a_ref[d] = a_d
        a_gate.append(a_d)

    k = jnp.dot(mix(2), wk_ref[...], preferred_element_type=F32)
    k_ref[...] = k
    kk = k * kk_ref[...]
    kkn_ref[...] = kk * lax.rsqrt(jnp.maximum(_group_sum(kk * kk), 1e-24))

    xv = mix(3)
    v = jnp.dot(xv, wv_ref[...], preferred_element_type=F32)
    if has_vfirst:
        lv = jnp.dot(xv, v1_ref[...], preferred_element_type=F32)
        v = v + (vf_ref[...] - v) * _sigmoid(v0_ref[...] + _dot(lv, v2_ref[...]))
    v_ref[...] = v

    ka = ka_ref[...]
    k_sum = k * (2.0 + (a_gate[0] + a_gate[1] - 2.0) * ka)
    bonus_ref[...] = (_group_sum(r * k_sum * rk_ref[...]) * v).astype(BF16)

    gl = _sigmoid(jnp.dot(mix(5), g1_ref[...], preferred_element_type=F32))
    gate_ref[...] = _dot(gl, g2_ref[...]).astype(BF16)


def _rwkv_proj(x, mod, g, p, v_first, ctx_tiles):
    bsz, n_rows, _ = x.shape
    n_tiles = n_rows // ROW_TILE
    prev, nxt = _halo_specs(n_rows)
    has_vfirst = v_first is not None
    args = [x, x, x, mod, g, p['mix_prev'], p['mix_next'], p['w_r'], p['w_k'], p['w_v'],
            p['w0'], p['w1'], p['w2'], p['a0'], p['a1'], p['a2'], p['k_k'], p['k_a'], p['r_k'],
            p['g1'], p['g2']]
    specs = [_tile_spec(), prev, nxt, _mod_spec(ctx_tiles)] + [
        _const_spec(a.shape) for a in args[4:]]
    if has_vfirst:
        args += [v_first, p['v0'], p['v1'], p['v2']]
        specs += [_tile_spec()] + [_const_spec(a.shape) for a in args[-3:]]
    one = jax.ShapeDtypeStruct((bsz, n_rows, D_MODEL), F32)
    half = jax.ShapeDtypeStruct((bsz, n_rows, D_MODEL), BF16)
    two = jax.ShapeDtypeStruct((2, bsz, n_rows, D_MODEL), F32)
    two_spec = pl.BlockSpec((2, None, ROW_TILE, D_MODEL), lambda b, t: (0, b, t, 0))
    kern = functools.partial(_rwkv_proj_kernel, n_tiles=n_tiles, ctx_tiles=ctx_tiles,
                             has_vfirst=has_vfirst)
    return pl.pallas_call(
        kern,
        grid=(bsz, n_tiles),
        in_specs=specs,
        out_specs=[_tile_spec()] * 6 + [two_spec, two_spec],
        out_shape=[one] * 4 + [half, half, two, two],
        compiler_params=_params(),
        name="rwkv_proj",
    )(*args)


def _mask_stack(x, n_blocks, width):
    blk = _lane_iota(x.shape) // width
    zero = jnp.zeros_like(x)
    return jnp.concatenate([jnp.where(blk == i, x, zero) for i in range(n_blocks)], axis=0)


def _wkv_prep_kernel(r_ref, k_ref, v_ref, kk_ref, lw_ref, a_ref, ka_ref,
                     rh_ref, yl_ref, mt_ref, gt_ref):
    n = WKV_CHUNK
    hpg = LANES // n
    gw = hpg * RW_HEAD
    groups = D_MODEL // gw
    ri = lax.broadcasted_iota(jnp.int32, (n, n), 0)
    ci = lax.broadcasted_iota(jnp.int32, (n, n), 1)
    row = _row_iota((n, LANES))
    src = _lane_iota((n, LANES)) % n
    eye = (src == row).astype(F32)
    order = ((ci <= ri).astype(BF16), src < row, src <= row, n - 1), \
            ((ci >= ri).astype(BF16), src > row, src >= row, 0)
    lo_st = _lane_iota((RW_HEAD, LANES)) < RW_HEAD
    diag = (_lane_iota((RW_HEAD, LANES)) % RW_HEAD) == _row_iota((RW_HEAD, LANES))
    ka = ka_ref[...]
    bf = lambda z: z.astype(BF16)

    def body(it, carry):
        items = []
        for cc in range(PREP_CHUNKS):
            c = it * PREP_CHUNKS + cc
            rows = pl.ds(pl.multiple_of(c * n, n), n)
            kk = kk_ref[rows, :]
            k = k_ref[rows, :]
            r = r_ref[rows, :]
            v = bf(v_ref[rows, :])
            for d in range(2):
                tri, strict, incl, last = order[d]
                lw = lw_ref[d, rows, :]
                gate = a_ref[d, rows, :]
                kd = k * (1.0 + (gate - 1.0) * ka)
                b = kk * gate
                lw_hi = bf(lw)
                lw_lo = bf(lw - lw_hi.astype(F32))
                cum = (jnp.dot(tri, lw_hi, preferred_element_type=F32)
                       + jnp.dot(tri, lw_lo, preferred_element_type=F32))
                cum_end = cum[last:last + 1, :]
                e_neg = jnp.exp(-cum)
                e_rem = jnp.exp(cum_end - cum)
                rt = r * jnp.exp(cum)
                full = dict(c=c, d=d, rows=rows, rt=rt, w_end=jnp.exp(cum_end), v=v,
                            at=bf(-kk * jnp.exp(cum - lw)), rtb=bf(rt), kt=bf(kd * e_neg),
                            bt=bf(b * e_neg), kh=bf(kd * e_rem), bh=bf(b * e_rem),
                            strict=strict, incl=incl)
                for g in range(groups):
                    items.append(dict(full=full, sl=slice(g * gw, (g + 1) * gw)))

        for it_ in items:
            f, sl = it_['full'], it_['sl']
            lhs = jnp.concatenate([f['at'][:, sl], f['rtb'][:, sl]], axis=0)
            rhs = jnp.concatenate([_mask_stack(f['kt'][:, sl], hpg, RW_HEAD),
                                   _mask_stack(f['bt'][:, sl], hpg, RW_HEAD)], axis=0)
            a = lax.dot_general(lhs, rhs, (((1,), (1,)), ((), ())), preferred_element_type=F32)
            it_['aak'] = jnp.where(f['strict'], a[:n, :LANES], 0.0)
            it_['ark'] = jnp.where(f['incl'], a[n:, :LANES], 0.0)
            it_['arb'] = bf(jnp.where(f['incl'], a[n:, LANES:], 0.0))
            pw = jnp.where(f['strict'], a[:n, LANES:], 0.0)
            it_['inv'] = eye + pw
            it_['pw'] = bf(pw)

        for it_ in items:
            it_['pw'] = bf(jnp.dot(it_['pw'], _mask_stack(it_['pw'], hpg, n),
                                   preferred_element_type=F32))
        steps = 2
        while steps < n:
            final = 2 * steps >= n
            for it_ in items:
                rhs = _mask_stack(bf(it_['inv']), hpg, n)
                if not final:
                    rhs = jnp.concatenate([_mask_stack(it_['pw'], hpg, n), rhs], axis=1)
                x = jnp.dot(it_['pw'], rhs, preferred_element_type=F32)
                it_['inv'] = it_['inv'] + x[:, -LANES:]
                if not final:
                    it_['pw'] = bf(x[:, :LANES])
            steps *= 2

        for it_ in items:
            f, sl = it_['full'], it_['sl']
            vms = _mask_stack(f['v'][:, sl], hpg, RW_HEAD)
            x = jnp.dot(bf(jnp.concatenate([it_['aak'], it_['ark']], axis=0)), vms,
                        preferred_element_type=F32)
            it_['arkv'] = x[n:]
            it_['rhs'] = jnp.concatenate([_mask_stack(f['at'][:, sl], hpg, RW_HEAD),
                                          _mask_stack(bf(x[:n]), hpg, RW_HEAD)], axis=1)
        for it_ in items:
            au = bf(jnp.dot(bf(it_['inv']), it_['rhs'], preferred_element_type=F32))
            it_['ah'] = au[:, :gw]
            it_['ul'] = au[:, gw:]
            it_['rhs'] = jnp.concatenate([_mask_stack(it_['ah'], hpg, RW_HEAD),
                                          _mask_stack(it_['ul'], hpg, RW_HEAD)], axis=1)
        for it_ in items:
            f, sl = it_['full'], it_['sl']
            x = jnp.dot(it_['arb'], it_['rhs'], preferred_element_type=F32)
            rh_ref[f['d'], f['rows'], sl] = bf(f['rt'][:, sl] + x[:, :gw])
            yl_ref[f['d'], f['rows'], sl] = it_['arkv'] + x[:, gw:]
        for it_ in items:
            f, sl = it_['full'], it_['sl']
            for j in range(0, gw, LANES):
                loc = slice(j, j + LANES)
                hs = slice(sl.start + j, sl.start + j + LANES)
                bh = f['bh'][:, hs]
                none = jnp.zeros((n, LANES), BF16)
                x = lax.dot_general(
                    jnp.concatenate([f['kh'][:, hs], bh], axis=0),
                    jnp.concatenate(
                        [jnp.concatenate([f['v'][:, hs], it_['ul'][:, loc]], axis=0),
                         jnp.concatenate([none, it_['ah'][:, loc]], axis=0)], axis=1),
                    (((0,), (0,)), ((), ())), preferred_element_type=F32)
                gf = x[:, :LANES]
                pm = x[:, LANES:]
                mt = jnp.where(lo_st, pm[:RW_HEAD], pm[RW_HEAD:])
                mt_ref[f['d'], f['c'], :, hs] = bf(mt + jnp.where(diag, f['w_end'][:, hs], 0.0))
                gt_ref[f['d'], f['c'], :, hs] = bf(jnp.where(lo_st, gf[:RW_HEAD], gf[RW_HEAD:]))
        return carry

    lax.fori_loop(0, ROW_TILE // (n * PREP_CHUNKS), body, 0)


def _wkv_prep(r, k, v, kk, lw, a, k_a):
    bsz, n_rows, _ = r.shape
    n_tiles = n_rows // ROW_TILE
    per = ROW_TILE // WKV_CHUNK
    dir_spec = pl.BlockSpec((2, None, ROW_TILE, D_MODEL), lambda b, t: (0, b, t, 0))
    st_spec = pl.BlockSpec((2, None, per, RW_HEAD, D_MODEL), lambda b, t: (0, b, t, 0, 0))
    rows = (2, bsz, n_rows, D_MODEL)
    st = (2, bsz, n_rows // WKV_CHUNK, RW_HEAD, D_MODEL)
    return pl.pallas_call(
        _wkv_prep_kernel,
        grid=(bsz, n_tiles),
        in_specs=[_tile_spec()] * 4 + [dir_spec, dir_spec, _const_spec((1, D_MODEL))],
        out_specs=[dir_spec, dir_spec, st_spec, st_spec],
        out_shape=[jax.ShapeDtypeStruct(rows, BF16), jax.ShapeDtypeStruct(rows, F32),
                   jax.ShapeDtypeStruct(st, BF16), jax.ShapeDtypeStruct(st, BF16)],
        compiler_params=_params(),
        name="wkv_prep",
    )(r, k, v, kk, lw, a, k_a)


def _wkv_scan_kernel(rhf_ref, ylf_ref, mtf_ref, gtf_ref, rhb_ref, ylb_ref, mtb_ref, gtb_ref,
                     yf_ref, yb_ref, state_ref):
    n = WKV_CHUNK
    per = ROW_TILE // n

    @pl.when(pl.program_id(1) == 0)
    def _():
        state_ref[...] = jnp.zeros(state_ref.shape, F32)

    lo = _lane_iota((RW_HEAD, LANES)) < RW_HEAD
    dirs = ((rhf_ref, ylf_ref, mtf_ref, gtf_ref, yf_ref), (rhb_ref, ylb_ref, mtb_ref, gtb_ref, yb_ref))
    for i in range(per):
        for d, (rh_ref, yl_ref, mt_ref, gt_ref, y_ref) in enumerate(dirs):
            c = per - 1 - i if d == 1 else i
            for hp in range(D_MODEL // LANES):
                col = slice(hp * LANES, (hp + 1) * LANES)
                lhs = jnp.concatenate([rh_ref[c * n:(c + 1) * n, col], mt_ref[c, :, col]],
                                      axis=0)
                out = jnp.dot(lhs, state_ref[d, hp].astype(BF16), preferred_element_type=F32)
                y_ref[c * n:(c + 1) * n, col] = out[:n] + yl_ref[c * n:(c + 1) * n, col]
                st = out[n:] + gt_ref[c, :, col]
                state_ref[d, hp] = jnp.concatenate(
                    [jnp.where(lo, st, 0.0), jnp.where(lo, 0.0, st)], axis=0)


def _wkv_scan(rh, yl, mt, gt, ctx_tiles):
    _, bsz, n_rows, _ = rh.shape
    n_tiles = n_rows // ROW_TILE
    per = ROW_TILE // WKV_CHUNK

    def back(t):
        return jnp.where(t < ctx_tiles, ctx_tiles - 1 - t, n_tiles - 1 - (t - ctx_tiles))

    def specs(d, tile):
        row = pl.BlockSpec((None, None, ROW_TILE, D_MODEL), lambda b, t: (d, b, tile(t), 0))
        st = pl.BlockSpec((None, None, per, RW_HEAD, D_MODEL),
                          lambda b, t: (d, b, tile(t), 0, 0))
        return [row, row, st, st]

    fwd = lambda t: t
    y_f = pl.BlockSpec((None, ROW_TILE, D_MODEL), lambda b, t: (b, t, 0))
    y_b = pl.BlockSpec((None, ROW_TILE, D_MODEL), lambda b, t: (b, back(t), 0))
    out = jax.ShapeDtypeStruct((bsz, n_rows, D_MODEL), F32)
    return pl.pallas_call(
        _wkv_scan_kernel,
        grid=(bsz, n_tiles),
        in_specs=specs(0, fwd) + specs(1, back),
        out_specs=[y_f, y_b],
        out_shape=[out, out],
        scratch_shapes=[pltpu.VMEM((2, D_MODEL // LANES, LANES, LANES), F32)],
        compiler_params=pltpu.CompilerParams(dimension_semantics=("arbitrary", "arbitrary"),
                                             vmem_limit_bytes=VMEM_LIMIT),
        name="wkv_scan",
    )(rh, yl, mt, gt, rh, yl, mt, gt)


def _rwkv_out_kernel(x_ref, yf_ref, yb_ref, bonus_ref, gate_ref, mod_ref, lg_ref, lb_ref,
                     w_ref, o_ref):
    y = yf_ref[...] + yb_ref[...]
    mu = _group_sum(y) * (1.0 / RW_HEAD)
    yc = y - mu
    var = _group_sum(yc * yc) * (1.0 / RW_HEAD)
    yn = yc * lax.rsqrt(var + LNX_EPS) * lg_ref[...] + lb_ref[...]
    z = ((yn + bonus_ref[...]) * gate_ref[...]).astype(BF16)
    out = jnp.dot(z, w_ref[...], preferred_element_type=F32)
    o_ref[...] = x_ref[...] + mod_ref[2:3, :] * out


def _rwkv_out(x, yf, yb, bonus, gate, mod, lnx_g, lnx_b, w_o, ctx_tiles, t0):
    bsz, n_rows, _ = x.shape
    n_tiles = n_rows // ROW_TILE - t0
    return pl.pallas_call(
        _rwkv_out_kernel,
        grid=(bsz, n_tiles),
        in_specs=[_tile_spec(t0)] * 5 + [_mod_spec(ctx_tiles, t0), _const_spec((1, D_MODEL)),
                                          _const_spec((1, D_MODEL)),
                                          _const_spec((D_MODEL, D_MODEL))],
        out_specs=_tile_spec(),
        out_shape=jax.ShapeDtypeStruct((bsz, n_tiles * ROW_TILE, D_MODEL), F32),
        compiler_params=_params(),
        name="rwkv_out",
    )(x, yf, yb, bonus, gate, mod, lnx_g, lnx_b, w_o)


def _rope_tables(n_ctx, n_lat):
    t = jnp.arange(n_lat)
    row_pos = (t // GRID_W).astype(F32)
    col_pos = (t % GRID_W).astype(F32)
    n_freq = DA_HEAD_DIM // 4
    inv_freq = ROPE_THETA ** (-jnp.arange(n_freq, dtype=F32) / n_freq)
    ang = jnp.concatenate([row_pos[:, None] * inv_freq, col_pos[:, None] * inv_freq], axis=-1)
    cos, sin = jnp.cos(ang), jnp.sin(ang)
    reps = LANES // DA_HEAD_DIM
    cos = jnp.tile(jnp.concatenate([cos, cos], axis=-1), (1, reps))
    sin = jnp.tile(jnp.concatenate([-sin, sin], axis=-1), (1, reps))
    cos = jnp.concatenate([jnp.ones((n_ctx, LANES), F32), cos], axis=0)
    sin = jnp.concatenate([jnp.zeros((n_ctx, LANES), F32), sin], axis=0)
    return cos, sin


def kernel(x, c, ctx, c_ctx, ada_w, ada_b, norm_mix_g, norm_ffn_g, ffn_w_up, ffn_conv_w, ffn_conv_b, ffn_w_down, da_w_qkv, da_lambda, da_subln_g, da_w_o, rw_mix_prev, rw_mix_next, rw_w_r, rw_w_k, rw_w_v, rw_w0, rw_w1, rw_w2, rw_a0, rw_a1, rw_a2, rw_v0, rw_v1, rw_v2, rw_k_k, rw_k_a, rw_r_k, rw_g1, rw_g2, rw_lnx_g, rw_lnx_b, rw_w_o, final_norm_g):
    bsz, n_lat, d = x.shape
    n_ctx = ctx.shape[1]
    depth = ada_w.shape[0]
    assert d == D_MODEL and n_lat % ROW_TILE == 0 and n_ctx % ROW_TILE == 0
    assert n_lat % GRID_W == 0
    ctx_tiles = n_ctx // ROW_TILE
    row = lambda a: a.reshape(1, -1)
    cat = lambda a: jnp.concatenate([a[0], a[1]], axis=0 if a.shape[1] == RW_HEAD else 1)

    pad = (-(bsz + 1)) % SUBLANES
    cc = jnp.concatenate([c, c_ctx[None, :], jnp.zeros((pad, d), F32)], axis=0)
    table = _ada_table(cc, ada_w, ada_b)
    mod_l = table[:, :bsz].reshape(depth, bsz, 6, d)
    mod_c = jnp.broadcast_to(table[:, bsz].reshape(depth, 1, 6, d), (depth, bsz, 6, d))
    mods = jnp.stack([mod_c, mod_l], axis=2)
    mods = jnp.pad(mods, ((0, 0), (0, 0), (0, 0), (0, SUBLANES - 6), (0, 0)))

    cos, sin = _rope_tables(n_ctx, n_lat)
    stream = jnp.concatenate([ctx, x], axis=1)
    v_first = None
    no_final = jnp.ones((1, d), F32)

    for i in range(depth):
        last = i == depth - 1
        j = i // 2
        t0 = ctx_tiles if last else 0
        mod = mods[i]
        g_mix = row(norm_mix_g[i])
        if i % 2 == 0:
            lambda_init = 0.8 - 0.6 * math.exp(-0.3 * i)
            q, k, v = _qkv(stream, mod, g_mix, da_w_qkv[j].astype(BF16), cos, sin, ctx_tiles)
            o = _flash(q, k, v, da_lambda[j], da_subln_g[j], ctx_tiles, lambda_init)
            stream = _proj_residual(stream, o, mod, da_w_o[j].astype(BF16), ctx_tiles, t0)
        else:
            p = dict(mix_prev=rw_mix_prev[j], mix_next=rw_mix_next[j],
                     w_r=rw_w_r[j].astype(BF16), w_k=rw_w_k[j].astype(BF16),
                     w_v=rw_w_v[j].astype(BF16), w0=rw_w0[j], w1=cat(rw_w1[j]).astype(BF16),
                     w2=cat(rw_w2[j]).astype(BF16), a0=rw_a0[j], a1=cat(rw_a1[j]).astype(BF16),
                     a2=cat(rw_a2[j]).astype(BF16), k_k=row(rw_k_k[j]), k_a=row(rw_k_a[j]),
                     r_k=row(rw_r_k[j]), g1=rw_g1[j].astype(BF16), g2=rw_g2[j].astype(BF16))
            if j > 0:
                n_v = rw_v1.shape[-1]
                p.update(v0=row(rw_v0[j - 1]),
                         v1=jnp.pad(rw_v1[j - 1], ((0, 0), (0, LANES - n_v))).astype(BF16),
                         v2=jnp.pad(rw_v2[j - 1], ((0, LANES - n_v), (0, 0))).astype(BF16))
            r, k, v, kk, bonus, gate, lw, a = _rwkv_proj(stream, mod, g_mix, p, v_first,
                                                         ctx_tiles)
            if v_first is None:
                v_first = v
            rh, yl, mt, gt = _wkv_prep(r, k, v, kk, lw, a, p['k_a'])
            ys = _wkv_scan(rh, yl, mt, gt, ctx_tiles)
            stream = _rwkv_out(stream, ys[0], ys[1], bonus, gate, mod, row(rw_lnx_g[j]),
                               row(rw_lnx_b[j]), rw_w_o[j].astype(BF16), ctx_tiles, t0)
        stream = _ffn(stream, mod, row(norm_ffn_g[i]), ffn_w_up[i].astype(BF16),
                      ffn_conv_w[i], row(ffn_conv_b[i]), ffn_w_down[i].astype(BF16),
                      row(final_norm_g) if last else no_final, 0 if last else ctx_tiles, last)
    return stream
```

```python
import functools
import math

import jax
import jax.numpy as jnp
from jax import lax
from jax.experimental import pallas as pl
from jax.experimental.pallas import tpu as pltpu

D_MODEL = 1024
GRID_W = 64
DA_HEAD_DIM = 64
DA_HEADS = D_MODEL // (2 * DA_HEAD_DIM)
ROPE_THETA = 10000.0
RW_HEAD = 64
N_SHIFT_MIX = 6
LNX_EPS = 64e-5
EPS = 1e-6

LANES = 128
SUBLANES = 8
ROW_TILE = 256
WKV_CHUNK = 32
FF_TILE = 256
PREP_CHUNKS = 4
VMEM_LIMIT = 56 * 1024 * 1024

F32 = jnp.float32
BF16 = jnp.bfloat16

Q_SCALE = DA_HEAD_DIM ** -0.5 * math.log2(math.e)


def _dot(a, b):
    return jnp.dot(a.astype(BF16), b.astype(BF16), preferred_element_type=F32)


def _dot_nt(a, b):
    return lax.dot_general(a.astype(BF16), b.astype(BF16), (((1,), (1,)), ((), ())),
                           preferred_element_type=F32)


def _dot_tn(a, b):
    return lax.dot_general(a.astype(BF16), b.astype(BF16), (((0,), (0,)), ((), ())),
                           preferred_element_type=F32)


def _rms(x):
    return x * lax.rsqrt(jnp.mean(x * x, axis=-1, keepdims=True) + EPS)


def _norm_mod(x, g, shift, scale):
    return _rms(x) * g * (1.0 + scale) + shift


def _sigmoid(x):
    return 0.5 * jnp.tanh(0.5 * x) + 0.5


def _lane_iota(shape):
    return lax.broadcasted_iota(jnp.int32, shape, len(shape) - 1)


def _row_iota(shape):
    return lax.broadcasted_iota(jnp.int32, shape, len(shape) - 2)


def _group_ones():
    r = lax.broadcasted_iota(jnp.int32, (LANES, LANES), 0) // RW_HEAD
    c = lax.broadcasted_iota(jnp.int32, (LANES, LANES), 1) // RW_HEAD
    return (r == c).astype(BF16)


def _group_sum(z):
    ones = _group_ones()
    parts = [jnp.dot(z[:, j:j + LANES].astype(BF16), ones, preferred_element_type=F32)
             for j in range(0, z.shape[1], LANES)]
    return jnp.concatenate(parts, axis=1)


def _shifted(h, halo_prev, halo_next, prev_ok, next_ok):
    rows = h.shape[0]
    ri = _row_iota(h.shape)
    first = jnp.where(prev_ok, halo_prev[-1:, :], 0.0)
    last = jnp.where(next_ok, halo_next[0:1, :], 0.0)
    h_prev = jnp.where(ri == 0, first, pltpu.roll(h, 1, 0))
    h_next = jnp.where(ri == rows - 1, last, pltpu.roll(h, rows - 1, 0))
    return h_prev, h_next


def _edge_flags(t, n_tiles, ctx_tiles):
    prev_ok = jnp.logical_and(t != 0, t != ctx_tiles)
    next_ok = jnp.logical_and(t != n_tiles - 1, t != ctx_tiles - 1)
    return prev_ok, next_ok


def _const_spec(shape):
    zeros = (0,) * len(shape)
    return pl.BlockSpec(shape, lambda *_: zeros, pipeline_mode=pl.Buffered(1))


def _tile_spec(t0=0):
    return pl.BlockSpec((None, ROW_TILE, D_MODEL), lambda b, t: (b, t + t0, 0))


def _halo_specs(n_rows, t0=0, rows=SUBLANES):
    per = ROW_TILE // rows
    last = n_rows // rows - 1
    prev = pl.BlockSpec((None, rows, D_MODEL),
                        lambda b, t: (b, jnp.maximum((t + t0) * per - 1, 0), 0))
    nxt = pl.BlockSpec((None, rows, D_MODEL),
                       lambda b, t: (b, jnp.minimum((t + t0 + 1) * per, last), 0))
    return prev, nxt


def _mod_spec(ctx_tiles, t0=0):
    return pl.BlockSpec((None, None, SUBLANES, D_MODEL),
                        lambda b, t: (b, ((t + t0) >= ctx_tiles).astype(jnp.int32), 0, 0))


def _params():
    return pltpu.CompilerParams(vmem_limit_bytes=VMEM_LIMIT)


def _ada_kernel(c_ref, w_ref, b_ref, o_ref):
    c = c_ref[...]
    s = c * _sigmoid(c)
    o_ref[...] = jnp.dot(s, w_ref[...], precision=lax.Precision.HIGHEST,
                         preferred_element_type=F32) + b_ref[...]


def _ada_table(cc, ada_w, ada_b):
    n_layers, _, six_d = ada_w.shape
    rows = cc.shape[0]
    nb = 1536
    return pl.pallas_call(
        _ada_kernel,
        grid=(n_layers, six_d // nb),
        in_specs=[pl.BlockSpec((rows, D_MODEL), lambda l, n: (0, 0)),
                  pl.BlockSpec((None, D_MODEL, nb), lambda l, n: (l, 0, n)),
                  pl.BlockSpec((None, 1, nb), lambda l, n: (l, 0, n))],
        out_specs=pl.BlockSpec((None, rows, nb), lambda l, n: (l, 0, n)),
        out_shape=jax.ShapeDtypeStruct((n_layers, rows, six_d), F32),
        compiler_params=_params(),
        name="ada_table",
    )(cc, ada_w, ada_b.reshape(n_layers, 1, six_d))


def _rope(slab, cos, sin_signed):
    lane = _lane_iota(slab.shape)
    first = (lane % DA_HEAD_DIM) < (DA_HEAD_DIM // 2)
    partner = jnp.where(first, pltpu.roll(slab, LANES - DA_HEAD_DIM // 2, 1),
                        pltpu.roll(slab, DA_HEAD_DIM // 2, 1))
    return slab * cos + partner * sin_signed


def _qkv_kernel(x_ref, mod_ref, g_ref, w_ref, cos_ref, sin_ref, q_ref, k_ref, v_ref):
    mod = mod_ref[...]
    h = _norm_mod(x_ref[...], g_ref[...], mod[0:1], mod[1:2]).astype(BF16)
    cos = cos_ref[...]
    sin = sin_ref[...]
    wide = 2 * LANES
    for j in range(0, D_MODEL, wide):
        qa = jnp.dot(h, w_ref[:, j:j + wide], preferred_element_type=F32)
        ka = jnp.dot(h, w_ref[:, D_MODEL + j:D_MODEL + j + wide], preferred_element_type=F32)
        for i in range(0, wide, LANES):
            q_ref[:, j + i:j + i + LANES] = (
                _rope(qa[:, i:i + LANES], cos, sin) * Q_SCALE).astype(BF16)
            k_ref[:, j + i:j + i + LANES] = _rope(ka[:, i:i + LANES], cos, sin).astype(BF16)
        v_ref[:, j:j + wide] = jnp.dot(
            h, w_ref[:, 2 * D_MODEL + j:2 * D_MODEL + j + wide],
            preferred_element_type=F32).astype(BF16)


def _qkv(x, mod, g, w_qkv, cos, sin, ctx_tiles):
    bsz, n_rows, _ = x.shape
    out = jax.ShapeDtypeStruct((bsz, n_rows, D_MODEL), BF16)
    tab = pl.BlockSpec((ROW_TILE, LANES), lambda b, t: (t, 0))
    return pl.pallas_call(
        _qkv_kernel,
        grid=(bsz, n_rows // ROW_TILE),
        in_specs=[_tile_spec(), _mod_spec(ctx_tiles), _const_spec((1, D_MODEL)),
                  _const_spec((D_MODEL, 3 * D_MODEL)), tab, tab],
        out_specs=[_tile_spec(), _tile_spec(), _tile_spec()],
        out_shape=[out, out, out],
        compiler_params=_params(),
        name="attn_qkv",
    )(x, mod, g, w_qkv, cos, sin)


def _flash_kernel(lam_ref, g_ref, q_ref, qn_ref, k_ref, v_ref, o_ref,
                  sa_ref, sb_ref, ma_ref, mb_ref, vx_ref, *, ctx_tiles, lambda_init):
    t = pl.program_id(2)
    n_keys = k_ref.shape[0]
    ctx_keys = ctx_tiles * ROW_TILE
    lv = lam_ref[...]
    lam = (jnp.exp(jnp.sum(lv[0:1] * lv[1:2], axis=-1, keepdims=True))
           - jnp.exp(jnp.sum(lv[2:3] * lv[3:4], axis=-1, keepdims=True)) + lambda_init)

    @pl.when(t == 0)
    def _():
        vx_ref[:, :LANES] = v_ref[...]
        vx_ref[:, LANES:] = (_lane_iota((n_keys, LANES)) == 0).astype(BF16)

    def scores(q, n):
        lo = _lane_iota(q.shape) < DA_HEAD_DIM
        zero = jnp.zeros_like(q)
        k = k_ref[0:n, :]
        return [lax.dot_general(qc, k, (((1,), (1,)), ((), ())), preferred_element_type=F32)
                for qc in (jnp.where(lo, q, zero), jnp.where(lo, zero, q))]

    def stash(q, s_ref, m_ref, comps=(0, 1)):
        lo = _lane_iota(q.shape) < DA_HEAD_DIM
        zero = jnp.zeros_like(q)
        for c in comps:
            qc = jnp.where(lo, q, zero) if c == 0 else jnp.where(lo, zero, q)
            s = lax.dot_general(qc, k_ref[...], (((1,), (1,)), ((), ())),
                                preferred_element_type=F32)
            s_ref[c] = s
            m_ref[c] = jnp.max(s, axis=-1, keepdims=True)

    def attend(shifted, n):
        outs = []
        for x in shifted:
            p = jnp.exp2(x.astype(BF16))
            acc = jnp.dot(p, vx_ref[0:n, :], preferred_element_type=F32)
            outs.append(acc[:, :LANES] / acc[:, LANES:LANES + 1])
        o = outs[0] - lam * outs[1]
        o_ref[...] = (_rms(o) * g_ref[...] * (1.0 - lambda_init)).astype(BF16)

    @pl.when(t < ctx_tiles)
    def _():
        attend([s - jnp.max(s, axis=-1, keepdims=True) for s in scores(q_ref[...], ctx_keys)],
               ctx_keys)

    slots = ((sa_ref, ma_ref), (sb_ref, mb_ref))

    @pl.when(t == ctx_tiles)
    def _():
        stash(q_ref[...], *slots[ctx_tiles % 2])

    def shifted(slot):
        s_ref, m_ref = slots[slot]
        return [s_ref[c] - m_ref[c] for c in range(2)]

    last = pl.num_programs(2) - 1
    for parity in range(2):
        @pl.when(jnp.logical_and(jnp.logical_and(t >= ctx_tiles, t < last), t % 2 == parity))
        def _():
            stash(qn_ref[...], *slots[1 - parity], comps=(0,))
            attend(shifted(parity), n_keys)
            stash(qn_ref[...], *slots[1 - parity], comps=(1,))

        @pl.when(jnp.logical_and(jnp.logical_and(t >= ctx_tiles, t == last), t % 2 == parity))
        def _():
            attend(shifted(parity), n_keys)


def _flash(q, k, v, lam_vecs, subln_g, ctx_tiles, lambda_init):
    bsz, n_rows, _ = q.shape
    n_tiles = n_rows // ROW_TILE
    kv_spec = pl.BlockSpec((None, n_rows, LANES), lambda b, h, t: (b, 0, h))
    q_spec = pl.BlockSpec((None, ROW_TILE, LANES), lambda b, h, t: (b, t, h))
    q_next = pl.BlockSpec((None, ROW_TILE, LANES),
                          lambda b, h, t: (b, jnp.minimum(t + 1, n_tiles - 1), h))
    kern = functools.partial(_flash_kernel, ctx_tiles=ctx_tiles, lambda_init=lambda_init)
    score = pltpu.VMEM((2, ROW_TILE, n_rows), F32)
    rowmax = pltpu.VMEM((2, ROW_TILE, 1), F32)
    return pl.pallas_call(
        kern,
        grid=(bsz, DA_HEADS, n_tiles),
        in_specs=[pl.BlockSpec(lam_vecs.shape, lambda b, h, t: (0, 0)),
                  pl.BlockSpec((1, LANES), lambda b, h, t: (0, 0)),
                  q_spec, q_next, kv_spec, kv_spec],
        out_specs=q_spec,
        out_shape=jax.ShapeDtypeStruct((bsz, n_rows, D_MODEL), BF16),
        scratch_shapes=[score, score, rowmax, rowmax,
                        pltpu.VMEM((n_rows, 2 * LANES), BF16)],
        compiler_params=pltpu.CompilerParams(
            dimension_semantics=("arbitrary", "arbitrary", "arbitrary"),
            vmem_limit_bytes=VMEM_LIMIT),
        name="diff_flash",
    )(lam_vecs, subln_g.reshape(1, LANES), q, q, k, v)


def _proj_residual_kernel(x_ref, z_ref, mod_ref, w_ref, o_ref):
    out = jnp.dot(z_ref[...], w_ref[...], preferred_element_type=F32)
    o_ref[...] = x_ref[...] + mod_ref[2:3, :] * out


def _proj_residual(x, z, mod, w, ctx_tiles, t0):
    bsz, n_rows, _ = x.shape
    n_tiles = n_rows // ROW_TILE - t0
    return pl.pallas_call(
        _proj_residual_kernel,
        grid=(bsz, n_tiles),
        in_specs=[_tile_spec(t0), _tile_spec(t0), _mod_spec(ctx_tiles, t0),
                  _const_spec((D_MODEL, D_MODEL))],
        out_specs=_tile_spec(),
        out_shape=jax.ShapeDtypeStruct((bsz, n_tiles * ROW_TILE, D_MODEL), F32),
        compiler_params=_params(),
        name="proj_residual",
    )(x, z, mod, w)


def _ffn_kernel(*refs, n_tiles, ctx_tiles, d_ff, final, mixer_out):
    x_ref, xp_ref, xn_ref = refs[:3]
    refs = refs[3:]
    if mixer_out:
        z_ref, zp_ref, zn_ref, wz_ref = refs[:4]
        refs = refs[4:]
    mod_ref, g_ref, wup_ref, cw_ref, cb_ref, wdn_ref, fin_ref, o_ref = refs
    t = pl.program_id(1)
    prev_ok, next_ok = _edge_flags(t, n_tiles, ctx_tiles)
    mod = mod_ref[...]
    g = g_ref[...]
    xcat = jnp.concatenate([x_ref[...], xp_ref[...], xn_ref[...]], axis=0)
    if mixer_out:
        zcat = jnp.concatenate([z_ref[...], zp_ref[...], zn_ref[...]], axis=0)
        xcat = xcat + mod[2:3] * jnp.dot(zcat, wz_ref[...], preferred_element_type=F32)
    x = xcat[:ROW_TILE]
    hcat = _norm_mod(xcat, g, mod[3:4], mod[4:5]).astype(BF16)
    h = hcat[:ROW_TILE]
    halo = (xcat.shape[0] - ROW_TILE) // 2
    tiles = list(range(0, d_ff, FF_TILE))

    def up(f):
        gate = jnp.dot(hcat, wup_ref[:, f:f + FF_TILE], preferred_element_type=F32)
        val = jnp.dot(h, wup_ref[:, d_ff + f:d_ff + f + FF_TILE], preferred_element_type=F32)
        return gate, val

    acc = jnp.zeros((ROW_TILE, D_MODEL), F32)
    nxt = up(tiles[0])
    for i, f in enumerate(tiles):
        gate_all, val = nxt
        if i + 1 < len(tiles):
            nxt = up(tiles[i + 1])
        gate = gate_all[:ROW_TILE]
        g_prev, g_next = _shifted(gate, gate_all[ROW_TILE:ROW_TILE + halo],
                                  gate_all[ROW_TILE + halo:], prev_ok, next_ok)
        cw = cw_ref[:, f:f + FF_TILE]
        conv = g_prev * cw[0:1] + gate * cw[1:2] + g_next * cw[2:3] + cb_ref[:, f:f + FF_TILE]
        act = conv * _sigmoid(conv) * val
        acc = acc + jnp.dot(act.astype(BF16), wdn_ref[f:f + FF_TILE, :],
                            preferred_element_type=F32)
    y = x + mod[5:6] * acc
    if final:
        y = _rms(y) * fin_ref[...]
    o_ref[...] = y


def _ffn(x, mod, g, w_up, conv_w, conv_b, w_down, final_g, ctx_tiles, final, mixer=None):
    bsz, n_rows, _ = x.shape
    n_tiles = n_rows // ROW_TILE
    d_ff = w_down.shape[0]
    halo = SUBLANES if mixer is None else 2 * SUBLANES
    prev, nxt = _halo_specs(n_rows, rows=halo)
    args, specs = [x, x, x], [_tile_spec(), prev, nxt]
    if mixer is not None:
        args += [mixer[0]] * 3 + [mixer[1]]
        specs += [_tile_spec(), prev, nxt, _const_spec((D_MODEL, D_MODEL))]
    args += [mod, g, w_up, conv_w, conv_b, w_down, final_g]
    specs += [_mod_spec(ctx_tiles), _const_spec((1, D_MODEL)),
              _const_spec((D_MODEL, 2 * d_ff)), _const_spec((3, d_ff)),
              _const_spec((1, d_ff)), _const_spec((d_ff, D_MODEL)), _const_spec((1, D_MODEL))]
    kern = functools.partial(_ffn_kernel, n_tiles=n_tiles, ctx_tiles=ctx_tiles, d_ff=d_ff,
                             final=final, mixer_out=mixer is not None)
    return pl.pallas_call(
        kern,
        grid=(bsz, n_tiles),
        in_specs=specs,
        out_specs=_tile_spec(),
        out_shape=jax.ShapeDtypeStruct((bsz, n_rows, D_MODEL), F32),
        compiler_params=_params(),
        name="conv_glu",
    )(*args)


def _rwkv_proj_kernel(*refs, n_tiles, ctx_tiles, has_vfirst):
    (x_ref, xp_ref, xn_ref, mod_ref, g_ref, mp_ref, mn_ref, wr_ref, wk_ref, wv_ref,
     w0_ref, w1_ref, w2_ref, a0_ref, a1_ref, a2_ref, kk_ref, ka_ref, rk_ref,
     g1_ref, g2_ref) = refs[:21]
    refs = refs[21:]
    if has_vfirst:
        vf_ref, v0_ref, v1_ref, v2_ref = refs[:4]
        refs = refs[4:]
    r_ref, k_ref, v_ref, kkn_ref, bonus_ref, gate_ref, lw_ref, a_ref = refs

    t = pl.program_id(1)
    prev_ok, next_ok = _edge_flags(t, n_tiles, ctx_tiles)
    mod = mod_ref[...]
    g = g_ref[...]
    h = _norm_mod(x_ref[...], g, mod[0:1], mod[1:2])
    hp = _norm_mod(xp_ref[...], g, mod[0:1], mod[1:2])
    hn = _norm_mod(xn_ref[...], g, mod[0:1], mod[1:2])
    h_prev, h_next = _shifted(h, hp, hn, prev_ok, next_ok)
    hb = h.astype(BF16)
    xx_p = (h_prev - h).astype(BF16)
    xx_n = (h_next - h).astype(BF16)
    mp = mp_ref[...].astype(BF16)
    mn = mn_ref[...].astype(BF16)

    def mix(m):
        return hb + xx_p * mp[m:m + 1] + xx_n * mn[m:m + 1]

    lo = _lane_iota((ROW_TILE, LANES)) < RW_HEAD

    def halves(z):
        return jnp.where(lo, z, 0.0), jnp.where(lo, 0.0, z)

    r = jnp.dot(mix(0), wr_ref[...], preferred_element_type=F32)
    r_ref[...] = r

    lw = halves(jnp.tanh(jnp.dot(mix(1), w1_ref[...], preferred_element_type=F32)))
    la = halves(jnp.dot(mix(4), a1_ref[...], preferred_element_type=F32))
    a_gate = []
    for d in range(2):
        wl = w0_ref[d:d + 1, :] + _dot(lw[d], w2_ref[...])
        lw_ref[d] = _sigmoid(wl) * (-math.exp(-0.5))
        a_d = _sigmoid(a0_ref[d:d + 1, :] + _dot(la[d], a2_ref[...]))
        a_ref[d] = a_d
        a_gate.append(a_d)

    k = jnp.dot(mix(2), wk_ref[...], preferred_element_type=F32)
    k_ref[...] = k
    kk = k * kk_ref[...]
    kkn_ref[...] = kk * lax.rsqrt(jnp.maximum(_group_sum(kk * kk), 1e-24))

    xv = mix(3)
    v = jnp.dot(xv, wv_ref[...], preferred_element_type=F32)
    if has_vfirst:
        lv = jnp.dot(xv, v1_ref[...], preferred_element_type=F32)
        v = v + (vf_ref[...] - v) * _sigmoid(v0_ref[...] + _dot(lv, v2_ref[...]))
    v_ref[...] = v

    ka = ka_ref[...]
    k_sum = k * (2.0 + (a_gate[0] + a_gate[1] - 2.0) * ka)
    bonus_ref[...] = (_group_sum(r * k_sum * rk_ref[...]) * v).astype(BF16)

    gl = _sigmoid(jnp.dot(mix(5), g1_ref[...], preferred_element_type=F32))
    gate_ref[...] = _dot(gl, g2_ref[...]).astype(BF16)


def _rwkv_proj(x, mod, g, p, v_first, ctx_tiles):
    bsz, n_rows, _ = x.shape
    n_tiles = n_rows // ROW_TILE
    prev, nxt = _halo_specs(n_rows)
    has_vfirst = v_first is not None
    args = [x, x, x, mod, g, p['mix_prev'], p['mix_next'], p['w_r'], p['w_k'], p['w_v'],
            p['w0'], p['w1'], p['w2'], p['a0'], p['a1'], p['a2'], p['k_k'], p['k_a'], p['r_k'],
            p['g1'], p['g2']]
    specs = [_tile_spec(), prev, nxt, _mod_spec(ctx_tiles)] + [
        _const_spec(a.shape) for a in args[4:]]
    if has_vfirst:
        args += [v_first, p['v0'], p['v1'], p['v2']]
        specs += [_tile_spec()] + [_const_spec(a.shape) for a in args[-3:]]
    one = jax.ShapeDtypeStruct((bsz, n_rows, D_MODEL), F32)
    half = jax.ShapeDtypeStruct((bsz, n_rows, D_MODEL), BF16)
    two = jax.ShapeDtypeStruct((2, bsz, n_rows, D_MODEL), F32)
    two_spec = pl.BlockSpec((2, None, ROW_TILE, D_MODEL), lambda b, t: (0, b, t, 0))
    kern = functools.partial(_rwkv_proj_kernel, n_tiles=n_tiles, ctx_tiles=ctx_tiles,
                             has_vfirst=has_vfirst)
    return pl.pallas_call(
        kern,
        grid=(bsz, n_tiles),
        in_specs=specs,
        out_specs=[_tile_spec()] * 6 + [two_spec, two_spec],
        out_shape=[one] * 4 + [half, half, two, two],
        compiler_params=_params(),
        name="rwkv_proj",
    )(*args)


def _mask_stack(x, n_blocks, width):
    blk = _lane_iota(x.shape) // width
    zero = jnp.zeros_like(x)
    return jnp.concatenate([jnp.where(blk == i, x, zero) for i in range(n_blocks)], axis=0)


def _wkv_prep_kernel(r_ref, k_ref, v_ref, kk_ref, lw_ref, a_ref, ka_ref,
                     rh_ref, yl_ref, mt_ref, gt_ref):
    n = WKV_CHUNK
    hpg = LANES // n
    gw = hpg * RW_HEAD
    groups = D_MODEL // gw
    ri = lax.broadcasted_iota(jnp.int32, (n, n), 0)
    ci = lax.broadcasted_iota(jnp.int32, (n, n), 1)
    row = _row_iota((n, LANES))
    src = _lane_iota((n, LANES)) % n
    eye = (src == row).astype(F32)
    order = ((ci <= ri).astype(BF16), src < row, src <= row, n - 1), \
            ((ci >= ri).astype(BF16), src > row, src >= row, 0)
    lo_st = _lane_iota((RW_HEAD, LANES)) < RW_HEAD
    diag = (_lane_iota((RW_HEAD, LANES)) % RW_HEAD) == _row_iota((RW_HEAD, LANES))
    ka = ka_ref[...]
    bf = lambda z: z.astype(BF16)

    def body(it, carry):
        items = []
        for cc in range(PREP_CHUNKS):
            c = it * PREP_CHUNKS + cc
            rows = pl.ds(pl.multiple_of(c * n, n), n)
            kk = kk_ref[rows, :]
            k = k_ref[rows, :]
            r = r_ref[rows, :]
            v = bf(v_ref[rows, :])
            for d in range(2):
                tri, strict, incl, last = order[d]
                lw = lw_ref[d, rows, :]
                gate = a_ref[d, rows, :]
                kd = k * (1.0 + (gate - 1.0) * ka)
                b = kk * gate
                lw_hi = bf(lw)
                lw_lo = bf(lw - lw_hi.astype(F32))
                cum = (jnp.dot(tri, lw_hi, preferred_element_type=F32)
                       + jnp.dot(tri, lw_lo, preferred_element_type=F32))
                cum_end = cum[last:last + 1, :]
                e_neg = jnp.exp(-cum)
                e_rem = jnp.exp(cum_end - cum)
                rt = r * jnp.exp(cum)
                full = dict(c=c, d=d, rows=rows, rt=rt, w_end=jnp.exp(cum_end), v=v,
                            at=bf(-kk * jnp.exp(cum - lw)), rtb=bf(rt), kt=bf(kd * e_neg),
                            bt=bf(b * e_neg), kh=bf(kd * e_rem), bh=bf(b * e_rem),
                            strict=strict, incl=incl)
                for g in range(groups):
                    items.append(dict(full=full, sl=slice(g * gw, (g + 1) * gw)))

        for it_ in items:
            f, sl = it_['full'], it_['sl']
            lhs = jnp.concatenate([f['at'][:, sl], f['rtb'][:, sl]], axis=0)
            rhs = jnp.concatenate([_mask_stack(f['kt'][:, sl], hpg, RW_HEAD),
                                   _mask_stack(f['bt'][:, sl], hpg, RW_HEAD)], axis=0)
            a = lax.dot_general(lhs, rhs, (((1,), (1,)), ((), ())), preferred_element_type=F32)
            it_['aak'] = jnp.where(f['strict'], a[:n, :LANES], 0.0)
            it_['ark'] = jnp.where(f['incl'], a[n:, :LANES], 0.0)
            it_['arb'] = bf(jnp.where(f['incl'], a[n:, LANES:], 0.0))
            pw = jnp.where(f['strict'], a[:n, LANES:], 0.0)
            it_['inv'] = eye + pw
            it_['pw'] = bf(pw)

        for it_ in items:
            it_['pw'] = bf(jnp.dot(it_['pw'], _mask_stack(it_['pw'], hpg, n),
                                   preferred_element_type=F32))
        steps = 2
        while steps < n:
            final = 2 * steps >= n
            for it_ in items:
                rhs = _mask_stack(bf(it_['inv']), hpg, n)
                if not final:
                    rhs = jnp.concatenate([_mask_stack(it_['pw'], hpg, n), rhs], axis=1)
                x = jnp.dot(it_['pw'], rhs, preferred_element_type=F32)
                it_['inv'] = it_['inv'] + x[:, -LANES:]
                if not final:
                    it_['pw'] = bf(x[:, :LANES])
            steps *= 2

        for it_ in items:
            f, sl = it_['full'], it_['sl']
            vms = _mask_stack(f['v'][:, sl], hpg, RW_HEAD)
            x = jnp.dot(bf(jnp.concatenate([it_['aak'], it_['ark']], axis=0)), vms,
                        preferred_element_type=F32)
            it_['arkv'] = x[n:]
            it_['rhs'] = jnp.concatenate([_mask_stack(f['at'][:, sl], hpg, RW_HEAD),
                                          _mask_stack(bf(x[:n]), hpg, RW_HEAD)], axis=1)
        for it_ in items:
            au = bf(jnp.dot(bf(it_['inv']), it_['rhs'], preferred_element_type=F32))
            it_['ah'] = au[:, :gw]
            it_['ul'] = au[:, gw:]
            it_['rhs'] = jnp.concatenate([_mask_stack(it_['ah'], hpg, RW_HEAD),
                                          _mask_stack(it_['ul'], hpg, RW_HEAD)], axis=1)
        for it_ in items:
            f, sl = it_['full'], it_['sl']
            x = jnp.dot(it_['arb'], it_['rhs'], preferred_element_type=F32)
            rh_ref[f['d'], f['rows'], sl] = bf(f['rt'][:, sl] + x[:, :gw])
            yl_ref[f['d'], f['rows'], sl] = it_['arkv'] + x[:, gw:]
        for it_ in items:
            f, sl = it_['full'], it_['sl']
            for j in range(0, gw, LANES):
                loc = slice(j, j + LANES)
                hs = slice(sl.start + j, sl.start + j + LANES)
                bh = f['bh'][:, hs]
                none = jnp.zeros((n, LANES), BF16)
                x = lax.dot_general(
                    jnp.concatenate([f['kh'][:, hs], bh], axis=0),
                    jnp.concatenate(
                        [jnp.concatenate([f['v'][:, hs], it_['ul'][:, loc]], axis=0),
                         jnp.concatenate([none, it_['ah'][:, loc]], axis=0)], axis=1),
                    (((0,), (0,)), ((), ())), preferred_element_type=F32)
                gf = x[:, :LANES]
                pm = x[:, LANES:]
                mt = jnp.where(lo_st, pm[:RW_HEAD], pm[RW_HEAD:])
                mt_ref[f['d'], f['c'], :, hs] = bf(mt + jnp.where(diag, f['w_end'][:, hs], 0.0))
                gt_ref[f['d'], f['c'], :, hs] = bf(jnp.where(lo_st, gf[:RW_HEAD], gf[RW_HEAD:]))
        return carry

    lax.fori_loop(0, ROW_TILE // (n * PREP_CHUNKS), body, 0)


def _wkv_prep(r, k, v, kk, lw, a, k_a):
    bsz, n_rows, _ = r.shape
    n_tiles = n_rows // ROW_TILE
    per = ROW_TILE // WKV_CHUNK
    dir_spec = pl.BlockSpec((2, None, ROW_TILE, D_MODEL), lambda b, t: (0, b, t, 0))
    st_spec = pl.BlockSpec((2, None, per, RW_HEAD, D_MODEL), lambda b, t: (0, b, t, 0, 0))
    rows = (2, bsz, n_rows, D_MODEL)
    st = (2, bsz, n_rows // WKV_CHUNK, RW_HEAD, D_MODEL)
    return pl.pallas_call(
        _wkv_prep_kernel,
        grid=(bsz, n_tiles),
        in_specs=[_tile_spec()] * 4 + [dir_spec, dir_spec, _const_spec((1, D_MODEL))],
        out_specs=[dir_spec, dir_spec, st_spec, st_spec],
        out_shape=[jax.ShapeDtypeStruct(rows, BF16), jax.ShapeDtypeStruct(rows, F32),
                   jax.ShapeDtypeStruct(st, BF16), jax.ShapeDtypeStruct(st, BF16)],
        compiler_params=_params(),
        name="wkv_prep",
    )(r, k, v, kk, lw, a, k_a)


def _wkv_scan_kernel(rhf_ref, ylf_ref, mtf_ref, gtf_ref, rhb_ref, ylb_ref, mtb_ref, gtb_ref,
                     yf_ref, yb_ref, state_ref):
    n = WKV_CHUNK
    per = ROW_TILE // n

    @pl.when(pl.program_id(1) == 0)
    def _():
        state_ref[...] = jnp.zeros(state_ref.shape, F32)

    lo = _lane_iota((RW_HEAD, LANES)) < RW_HEAD
    dirs = ((rhf_ref, ylf_ref, mtf_ref, gtf_ref, yf_ref), (rhb_ref, ylb_ref, mtb_ref, gtb_ref, yb_ref))
    for i in range(per):
        for d, (rh_ref, yl_ref, mt_ref, gt_ref, y_ref) in enumerate(dirs):
            c = per - 1 - i if d == 1 else i
            for hp in range(D_MODEL // LANES):
                col = slice(hp * LANES, (hp + 1) * LANES)
                lhs = jnp.concatenate([rh_ref[c * n:(c + 1) * n, col], mt_ref[c, :, col]],
                                      axis=0)
                out = jnp.dot(lhs, state_ref[d, hp].astype(BF16), preferred_element_type=F32)
                y_ref[c * n:(c + 1) * n, col] = out[:n] + yl_ref[c * n:(c + 1) * n, col]
                st = out[n:] + gt_ref[c, :, col]
                state_ref[d, hp] = jnp.concatenate(
                    [jnp.where(lo, st, 0.0), jnp.where(lo, 0.0, st)], axis=0)


def _wkv_scan(rh, yl, mt, gt, ctx_tiles):
    _, bsz, n_rows, _ = rh.shape
    n_tiles = n_rows // ROW_TILE
    per = ROW_TILE // WKV_CHUNK

    def back(t):
        return jnp.where(t < ctx_tiles, ctx_tiles - 1 - t, n_tiles - 1 - (t - ctx_tiles))

    def specs(d, tile):
        row = pl.BlockSpec((None, None, ROW_TILE, D_MODEL), lambda b, t: (d, b, tile(t), 0))
        st = pl.BlockSpec((None, None, per, RW_HEAD, D_MODEL),
                          lambda b, t: (d, b, tile(t), 0, 0))
        return [row, row, st, st]

    fwd = lambda t: t
    y_f = pl.BlockSpec((None, ROW_TILE, D_MODEL), lambda b, t: (b, t, 0))
    y_b = pl.BlockSpec((None, ROW_TILE, D_MODEL), lambda b, t: (b, back(t), 0))
    out = jax.ShapeDtypeStruct((bsz, n_rows, D_MODEL), F32)
    return pl.pallas_call(
        _wkv_scan_kernel,
        grid=(bsz, n_tiles),
        in_specs=specs(0, fwd) + specs(1, back),
        out_specs=[y_f, y_b],
        out_shape=[out, out],
        scratch_shapes=[pltpu.VMEM((2, D_MODEL // LANES, LANES, LANES), F32)],
        compiler_params=pltpu.CompilerParams(dimension_semantics=("arbitrary", "arbitrary"),
                                             vmem_limit_bytes=VMEM_LIMIT),
        name="wkv_scan",
    )(rh, yl, mt, gt, rh, yl, mt, gt)


def _rwkv_out_kernel(x_ref, yf_ref, yb_ref, bonus_ref, gate_ref, mod_ref, lg_ref, lb_ref,
                     w_ref, o_ref):
    y = yf_ref[...] + yb_ref[...]
    mu = _group_sum(y) * (1.0 / RW_HEAD)
    yc = y - mu
    var = _group_sum(yc * yc) * (1.0 / RW_HEAD)
    yn = yc * lax.rsqrt(var + LNX_EPS) * lg_ref[...] + lb_ref[...]
    z = ((yn + bonus_ref[...]) * gate_ref[...]).astype(BF16)
    out = jnp.dot(z, w_ref[...], preferred_element_type=F32)
    o_ref[...] = x_ref[...] + mod_ref[2:3, :] * out


def _rwkv_out(x, yf, yb, bonus, gate, mod, lnx_g, lnx_b, w_o, ctx_tiles, t0):
    bsz, n_rows, _ = x.shape
    n_tiles = n_rows // ROW_TILE - t0
    return pl.pallas_call(
        _rwkv_out_kernel,
        grid=(bsz, n_tiles),
        in_specs=[_tile_spec(t0)] * 5 + [_mod_spec(ctx_tiles, t0), _const_spec((1, D_MODEL)),
                                          _const_spec((1, D_MODEL)),
                                          _const_spec((D_MODEL, D_MODEL))],
        out_specs=_tile_spec(),
        out_shape=jax.ShapeDtypeStruct((bsz, n_tiles * ROW_TILE, D_MODEL), F32),
        compiler_params=_params(),
        name="rwkv_out",
    )(x, yf, yb, bonus, gate, mod, lnx_g, lnx_b, w_o)


def _rope_tables(n_ctx, n_lat):
    t = jnp.arange(n_lat)
    row_pos = (t // GRID_W).astype(F32)
    col_pos = (t % GRID_W).astype(F32)
    n_freq = DA_HEAD_DIM // 4
    inv_freq = ROPE_THETA ** (-jnp.arange(n_freq, dtype=F32) / n_freq)
    ang = jnp.concatenate([row_pos[:, None] * inv_freq, col_pos[:, None] * inv_freq], axis=-1)
    cos, sin = jnp.cos(ang), jnp.sin(ang)
    reps = LANES // DA_HEAD_DIM
    cos = jnp.tile(jnp.concatenate([cos, cos], axis=-1), (1, reps))
    sin = jnp.tile(jnp.concatenate([-sin, sin], axis=-1), (1, reps))
    cos = jnp.concatenate([jnp.ones((n_ctx, LANES), F32), cos], axis=0)
    sin = jnp.concatenate([jnp.zeros((n_ctx, LANES), F32), sin], axis=0)
    return cos, sin


def kernel(x, c, ctx, c_ctx, ada_w, ada_b, norm_mix_g, norm_ffn_g, ffn_w_up, ffn_conv_w, ffn_conv_b, ffn_w_down, da_w_qkv, da_lambda, da_subln_g, da_w_o, rw_mix_prev, rw_mix_next, rw_w_r, rw_w_k, rw_w_v, rw_w0, rw_w1, rw_w2, rw_a0, rw_a1, rw_a2, rw_v0, rw_v1, rw_v2, rw_k_k, rw_k_a, rw_r_k, rw_g1, rw_g2, rw_lnx_g, rw_lnx_b, rw_w_o, final_norm_g):
    bsz, n_lat, d = x.shape
    n_ctx = ctx.shape[1]
    depth = ada_w.shape[0]
    assert d == D_MODEL and n_lat % ROW_TILE == 0 and n_ctx % ROW_TILE == 0
    assert n_lat % GRID_W == 0
    ctx_tiles = n_ctx // ROW_TILE
    row = lambda a: a.reshape(1, -1)
    cat = lambda a: jnp.concatenate([a[0], a[1]], axis=0 if a.shape[1] == RW_HEAD else 1)

    pad = (-(bsz + 1)) % SUBLANES
    cc = jnp.concatenate([c, c_ctx[None, :], jnp.zeros((pad, d), F32)], axis=0)
    table = _ada_table(cc, ada_w, ada_b)
    mod_l = table[:, :bsz].reshape(depth, bsz, 6, d)
    mod_c = jnp.broadcast_to(table[:, bsz].reshape(depth, 1, 6, d), (depth, bsz, 6, d))
    mods = jnp.stack([mod_c, mod_l], axis=2)
    mods = jnp.pad(mods, ((0, 0), (0, 0), (0, 0), (0, SUBLANES - 6), (0, 0)))

    cos, sin = _rope_tables(n_ctx, n_lat)
    stream = jnp.concatenate([ctx, x], axis=1)
    v_first = None
    no_final = jnp.ones((1, d), F32)

    for i in range(depth):
        last = i == depth - 1
        j = i // 2
        t0 = ctx_tiles if last else 0
        mod = mods[i]
        g_mix = row(norm_mix_g[i])
        mixer = None
        if i % 2 == 0:
            lambda_init = 0.8 - 0.6 * math.exp(-0.3 * i)
            q, k, v = _qkv(stream, mod, g_mix, da_w_qkv[j].astype(BF16), cos, sin, ctx_tiles)
            o = _flash(q, k, v, da_lambda[j], da_subln_g[j], ctx_tiles, lambda_init)
            if last:
                stream = _proj_residual(stream, o, mod, da_w_o[j].astype(BF16), ctx_tiles, t0)
            else:
                mixer = (o, da_w_o[j].astype(BF16))
        else:
            p = dict(mix_prev=rw_mix_prev[j], mix_next=rw_mix_next[j],
                     w_r=rw_w_r[j].astype(BF16), w_k=rw_w_k[j].astype(BF16),
                     w_v=rw_w_v[j].astype(BF16), w0=rw_w0[j], w1=cat(rw_w1[j]).astype(BF16),
                     w2=cat(rw_w2[j]).astype(BF16), a0=rw_a0[j], a1=cat(rw_a1[j]).astype(BF16),
                     a2=cat(rw_a2[j]).astype(BF16), k_k=row(rw_k_k[j]), k_a=row(rw_k_a[j]),
                     r_k=row(rw_r_k[j]), g1=rw_g1[j].astype(BF16), g2=rw_g2[j].astype(BF16))
            if j > 0:
                n_v = rw_v1.shape[-1]
                p.update(v0=row(rw_v0[j - 1]),
                         v1=jnp.pad(rw_v1[j - 1], ((0, 0), (0, LANES - n_v))).astype(BF16),
                         v2=jnp.pad(rw_v2[j - 1], ((0, LANES - n_v), (0, 0))).astype(BF16))
            r, k, v, kk, bonus, gate, lw, a = _rwkv_proj(stream, mod, g_mix, p, v_first,
                                                         ctx_tiles)
            if v_first is None:
                v_first = v
            rh, yl, mt, gt = _wkv_prep(r, k, v, kk, lw, a, p['k_a'])
            ys = _wkv_scan(rh, yl, mt, gt, ctx_tiles)
            stream = _rwkv_out(stream, ys[0], ys[1], bonus, gate, mod, row(rw_lnx_g[j]),
                               row(rw_lnx_b[j]), rw_w_o[j].astype(BF16), ctx_tiles, t0)
        stream = _ffn(stream, mod, row(norm_ffn_g[i]), ffn_w_up[i].astype(BF16),
                      ffn_conv_w[i], row(ffn_conv_b[i]), ffn_w_down[i].astype(BF16),
                      row(final_norm_g) if last else no_final, 0 if last else ctx_tiles, last,
                      mixer)
    return stream
```

```python
import functools
import math

import jax
import jax.numpy as jnp
from jax import lax
from jax.experimental import pallas as pl
from jax.experimental.pallas import tpu as pltpu

D_MODEL = 1024
GRID_W = 64
DA_HEAD_DIM = 64
DA_HEADS = D_MODEL // (2 * DA_HEAD_DIM)
ROPE_THETA = 10000.0
RW_HEAD = 64
N_SHIFT_MIX = 6
LNX_EPS = 64e-5
EPS = 1e-6

LANES = 128
SUBLANES = 8
ROW_TILE = 256
WKV_CHUNK = 32
FF_TILE = 256
PREP_CHUNKS = 4
VMEM_LIMIT = 56 * 1024 * 1024

F32 = jnp.float32
BF16 = jnp.bfloat16

Q_SCALE = DA_HEAD_DIM ** -0.5 * math.log2(math.e)


def _dot(a, b):
    return jnp.dot(a.astype(BF16), b.astype(BF16), preferred_element_type=F32)


def _dot_nt(a, b):
    return lax.dot_general(a.astype(BF16), b.astype(BF16), (((1,), (1,)), ((), ())),
                           preferred_element_type=F32)


def _dot_tn(a, b):
    return lax.dot_general(a.astype(BF16), b.astype(BF16), (((0,), (0,)), ((), ())),
                           preferred_element_type=F32)


def _rms(x):
    return x * lax.rsqrt(jnp.mean(x * x, axis=-1, keepdims=True) + EPS)


def _norm_mod(x, g, shift, scale):
    return _rms(x) * g * (1.0 + scale) + shift


def _sigmoid(x):
    return 0.5 * jnp.tanh(0.5 * x) + 0.5


def _lane_iota(shape):
    return lax.broadcasted_iota(jnp.int32, shape, len(shape) - 1)


def _row_iota(shape):
    return lax.broadcasted_iota(jnp.int32, shape, len(shape) - 2)


def _group_ones():
    r = lax.broadcasted_iota(jnp.int32, (LANES, LANES), 0) // RW_HEAD
    c = lax.broadcasted_iota(jnp.int32, (LANES, LANES), 1) // RW_HEAD
    return (r == c).astype(BF16)


def _group_sum(z):
    ones = _group_ones()
    parts = [jnp.dot(z[:, j:j + LANES].astype(BF16), ones, preferred_element_type=F32)
             for j in range(0, z.shape[1], LANES)]
    return jnp.concatenate(parts, axis=1)


def _shifted(h, halo_prev, halo_next, prev_ok, next_ok):
    rows = h.shape[0]
    ri = _row_iota(h.shape)
    first = jnp.where(prev_ok, halo_prev[-1:, :], 0.0)
    last = jnp.where(next_ok, halo_next[0:1, :], 0.0)
    h_prev = jnp.where(ri == 0, first, pltpu.roll(h, 1, 0))
    h_next = jnp.where(ri == rows - 1, last, pltpu.roll(h, rows - 1, 0))
    return h_prev, h_next


def _edge_flags(t, n_tiles, ctx_tiles):
    prev_ok = jnp.logical_and(t != 0, t != ctx_tiles)
    next_ok = jnp.logical_and(t != n_tiles - 1, t != ctx_tiles - 1)
    return prev_ok, next_ok


def _const_spec(shape):
    zeros = (0,) * len(shape)
    return pl.BlockSpec(shape, lambda *_: zeros, pipeline_mode=pl.Buffered(1))


def _tile_spec(t0=0):
    return pl.BlockSpec((None, ROW_TILE, D_MODEL), lambda b, t: (b, t + t0, 0))


def _halo_specs(n_rows, t0=0, rows=SUBLANES):
    per = ROW_TILE // rows
    last = n_rows // rows - 1
    prev = pl.BlockSpec((None, rows, D_MODEL),
                        lambda b, t: (b, jnp.maximum((t + t0) * per - 1, 0), 0))
    nxt = pl.BlockSpec((None, rows, D_MODEL),
                       lambda b, t: (b, jnp.minimum((t + t0 + 1) * per, last), 0))
    return prev, nxt


def _mod_spec(ctx_tiles, t0=0):
    return pl.BlockSpec((None, None, SUBLANES, D_MODEL),
                        lambda b, t: (b, ((t + t0) >= ctx_tiles).astype(jnp.int32), 0, 0))


def _params():
    return pltpu.CompilerParams(vmem_limit_bytes=VMEM_LIMIT)


def _ada_kernel(c_ref, w_ref, b_ref, o_ref):
    c = c_ref[...]
    s = c * _sigmoid(c)
    o_ref[...] = jnp.dot(s, w_ref[...], precision=lax.Precision.HIGHEST,
                         preferred_element_type=F32) + b_ref[...]


def _ada_table(cc, ada_w, ada_b):
    n_layers, _, six_d = ada_w.shape
    rows = cc.shape[0]
    nb = 1536
    return pl.pallas_call(
        _ada_kernel,
        grid=(n_layers, six_d // nb),
        in_specs=[pl.BlockSpec((rows, D_MODEL), lambda l, n: (0, 0)),
                  pl.BlockSpec((None, D_MODEL, nb), lambda l, n: (l, 0, n)),
                  pl.BlockSpec((None, 1, nb), lambda l, n: (l, 0, n))],
        out_specs=pl.BlockSpec((None, rows, nb), lambda l, n: (l, 0, n)),
        out_shape=jax.ShapeDtypeStruct((n_layers, rows, six_d), F32),
        compiler_params=_params(),
        name="ada_table",
    )(cc, ada_w, ada_b.reshape(n_layers, 1, six_d))


def _rope(slab, cos, sin_signed):
    lane = _lane_iota(slab.shape)
    first = (lane % DA_HEAD_DIM) < (DA_HEAD_DIM // 2)
    partner = jnp.where(first, pltpu.roll(slab, LANES - DA_HEAD_DIM // 2, 1),
                        pltpu.roll(slab, DA_HEAD_DIM // 2, 1))
    return slab * cos + partner * sin_signed


def _qkv_kernel(x_ref, mod_ref, g_ref, w_ref, cos_ref, sin_ref, q_ref, k_ref, v_ref):
    mod = mod_ref[...]
    h = _norm_mod(x_ref[...], g_ref[...], mod[0:1], mod[1:2]).astype(BF16)
    cos = cos_ref[...]
    sin = sin_ref[...]
    wide = 2 * LANES
    for j in range(0, D_MODEL, wide):
        qa = jnp.dot(h, w_ref[:, j:j + wide], preferred_element_type=F32)
        ka = jnp.dot(h, w_ref[:, D_MODEL + j:D_MODEL + j + wide], preferred_element_type=F32)
        for i in range(0, wide, LANES):
            q_ref[:, j + i:j + i + LANES] = (
                _rope(qa[:, i:i + LANES], cos, sin) * Q_SCALE).astype(BF16)
            k_ref[:, j + i:j + i + LANES] = _rope(ka[:, i:i + LANES], cos, sin).astype(BF16)
        v_ref[:, j:j + wide] = jnp.dot(
            h, w_ref[:, 2 * D_MODEL + j:2 * D_MODEL + j + wide],
            preferred_element_type=F32).astype(BF16)


def _qkv(x, mod, g, w_qkv, cos, sin, ctx_tiles):
    bsz, n_rows, _ = x.shape
    out = jax.ShapeDtypeStruct((bsz, n_rows, D_MODEL), BF16)
    tab = pl.BlockSpec((ROW_TILE, LANES), lambda b, t: (t, 0))
    return pl.pallas_call(
        _qkv_kernel,
        grid=(bsz, n_rows // ROW_TILE),
        in_specs=[_tile_spec(), _mod_spec(ctx_tiles), _const_spec((1, D_MODEL)),
                  _const_spec((D_MODEL, 3 * D_MODEL)), tab, tab],
        out_specs=[_tile_spec(), _tile_spec(), _tile_spec()],
        out_shape=[out, out, out],
        compiler_params=_params(),
        name="attn_qkv",
    )(x, mod, g, w_qkv, cos, sin)


def _flash_kernel(lam_ref, g_ref, q_ref, qn_ref, k_ref, v_ref, o_ref,
                  sa_ref, sb_ref, ma_ref, mb_ref, vx_ref, *, ctx_tiles, lambda_init):
    t = pl.program_id(2)
    n_keys = k_ref.shape[0]
    ctx_keys = ctx_tiles * ROW_TILE
    lv = lam_ref[...]
    lam = (jnp.exp(jnp.sum(lv[0:1] * lv[1:2], axis=-1, keepdims=True))
           - jnp.exp(jnp.sum(lv[2:3] * lv[3:4], axis=-1, keepdims=True)) + lambda_init)

    @pl.when(t == 0)
    def _():
        vx_ref[:, :LANES] = v_ref[...]
        vx_ref[:, LANES:] = (_lane_iota((n_keys, LANES)) == 0).astype(BF16)

    def scores(q, n):
        lo = _lane_iota(q.shape) < DA_HEAD_DIM
        zero = jnp.zeros_like(q)
        k = k_ref[0:n, :]
        return [lax.dot_general(qc, k, (((1,), (1,)), ((), ())), preferred_element_type=F32)
                for qc in (jnp.where(lo, q, zero), jnp.where(lo, zero, q))]

    def stash(q, s_ref, m_ref, comps=(0, 1)):
        lo = _lane_iota(q.shape) < DA_HEAD_DIM
        zero = jnp.zeros_like(q)
        for c in comps:
            qc = jnp.where(lo, q, zero) if c == 0 else jnp.where(lo, zero, q)
            s = lax.dot_general(qc, k_ref[...], (((1,), (1,)), ((), ())),
                                preferred_element_type=F32)
            s_ref[c] = s
            m_ref[c] = jnp.max(s, axis=-1, keepdims=True)

    def attend(shifted, n):
        outs = []
        for x in shifted:
            p = jnp.exp2(x.astype(BF16))
            acc = jnp.dot(p, vx_ref[0:n, :], preferred_element_type=F32)
            outs.append(acc[:, :LANES] / acc[:, LANES:LANES + 1])
        o = outs[0] - lam * outs[1]
        o_ref[...] = (_rms(o) * g_ref[...] * (1.0 - lambda_init)).astype(BF16)

    @pl.when(t < ctx_tiles)
    def _():
        attend([s - jnp.max(s, axis=-1, keepdims=True) for s in scores(q_ref[...], ctx_keys)],
               ctx_keys)

    slots = ((sa_ref, ma_ref), (sb_ref, mb_ref))

    @pl.when(t == ctx_tiles)
    def _():
        stash(q_ref[...], *slots[ctx_tiles % 2])

    def shifted(slot):
        s_ref, m_ref = slots[slot]
        return [s_ref[c] - m_ref[c] for c in range(2)]

    last = pl.num_programs(2) - 1
    for parity in range(2):
        @pl.when(jnp.logical_and(jnp.logical_and(t >= ctx_tiles, t < last), t % 2 == parity))
        def _():
            stash(qn_ref[...], *slots[1 - parity], comps=(0,))
            attend(shifted(parity), n_keys)
            stash(qn_ref[...], *slots[1 - parity], comps=(1,))

        @pl.when(jnp.logical_and(jnp.logical_and(t >= ctx_tiles, t == last), t % 2 == parity))
        def _():
            attend(shifted(parity), n_keys)


def _flash(q, k, v, lam_vecs, subln_g, ctx_tiles, lambda_init):
    bsz, n_rows, _ = q.shape
    n_tiles = n_rows // ROW_TILE
    kv_spec = pl.BlockSpec((None, n_rows, LANES), lambda b, h, t: (b, 0, h))
    q_spec = pl.BlockSpec((None, ROW_TILE, LANES), lambda b, h, t: (b, t, h))
    q_next = pl.BlockSpec((None, ROW_TILE, LANES),
                          lambda b, h, t: (b, jnp.minimum(t + 1, n_tiles - 1), h))
    kern = functools.partial(_flash_kernel, ctx_tiles=ctx_tiles, lambda_init=lambda_init)
    score = pltpu.VMEM((2, ROW_TILE, n_rows), F32)
    rowmax = pltpu.VMEM((2, ROW_TILE, 1), F32)
    return pl.pallas_call(
        kern,
        grid=(bsz, DA_HEADS, n_tiles),
        in_specs=[pl.BlockSpec(lam_vecs.shape, lambda b, h, t: (0, 0)),
                  pl.BlockSpec((1, LANES), lambda b, h, t: (0, 0)),
                  q_spec, q_next, kv_spec, kv_spec],
        out_specs=q_spec,
        out_shape=jax.ShapeDtypeStruct((bsz, n_rows, D_MODEL), BF16),
        scratch_shapes=[score, score, rowmax, rowmax,
                        pltpu.VMEM((n_rows, 2 * LANES), BF16)],
        compiler_params=pltpu.CompilerParams(
            dimension_semantics=("arbitrary", "arbitrary", "arbitrary"),
            vmem_limit_bytes=VMEM_LIMIT),
        name="diff_flash",
    )(lam_vecs, subln_g.reshape(1, LANES), q, q, k, v)


def _proj_residual_kernel(x_ref, z_ref, mod_ref, w_ref, o_ref):
    out = jnp.dot(z_ref[...], w_ref[...], preferred_element_type=F32)
    o_ref[...] = x_ref[...] + mod_ref[2:3, :] * out


def _proj_residual(x, z, mod, w, ctx_tiles, t0):
    bsz, n_rows, _ = x.shape
    n_tiles = n_rows // ROW_TILE - t0
    return pl.pallas_call(
        _proj_residual_kernel,
        grid=(bsz, n_tiles),
        in_specs=[_tile_spec(t0), _tile_spec(t0), _mod_spec(ctx_tiles, t0),
                  _const_spec((D_MODEL, D_MODEL))],
        out_specs=_tile_spec(),
        out_shape=jax.ShapeDtypeStruct((bsz, n_tiles * ROW_TILE, D_MODEL), F32),
        compiler_params=_params(),
        name="proj_residual",
    )(x, z, mod, w)


def _rwkv_mix(y, bonus, gate, lnx_g, lnx_b):
    mu = _group_sum(y) * (1.0 / RW_HEAD)
    yc = y - mu
    var = _group_sum(yc * yc) * (1.0 / RW_HEAD)
    yn = yc * lax.rsqrt(var + LNX_EPS) * lnx_g + lnx_b
    return ((yn + bonus) * gate).astype(BF16)


def _ffn_kernel(*refs, n_tiles, ctx_tiles, t0, d_ff, final, mixer):
    cat = lambda trio: jnp.concatenate([r[...] for r in trio], axis=0)
    xcat = cat(refs[:3])
    refs = refs[3:]
    zcat = None
    if mixer == 'proj':
        zcat, wz_ref = cat(refs[:3]), refs[3]
        refs = refs[4:]
    elif mixer == 'rwkv':
        yf, yb, bonus, gate = (cat(refs[3 * i:3 * i + 3]) for i in range(4))
        lg_ref, lb_ref, wz_ref = refs[12:15]
        refs = refs[15:]
        zcat = _rwkv_mix(yf + yb, bonus, gate, lg_ref[...], lb_ref[...])
    mod_ref, g_ref, wup_ref, cw_ref, cb_ref, wdn_ref, fin_ref, o_ref = refs
    t = pl.program_id(1) + t0
    prev_ok, next_ok = _edge_flags(t, n_tiles, ctx_tiles)
    mod = mod_ref[...]
    g = g_ref[...]
    if zcat is not None:
        xcat = xcat + mod[2:3] * jnp.dot(zcat, wz_ref[...], preferred_element_type=F32)
    x = xcat[:ROW_TILE]
    hcat = _norm_mod(xcat, g, mod[3:4], mod[4:5]).astype(BF16)
    h = hcat[:ROW_TILE]
    halo = (xcat.shape[0] - ROW_TILE) // 2
    tiles = list(range(0, d_ff, FF_TILE))

    def up(f):
        gate = jnp.dot(hcat, wup_ref[:, f:f + FF_TILE], preferred_element_type=F32)
        val = jnp.dot(h, wup_ref[:, d_ff + f:d_ff + f + FF_TILE], preferred_element_type=F32)
        return gate, val

    acc = jnp.zeros((ROW_TILE, D_MODEL), F32)
    nxt = up(tiles[0])
    for i, f in enumerate(tiles):
        gate_all, val = nxt
        if i + 1 < len(tiles):
            nxt = up(tiles[i + 1])
        gate = gate_all[:ROW_TILE]
        g_prev, g_next = _shifted(gate, gate_all[ROW_TILE:ROW_TILE + halo],
                                  gate_all[ROW_TILE + halo:], prev_ok, next_ok)
        cw = cw_ref[:, f:f + FF_TILE]
        conv = g_prev * cw[0:1] + gate * cw[1:2] + g_next * cw[2:3] + cb_ref[:, f:f + FF_TILE]
        act = conv * _sigmoid(conv) * val
        acc = acc + jnp.dot(act.astype(BF16), wdn_ref[f:f + FF_TILE, :],
                            preferred_element_type=F32)
    y = x + mod[5:6] * acc
    if final:
        y = _rms(y) * fin_ref[...]
    o_ref[...] = y


def _ffn(x, mod, g, w_up, conv_w, conv_b, w_down, final_g, ctx_tiles, final, mixer=None, t0=0):
    bsz, n_rows, _ = x.shape
    n_tiles = n_rows // ROW_TILE
    d_ff = w_down.shape[0]
    halo = SUBLANES if mixer is None else 2 * SUBLANES
    prev, nxt = _halo_specs(n_rows, t0, rows=halo)
    trio = [_tile_spec(t0), prev, nxt]
    args, specs = [x, x, x], list(trio)
    if mixer is not None:
        streams = mixer[1:2] if mixer[0] == 'proj' else mixer[1:5]
        for arr in streams:
            args += [arr] * 3
            specs += trio
        for arr in mixer[1 + len(streams):]:
            args.append(arr)
            specs.append(_const_spec(arr.shape))
    args += [mod, g, w_up, conv_w, conv_b, w_down, final_g]
    specs += [_mod_spec(ctx_tiles, t0), _const_spec((1, D_MODEL)),
              _const_spec((D_MODEL, 2 * d_ff)), _const_spec((3, d_ff)),
              _const_spec((1, d_ff)), _const_spec((d_ff, D_MODEL)), _const_spec((1, D_MODEL))]
    kern = functools.partial(_ffn_kernel, n_tiles=n_tiles, ctx_tiles=ctx_tiles, t0=t0, d_ff=d_ff,
                             final=final, mixer=None if mixer is None else mixer[0])
    return pl.pallas_call(
        kern,
        grid=(bsz, n_tiles - t0),
        in_specs=specs,
        out_specs=_tile_spec(),
        out_shape=jax.ShapeDtypeStruct((bsz, (n_tiles - t0) * ROW_TILE, D_MODEL), F32),
        compiler_params=_params(),
        name="conv_glu",
    )(*args)


def _rwkv_proj_kernel(*refs, n_tiles, ctx_tiles, has_vfirst):
    (x_ref, xp_ref, xn_ref, mod_ref, g_ref, mp_ref, mn_ref, wr_ref, wk_ref, wv_ref,
     w0_ref, w1_ref, w2_ref, a0_ref, a1_ref, a2_ref, kk_ref, ka_ref, rk_ref,
     g1_ref, g2_ref) = refs[:21]
    refs = refs[21:]
    if has_vfirst:
        vf_ref, v0_ref, v1_ref, v2_ref = refs[:4]
        refs = refs[4:]
    r_ref, k_ref, v_ref, kkn_ref, bonus_ref, gate_ref, lw_ref, a_ref = refs

    t = pl.program_id(1)
    prev_ok, next_ok = _edge_flags(t, n_tiles, ctx_tiles)
    mod = mod_ref[...]
    g = g_ref[...]
    h = _norm_mod(x_ref[...], g, mod[0:1], mod[1:2])
    hp = _norm_mod(xp_ref[...], g, mod[0:1], mod[1:2])
    hn = _norm_mod(xn_ref[...], g, mod[0:1], mod[1:2])
    h_prev, h_next = _shifted(h, hp, hn, prev_ok, next_ok)
    hb = h.astype(BF16)
    xx_p = (h_prev - h).astype(BF16)
    xx_n = (h_next - h).astype(BF16)
    mp = mp_ref[...].astype(BF16)
    mn = mn_ref[...].astype(BF16)

    def mix(m):
        return hb + xx_p * mp[m:m + 1] + xx_n * mn[m:m + 1]

    lo = _lane_iota((ROW_TILE, LANES)) < RW_HEAD

    def halves(z):
        return jnp.where(lo, z, 0.0), jnp.where(lo, 0.0, z)

    r = jnp.dot(mix(0), wr_ref[...], preferred_element_type=F32)
    r_ref[...] = r

    lw = halves(jnp.tanh(jnp.dot(mix(1), w1_ref[...], preferred_element_type=F32)))
    la = halves(jnp.dot(mix(4), a1_ref[...], preferred_element_type=F32))
    a_gate = []
    for d in range(2):
        wl = w0_ref[d:d + 1, :] + _dot(lw[d], w2_ref[...])
        lw_ref[d] = _sigmoid(wl) * (-math.exp(-0.5))
        a_d = _sigmoid(a0_ref[d:d + 1, :] + _dot(la[d], a2_ref[...]))
        a_ref[d] = a_d
        a_gate.append(a_d)

    k = jnp.dot(mix(2), wk_ref[...], preferred_element_type=F32)
    k_ref[...] = k
    kk = k * kk_ref[...]
    kkn_ref[...] = kk * lax.rsqrt(jnp.maximum(_group_sum(kk * kk), 1e-24))

    xv = mix(3)
    v = jnp.dot(xv, wv_ref[...], preferred_element_type=F32)
    if has_vfirst:
        lv = jnp.dot(xv, v1_ref[...], preferred_element_type=F32)
        v = v + (vf_ref[...] - v) * _sigmoid(v0_ref[...] + _dot(lv, v2_ref[...]))
    v_ref[...] = v

    ka = ka_ref[...]
    k_sum = k * (2.0 + (a_gate[0] + a_gate[1] - 2.0) * ka)
    bonus_ref[...] = (_group_sum(r * k_sum * rk_ref[...]) * v).astype(BF16)

    gl = _sigmoid(jnp.dot(mix(5), g1_ref[...], preferred_element_type=F32))
    gate_ref[...] = _dot(gl, g2_ref[...]).astype(BF16)


def _rwkv_proj(x, mod, g, p, v_first, ctx_tiles):
    bsz, n_rows, _ = x.shape
    n_tiles = n_rows // ROW_TILE
    prev, nxt = _halo_specs(n_rows)
    has_vfirst = v_first is not None
    args = [x, x, x, mod, g, p['mix_prev'], p['mix_next'], p['w_r'], p['w_k'], p['w_v'],
            p['w0'], p['w1'], p['w2'], p['a0'], p['a1'], p['a2'], p['k_k'], p['k_a'], p['r_k'],
            p['g1'], p['g2']]
    specs = [_tile_spec(), prev, nxt, _mod_spec(ctx_tiles)] + [
        _const_spec(a.shape) for a in args[4:]]
    if has_vfirst:
        args += [v_first, p['v0'], p['v1'], p['v2']]
        specs += [_tile_spec()] + [_const_spec(a.shape) for a in args[-3:]]
    one = jax.ShapeDtypeStruct((bsz, n_rows, D_MODEL), F32)
    half = jax.ShapeDtypeStruct((bsz, n_rows, D_MODEL), BF16)
    two = jax.ShapeDtypeStruct((2, bsz, n_rows, D_MODEL), F32)
    two_spec = pl.BlockSpec((2, None, ROW_TILE, D_MODEL), lambda b, t: (0, b, t, 0))
    kern = functools.partial(_rwkv_proj_kernel, n_tiles=n_tiles, ctx_tiles=ctx_tiles,
                             has_vfirst=has_vfirst)
    return pl.pallas_call(
        kern,
        grid=(bsz, n_tiles),
        in_specs=specs,
        out_specs=[_tile_spec()] * 6 + [two_spec, two_spec],
        out_shape=[one] * 4 + [half, half, two, two],
        compiler_params=_params(),
        name="rwkv_proj",
    )(*args)


def _mask_stack(x, n_blocks, width):
    blk = _lane_iota(x.shape) // width
    zero = jnp.zeros_like(x)
    return jnp.concatenate([jnp.where(blk == i, x, zero) for i in range(n_blocks)], axis=0)


def _wkv_prep_kernel(r_ref, k_ref, v_ref, kk_ref, lw_ref, a_ref, ka_ref,
                     rh_ref, yl_ref, mt_ref, gt_ref):
    n = WKV_CHUNK
    hpg = LANES // n
    gw = hpg * RW_HEAD
    groups = D_MODEL // gw
    ri = lax.broadcasted_iota(jnp.int32, (n, n), 0)
    ci = lax.broadcasted_iota(jnp.int32, (n, n), 1)
    row = _row_iota((n, LANES))
    src = _lane_iota((n, LANES)) % n
    eye = (src == row).astype(F32)
    order = ((ci <= ri).astype(BF16), src < row, src <= row, n - 1), \
            ((ci >= ri).astype(BF16), src > row, src >= row, 0)
    lo_st = _lane_iota((RW_HEAD, LANES)) < RW_HEAD
    diag = (_lane_iota((RW_HEAD, LANES)) % RW_HEAD) == _row_iota((RW_HEAD, LANES))
    ka = ka_ref[...]
    bf = lambda z: z.astype(BF16)

    def body(it, carry):
        items = []
        for cc in range(PREP_CHUNKS):
            c = it * PREP_CHUNKS + cc
            rows = pl.ds(pl.multiple_of(c * n, n), n)
            kk = kk_ref[rows, :]
            k = k_ref[rows, :]
            r = r_ref[rows, :]
            v = bf(v_ref[rows, :])
            for d in range(2):
                tri, strict, incl, last = order[d]
                lw = lw_ref[d, rows, :]
                gate = a_ref[d, rows, :]
                kd = k * (1.0 + (gate - 1.0) * ka)
                b = kk * gate
                lw_hi = bf(lw)
                lw_lo = bf(lw - lw_hi.astype(F32))
                cum = (jnp.dot(tri, lw_hi, preferred_element_type=F32)
                       + jnp.dot(tri, lw_lo, preferred_element_type=F32))
                cum_end = cum[last:last + 1, :]
                e_neg = jnp.exp(-cum)
                e_rem = jnp.exp(cum_end - cum)
                rt = r * jnp.exp(cum)
                full = dict(c=c, d=d, rows=rows, rt=rt, w_end=jnp.exp(cum_end), v=v,
                            at=bf(-kk * jnp.exp(cum - lw)), rtb=bf(rt), kt=bf(kd * e_neg),
                            bt=bf(b * e_neg), kh=bf(kd * e_rem), bh=bf(b * e_rem),
                            strict=strict, incl=incl)
                for g in range(groups):
                    items.append(dict(full=full, sl=slice(g * gw, (g + 1) * gw)))

        for it_ in items:
            f, sl = it_['full'], it_['sl']
            lhs = jnp.concatenate([f['at'][:, sl], f['rtb'][:, sl]], axis=0)
            rhs = jnp.concatenate([_mask_stack(f['kt'][:, sl], hpg, RW_HEAD),
                                   _mask_stack(f['bt'][:, sl], hpg, RW_HEAD)], axis=0)
            a = lax.dot_general(lhs, rhs, (((1,), (1,)), ((), ())), preferred_element_type=F32)
            it_['aak'] = jnp.where(f['strict'], a[:n, :LANES], 0.0)
            it_['ark'] = jnp.where(f['incl'], a[n:, :LANES], 0.0)
            it_['arb'] = bf(jnp.where(f['incl'], a[n:, LANES:], 0.0))
            pw = jnp.where(f['strict'], a[:n, LANES:], 0.0)
            it_['inv'] = eye + pw
            it_['pw'] = bf(pw)

        for it_ in items:
            it_['pw'] = bf(jnp.dot(it_['pw'], _mask_stack(it_['pw'], hpg, n),
                                   preferred_element_type=F32))
        steps = 2
        while steps < n:
            final = 2 * steps >= n
            for it_ in items:
                rhs = _mask_stack(bf(it_['inv']), hpg, n)
                if not final:
                    rhs = jnp.concatenate([_mask_stack(it_['pw'], hpg, n), rhs], axis=1)
                x = jnp.dot(it_['pw'], rhs, preferred_element_type=F32)
                it_['inv'] = it_['inv'] + x[:, -LANES:]
                if not final:
                    it_['pw'] = bf(x[:, :LANES])
            steps *= 2

        for it_ in items:
            f, sl = it_['full'], it_['sl']
            vms = _mask_stack(f['v'][:, sl], hpg, RW_HEAD)
            x = jnp.dot(bf(jnp.concatenate([it_['aak'], it_['ark']], axis=0)), vms,
                        preferred_element_type=F32)
            it_['arkv'] = x[n:]
            it_['rhs'] = jnp.concatenate([_mask_stack(f['at'][:, sl], hpg, RW_HEAD),
                                          _mask_stack(bf(x[:n]), hpg, RW_HEAD)], axis=1)
        for it_ in items:
            au = bf(jnp.dot(bf(it_['inv']), it_['rhs'], preferred_element_type=F32))
            it_['ah'] = au[:, :gw]
            it_['ul'] = au[:, gw:]
            it_['rhs'] = jnp.concatenate([_mask_stack(it_['ah'], hpg, RW_HEAD),
                                          _mask_stack(it_['ul'], hpg, RW_HEAD)], axis=1)
        for it_ in items:
            f, sl = it_['full'], it_['sl']
            x = jnp.dot(it_['arb'], it_['rhs'], preferred_element_type=F32)
            rh_ref[f['d'], f['rows'], sl] = bf(f['rt'][:, sl] + x[:, :gw])
            yl_ref[f['d'], f['rows'], sl] = it_['arkv'] + x[:, gw:]
        for it_ in items:
            f, sl = it_['full'], it_['sl']
            for j in range(0, gw, LANES):
                loc = slice(j, j + LANES)
                hs = slice(sl.start + j, sl.start + j + LANES)
                bh = f['bh'][:, hs]
                none = jnp.zeros((n, LANES), BF16)
                x = lax.dot_general(
                    jnp.concatenate([f['kh'][:, hs], bh], axis=0),
                    jnp.concatenate(
                        [jnp.concatenate([f['v'][:, hs], it_['ul'][:, loc]], axis=0),
                         jnp.concatenate([none, it_['ah'][:, loc]], axis=0)], axis=1),
                    (((0,), (0,)), ((), ())), preferred_element_type=F32)
                gf = x[:, :LANES]
                pm = x[:, LANES:]
                mt = jnp.where(lo_st, pm[:RW_HEAD], pm[RW_HEAD:])
                mt_ref[f['d'], f['c'], :, hs] = bf(mt + jnp.where(diag, f['w_end'][:, hs], 0.0))
                gt_ref[f['d'], f['c'], :, hs] = bf(jnp.where(lo_st, gf[:RW_HEAD], gf[RW_HEAD:]))
        return carry

    lax.fori_loop(0, ROW_TILE // (n * PREP_CHUNKS), body, 0)


def _wkv_prep(r, k, v, kk, lw, a, k_a):
    bsz, n_rows, _ = r.shape
    n_tiles = n_rows // ROW_TILE
    per = ROW_TILE // WKV_CHUNK
    dir_spec = pl.BlockSpec((2, None, ROW_TILE, D_MODEL), lambda b, t: (0, b, t, 0))
    st_spec = pl.BlockSpec((2, None, per, RW_HEAD, D_MODEL), lambda b, t: (0, b, t, 0, 0))
    rows = (2, bsz, n_rows, D_MODEL)
    st = (2, bsz, n_rows // WKV_CHUNK, RW_HEAD, D_MODEL)
    return pl.pallas_call(
        _wkv_prep_kernel,
        grid=(bsz, n_tiles),
        in_specs=[_tile_spec()] * 4 + [dir_spec, dir_spec, _const_spec((1, D_MODEL))],
        out_specs=[dir_spec, dir_spec, st_spec, st_spec],
        out_shape=[jax.ShapeDtypeStruct(rows, BF16), jax.ShapeDtypeStruct(rows, F32),
                   jax.ShapeDtypeStruct(st, BF16), jax.ShapeDtypeStruct(st, BF16)],
        compiler_params=_params(),
        name="wkv_prep",
    )(r, k, v, kk, lw, a, k_a)


def _wkv_scan_kernel(rhf_ref, ylf_ref, mtf_ref, gtf_ref, rhb_ref, ylb_ref, mtb_ref, gtb_ref,
                     yf_ref, yb_ref, state_ref):
    n = WKV_CHUNK
    per = ROW_TILE // n

    @pl.when(pl.program_id(1) == 0)
    def _():
        state_ref[...] = jnp.zeros(state_ref.shape, F32)

    lo = _lane_iota((RW_HEAD, LANES)) < RW_HEAD
    dirs = ((rhf_ref, ylf_ref, mtf_ref, gtf_ref, yf_ref), (rhb_ref, ylb_ref, mtb_ref, gtb_ref, yb_ref))
    for i in range(per):
        for d, (rh_ref, yl_ref, mt_ref, gt_ref, y_ref) in enumerate(dirs):
            c = per - 1 - i if d == 1 else i
            for hp in range(D_MODEL // LANES):
                col = slice(hp * LANES, (hp + 1) * LANES)
                lhs = jnp.concatenate([rh_ref[c * n:(c + 1) * n, col], mt_ref[c, :, col]],
                                      axis=0)
                out = jnp.dot(lhs, state_ref[d, hp].astype(BF16), preferred_element_type=F32)
                y_ref[c * n:(c + 1) * n, col] = out[:n] + yl_ref[c * n:(c + 1) * n, col]
                st = out[n:] + gt_ref[c, :, col]
                state_ref[d, hp] = jnp.concatenate(
                    [jnp.where(lo, st, 0.0), jnp.where(lo, 0.0, st)], axis=0)


def _wkv_scan(rh, yl, mt, gt, ctx_tiles):
    _, bsz, n_rows, _ = rh.shape
    n_tiles = n_rows // ROW_TILE
    per = ROW_TILE // WKV_CHUNK

    def back(t):
        return jnp.where(t < ctx_tiles, ctx_tiles - 1 - t, n_tiles - 1 - (t - ctx_tiles))

    def specs(d, tile):
        row = pl.BlockSpec((None, None, ROW_TILE, D_MODEL), lambda b, t: (d, b, tile(t), 0))
        st = pl.BlockSpec((None, None, per, RW_HEAD, D_MODEL),
                          lambda b, t: (d, b, tile(t), 0, 0))
        return [row, row, st, st]

    fwd = lambda t: t
    y_f = pl.BlockSpec((None, ROW_TILE, D_MODEL), lambda b, t: (b, t, 0))
    y_b = pl.BlockSpec((None, ROW_TILE, D_MODEL), lambda b, t: (b, back(t), 0))
    out = jax.ShapeDtypeStruct((bsz, n_rows, D_MODEL), F32)
    return pl.pallas_call(
        _wkv_scan_kernel,
        grid=(bsz, n_tiles),
        in_specs=specs(0, fwd) + specs(1, back),
        out_specs=[y_f, y_b],
        out_shape=[out, out],
        scratch_shapes=[pltpu.VMEM((2, D_MODEL // LANES, LANES, LANES), F32)],
        compiler_params=pltpu.CompilerParams(dimension_semantics=("arbitrary", "arbitrary"),
                                             vmem_limit_bytes=VMEM_LIMIT),
        name="wkv_scan",
    )(rh, yl, mt, gt, rh, yl, mt, gt)


def _rope_tables(n_ctx, n_lat):
    t = jnp.arange(n_lat)
    row_pos = (t // GRID_W).astype(F32)
    col_pos = (t % GRID_W).astype(F32)
    n_freq = DA_HEAD_DIM // 4
    inv_freq = ROPE_THETA ** (-jnp.arange(n_freq, dtype=F32) / n_freq)
    ang = jnp.concatenate([row_pos[:, None] * inv_freq, col_pos[:, None] * inv_freq], axis=-1)
    cos, sin = jnp.cos(ang), jnp.sin(ang)
    reps = LANES // DA_HEAD_DIM
    cos = jnp.tile(jnp.concatenate([cos, cos], axis=-1), (1, reps))
    sin = jnp.tile(jnp.concatenate([-sin, sin], axis=-1), (1, reps))
    cos = jnp.concatenate([jnp.ones((n_ctx, LANES), F32), cos], axis=0)
    sin = jnp.concatenate([jnp.zeros((n_ctx, LANES), F32), sin], axis=0)
    return cos, sin


def kernel(x, c, ctx, c_ctx, ada_w, ada_b, norm_mix_g, norm_ffn_g, ffn_w_up, ffn_conv_w, ffn_conv_b, ffn_w_down, da_w_qkv, da_lambda, da_subln_g, da_w_o, rw_mix_prev, rw_mix_next, rw_w_r, rw_w_k, rw_w_v, rw_w0, rw_w1, rw_w2, rw_a0, rw_a1, rw_a2, rw_v0, rw_v1, rw_v2, rw_k_k, rw_k_a, rw_r_k, rw_g1, rw_g2, rw_lnx_g, rw_lnx_b, rw_w_o, final_norm_g):
    bsz, n_lat, d = x.shape
    n_ctx = ctx.shape[1]
    depth = ada_w.shape[0]
    assert d == D_MODEL and n_lat % ROW_TILE == 0 and n_ctx % ROW_TILE == 0
    assert n_lat % GRID_W == 0
    ctx_tiles = n_ctx // ROW_TILE
    row = lambda a: a.reshape(1, -1)
    cat = lambda a: jnp.concatenate([a[0], a[1]], axis=0 if a.shape[1] == RW_HEAD else 1)

    pad = (-(bsz + 1)) % SUBLANES
    cc = jnp.concatenate([c, c_ctx[None, :], jnp.zeros((pad, d), F32)], axis=0)
    table = _ada_table(cc, ada_w, ada_b)
    mod_l = table[:, :bsz].reshape(depth, bsz, 6, d)
    mod_c = jnp.broadcast_to(table[:, bsz].reshape(depth, 1, 6, d), (depth, bsz, 6, d))
    mods = jnp.stack([mod_c, mod_l], axis=2)
    mods = jnp.pad(mods, ((0, 0), (0, 0), (0, 0), (0, SUBLANES - 6), (0, 0)))

    cos, sin = _rope_tables(n_ctx, n_lat)
    stream = jnp.concatenate([ctx, x], axis=1)
    v_first = None
    no_final = jnp.ones((1, d), F32)

    for i in range(depth):
        last = i == depth - 1
        j = i // 2
        t0 = ctx_tiles if last else 0
        mod = mods[i]
        g_mix = row(norm_mix_g[i])
        mixer = None
        if i % 2 == 0:
            lambda_init = 0.8 - 0.6 * math.exp(-0.3 * i)
            q, k, v = _qkv(stream, mod, g_mix, da_w_qkv[j].astype(BF16), cos, sin, ctx_tiles)
            o = _flash(q, k, v, da_lambda[j], da_subln_g[j], ctx_tiles, lambda_init)
            if last:
                stream = _proj_residual(stream, o, mod, da_w_o[j].astype(BF16), ctx_tiles, t0)
            else:
                mixer = ('proj', o, da_w_o[j].astype(BF16))
        else:
            p = dict(mix_prev=rw_mix_prev[j], mix_next=rw_mix_next[j],
                     w_r=rw_w_r[j].astype(BF16), w_k=rw_w_k[j].astype(BF16),
                     w_v=rw_w_v[j].astype(BF16), w0=rw_w0[j], w1=cat(rw_w1[j]).astype(BF16),
                     w2=cat(rw_w2[j]).astype(BF16), a0=rw_a0[j], a1=cat(rw_a1[j]).astype(BF16),
                     a2=cat(rw_a2[j]).astype(BF16), k_k=row(rw_k_k[j]), k_a=row(rw_k_a[j]),
                     r_k=row(rw_r_k[j]), g1=rw_g1[j].astype(BF16), g2=rw_g2[j].astype(BF16))
            if j > 0:
                n_v = rw_v1.shape[-1]
                p.update(v0=row(rw_v0[j - 1]),
                         v1=jnp.pad(rw_v1[j - 1], ((0, 0), (0, LANES - n_v))).astype(BF16),
                         v2=jnp.pad(rw_v2[j - 1], ((0, LANES - n_v), (0, 0))).astype(BF16))
            r, k, v, kk, bonus, gate, lw, a = _rwkv_proj(stream, mod, g_mix, p, v_first,
                                                         ctx_tiles)
            if v_first is None:
                v_first = v
            rh, yl, mt, gt = _wkv_prep(r, k, v, kk, lw, a, p['k_a'])
            ys = _wkv_scan(rh, yl, mt, gt, ctx_tiles)
            mixer = ('rwkv', ys[0], ys[1], bonus, gate, row(rw_lnx_g[j]), row(rw_lnx_b[j]),
                     rw_w_o[j].astype(BF16))
        fused = mixer is not None
        stream = _ffn(stream, mod, row(norm_ffn_g[i]), ffn_w_up[i].astype(BF16),
                      ffn_conv_w[i], row(ffn_conv_b[i]), ffn_w_down[i].astype(BF16),
                      row(final_norm_g) if last else no_final,
                      ctx_tiles if fused or not last else 0, last, mixer, t0 if fused else 0)
    return stream
```

```python
import functools
import math

import jax
import jax.numpy as jnp
from jax import lax
from jax.experimental import pallas as pl
from jax.experimental.pallas import tpu as pltpu

D_MODEL = 1024
GRID_W = 64
DA_HEAD_DIM = 64
DA_HEADS = D_MODEL // (2 * DA_HEAD_DIM)
ROPE_THETA = 10000.0
RW_HEAD = 64
N_SHIFT_MIX = 6
LNX_EPS = 64e-5
EPS = 1e-6

LANES = 128
SUBLANES = 8
ROW_TILE = 256
WKV_CHUNK = 32
FF_TILE = 256
PREP_CHUNKS = 4
VMEM_LIMIT = 56 * 1024 * 1024

F32 = jnp.float32
BF16 = jnp.bfloat16

Q_SCALE = DA_HEAD_DIM ** -0.5 * math.log2(math.e)


def _dot(a, b):
    return jnp.dot(a.astype(BF16), b.astype(BF16), preferred_element_type=F32)


def _dot_nt(a, b):
    return lax.dot_general(a.astype(BF16), b.astype(BF16), (((1,), (1,)), ((), ())),
                           preferred_element_type=F32)


def _dot_tn(a, b):
    return lax.dot_general(a.astype(BF16), b.astype(BF16), (((0,), (0,)), ((), ())),
                           preferred_element_type=F32)


def _rms(x):
    return x * lax.rsqrt(jnp.mean(x * x, axis=-1, keepdims=True) + EPS)


def _norm_mod(x, g, shift, scale):
    return _rms(x) * g * (1.0 + scale) + shift


def _sigmoid(x):
    return 0.5 * jnp.tanh(0.5 * x) + 0.5


def _lane_iota(shape):
    return lax.broadcasted_iota(jnp.int32, shape, len(shape) - 1)


def _row_iota(shape):
    return lax.broadcasted_iota(jnp.int32, shape, len(shape) - 2)


def _group_ones():
    r = lax.broadcasted_iota(jnp.int32, (LANES, LANES), 0) // RW_HEAD
    c = lax.broadcasted_iota(jnp.int32, (LANES, LANES), 1) // RW_HEAD
    return (r == c).astype(BF16)


def _group_sum(z):
    ones = _group_ones()
    parts = [jnp.dot(z[:, j:j + LANES].astype(BF16), ones, preferred_element_type=F32)
             for j in range(0, z.shape[1], LANES)]
    return jnp.concatenate(parts, axis=1)


def _shifted(h, halo_prev, halo_next, prev_ok, next_ok):
    rows = h.shape[0]
    ri = _row_iota(h.shape)
    first = jnp.where(prev_ok, halo_prev[-1:, :], 0.0)
    last = jnp.where(next_ok, halo_next[0:1, :], 0.0)
    h_prev = jnp.where(ri == 0, first, pltpu.roll(h, 1, 0))
    h_next = jnp.where(ri == rows - 1, last, pltpu.roll(h, rows - 1, 0))
    return h_prev, h_next


def _edge_flags(t, n_tiles, ctx_tiles):
    prev_ok = jnp.logical_and(t != 0, t != ctx_tiles)
    next_ok = jnp.logical_and(t != n_tiles - 1, t != ctx_tiles - 1)
    return prev_ok, next_ok


def _const_spec(shape):
    zeros = (0,) * len(shape)
    return pl.BlockSpec(shape, lambda *_: zeros, pipeline_mode=pl.Buffered(1))


def _tile_spec(t0=0):
    return pl.BlockSpec((None, ROW_TILE, D_MODEL), lambda b, t: (b, t + t0, 0))


def _halo_specs(n_rows, t0=0, rows=SUBLANES):
    per = ROW_TILE // rows
    last = n_rows // rows - 1
    prev = pl.BlockSpec((None, rows, D_MODEL),
                        lambda b, t: (b, jnp.maximum((t + t0) * per - 1, 0), 0))
    nxt = pl.BlockSpec((None, rows, D_MODEL),
                       lambda b, t: (b, jnp.minimum((t + t0 + 1) * per, last), 0))
    return prev, nxt


def _mod_spec(ctx_tiles, t0=0):
    return pl.BlockSpec((None, None, SUBLANES, D_MODEL),
                        lambda b, t: (b, ((t + t0) >= ctx_tiles).astype(jnp.int32), 0, 0))


def _params():
    return pltpu.CompilerParams(vmem_limit_bytes=VMEM_LIMIT)


def _ada_kernel(c_ref, w_ref, b_ref, o_ref):
    c = c_ref[...]
    s = c * _sigmoid(c)
    o_ref[...] = jnp.dot(s, w_ref[...], precision=lax.Precision.HIGHEST,
                         preferred_element_type=F32) + b_ref[...]


def _ada_table(cc, ada_w, ada_b):
    n_layers, _, six_d = ada_w.shape
    rows = cc.shape[0]
    nb = 1536
    return pl.pallas_call(
        _ada_kernel,
        grid=(n_layers, six_d // nb),
        in_specs=[pl.BlockSpec((rows, D_MODEL), lambda l, n: (0, 0)),
                  pl.BlockSpec((None, D_MODEL, nb), lambda l, n: (l, 0, n)),
                  pl.BlockSpec((None, 1, nb), lambda l, n: (l, 0, n))],
        out_specs=pl.BlockSpec((None, rows, nb), lambda l, n: (l, 0, n)),
        out_shape=jax.ShapeDtypeStruct((n_layers, rows, six_d), F32),
        compiler_params=_params(),
        name="ada_table",
    )(cc, ada_w, ada_b.reshape(n_layers, 1, six_d))


def _rope(slab, cos, sin_signed):
    lane = _lane_iota(slab.shape)
    first = (lane % DA_HEAD_DIM) < (DA_HEAD_DIM // 2)
    partner = jnp.where(first, pltpu.roll(slab, LANES - DA_HEAD_DIM // 2, 1),
                        pltpu.roll(slab, DA_HEAD_DIM // 2, 1))
    return slab * cos + partner * sin_signed


def _qkv_kernel(x_ref, mod_ref, g_ref, w_ref, cos_ref, sin_ref, q_ref, k_ref, v_ref):
    mod = mod_ref[...]
    h = _norm_mod(x_ref[...], g_ref[...], mod[0:1], mod[1:2]).astype(BF16)
    cos = cos_ref[...]
    sin = sin_ref[...]
    wide = 2 * LANES
    for j in range(0, D_MODEL, wide):
        qa = jnp.dot(h, w_ref[:, j:j + wide], preferred_element_type=F32)
        ka = jnp.dot(h, w_ref[:, D_MODEL + j:D_MODEL + j + wide], preferred_element_type=F32)
        for i in range(0, wide, LANES):
            q_ref[:, j + i:j + i + LANES] = (
                _rope(qa[:, i:i + LANES], cos, sin) * Q_SCALE).astype(BF16)
            k_ref[:, j + i:j + i + LANES] = _rope(ka[:, i:i + LANES], cos, sin).astype(BF16)
        v_ref[:, j:j + wide] = jnp.dot(
            h, w_ref[:, 2 * D_MODEL + j:2 * D_MODEL + j + wide],
            preferred_element_type=F32).astype(BF16)


def _qkv(x, mod, g, w_qkv, cos, sin, ctx_tiles):
    bsz, n_rows, _ = x.shape
    out = jax.ShapeDtypeStruct((bsz, n_rows, D_MODEL), BF16)
    tab = pl.BlockSpec((ROW_TILE, LANES), lambda b, t: (t, 0))
    return pl.pallas_call(
        _qkv_kernel,
        grid=(bsz, n_rows // ROW_TILE),
        in_specs=[_tile_spec(), _mod_spec(ctx_tiles), _const_spec((1, D_MODEL)),
                  _const_spec((D_MODEL, 3 * D_MODEL)), tab, tab],
        out_specs=[_tile_spec(), _tile_spec(), _tile_spec()],
        out_shape=[out, out, out],
        compiler_params=_params(),
        name="attn_qkv",
    )(x, mod, g, w_qkv, cos, sin)


def _flash_kernel(lam_ref, g_ref, q_ref, qn_ref, k_ref, v_ref, o_ref,
                  sa_ref, sb_ref, ma_ref, mb_ref, vx_ref, *, ctx_tiles, lambda_init):
    t = pl.program_id(2)
    n_keys = k_ref.shape[0]
    ctx_keys = ctx_tiles * ROW_TILE
    lv = lam_ref[...]
    lam = (jnp.exp(jnp.sum(lv[0:1] * lv[1:2], axis=-1, keepdims=True))
           - jnp.exp(jnp.sum(lv[2:3] * lv[3:4], axis=-1, keepdims=True)) + lambda_init)

    @pl.when(t == 0)
    def _():
        vx_ref[:, :LANES] = v_ref[...]
        vx_ref[:, LANES:] = (_lane_iota((n_keys, LANES)) == 0).astype(BF16)

    def scores(q, n):
        lo = _lane_iota(q.shape) < DA_HEAD_DIM
        zero = jnp.zeros_like(q)
        k = k_ref[0:n, :]
        return [lax.dot_general(qc, k, (((1,), (1,)), ((), ())), preferred_element_type=F32)
                for qc in (jnp.where(lo, q, zero), jnp.where(lo, zero, q))]

    def stash(q, s_ref, m_ref, comps=(0, 1)):
        lo = _lane_iota(q.shape) < DA_HEAD_DIM
        zero = jnp.zeros_like(q)
        for c in comps:
            qc = jnp.where(lo, q, zero) if c == 0 else jnp.where(lo, zero, q)
            s = lax.dot_general(qc, k_ref[...], (((1,), (1,)), ((), ())),
                                preferred_element_type=F32)
            s_ref[c] = s
            m_ref[c] = jnp.max(s, axis=-1, keepdims=True)

    def attend(shifted, n):
        outs = []
        for x in shifted:
            p = jnp.exp2(x.astype(BF16))
            acc = jnp.dot(p, vx_ref[0:n, :], preferred_element_type=F32)
            outs.append(acc[:, :LANES] / acc[:, LANES:LANES + 1])
        o = outs[0] - lam * outs[1]
        o_ref[...] = (_rms(o) * g_ref[...] * (1.0 - lambda_init)).astype(BF16)

    @pl.when(t < ctx_tiles)
    def _():
        attend([s - jnp.max(s, axis=-1, keepdims=True) for s in scores(q_ref[...], ctx_keys)],
               ctx_keys)

    slots = ((sa_ref, ma_ref), (sb_ref, mb_ref))

    @pl.when(t == ctx_tiles)
    def _():
        stash(q_ref[...], *slots[ctx_tiles % 2])

    def shifted(slot):
        s_ref, m_ref = slots[slot]
        return [s_ref[c] - m_ref[c] for c in range(2)]

    last = pl.num_programs(2) - 1
    for parity in range(2):
        @pl.when(jnp.logical_and(jnp.logical_and(t >= ctx_tiles, t < last), t % 2 == parity))
        def _():
            stash(qn_ref[...], *slots[1 - parity], comps=(0,))
            attend(shifted(parity), n_keys)
            stash(qn_ref[...], *slots[1 - parity], comps=(1,))

        @pl.when(jnp.logical_and(jnp.logical_and(t >= ctx_tiles, t == last), t % 2 == parity))
        def _():
            attend(shifted(parity), n_keys)


def _flash(q, k, v, lam_vecs, subln_g, ctx_tiles, lambda_init):
    bsz, n_rows, _ = q.shape
    n_tiles = n_rows // ROW_TILE
    kv_spec = pl.BlockSpec((None, n_rows, LANES), lambda b, h, t: (b, 0, h))
    q_spec = pl.BlockSpec((None, ROW_TILE, LANES), lambda b, h, t: (b, t, h))
    q_next = pl.BlockSpec((None, ROW_TILE, LANES),
                          lambda b, h, t: (b, jnp.minimum(t + 1, n_tiles - 1), h))
    kern = functools.partial(_flash_kernel, ctx_tiles=ctx_tiles, lambda_init=lambda_init)
    score = pltpu.VMEM((2, ROW_TILE, n_rows), F32)
    rowmax = pltpu.VMEM((2, ROW_TILE, 1), F32)
    return pl.pallas_call(
        kern,
        grid=(bsz, DA_HEADS, n_tiles),
        in_specs=[pl.BlockSpec(lam_vecs.shape, lambda b, h, t: (0, 0)),
                  pl.BlockSpec((1, LANES), lambda b, h, t: (0, 0)),
                  q_spec, q_next, kv_spec, kv_spec],
        out_specs=q_spec,
        out_shape=jax.ShapeDtypeStruct((bsz, n_rows, D_MODEL), BF16),
        scratch_shapes=[score, score, rowmax, rowmax,
                        pltpu.VMEM((n_rows, 2 * LANES), BF16)],
        compiler_params=pltpu.CompilerParams(
            dimension_semantics=("arbitrary", "arbitrary", "arbitrary"),
            vmem_limit_bytes=VMEM_LIMIT),
        name="diff_flash",
    )(lam_vecs, subln_g.reshape(1, LANES), q, q, k, v)


def _proj_residual_kernel(x_ref, z_ref, mod_ref, w_ref, o_ref):
    out = jnp.dot(z_ref[...], w_ref[...], preferred_element_type=F32)
    o_ref[...] = x_ref[...] + mod_ref[2:3, :] * out


def _proj_residual(x, z, mod, w, ctx_tiles, t0):
    bsz, n_rows, _ = x.shape
    n_tiles = n_rows // ROW_TILE - t0
    return pl.pallas_call(
        _proj_residual_kernel,
        grid=(bsz, n_tiles),
        in_specs=[_tile_spec(t0), _tile_spec(t0), _mod_spec(ctx_tiles, t0),
                  _const_spec((D_MODEL, D_MODEL))],
        out_specs=_tile_spec(),
        out_shape=jax.ShapeDtypeStruct((bsz, n_tiles * ROW_TILE, D_MODEL), F32),
        compiler_params=_params(),
        name="proj_residual",
    )(x, z, mod, w)


def _rwkv_mix(y, bonus, gate, lnx_g, lnx_b):
    mu = _group_sum(y) * (1.0 / RW_HEAD)
    yc = y - mu
    var = _group_sum(yc * yc) * (1.0 / RW_HEAD)
    yn = yc * lax.rsqrt(var + LNX_EPS) * lnx_g + lnx_b
    return ((yn + bonus) * gate).astype(BF16)


def _ffn_kernel(*refs, n_tiles, ctx_tiles, t0, d_ff, final, mixer):
    cat = lambda trio: jnp.concatenate([r[...] for r in trio], axis=0)
    xcat = cat(refs[:3])
    refs = refs[3:]
    zcat = None
    if mixer == 'proj':
        zcat, wz_ref = cat(refs[:3]), refs[3]
        refs = refs[4:]
    elif mixer == 'rwkv':
        yf, yb, bonus, gate = (cat(refs[3 * i:3 * i + 3]) for i in range(4))
        lg_ref, lb_ref, wz_ref = refs[12:15]
        refs = refs[15:]
        zcat = _rwkv_mix(yf.astype(F32) + yb.astype(F32), bonus, gate, lg_ref[...], lb_ref[...])
    mod_ref, g_ref, wup_ref, cw_ref, cb_ref, wdn_ref, fin_ref, o_ref = refs
    t = pl.program_id(1) + t0
    prev_ok, next_ok = _edge_flags(t, n_tiles, ctx_tiles)
    mod = mod_ref[...]
    g = g_ref[...]
    if zcat is not None:
        xcat = xcat + mod[2:3] * jnp.dot(zcat, wz_ref[...], preferred_element_type=F32)
    x = xcat[:ROW_TILE]
    hcat = _norm_mod(xcat, g, mod[3:4], mod[4:5]).astype(BF16)
    h = hcat[:ROW_TILE]
    halo = (xcat.shape[0] - ROW_TILE) // 2
    tiles = list(range(0, d_ff, FF_TILE))

    def up(f):
        gate = jnp.dot(hcat, wup_ref[:, f:f + FF_TILE], preferred_element_type=F32)
        val = jnp.dot(h, wup_ref[:, d_ff + f:d_ff + f + FF_TILE], preferred_element_type=F32)
        return gate, val

    acc = jnp.zeros((ROW_TILE, D_MODEL), F32)
    nxt = up(tiles[0])
    for i, f in enumerate(tiles):
        gate_all, val = nxt
        if i + 1 < len(tiles):
            nxt = up(tiles[i + 1])
        gate = gate_all[:ROW_TILE]
        g_prev, g_next = _shifted(gate, gate_all[ROW_TILE:ROW_TILE + halo],
                                  gate_all[ROW_TILE + halo:], prev_ok, next_ok)
        cw = cw_ref[:, f:f + FF_TILE]
        conv = g_prev * cw[0:1] + gate * cw[1:2] + g_next * cw[2:3] + cb_ref[:, f:f + FF_TILE]
        act = conv * _sigmoid(conv) * val
        acc = acc + jnp.dot(act.astype(BF16), wdn_ref[f:f + FF_TILE, :],
                            preferred_element_type=F32)
    y = x + mod[5:6] * acc
    if final:
        y = _rms(y) * fin_ref[...]
    o_ref[...] = y


def _ffn(x, mod, g, w_up, conv_w, conv_b, w_down, final_g, ctx_tiles, final, mixer=None, t0=0):
    bsz, n_rows, _ = x.shape
    n_tiles = n_rows // ROW_TILE
    d_ff = w_down.shape[0]
    halo = SUBLANES if mixer is None else 2 * SUBLANES
    prev, nxt = _halo_specs(n_rows, t0, rows=halo)
    trio = [_tile_spec(t0), prev, nxt]
    args, specs = [x, x, x], list(trio)
    if mixer is not None:
        streams = mixer[1:2] if mixer[0] == 'proj' else mixer[1:5]
        for arr in streams:
            args += [arr] * 3
            specs += trio
        for arr in mixer[1 + len(streams):]:
            args.append(arr)
            specs.append(_const_spec(arr.shape))
    args += [mod, g, w_up, conv_w, conv_b, w_down, final_g]
    specs += [_mod_spec(ctx_tiles, t0), _const_spec((1, D_MODEL)),
              _const_spec((D_MODEL, 2 * d_ff)), _const_spec((3, d_ff)),
              _const_spec((1, d_ff)), _const_spec((d_ff, D_MODEL)), _const_spec((1, D_MODEL))]
    kern = functools.partial(_ffn_kernel, n_tiles=n_tiles, ctx_tiles=ctx_tiles, t0=t0, d_ff=d_ff,
                             final=final, mixer=None if mixer is None else mixer[0])
    return pl.pallas_call(
        kern,
        grid=(bsz, n_tiles - t0),
        in_specs=specs,
        out_specs=_tile_spec(),
        out_shape=jax.ShapeDtypeStruct((bsz, (n_tiles - t0) * ROW_TILE, D_MODEL), F32),
        compiler_params=_params(),
        name="conv_glu",
    )(*args)


def _rwkv_proj_kernel(*refs, n_tiles, ctx_tiles, has_vfirst):
    (x_ref, xp_ref, xn_ref, mod_ref, g_ref, mp_ref, mn_ref, wr_ref, wk_ref, wv_ref,
     w0_ref, w1_ref, w2_ref, a0_ref, a1_ref, a2_ref, kk_ref, ka_ref, rk_ref,
     g1_ref, g2_ref) = refs[:21]
    refs = refs[21:]
    if has_vfirst:
        vf_ref, v0_ref, v1_ref, v2_ref = refs[:4]
        refs = refs[4:]
    r_ref, k_ref, v_ref, kkn_ref, bonus_ref, gate_ref, lw_ref, a_ref = refs

    t = pl.program_id(1)
    prev_ok, next_ok = _edge_flags(t, n_tiles, ctx_tiles)
    mod = mod_ref[...]
    g = g_ref[...]
    h = _norm_mod(x_ref[...], g, mod[0:1], mod[1:2])
    hp = _norm_mod(xp_ref[...], g, mod[0:1], mod[1:2])
    hn = _norm_mod(xn_ref[...], g, mod[0:1], mod[1:2])
    h_prev, h_next = _shifted(h, hp, hn, prev_ok, next_ok)
    hb = h.astype(BF16)
    xx_p = (h_prev - h).astype(BF16)
    xx_n = (h_next - h).astype(BF16)
    mp = mp_ref[...].astype(BF16)
    mn = mn_ref[...].astype(BF16)

    def mix(m):
        return hb + xx_p * mp[m:m + 1] + xx_n * mn[m:m + 1]

    lo = _lane_iota((ROW_TILE, LANES)) < RW_HEAD

    def halves(z):
        return jnp.where(lo, z, 0.0), jnp.where(lo, 0.0, z)

    r = jnp.dot(mix(0), wr_ref[...], preferred_element_type=F32)
    r_ref[...] = r

    lw = halves(jnp.tanh(jnp.dot(mix(1), w1_ref[...], preferred_element_type=F32)))
    la = halves(jnp.dot(mix(4), a1_ref[...], preferred_element_type=F32))
    a_gate = []
    for d in range(2):
        wl = w0_ref[d:d + 1, :] + _dot(lw[d], w2_ref[...])
        lw_ref[d] = _sigmoid(wl) * (-math.exp(-0.5))
        a_d = _sigmoid(a0_ref[d:d + 1, :] + _dot(la[d], a2_ref[...]))
        a_ref[d] = a_d
        a_gate.append(a_d)

    k = jnp.dot(mix(2), wk_ref[...], preferred_element_type=F32)
    k_ref[...] = k
    kk = k * kk_ref[...]
    kkn_ref[...] = kk * lax.rsqrt(jnp.maximum(_group_sum(kk * kk), 1e-24))

    xv = mix(3)
    v = jnp.dot(xv, wv_ref[...], preferred_element_type=F32)
    if has_vfirst:
        lv = jnp.dot(xv, v1_ref[...], preferred_element_type=F32)
        v = v + (vf_ref[...] - v) * _sigmoid(v0_ref[...] + _dot(lv, v2_ref[...]))
    v_ref[...] = v

    ka = ka_ref[...]
    k_sum = k * (2.0 + (a_gate[0] + a_gate[1] - 2.0) * ka)
    bonus_ref[...] = (_group_sum(r * k_sum * rk_ref[...]) * v).astype(BF16)

    gl = _sigmoid(jnp.dot(mix(5), g1_ref[...], preferred_element_type=F32))
    gate_ref[...] = _dot(gl, g2_ref[...]).astype(BF16)


def _rwkv_proj(x, mod, g, p, v_first, ctx_tiles):
    bsz, n_rows, _ = x.shape
    n_tiles = n_rows // ROW_TILE
    prev, nxt = _halo_specs(n_rows)
    has_vfirst = v_first is not None
    args = [x, x, x, mod, g, p['mix_prev'], p['mix_next'], p['w_r'], p['w_k'], p['w_v'],
            p['w0'], p['w1'], p['w2'], p['a0'], p['a1'], p['a2'], p['k_k'], p['k_a'], p['r_k'],
            p['g1'], p['g2']]
    specs = [_tile_spec(), prev, nxt, _mod_spec(ctx_tiles)] + [
        _const_spec(a.shape) for a in args[4:]]
    if has_vfirst:
        args += [v_first, p['v0'], p['v1'], p['v2']]
        specs += [_tile_spec()] + [_const_spec(a.shape) for a in args[-3:]]
    one = jax.ShapeDtypeStruct((bsz, n_rows, D_MODEL), F32)
    half = jax.ShapeDtypeStruct((bsz, n_rows, D_MODEL), BF16)
    two = jax.ShapeDtypeStruct((2, bsz, n_rows, D_MODEL), F32)
    two_spec = pl.BlockSpec((2, None, ROW_TILE, D_MODEL), lambda b, t: (0, b, t, 0))
    kern = functools.partial(_rwkv_proj_kernel, n_tiles=n_tiles, ctx_tiles=ctx_tiles,
                             has_vfirst=has_vfirst)
    return pl.pallas_call(
        kern,
        grid=(bsz, n_tiles),
        in_specs=specs,
        out_specs=[_tile_spec()] * 6 + [two_spec, two_spec],
        out_shape=[one] * 4 + [half, half, two, two],
        compiler_params=_params(),
        name="rwkv_proj",
    )(*args)


def _mask_stack(x, n_blocks, width):
    blk = _lane_iota(x.shape) // width
    zero = jnp.zeros_like(x)
    return jnp.concatenate([jnp.where(blk == i, x, zero) for i in range(n_blocks)], axis=0)


def _wkv_prep_kernel(r_ref, k_ref, v_ref, kk_ref, lw_ref, a_ref, ka_ref,
                     rh_ref, yl_ref, mt_ref, gt_ref):
    n = WKV_CHUNK
    hpg = LANES // n
    gw = hpg * RW_HEAD
    groups = D_MODEL // gw
    ri = lax.broadcasted_iota(jnp.int32, (n, n), 0)
    ci = lax.broadcasted_iota(jnp.int32, (n, n), 1)
    row = _row_iota((n, LANES))
    src = _lane_iota((n, LANES)) % n
    eye = (src == row).astype(F32)
    order = ((ci <= ri).astype(BF16), src < row, src <= row, n - 1), \
            ((ci >= ri).astype(BF16), src > row, src >= row, 0)
    lo_st = _lane_iota((RW_HEAD, LANES)) < RW_HEAD
    diag = (_lane_iota((RW_HEAD, LANES)) % RW_HEAD) == _row_iota((RW_HEAD, LANES))
    ka = ka_ref[...]
    bf = lambda z: z.astype(BF16)

    def body(it, carry):
        items = []
        for cc in range(PREP_CHUNKS):
            c = it * PREP_CHUNKS + cc
            rows = pl.ds(pl.multiple_of(c * n, n), n)
            kk = kk_ref[rows, :]
            k = k_ref[rows, :]
            r = r_ref[rows, :]
            v = bf(v_ref[rows, :])
            for d in range(2):
                tri, strict, incl, last = order[d]
                lw = lw_ref[d, rows, :]
                gate = a_ref[d, rows, :]
                kd = k * (1.0 + (gate - 1.0) * ka)
                b = kk * gate
                lw_hi = bf(lw)
                lw_lo = bf(lw - lw_hi.astype(F32))
                cum = (jnp.dot(tri, lw_hi, preferred_element_type=F32)
                       + jnp.dot(tri, lw_lo, preferred_element_type=F32))
                cum_end = cum[last:last + 1, :]
                e_neg = jnp.exp(-cum)
                e_rem = jnp.exp(cum_end - cum)
                rt = r * jnp.exp(cum)
                full = dict(c=c, d=d, rows=rows, rt=rt, w_end=jnp.exp(cum_end), v=v,
                            at=bf(-kk * jnp.exp(cum - lw)), rtb=bf(rt), kt=bf(kd * e_neg),
                            bt=bf(b * e_neg), kh=bf(kd * e_rem), bh=bf(b * e_rem),
                            strict=strict, incl=incl)
                for g in range(groups):
                    items.append(dict(full=full, sl=slice(g * gw, (g + 1) * gw)))

        for it_ in items:
            f, sl = it_['full'], it_['sl']
            lhs = jnp.concatenate([f['at'][:, sl], f['rtb'][:, sl]], axis=0)
            rhs = jnp.concatenate([_mask_stack(f['kt'][:, sl], hpg, RW_HEAD),
                                   _mask_stack(f['bt'][:, sl], hpg, RW_HEAD)], axis=0)
            a = lax.dot_general(lhs, rhs, (((1,), (1,)), ((), ())), preferred_element_type=F32)
            it_['aak'] = jnp.where(f['strict'], a[:n, :LANES], 0.0)
            it_['ark'] = jnp.where(f['incl'], a[n:, :LANES], 0.0)
            it_['arb'] = bf(jnp.where(f['incl'], a[n:, LANES:], 0.0))
            pw = jnp.where(f['strict'], a[:n, LANES:], 0.0)
            it_['inv'] = eye + pw
            it_['pw'] = bf(pw)

        for it_ in items:
            it_['pw'] = bf(jnp.dot(it_['pw'], _mask_stack(it_['pw'], hpg, n),
                                   preferred_element_type=F32))
        steps = 2
        while steps < n:
            final = 2 * steps >= n
            for it_ in items:
                rhs = _mask_stack(bf(it_['inv']), hpg, n)
                if not final:
                    rhs = jnp.concatenate([_mask_stack(it_['pw'], hpg, n), rhs], axis=1)
                x = jnp.dot(it_['pw'], rhs, preferred_element_type=F32)
                it_['inv'] = it_['inv'] + x[:, -LANES:]
                if not final:
                    it_['pw'] = bf(x[:, :LANES])
            steps *= 2

        for it_ in items:
            f, sl = it_['full'], it_['sl']
            vms = _mask_stack(f['v'][:, sl], hpg, RW_HEAD)
            x = jnp.dot(bf(jnp.concatenate([it_['aak'], it_['ark']], axis=0)), vms,
                        preferred_element_type=F32)
            it_['arkv'] = x[n:]
            it_['rhs'] = jnp.concatenate([_mask_stack(f['at'][:, sl], hpg, RW_HEAD),
                                          _mask_stack(bf(x[:n]), hpg, RW_HEAD)], axis=1)
        for it_ in items:
            au = bf(jnp.dot(bf(it_['inv']), it_['rhs'], preferred_element_type=F32))
            it_['ah'] = au[:, :gw]
            it_['ul'] = au[:, gw:]
            it_['rhs'] = jnp.concatenate([_mask_stack(it_['ah'], hpg, RW_HEAD),
                                          _mask_stack(it_['ul'], hpg, RW_HEAD)], axis=1)
        for it_ in items:
            f, sl = it_['full'], it_['sl']
            x = jnp.dot(it_['arb'], it_['rhs'], preferred_element_type=F32)
            rh_ref[f['d'], f['rows'], sl] = bf(f['rt'][:, sl] + x[:, :gw])
            yl_ref[f['d'], f['rows'], sl] = bf(it_['arkv'] + x[:, gw:])
        for it_ in items:
            f, sl = it_['full'], it_['sl']
            for j in range(0, gw, LANES):
                loc = slice(j, j + LANES)
                hs = slice(sl.start + j, sl.start + j + LANES)
                bh = f['bh'][:, hs]
                none = jnp.zeros((n, LANES), BF16)
                x = lax.dot_general(
                    jnp.concatenate([f['kh'][:, hs], bh], axis=0),
                    jnp.concatenate(
                        [jnp.concatenate([f['v'][:, hs], it_['ul'][:, loc]], axis=0),
                         jnp.concatenate([none, it_['ah'][:, loc]], axis=0)], axis=1),
                    (((0,), (0,)), ((), ())), preferred_element_type=F32)
                gf = x[:, :LANES]
                pm = x[:, LANES:]
                mt = jnp.where(lo_st, pm[:RW_HEAD], pm[RW_HEAD:])
                mt_ref[f['d'], f['c'], :, hs] = bf(mt + jnp.where(diag, f['w_end'][:, hs], 0.0))
                gt_ref[f['d'], f['c'], :, hs] = bf(jnp.where(lo_st, gf[:RW_HEAD], gf[RW_HEAD:]))
        return carry

    lax.fori_loop(0, ROW_TILE // (n * PREP_CHUNKS), body, 0)


def _wkv_prep(r, k, v, kk, lw, a, k_a):
    bsz, n_rows, _ = r.shape
    n_tiles = n_rows // ROW_TILE
    per = ROW_TILE // WKV_CHUNK
    dir_spec = pl.BlockSpec((2, None, ROW_TILE, D_MODEL), lambda b, t: (0, b, t, 0))
    st_spec = pl.BlockSpec((2, None, per, RW_HEAD, D_MODEL), lambda b, t: (0, b, t, 0, 0))
    rows = (2, bsz, n_rows, D_MODEL)
    st = (2, bsz, n_rows // WKV_CHUNK, RW_HEAD, D_MODEL)
    return pl.pallas_call(
        _wkv_prep_kernel,
        grid=(bsz, n_tiles),
        in_specs=[_tile_spec()] * 4 + [dir_spec, dir_spec, _const_spec((1, D_MODEL))],
        out_specs=[dir_spec, dir_spec, st_spec, st_spec],
        out_shape=[jax.ShapeDtypeStruct(rows, BF16), jax.ShapeDtypeStruct(rows, BF16),
                   jax.ShapeDtypeStruct(st, BF16), jax.ShapeDtypeStruct(st, BF16)],
        compiler_params=_params(),
        name="wkv_prep",
    )(r, k, v, kk, lw, a, k_a)


def _wkv_scan_kernel(rhf_ref, ylf_ref, mtf_ref, gtf_ref, rhb_ref, ylb_ref, mtb_ref, gtb_ref,
                     yf_ref, yb_ref, state_ref):
    n = WKV_CHUNK
    per = ROW_TILE // n

    @pl.when(pl.program_id(1) == 0)
    def _():
        state_ref[...] = jnp.zeros(state_ref.shape, F32)

    lo = _lane_iota((RW_HEAD, LANES)) < RW_HEAD
    dirs = ((rhf_ref, ylf_ref, mtf_ref, gtf_ref, yf_ref), (rhb_ref, ylb_ref, mtb_ref, gtb_ref, yb_ref))
    for i in range(per):
        for d, (rh_ref, yl_ref, mt_ref, gt_ref, y_ref) in enumerate(dirs):
            c = per - 1 - i if d == 1 else i
            for hp in range(D_MODEL // LANES):
                col = slice(hp * LANES, (hp + 1) * LANES)
                lhs = jnp.concatenate([rh_ref[c * n:(c + 1) * n, col], mt_ref[c, :, col]],
                                      axis=0)
                out = jnp.dot(lhs, state_ref[d, hp].astype(BF16), preferred_element_type=F32)
                y_ref[c * n:(c + 1) * n, col] = (out[:n] + yl_ref[c * n:(c + 1) * n, col]).astype(BF16)
                st = out[n:] + gt_ref[c, :, col]
                state_ref[d, hp] = jnp.concatenate(
                    [jnp.where(lo, st, 0.0), jnp.where(lo, 0.0, st)], axis=0)


def _wkv_scan(rh, yl, mt, gt, ctx_tiles):
    _, bsz, n_rows, _ = rh.shape
    n_tiles = n_rows // ROW_TILE
    per = ROW_TILE // WKV_CHUNK

    def back(t):
        return jnp.where(t < ctx_tiles, ctx_tiles - 1 - t, n_tiles - 1 - (t - ctx_tiles))

    def specs(d, tile):
        row = pl.BlockSpec((None, None, ROW_TILE, D_MODEL), lambda b, t: (d, b, tile(t), 0))
        st = pl.BlockSpec((None, None, per, RW_HEAD, D_MODEL),
                          lambda b, t: (d, b, tile(t), 0, 0))
        return [row, row, st, st]

    fwd = lambda t: t
    y_f = pl.BlockSpec((None, ROW_TILE, D_MODEL), lambda b, t: (b, t, 0))
    y_b = pl.BlockSpec((None, ROW_TILE, D_MODEL), lambda b, t: (b, back(t), 0))
    out = jax.ShapeDtypeStruct((bsz, n_rows, D_MODEL), BF16)
    return pl.pallas_call(
        _wkv_scan_kernel,
        grid=(bsz, n_tiles),
        in_specs=specs(0, fwd) + specs(1, back),
        out_specs=[y_f, y_b],
        out_shape=[out, out],
        scratch_shapes=[pltpu.VMEM((2, D_MODEL // LANES, LANES, LANES), F32)],
        compiler_params=pltpu.CompilerParams(dimension_semantics=("arbitrary", "arbitrary"),
                                             vmem_limit_bytes=VMEM_LIMIT),
        name="wkv_scan",
    )(rh, yl, mt, gt, rh, yl, mt, gt)


def _rope_tables(n_ctx, n_lat):
    t = jnp.arange(n_lat)
    row_pos = (t // GRID_W).astype(F32)
    col_pos = (t % GRID_W).astype(F32)
    n_freq = DA_HEAD_DIM // 4
    inv_freq = ROPE_THETA ** (-jnp.arange(n_freq, dtype=F32) / n_freq)
    ang = jnp.concatenate([row_pos[:, None] * inv_freq, col_pos[:, None] * inv_freq], axis=-1)
    cos, sin = jnp.cos(ang), jnp.sin(ang)
    reps = LANES // DA_HEAD_DIM
    cos = jnp.tile(jnp.concatenate([cos, cos], axis=-1), (1, reps))
    sin = jnp.tile(jnp.concatenate([-sin, sin], axis=-1), (1, reps))
    cos = jnp.concatenate([jnp.ones((n_ctx, LANES), F32), cos], axis=0)
    sin = jnp.concatenate([jnp.zeros((n_ctx, LANES), F32), sin], axis=0)
    return cos, sin


def kernel(x, c, ctx, c_ctx, ada_w, ada_b, norm_mix_g, norm_ffn_g, ffn_w_up, ffn_conv_w, ffn_conv_b, ffn_w_down, da_w_qkv, da_lambda, da_subln_g, da_w_o, rw_mix_prev, rw_mix_next, rw_w_r, rw_w_k, rw_w_v, rw_w0, rw_w1, rw_w2, rw_a0, rw_a1, rw_a2, rw_v0, rw_v1, rw_v2, rw_k_k, rw_k_a, rw_r_k, rw_g1, rw_g2, rw_lnx_g, rw_lnx_b, rw_w_o, final_norm_g):
    bsz, n_lat, d = x.shape
    n_ctx = ctx.shape[1]
    depth = ada_w.shape[0]
    assert d == D_MODEL and n_lat % ROW_TILE == 0 and n_ctx % ROW_TILE == 0
    assert n_lat % GRID_W == 0
    ctx_tiles = n_ctx // ROW_TILE
    row = lambda a: a.reshape(1, -1)
    cat = lambda a: jnp.concatenate([a[0], a[1]], axis=0 if a.shape[1] == RW_HEAD else 1)

    pad = (-(bsz + 1)) % SUBLANES
    cc = jnp.concatenate([c, c_ctx[None, :], jnp.zeros((pad, d), F32)], axis=0)
    table = _ada_table(cc, ada_w, ada_b)
    mod_l = table[:, :bsz].reshape(depth, bsz, 6, d)
    mod_c = jnp.broadcast_to(table[:, bsz].reshape(depth, 1, 6, d), (depth, bsz, 6, d))
    mods = jnp.stack([mod_c, mod_l], axis=2)
    mods = jnp.pad(mods, ((0, 0), (0, 0), (0, 0), (0, SUBLANES - 6), (0, 0)))

    cos, sin = _rope_tables(n_ctx, n_lat)
    stream = jnp.concatenate([ctx, x], axis=1)
    v_first = None
    no_final = jnp.ones((1, d), F32)

    for i in range(depth):
        last = i == depth - 1
        j = i // 2
        t0 = ctx_tiles if last else 0
        mod = mods[i]
        g_mix = row(norm_mix_g[i])
        mixer = None
        if i % 2 == 0:
            lambda_init = 0.8 - 0.6 * math.exp(-0.3 * i)
            q, k, v = _qkv(stream, mod, g_mix, da_w_qkv[j].astype(BF16), cos, sin, ctx_tiles)
            o = _flash(q, k, v, da_lambda[j], da_subln_g[j], ctx_tiles, lambda_init)
            if last:
                stream = _proj_residual(stream, o, mod, da_w_o[j].astype(BF16), ctx_tiles, t0)
            else:
                mixer = ('proj', o, da_w_o[j].astype(BF16))
        else:
            p = dict(mix_prev=rw_mix_prev[j], mix_next=rw_mix_next[j],
                     w_r=rw_w_r[j].astype(BF16), w_k=rw_w_k[j].astype(BF16),
                     w_v=rw_w_v[j].astype(BF16), w0=rw_w0[j], w1=cat(rw_w1[j]).astype(BF16),
                     w2=cat(rw_w2[j]).astype(BF16), a0=rw_a0[j], a1=cat(rw_a1[j]).astype(BF16),
                     a2=cat(rw_a2[j]).astype(BF16), k_k=row(rw_k_k[j]), k_a=row(rw_k_a[j]),
                     r_k=row(rw_r_k[j]), g1=rw_g1[j].astype(BF16), g2=rw_g2[j].astype(BF16))
            if j > 0:
                n_v = rw_v1.shape[-1]
                p.update(v0=row(rw_v0[j - 1]),
                         v1=jnp.pad(rw_v1[j - 1], ((0, 0), (0, LANES - n_v))).astype(BF16),
                         v2=jnp.pad(rw_v2[j - 1], ((0, LANES - n_v), (0, 0))).astype(BF16))
            r, k, v, kk, bonus, gate, lw, a = _rwkv_proj(stream, mod, g_mix, p, v_first,
                                                         ctx_tiles)
            if v_first is None:
                v_first = v
            rh, yl, mt, gt = _wkv_prep(r, k, v, kk, lw, a, p['k_a'])
            ys = _wkv_scan(rh, yl, mt, gt, ctx_tiles)
            mixer = ('rwkv', ys[0], ys[1], bonus, gate, row(rw_lnx_g[j]), row(rw_lnx_b[j]),
                     rw_w_o[j].astype(BF16))
        fused = mixer is not None
        stream = _ffn(stream, mod, row(norm_ffn_g[i]), ffn_w_up[i].astype(BF16),
                      ffn_conv_w[i], row(ffn_conv_b[i]), ffn_w_down[i].astype(BF16),
                      row(final_norm_g) if last else no_final,
                      ctx_tiles if fused or not last else 0, last, mixer, t0 if fused else 0)
    return stream
```

```python
import functools
import math

import jax
import jax.numpy as jnp
from jax import lax
from jax.experimental import pallas as pl
from jax.experimental.pallas import tpu as pltpu

D_MODEL = 1024
GRID_W = 64
DA_HEAD_DIM = 64
DA_HEADS = D_MODEL // (2 * DA_HEAD_DIM)
ROPE_THETA = 10000.0
RW_HEAD = 64
N_SHIFT_MIX = 6
LNX_EPS = 64e-5
EPS = 1e-6

LANES = 128
SUBLANES = 8
ROW_TILE = 256
WKV_CHUNK = 32
FF_TILE = 256
PREP_CHUNKS = 4
VMEM_LIMIT = 56 * 1024 * 1024

F32 = jnp.float32
BF16 = jnp.bfloat16

Q_SCALE = DA_HEAD_DIM ** -0.5 * math.log2(math.e)


def _dot(a, b):
    return jnp.dot(a.astype(BF16), b.astype(BF16), preferred_element_type=F32)


def _dot_nt(a, b):
    return lax.dot_general(a.astype(BF16), b.astype(BF16), (((1,), (1,)), ((), ())),
                           preferred_element_type=F32)


def _dot_tn(a, b):
    return lax.dot_general(a.astype(BF16), b.astype(BF16), (((0,), (0,)), ((), ())),
                           preferred_element_type=F32)


def _rms(x):
    return x * lax.rsqrt(jnp.mean(x * x, axis=-1, keepdims=True) + EPS)


def _norm_mod(x, g, shift, scale):
    return _rms(x) * g * (1.0 + scale) + shift


def _sigmoid(x):
    return 0.5 * jnp.tanh(0.5 * x) + 0.5


def _lane_iota(shape):
    return lax.broadcasted_iota(jnp.int32, shape, len(shape) - 1)


def _row_iota(shape):
    return lax.broadcasted_iota(jnp.int32, shape, len(shape) - 2)


def _group_ones():
    r = lax.broadcasted_iota(jnp.int32, (LANES, LANES), 0) // RW_HEAD
    c = lax.broadcasted_iota(jnp.int32, (LANES, LANES), 1) // RW_HEAD
    return (r == c).astype(BF16)


def _group_sum(z):
    ones = _group_ones()
    parts = [jnp.dot(z[:, j:j + LANES].astype(BF16), ones, preferred_element_type=F32)
             for j in range(0, z.shape[1], LANES)]
    return jnp.concatenate(parts, axis=1)


def _shifted(h, halo_prev, halo_next, prev_ok, next_ok):
    rows = h.shape[0]
    ri = _row_iota(h.shape)
    first = jnp.where(prev_ok, halo_prev[-1:, :], 0.0)
    last = jnp.where(next_ok, halo_next[0:1, :], 0.0)
    h_prev = jnp.where(ri == 0, first, pltpu.roll(h, 1, 0))
    h_next = jnp.where(ri == rows - 1, last, pltpu.roll(h, rows - 1, 0))
    return h_prev, h_next


def _edge_flags(t, n_tiles, ctx_tiles):
    prev_ok = jnp.logical_and(t != 0, t != ctx_tiles)
    next_ok = jnp.logical_and(t != n_tiles - 1, t != ctx_tiles - 1)
    return prev_ok, next_ok


def _const_spec(shape):
    zeros = (0,) * len(shape)
    return pl.BlockSpec(shape, lambda *_: zeros, pipeline_mode=pl.Buffered(1))


def _tile_spec(t0=0):
    return pl.BlockSpec((None, ROW_TILE, D_MODEL), lambda b, t: (b, t + t0, 0))


def _halo_specs(n_rows, t0=0, rows=SUBLANES):
    per = ROW_TILE // rows
    last = n_rows // rows - 1
    prev = pl.BlockSpec((None, rows, D_MODEL),
                        lambda b, t: (b, jnp.maximum((t + t0) * per - 1, 0), 0))
    nxt = pl.BlockSpec((None, rows, D_MODEL),
                       lambda b, t: (b, jnp.minimum((t + t0 + 1) * per, last), 0))
    return prev, nxt


def _mod_spec(ctx_tiles, t0=0):
    return pl.BlockSpec((None, None, SUBLANES, D_MODEL),
                        lambda b, t: (b, ((t + t0) >= ctx_tiles).astype(jnp.int32), 0, 0))


def _params():
    return pltpu.CompilerParams(vmem_limit_bytes=VMEM_LIMIT)


def _ada_kernel(c_ref, w_ref, b_ref, o_ref):
    c = c_ref[...]
    s = c * _sigmoid(c)
    o_ref[...] = jnp.dot(s, w_ref[...], precision=lax.Precision.HIGHEST,
                         preferred_element_type=F32) + b_ref[...]


def _ada_table(cc, ada_w, ada_b):
    n_layers, _, six_d = ada_w.shape
    rows = cc.shape[0]
    nb = 1536
    return pl.pallas_call(
        _ada_kernel,
        grid=(n_layers, six_d // nb),
        in_specs=[pl.BlockSpec((rows, D_MODEL), lambda l, n: (0, 0)),
                  pl.BlockSpec((None, D_MODEL, nb), lambda l, n: (l, 0, n)),
                  pl.BlockSpec((None, 1, nb), lambda l, n: (l, 0, n))],
        out_specs=pl.BlockSpec((None, rows, nb), lambda l, n: (l, 0, n)),
        out_shape=jax.ShapeDtypeStruct((n_layers, rows, six_d), F32),
        compiler_params=_params(),
        name="ada_table",
    )(cc, ada_w, ada_b.reshape(n_layers, 1, six_d))


def _rope(slab, cos, sin_signed):
    lane = _lane_iota(slab.shape)
    first = (lane % DA_HEAD_DIM) < (DA_HEAD_DIM // 2)
    partner = jnp.where(first, pltpu.roll(slab, LANES - DA_HEAD_DIM // 2, 1),
                        pltpu.roll(slab, DA_HEAD_DIM // 2, 1))
    return slab * cos + partner * sin_signed


def _qkv_kernel(x_ref, mod_ref, g_ref, w_ref, cos_ref, sin_ref, q_ref, k_ref, v_ref):
    mod = mod_ref[...]
    h = _norm_mod(x_ref[...], g_ref[...], mod[0:1], mod[1:2]).astype(BF16)
    cos = cos_ref[...]
    sin = sin_ref[...]
    wide = 2 * LANES
    for j in range(0, D_MODEL, wide):
        qa = jnp.dot(h, w_ref[:, j:j + wide], preferred_element_type=F32)
        ka = jnp.dot(h, w_ref[:, D_MODEL + j:D_MODEL + j + wide], preferred_element_type=F32)
        for i in range(0, wide, LANES):
            q_ref[:, j + i:j + i + LANES] = (
                _rope(qa[:, i:i + LANES], cos, sin) * Q_SCALE).astype(BF16)
            k_ref[:, j + i:j + i + LANES] = _rope(ka[:, i:i + LANES], cos, sin).astype(BF16)
        v_ref[:, j:j + wide] = jnp.dot(
            h, w_ref[:, 2 * D_MODEL + j:2 * D_MODEL + j + wide],
            preferred_element_type=F32).astype(BF16)


def _qkv(x, mod, g, w_qkv, cos, sin, ctx_tiles):
    bsz, n_rows, _ = x.shape
    out = jax.ShapeDtypeStruct((bsz, n_rows, D_MODEL), BF16)
    tab = pl.BlockSpec((ROW_TILE, LANES), lambda b, t: (t, 0))
    return pl.pallas_call(
        _qkv_kernel,
        grid=(bsz, n_rows // ROW_TILE),
        in_specs=[_tile_spec(), _mod_spec(ctx_tiles), _const_spec((1, D_MODEL)),
                  _const_spec((D_MODEL, 3 * D_MODEL)), tab, tab],
        out_specs=[_tile_spec(), _tile_spec(), _tile_spec()],
        out_shape=[out, out, out],
        compiler_params=_params(),
        name="attn_qkv",
    )(x, mod, g, w_qkv, cos, sin)


def _flash_kernel(lam_ref, g_ref, q_ref, qn_ref, k_ref, v_ref, o_ref,
                  sa_ref, sb_ref, ma_ref, mb_ref, vx_ref, *, ctx_tiles, lambda_init):
    t = pl.program_id(2)
    n_keys = k_ref.shape[0]
    ctx_keys = ctx_tiles * ROW_TILE
    lv = lam_ref[...]
    lam = (jnp.exp(jnp.sum(lv[0:1] * lv[1:2], axis=-1, keepdims=True))
           - jnp.exp(jnp.sum(lv[2:3] * lv[3:4], axis=-1, keepdims=True)) + lambda_init)

    @pl.when(t == 0)
    def _():
        vx_ref[:, :LANES] = v_ref[...]
        vx_ref[:, LANES:] = (_lane_iota((n_keys, LANES)) == 0).astype(BF16)

    def scores(q, n):
        lo = _lane_iota(q.shape) < DA_HEAD_DIM
        zero = jnp.zeros_like(q)
        k = k_ref[0:n, :]
        return [lax.dot_general(qc, k, (((1,), (1,)), ((), ())), preferred_element_type=F32)
                for qc in (jnp.where(lo, q, zero), jnp.where(lo, zero, q))]

    def stash(q, s_ref, m_ref, comps=(0, 1)):
        lo = _lane_iota(q.shape) < DA_HEAD_DIM
        zero = jnp.zeros_like(q)
        for c in comps:
            qc = jnp.where(lo, q, zero) if c == 0 else jnp.where(lo, zero, q)
            s = lax.dot_general(qc, k_ref[...], (((1,), (1,)), ((), ())),
                                preferred_element_type=F32)
            s_ref[c] = s
            m_ref[c] = jnp.max(s, axis=-1, keepdims=True)

    def attend(shifted, n):
        outs = []
        for x in shifted:
            p = jnp.exp2(x.astype(BF16))
            acc = jnp.dot(p, vx_ref[0:n, :], preferred_element_type=F32)
            outs.append(acc[:, :LANES] / acc[:, LANES:LANES + 1])
        o = outs[0] - lam * outs[1]
        o_ref[...] = (_rms(o) * g_ref[...] * (1.0 - lambda_init)).astype(BF16)

    @pl.when(t < ctx_tiles)
    def _():
        attend([s - jnp.max(s, axis=-1, keepdims=True) for s in scores(q_ref[...], ctx_keys)],
               ctx_keys)

    slots = ((sa_ref, ma_ref), (sb_ref, mb_ref))

    @pl.when(t == ctx_tiles)
    def _():
        stash(q_ref[...], *slots[ctx_tiles % 2])

    def shifted(slot):
        s_ref, m_ref = slots[slot]
        return [s_ref[c] - m_ref[c] for c in range(2)]

    last = pl.num_programs(2) - 1
    for parity in range(2):
        @pl.when(jnp.logical_and(jnp.logical_and(t >= ctx_tiles, t < last), t % 2 == parity))
        def _():
            stash(qn_ref[...], *slots[1 - parity], comps=(0,))
            attend(shifted(parity), n_keys)
            stash(qn_ref[...], *slots[1 - parity], comps=(1,))

        @pl.when(jnp.logical_and(jnp.logical_and(t >= ctx_tiles, t == last), t % 2 == parity))
        def _():
            attend(shifted(parity), n_keys)


def _flash(q, k, v, lam_vecs, subln_g, ctx_tiles, lambda_init):
    bsz, n_rows, _ = q.shape
    n_tiles = n_rows // ROW_TILE
    kv_spec = pl.BlockSpec((None, n_rows, LANES), lambda b, h, t: (b, 0, h))
    q_spec = pl.BlockSpec((None, ROW_TILE, LANES), lambda b, h, t: (b, t, h))
    q_next = pl.BlockSpec((None, ROW_TILE, LANES),
                          lambda b, h, t: (b, jnp.minimum(t + 1, n_tiles - 1), h))
    kern = functools.partial(_flash_kernel, ctx_tiles=ctx_tiles, lambda_init=lambda_init)
    score = pltpu.VMEM((2, ROW_TILE, n_rows), F32)
    rowmax = pltpu.VMEM((2, ROW_TILE, 1), F32)
    return pl.pallas_call(
        kern,
        grid=(bsz, DA_HEADS, n_tiles),
        in_specs=[pl.BlockSpec(lam_vecs.shape, lambda b, h, t: (0, 0)),
                  pl.BlockSpec((1, LANES), lambda b, h, t: (0, 0)),
                  q_spec, q_next, kv_spec, kv_spec],
        out_specs=q_spec,
        out_shape=jax.ShapeDtypeStruct((bsz, n_rows, D_MODEL), BF16),
        scratch_shapes=[score, score, rowmax, rowmax,
                        pltpu.VMEM((n_rows, 2 * LANES), BF16)],
        compiler_params=pltpu.CompilerParams(
            dimension_semantics=("arbitrary", "arbitrary", "arbitrary"),
            vmem_limit_bytes=VMEM_LIMIT),
        name="diff_flash",
    )(lam_vecs, subln_g.reshape(1, LANES), q, q, k, v)


def _proj_residual_kernel(x_ref, z_ref, mod_ref, w_ref, o_ref):
    out = jnp.dot(z_ref[...], w_ref[...], preferred_element_type=F32)
    o_ref[...] = x_ref[...] + mod_ref[2:3, :] * out


def _proj_residual(x, z, mod, w, ctx_tiles, t0):
    bsz, n_rows, _ = x.shape
    n_tiles = n_rows // ROW_TILE - t0
    return pl.pallas_call(
        _proj_residual_kernel,
        grid=(bsz, n_tiles),
        in_specs=[_tile_spec(t0), _tile_spec(t0), _mod_spec(ctx_tiles, t0),
                  _const_spec((D_MODEL, D_MODEL))],
        out_specs=_tile_spec(),
        out_shape=jax.ShapeDtypeStruct((bsz, n_tiles * ROW_TILE, D_MODEL), F32),
        compiler_params=_params(),
        name="proj_residual",
    )(x, z, mod, w)


def _rwkv_mix(y, bonus, gate, lnx_g, lnx_b):
    mu = _group_sum(y) * (1.0 / RW_HEAD)
    yc = y - mu
    var = _group_sum(yc * yc) * (1.0 / RW_HEAD)
    yn = yc * lax.rsqrt(var + LNX_EPS) * lnx_g + lnx_b
    return ((yn + bonus) * gate).astype(BF16)


def _ffn_kernel(*refs, n_tiles, ctx_tiles, t0, d_ff, final, mixer):
    cat = lambda trio: jnp.concatenate([r[...] for r in trio], axis=0)
    xcat = cat(refs[:3])
    refs = refs[3:]
    zcat = None
    if mixer == 'proj':
        zcat, wz_ref = cat(refs[:3]), refs[3]
        refs = refs[4:]
    elif mixer == 'rwkv':
        yf, yb, bonus, gate = (cat(refs[3 * i:3 * i + 3]) for i in range(4))
        lg_ref, lb_ref, wz_ref = refs[12:15]
        refs = refs[15:]
        zcat = _rwkv_mix(yf.astype(F32) + yb.astype(F32), bonus, gate, lg_ref[...], lb_ref[...])
    mod_ref, g_ref, wup_ref, cw_ref, cb_ref, wdn_ref, fin_ref, o_ref = refs
    t = pl.program_id(1) + t0
    prev_ok, next_ok = _edge_flags(t, n_tiles, ctx_tiles)
    mod = mod_ref[...]
    g = g_ref[...]
    if zcat is not None:
        xcat = xcat + mod[2:3] * jnp.dot(zcat, wz_ref[...], preferred_element_type=F32)
    x = xcat[:ROW_TILE]
    hcat = _norm_mod(xcat, g, mod[3:4], mod[4:5]).astype(BF16)
    h = hcat[:ROW_TILE]
    halo = (xcat.shape[0] - ROW_TILE) // 2
    tiles = list(range(0, d_ff, FF_TILE))

    def up(f):
        gate = jnp.dot(hcat, wup_ref[:, f:f + FF_TILE], preferred_element_type=F32)
        val = jnp.dot(h, wup_ref[:, d_ff + f:d_ff + f + FF_TILE], preferred_element_type=F32)
        return gate, val

    acc = jnp.zeros((ROW_TILE, D_MODEL), F32)
    nxt = up(tiles[0])
    for i, f in enumerate(tiles):
        gate_all, val = nxt
        if i + 1 < len(tiles):
            nxt = up(tiles[i + 1])
        gate = gate_all[:ROW_TILE]
        g_prev, g_next = _shifted(gate, gate_all[ROW_TILE:ROW_TILE + halo],
                                  gate_all[ROW_TILE + halo:], prev_ok, next_ok)
        cw = cw_ref[:, f:f + FF_TILE]
        conv = g_prev * cw[0:1] + gate * cw[1:2] + g_next * cw[2:3] + cb_ref[:, f:f + FF_TILE]
        act = conv * _sigmoid(conv) * val
        acc = acc + jnp.dot(act.astype(BF16), wdn_ref[f:f + FF_TILE, :],
                            preferred_element_type=F32)
    y = x + mod[5:6] * acc
    if final:
        y = _rms(y) * fin_ref[...]
    o_ref[...] = y


def _ffn(x, mod, g, w_up, conv_w, conv_b, w_down, final_g, ctx_tiles, final, mixer=None, t0=0):
    bsz, n_rows, _ = x.shape
    n_tiles = n_rows // ROW_TILE
    d_ff = w_down.shape[0]
    halo = SUBLANES if mixer is None else 2 * SUBLANES
    prev, nxt = _halo_specs(n_rows, t0, rows=halo)
    trio = [_tile_spec(t0), prev, nxt]
    args, specs = [x, x, x], list(trio)
    if mixer is not None:
        streams = mixer[1:2] if mixer[0] == 'proj' else mixer[1:5]
        for arr in streams:
            args += [arr] * 3
            specs += trio
        for arr in mixer[1 + len(streams):]:
            args.append(arr)
            specs.append(_const_spec(arr.shape))
    args += [mod, g, w_up, conv_w, conv_b, w_down, final_g]
    specs += [_mod_spec(ctx_tiles, t0), _const_spec((1, D_MODEL)),
              _const_spec((D_MODEL, 2 * d_ff)), _const_spec((3, d_ff)),
              _const_spec((1, d_ff)), _const_spec((d_ff, D_MODEL)), _const_spec((1, D_MODEL))]
    kern = functools.partial(_ffn_kernel, n_tiles=n_tiles, ctx_tiles=ctx_tiles, t0=t0, d_ff=d_ff,
                             final=final, mixer=None if mixer is None else mixer[0])
    return pl.pallas_call(
        kern,
        grid=(bsz, n_tiles - t0),
        in_specs=specs,
        out_specs=_tile_spec(),
        out_shape=jax.ShapeDtypeStruct((bsz, (n_tiles - t0) * ROW_TILE, D_MODEL), F32),
        compiler_params=_params(),
        name="conv_glu",
    )(*args)


def _rwkv_proj_kernel(*refs, n_tiles, ctx_tiles, has_vfirst):
    (x_ref, xp_ref, xn_ref, mod_ref, g_ref, mp_ref, mn_ref, wr_ref, wk_ref, wv_ref,
     w0_ref, w1_ref, w2_ref, a0_ref, a1_ref, a2_ref, kk_ref, ka_ref, rk_ref,
     g1_ref, g2_ref) = refs[:21]
    refs = refs[21:]
    if has_vfirst:
        vf_ref, v0_ref, v1_ref, v2_ref = refs[:4]
        refs = refs[4:]
    r_ref, k_ref, v_ref, kkn_ref, bonus_ref, gate_ref, lw_ref, a_ref = refs

    t = pl.program_id(1)
    prev_ok, next_ok = _edge_flags(t, n_tiles, ctx_tiles)
    mod = mod_ref[...]
    g = g_ref[...]
    h = _norm_mod(x_ref[...], g, mod[0:1], mod[1:2])
    hp = _norm_mod(xp_ref[...], g, mod[0:1], mod[1:2])
    hn = _norm_mod(xn_ref[...], g, mod[0:1], mod[1:2])
    h_prev, h_next = _shifted(h, hp, hn, prev_ok, next_ok)
    hb = h.astype(BF16)
    xx_p = (h_prev - h).astype(BF16)
    xx_n = (h_next - h).astype(BF16)
    mp = mp_ref[...].astype(BF16)
    mn = mn_ref[...].astype(BF16)

    def mix(m):
        return hb + xx_p * mp[m:m + 1] + xx_n * mn[m:m + 1]

    lo = _lane_iota((ROW_TILE, LANES)) < RW_HEAD

    def halves(z):
        return jnp.where(lo, z, 0.0), jnp.where(lo, 0.0, z)

    r = jnp.dot(mix(0), wr_ref[...], preferred_element_type=F32)
    r_ref[...] = r.astype(BF16)

    lw = halves(jnp.tanh(jnp.dot(mix(1), w1_ref[...], preferred_element_type=F32)))
    la = halves(jnp.dot(mix(4), a1_ref[...], preferred_element_type=F32))
    a_gate = []
    for d in range(2):
        wl = w0_ref[d:d + 1, :] + _dot(lw[d], w2_ref[...])
        lw_ref[d] = _sigmoid(wl) * (-math.exp(-0.5))
        a_d = _sigmoid(a0_ref[d:d + 1, :] + _dot(la[d], a2_ref[...]))
        a_ref[d] = a_d.astype(BF16)
        a_gate.append(a_d)

    k = jnp.dot(mix(2), wk_ref[...], preferred_element_type=F32)
    k_ref[...] = k.astype(BF16)
    kk = k * kk_ref[...]
    kkn_ref[...] = (kk * lax.rsqrt(jnp.maximum(_group_sum(kk * kk), 1e-24))).astype(BF16)

    xv = mix(3)
    v = jnp.dot(xv, wv_ref[...], preferred_element_type=F32)
    if has_vfirst:
        lv = jnp.dot(xv, v1_ref[...], preferred_element_type=F32)
        v = v + (vf_ref[...] - v) * _sigmoid(v0_ref[...] + _dot(lv, v2_ref[...]))
    v_ref[...] = v.astype(BF16)

    ka = ka_ref[...]
    k_sum = k * (2.0 + (a_gate[0] + a_gate[1] - 2.0) * ka)
    bonus_ref[...] = (_group_sum(r * k_sum * rk_ref[...]) * v).astype(BF16)

    gl = _sigmoid(jnp.dot(mix(5), g1_ref[...], preferred_element_type=F32))
    gate_ref[...] = _dot(gl, g2_ref[...]).astype(BF16)


def _rwkv_proj(x, mod, g, p, v_first, ctx_tiles):
    bsz, n_rows, _ = x.shape
    n_tiles = n_rows // ROW_TILE
    prev, nxt = _halo_specs(n_rows)
    has_vfirst = v_first is not None
    args = [x, x, x, mod, g, p['mix_prev'], p['mix_next'], p['w_r'], p['w_k'], p['w_v'],
            p['w0'], p['w1'], p['w2'], p['a0'], p['a1'], p['a2'], p['k_k'], p['k_a'], p['r_k'],
            p['g1'], p['g2']]
    specs = [_tile_spec(), prev, nxt, _mod_spec(ctx_tiles)] + [
        _const_spec(a.shape) for a in args[4:]]
    if has_vfirst:
        args += [v_first, p['v0'], p['v1'], p['v2']]
        specs += [_tile_spec()] + [_const_spec(a.shape) for a in args[-3:]]
    half = jax.ShapeDtypeStruct((bsz, n_rows, D_MODEL), BF16)
    two = jax.ShapeDtypeStruct((2, bsz, n_rows, D_MODEL), F32)
    two_spec = pl.BlockSpec((2, None, ROW_TILE, D_MODEL), lambda b, t: (0, b, t, 0))
    kern = functools.partial(_rwkv_proj_kernel, n_tiles=n_tiles, ctx_tiles=ctx_tiles,
                             has_vfirst=has_vfirst)
    return pl.pallas_call(
        kern,
        grid=(bsz, n_tiles),
        in_specs=specs,
        out_specs=[_tile_spec()] * 6 + [two_spec, two_spec],
        out_shape=[half] * 6 + [two, two.update(dtype=BF16)],
        compiler_params=_params(),
        name="rwkv_proj",
    )(*args)


def _mask_stack(x, n_blocks, width):
    blk = _lane_iota(x.shape) // width
    zero = jnp.zeros_like(x)
    return jnp.concatenate([jnp.where(blk == i, x, zero) for i in range(n_blocks)], axis=0)


def _wkv_prep_kernel(r_ref, k_ref, v_ref, kk_ref, lw_ref, a_ref, ka_ref,
                     rh_ref, yl_ref, mt_ref, gt_ref):
    n = WKV_CHUNK
    hpg = LANES // n
    gw = hpg * RW_HEAD
    groups = D_MODEL // gw
    ri = lax.broadcasted_iota(jnp.int32, (n, n), 0)
    ci = lax.broadcasted_iota(jnp.int32, (n, n), 1)
    row = _row_iota((n, LANES))
    src = _lane_iota((n, LANES)) % n
    eye = (src == row).astype(F32)
    order = ((ci <= ri).astype(BF16), src < row, src <= row, n - 1), \
            ((ci >= ri).astype(BF16), src > row, src >= row, 0)
    lo_st = _lane_iota((RW_HEAD, LANES)) < RW_HEAD
    diag = (_lane_iota((RW_HEAD, LANES)) % RW_HEAD) == _row_iota((RW_HEAD, LANES))
    ka = ka_ref[...]
    bf = lambda z: z.astype(BF16)

    def body(it, carry):
        items = []
        for cc in range(PREP_CHUNKS):
            c = it * PREP_CHUNKS + cc
            rows = pl.ds(pl.multiple_of(c * n, n), n)
            kk = kk_ref[rows, :].astype(F32)
            k = k_ref[rows, :].astype(F32)
            r = r_ref[rows, :].astype(F32)
            v = v_ref[rows, :]
            for d in range(2):
                tri, strict, incl, last = order[d]
                lw = lw_ref[d, rows, :]
                gate = a_ref[d, rows, :].astype(F32)
                kd = k * (1.0 + (gate - 1.0) * ka)
                b = kk * gate
                lw_hi = bf(lw)
                lw_lo = bf(lw - lw_hi.astype(F32))
                cum = (jnp.dot(tri, lw_hi, preferred_element_type=F32)
                       + jnp.dot(tri, lw_lo, preferred_element_type=F32))
                cum_end = cum[last:last + 1, :]
                e_neg = jnp.exp(-cum)
                e_rem = jnp.exp(cum_end - cum)
                rt = r * jnp.exp(cum)
                full = dict(c=c, d=d, rows=rows, rt=rt, w_end=jnp.exp(cum_end), v=v,
                            at=bf(-kk * jnp.exp(cum - lw)), rtb=bf(rt), kt=bf(kd * e_neg),
                            bt=bf(b * e_neg), kh=bf(kd * e_rem), bh=bf(b * e_rem),
                            strict=strict, incl=incl)
                for g in range(groups):
                    items.append(dict(full=full, sl=slice(g * gw, (g + 1) * gw)))

        for it_ in items:
            f, sl = it_['full'], it_['sl']
            lhs = jnp.concatenate([f['at'][:, sl], f['rtb'][:, sl]], axis=0)
            rhs = jnp.concatenate([_mask_stack(f['kt'][:, sl], hpg, RW_HEAD),
                                   _mask_stack(f['bt'][:, sl], hpg, RW_HEAD)], axis=0)
            a = lax.dot_general(lhs, rhs, (((1,), (1,)), ((), ())), preferred_element_type=F32)
            it_['aak'] = jnp.where(f['strict'], a[:n, :LANES], 0.0)
            it_['ark'] = jnp.where(f['incl'], a[n:, :LANES], 0.0)
            it_['arb'] = bf(jnp.where(f['incl'], a[n:, LANES:], 0.0))
            pw = jnp.where(f['strict'], a[:n, LANES:], 0.0)
            it_['inv'] = eye + pw
            it_['pw'] = bf(pw)

        for it_ in items:
            it_['pw'] = bf(jnp.dot(it_['pw'], _mask_stack(it_['pw'], hpg, n),
                                   preferred_element_type=F32))
        steps = 2
        while steps < n:
            final = 2 * steps >= n
            for it_ in items:
                rhs = _mask_stack(bf(it_['inv']), hpg, n)
                if not final:
                    rhs = jnp.concatenate([_mask_stack(it_['pw'], hpg, n), rhs], axis=1)
                x = jnp.dot(it_['pw'], rhs, preferred_element_type=F32)
                it_['inv'] = it_['inv'] + x[:, -LANES:]
                if not final:
                    it_['pw'] = bf(x[:, :LANES])
            steps *= 2

        for it_ in items:
            f, sl = it_['full'], it_['sl']
            vms = _mask_stack(f['v'][:, sl], hpg, RW_HEAD)
            x = jnp.dot(bf(jnp.concatenate([it_['aak'], it_['ark']], axis=0)), vms,
                        preferred_element_type=F32)
            it_['arkv'] = x[n:]
            it_['rhs'] = jnp.concatenate([_mask_stack(f['at'][:, sl], hpg, RW_HEAD),
                                          _mask_stack(bf(x[:n]), hpg, RW_HEAD)], axis=1)
        for it_ in items:
            au = bf(jnp.dot(bf(it_['inv']), it_['rhs'], preferred_element_type=F32))
            it_['ah'] = au[:, :gw]
            it_['ul'] = au[:, gw:]
            it_['rhs'] = jnp.concatenate([_mask_stack(it_['ah'], hpg, RW_HEAD),
                                          _mask_stack(it_['ul'], hpg, RW_HEAD)], axis=1)
        for it_ in items:
            f, sl = it_['full'], it_['sl']
            x = jnp.dot(it_['arb'], it_['rhs'], preferred_element_type=F32)
            rh_ref[f['d'], f['rows'], sl] = bf(f['rt'][:, sl] + x[:, :gw])
            yl_ref[f['d'], f['rows'], sl] = bf(it_['arkv'] + x[:, gw:])
        for it_ in items:
            f, sl = it_['full'], it_['sl']
            for j in range(0, gw, LANES):
                loc = slice(j, j + LANES)
                hs = slice(sl.start + j, sl.start + j + LANES)
                bh = f['bh'][:, hs]
                none = jnp.zeros((n, LANES), BF16)
                x = lax.dot_general(
                    jnp.concatenate([f['kh'][:, hs], bh], axis=0),
                    jnp.concatenate(
                        [jnp.concatenate([f['v'][:, hs], it_['ul'][:, loc]], axis=0),
                         jnp.concatenate([none, it_['ah'][:, loc]], axis=0)], axis=1),
                    (((0,), (0,)), ((), ())), preferred_element_type=F32)
                gf = x[:, :LANES]
                pm = x[:, LANES:]
                mt = jnp.where(lo_st, pm[:RW_HEAD], pm[RW_HEAD:])
                mt_ref[f['d'], f['c'], :, hs] = bf(mt + jnp.where(diag, f['w_end'][:, hs], 0.0))
                gt_ref[f['d'], f['c'], :, hs] = bf(jnp.where(lo_st, gf[:RW_HEAD], gf[RW_HEAD:]))
        return carry

    lax.fori_loop(0, ROW_TILE // (n * PREP_CHUNKS), body, 0)


def _wkv_prep(r, k, v, kk, lw, a, k_a):
    bsz, n_rows, _ = r.shape
    n_tiles = n_rows // ROW_TILE
    per = ROW_TILE // WKV_CHUNK
    dir_spec = pl.BlockSpec((2, None, ROW_TILE, D_MODEL), lambda b, t: (0, b, t, 0))
    st_spec = pl.BlockSpec((2, None, per, RW_HEAD, D_MODEL), lambda b, t: (0, b, t, 0, 0))
    rows = (2, bsz, n_rows, D_MODEL)
    st = (2, bsz, n_rows // WKV_CHUNK, RW_HEAD, D_MODEL)
    return pl.pallas_call(
        _wkv_prep_kernel,
        grid=(bsz, n_tiles),
        in_specs=[_tile_spec()] * 4 + [dir_spec, dir_spec, _const_spec((1, D_MODEL))],
        out_specs=[dir_spec, dir_spec, st_spec, st_spec],
        out_shape=[jax.ShapeDtypeStruct(rows, BF16), jax.ShapeDtypeStruct(rows, BF16),
                   jax.ShapeDtypeStruct(st, BF16), jax.ShapeDtypeStruct(st, BF16)],
        compiler_params=_params(),
        name="wkv_prep",
    )(r, k, v, kk, lw, a, k_a)


def _wkv_scan_kernel(rhf_ref, ylf_ref, mtf_ref, gtf_ref, rhb_ref, ylb_ref, mtb_ref, gtb_ref,
                     yf_ref, yb_ref, state_ref):
    n = WKV_CHUNK
    per = ROW_TILE // n

    @pl.when(pl.program_id(1) == 0)
    def _():
        state_ref[...] = jnp.zeros(state_ref.shape, F32)

    lo = _lane_iota((RW_HEAD, LANES)) < RW_HEAD
    dirs = ((rhf_ref, ylf_ref, mtf_ref, gtf_ref, yf_ref), (rhb_ref, ylb_ref, mtb_ref, gtb_ref, yb_ref))
    for i in range(per):
        for d, (rh_ref, yl_ref, mt_ref, gt_ref, y_ref) in enumerate(dirs):
            c = per - 1 - i if d == 1 else i
            for hp in range(D_MODEL // LANES):
                col = slice(hp * LANES, (hp + 1) * LANES)
                lhs = jnp.concatenate([rh_ref[c * n:(c + 1) * n, col], mt_ref[c, :, col]],
                                      axis=0)
                out = jnp.dot(lhs, state_ref[d, hp].astype(BF16), preferred_element_type=F32)
                y_ref[c * n:(c + 1) * n, col] = (out[:n] + yl_ref[c * n:(c + 1) * n, col]).astype(BF16)
                st = out[n:] + gt_ref[c, :, col]
                state_ref[d, hp] = jnp.concatenate(
                    [jnp.where(lo, st, 0.0), jnp.where(lo, 0.0, st)], axis=0)


def _wkv_scan(rh, yl, mt, gt, ctx_tiles):
    _, bsz, n_rows, _ = rh.shape
    n_tiles = n_rows // ROW_TILE
    per = ROW_TILE // WKV_CHUNK

    def back(t):
        return jnp.where(t < ctx_tiles, ctx_tiles - 1 - t, n_tiles - 1 - (t - ctx_tiles))

    def specs(d, tile):
        row = pl.BlockSpec((None, None, ROW_TILE, D_MODEL), lambda b, t: (d, b, tile(t), 0))
        st = pl.BlockSpec((None, None, per, RW_HEAD, D_MODEL),
                          lambda b, t: (d, b, tile(t), 0, 0))
        return [row, row, st, st]

    fwd = lambda t: t
    y_f = pl.BlockSpec((None, ROW_TILE, D_MODEL), lambda b, t: (b, t, 0))
    y_b = pl.BlockSpec((None, ROW_TILE, D_MODEL), lambda b, t: (b, back(t), 0))
    out = jax.ShapeDtypeStruct((bsz, n_rows, D_MODEL), BF16)
    return pl.pallas_call(
        _wkv_scan_kernel,
        grid=(bsz, n_tiles),
        in_specs=specs(0, fwd) + specs(1, back),
        out_specs=[y_f, y_b],
        out_shape=[out, out],
        scratch_shapes=[pltpu.VMEM((2, D_MODEL // LANES, LANES, LANES), F32)],
        compiler_params=pltpu.CompilerParams(dimension_semantics=("arbitrary", "arbitrary"),
                                             vmem_limit_bytes=VMEM_LIMIT),
        name="wkv_scan",
    )(rh, yl, mt, gt, rh, yl, mt, gt)


def _rope_tables(n_ctx, n_lat):
    t = jnp.arange(n_lat)
    row_pos = (t // GRID_W).astype(F32)
    col_pos = (t % GRID_W).astype(F32)
    n_freq = DA_HEAD_DIM // 4
    inv_freq = ROPE_THETA ** (-jnp.arange(n_freq, dtype=F32) / n_freq)
    ang = jnp.concatenate([row_pos[:, None] * inv_freq, col_pos[:, None] * inv_freq], axis=-1)
    cos, sin = jnp.cos(ang), jnp.sin(ang)
    reps = LANES // DA_HEAD_DIM
    cos = jnp.tile(jnp.concatenate([cos, cos], axis=-1), (1, reps))
    sin = jnp.tile(jnp.concatenate([-sin, sin], axis=-1), (1, reps))
    cos = jnp.concatenate([jnp.ones((n_ctx, LANES), F32), cos], axis=0)
    sin = jnp.concatenate([jnp.zeros((n_ctx, LANES), F32), sin], axis=0)
    return cos, sin


def kernel(x, c, ctx, c_ctx, ada_w, ada_b, norm_mix_g, norm_ffn_g, ffn_w_up, ffn_conv_w, ffn_conv_b, ffn_w_down, da_w_qkv, da_lambda, da_subln_g, da_w_o, rw_mix_prev, rw_mix_next, rw_w_r, rw_w_k, rw_w_v, rw_w0, rw_w1, rw_w2, rw_a0, rw_a1, rw_a2, rw_v0, rw_v1, rw_v2, rw_k_k, rw_k_a, rw_r_k, rw_g1, rw_g2, rw_lnx_g, rw_lnx_b, rw_w_o, final_norm_g):
    bsz, n_lat, d = x.shape
    n_ctx = ctx.shape[1]
    depth = ada_w.shape[0]
    assert d == D_MODEL and n_lat % ROW_TILE == 0 and n_ctx % ROW_TILE == 0
    assert n_lat % GRID_W == 0
    ctx_tiles = n_ctx // ROW_TILE
    row = lambda a: a.reshape(1, -1)
    cat = lambda a: jnp.concatenate([a[0], a[1]], axis=0 if a.shape[1] == RW_HEAD else 1)

    pad = (-(bsz + 1)) % SUBLANES
    cc = jnp.concatenate([c, c_ctx[None, :], jnp.zeros((pad, d), F32)], axis=0)
    table = _ada_table(cc, ada_w, ada_b)
    mod_l = table[:, :bsz].reshape(depth, bsz, 6, d)
    mod_c = jnp.broadcast_to(table[:, bsz].reshape(depth, 1, 6, d), (depth, bsz, 6, d))
    mods = jnp.stack([mod_c, mod_l], axis=2)
    mods = jnp.pad(mods, ((0, 0), (0, 0), (0, 0), (0, SUBLANES - 6), (0, 0)))

    cos, sin = _rope_tables(n_ctx, n_lat)
    stream = jnp.concatenate([ctx, x], axis=1)
    v_first = None
    no_final = jnp.ones((1, d), F32)

    for i in range(depth):
        last = i == depth - 1
        j = i // 2
        t0 = ctx_tiles if last else 0
        mod = mods[i]
        g_mix = row(norm_mix_g[i])
        mixer = None
        if i % 2 == 0:
            lambda_init = 0.8 - 0.6 * math.exp(-0.3 * i)
            q, k, v = _qkv(stream, mod, g_mix, da_w_qkv[j].astype(BF16), cos, sin, ctx_tiles)
            o = _flash(q, k, v, da_lambda[j], da_subln_g[j], ctx_tiles, lambda_init)
            if last:
                stream = _proj_residual(stream, o, mod, da_w_o[j].astype(BF16), ctx_tiles, t0)
            else:
                mixer = ('proj', o, da_w_o[j].astype(BF16))
        else:
            p = dict(mix_prev=rw_mix_prev[j], mix_next=rw_mix_next[j],
                     w_r=rw_w_r[j].astype(BF16), w_k=rw_w_k[j].astype(BF16),
                     w_v=rw_w_v[j].astype(BF16), w0=rw_w0[j], w1=cat(rw_w1[j]).astype(BF16),
                     w2=cat(rw_w2[j]).astype(BF16), a0=rw_a0[j], a1=cat(rw_a1[j]).astype(BF16),
                     a2=cat(rw_a2[j]).astype(BF16), k_k=row(rw_k_k[j]), k_a=row(rw_k_a[j]),
                     r_k=row(rw_r_k[j]), g1=rw_g1[j].astype(BF16), g2=rw_g2[j].astype(BF16))
            if j > 0:
                n_v = rw_v1.shape[-1]
                p.update(v0=row(rw_v0[j - 1]),
                         v1=jnp.pad(rw_v1[j - 1], ((0, 0), (0, LANES - n_v))).astype(BF16),
                         v2=jnp.pad(rw_v2[j - 1], ((0, LANES - n_v), (0, 0))).astype(BF16))
            r, k, v, kk, bonus, gate, lw, a = _rwkv_proj(stream, mod, g_mix, p, v_first,
                                                         ctx_tiles)
            if v_first is None:
                v_first = v
            rh, yl, mt, gt = _wkv_prep(r, k, v, kk, lw, a, p['k_a'])
            ys = _wkv_scan(rh, yl, mt, gt, ctx_tiles)
            mixer = ('rwkv', ys[0], ys[1], bonus, gate, row(rw_lnx_g[j]), row(rw_lnx_b[j]),
                     rw_w_o[j].astype(BF16))
        fused = mixer is not None
        stream = _ffn(stream, mod, row(norm_ffn_g[i]), ffn_w_up[i].astype(BF16),
                      ffn_conv_w[i], row(ffn_conv_b[i]), ffn_w_down[i].astype(BF16),
                      row(final_norm_g) if last else no_final,
                      ctx_tiles if fused or not last else 0, last, mixer, t0 if fused else 0)
    return stream
```

```python
import functools
import math

import jax
import jax.numpy as jnp
from jax import lax
from jax.experimental import pallas as pl
from jax.experimental.pallas import tpu as pltpu

D_MODEL = 1024
GRID_W = 64
DA_HEAD_DIM = 64
DA_HEADS = D_MODEL // (2 * DA_HEAD_DIM)
ROPE_THETA = 10000.0
RW_HEAD = 64
LNX_EPS = 64e-5
EPS = 1e-6

LANES = 128
SUBLANES = 8
ROW_TILE = 256
WKV_CHUNK = 32
FF_TILE = 256
PREP_CHUNKS = 4
VMEM_LIMIT = 56 * 1024 * 1024

F32 = jnp.float32
BF16 = jnp.bfloat16

Q_SCALE = DA_HEAD_DIM ** -0.5 * math.log2(math.e)


def _dot(a, b):
    return jnp.dot(a.astype(BF16), b.astype(BF16), preferred_element_type=F32)


def _rms(x):
    return x * lax.rsqrt(jnp.mean(x * x, axis=-1, keepdims=True) + EPS)


def _norm_mod(x, g, shift, scale):
    return _rms(x) * g * (1.0 + scale) + shift


def _sigmoid(x):
    return 0.5 * jnp.tanh(0.5 * x) + 0.5


def _lane_iota(shape):
    return lax.broadcasted_iota(jnp.int32, shape, len(shape) - 1)


def _row_iota(shape):
    return lax.broadcasted_iota(jnp.int32, shape, len(shape) - 2)


def _group_ones():
    r = lax.broadcasted_iota(jnp.int32, (LANES, LANES), 0) // RW_HEAD
    c = lax.broadcasted_iota(jnp.int32, (LANES, LANES), 1) // RW_HEAD
    return (r == c).astype(BF16)


def _group_sum(z):
    ones = _group_ones()
    parts = [jnp.dot(z[:, j:j + LANES].astype(BF16), ones, preferred_element_type=F32)
             for j in range(0, z.shape[1], LANES)]
    return jnp.concatenate(parts, axis=1)


def _shifted(h, halo_prev, halo_next, prev_ok, next_ok):
    rows = h.shape[0]
    ri = _row_iota(h.shape)
    first = jnp.where(prev_ok, halo_prev[-1:, :], 0.0)
    last = jnp.where(next_ok, halo_next[0:1, :], 0.0)
    h_prev = jnp.where(ri == 0, first, pltpu.roll(h, 1, 0))
    h_next = jnp.where(ri == rows - 1, last, pltpu.roll(h, rows - 1, 0))
    return h_prev, h_next


def _edge_flags(t, n_tiles, ctx_tiles):
    prev_ok = jnp.logical_and(t != 0, t != ctx_tiles)
    next_ok = jnp.logical_and(t != n_tiles - 1, t != ctx_tiles - 1)
    return prev_ok, next_ok


def _const_spec(shape):
    zeros = (0,) * len(shape)
    return pl.BlockSpec(shape, lambda *_: zeros, pipeline_mode=pl.Buffered(1))


def _tile_spec(t0=0):
    return pl.BlockSpec((None, ROW_TILE, D_MODEL), lambda b, t: (b, t + t0, 0))


def _halo_specs(n_rows, t0=0, rows=SUBLANES):
    per = ROW_TILE // rows
    last = n_rows // rows - 1
    prev = pl.BlockSpec((None, rows, D_MODEL),
                        lambda b, t: (b, jnp.maximum((t + t0) * per - 1, 0), 0))
    nxt = pl.BlockSpec((None, rows, D_MODEL),
                       lambda b, t: (b, jnp.minimum((t + t0 + 1) * per, last), 0))
    return prev, nxt


def _mod_spec(ctx_tiles, t0=0):
    return pl.BlockSpec((None, None, SUBLANES, D_MODEL),
                        lambda b, t: (b, ((t + t0) >= ctx_tiles).astype(jnp.int32), 0, 0))


def _params():
    return pltpu.CompilerParams(vmem_limit_bytes=VMEM_LIMIT)


def _ada_kernel(c_ref, w_ref, b_ref, o_ref):
    c = c_ref[...]
    s = c * _sigmoid(c)
    o_ref[...] = jnp.dot(s, w_ref[...], precision=lax.Precision.HIGHEST,
                         preferred_element_type=F32) + b_ref[...]


def _ada_table(cc, ada_w, ada_b):
    n_layers, _, six_d = ada_w.shape
    rows = cc.shape[0]
    nb = 1536
    return pl.pallas_call(
        _ada_kernel,
        grid=(n_layers, six_d // nb),
        in_specs=[pl.BlockSpec((rows, D_MODEL), lambda l, n: (0, 0)),
                  pl.BlockSpec((None, D_MODEL, nb), lambda l, n: (l, 0, n)),
                  pl.BlockSpec((None, 1, nb), lambda l, n: (l, 0, n))],
        out_specs=pl.BlockSpec((None, rows, nb), lambda l, n: (l, 0, n)),
        out_shape=jax.ShapeDtypeStruct((n_layers, rows, six_d), F32),
        compiler_params=_params(),
        name="ada_table",
    )(cc, ada_w, ada_b.reshape(n_layers, 1, six_d))


def _rope(slab, cos, sin_signed):
    lane = _lane_iota(slab.shape)
    first = (lane % DA_HEAD_DIM) < (DA_HEAD_DIM // 2)
    partner = jnp.where(first, pltpu.roll(slab, LANES - DA_HEAD_DIM // 2, 1),
                        pltpu.roll(slab, DA_HEAD_DIM // 2, 1))
    return slab * cos + partner * sin_signed


def _qkv_kernel(x_ref, mod_ref, g_ref, w_ref, cos_ref, sin_ref, q_ref, k_ref, v_ref):
    mod = mod_ref[...]
    h = _norm_mod(x_ref[...], g_ref[...], mod[0:1], mod[1:2]).astype(BF16)
    cos = cos_ref[...]
    sin = sin_ref[...]
    wide = 2 * LANES
    for j in range(0, D_MODEL, wide):
        qa = jnp.dot(h, w_ref[:, j:j + wide], preferred_element_type=F32)
        ka = jnp.dot(h, w_ref[:, D_MODEL + j:D_MODEL + j + wide], preferred_element_type=F32)
        for i in range(0, wide, LANES):
            q_ref[:, j + i:j + i + LANES] = (
                _rope(qa[:, i:i + LANES], cos, sin) * Q_SCALE).astype(BF16)
            k_ref[:, j + i:j + i + LANES] = _rope(ka[:, i:i + LANES], cos, sin).astype(BF16)
        v_ref[:, j:j + wide] = jnp.dot(
            h, w_ref[:, 2 * D_MODEL + j:2 * D_MODEL + j + wide],
            preferred_element_type=F32).astype(BF16)


def _qkv(x, mod, g, w_qkv, cos, sin, ctx_tiles):
    bsz, n_rows, _ = x.shape
    out = jax.ShapeDtypeStruct((bsz, n_rows, D_MODEL), BF16)
    tab = pl.BlockSpec((ROW_TILE, LANES), lambda b, t: (t, 0))
    return pl.pallas_call(
        _qkv_kernel,
        grid=(bsz, n_rows // ROW_TILE),
        in_specs=[_tile_spec(), _mod_spec(ctx_tiles), _const_spec((1, D_MODEL)),
                  _const_spec((D_MODEL, 3 * D_MODEL)), tab, tab],
        out_specs=[_tile_spec(), _tile_spec(), _tile_spec()],
        out_shape=[out, out, out],
        compiler_params=_params(),
        name="attn_qkv",
    )(x, mod, g, w_qkv, cos, sin)


def _flash_kernel(lam_ref, g_ref, q_ref, qn_ref, k_ref, v_ref, o_ref,
                  sa_ref, sb_ref, ma_ref, mb_ref, vx_ref, *, ctx_tiles, lambda_init):
    t = pl.program_id(2)
    n_keys = k_ref.shape[0]
    ctx_keys = ctx_tiles * ROW_TILE
    lv = lam_ref[...]
    lam = (jnp.exp(jnp.sum(lv[0:1] * lv[1:2], axis=-1, keepdims=True))
           - jnp.exp(jnp.sum(lv[2:3] * lv[3:4], axis=-1, keepdims=True)) + lambda_init)

    @pl.when(t == 0)
    def _():
        vx_ref[:, :LANES] = v_ref[...]
        vx_ref[:, LANES:] = (_lane_iota((n_keys, LANES)) == 0).astype(BF16)

    def scores(q, n):
        lo = _lane_iota(q.shape) < DA_HEAD_DIM
        zero = jnp.zeros_like(q)
        k = k_ref[0:n, :]
        return [lax.dot_general(qc, k, (((1,), (1,)), ((), ())), preferred_element_type=F32)
                for qc in (jnp.where(lo, q, zero), jnp.where(lo, zero, q))]

    def stash(q, s_ref, m_ref, comps=(0, 1)):
        lo = _lane_iota(q.shape) < DA_HEAD_DIM
        zero = jnp.zeros_like(q)
        for c in comps:
            qc = jnp.where(lo, q, zero) if c == 0 else jnp.where(lo, zero, q)
            s = lax.dot_general(qc, k_ref[...], (((1,), (1,)), ((), ())),
                                preferred_element_type=F32)
            s_ref[c] = s
            m_ref[c] = jnp.max(s, axis=-1, keepdims=True)

    def attend(shifted, n):
        outs = []
        for x in shifted:
            p = jnp.exp2(x.astype(BF16))
            acc = jnp.dot(p, vx_ref[0:n, :], preferred_element_type=F32)
            outs.append(acc[:, :LANES] / acc[:, LANES:LANES + 1])
        o = outs[0] - lam * outs[1]
        o_ref[...] = (_rms(o) * g_ref[...] * (1.0 - lambda_init)).astype(BF16)

    @pl.when(t < ctx_tiles)
    def _():
        attend([s - jnp.max(s, axis=-1, keepdims=True) for s in scores(q_ref[...], ctx_keys)],
               ctx_keys)

    slots = ((sa_ref, ma_ref), (sb_ref, mb_ref))

    @pl.when(t == ctx_tiles)
    def _():
        stash(q_ref[...], *slots[ctx_tiles % 2])

    def shifted(slot):
        s_ref, m_ref = slots[slot]
        return [s_ref[c] - m_ref[c] for c in range(2)]

    last = pl.num_programs(2) - 1
    for parity in range(2):
        @pl.when(jnp.logical_and(jnp.logical_and(t >= ctx_tiles, t < last), t % 2 == parity))
        def _():
            stash(qn_ref[...], *slots[1 - parity], comps=(0,))
            attend(shifted(parity), n_keys)
            stash(qn_ref[...], *slots[1 - parity], comps=(1,))

        @pl.when(jnp.logical_and(jnp.logical_and(t >= ctx_tiles, t == last), t % 2 == parity))
        def _():
            attend(shifted(parity), n_keys)


def _flash(q, k, v, lam_vecs, subln_g, ctx_tiles, lambda_init):
    bsz, n_rows, _ = q.shape
    n_tiles = n_rows // ROW_TILE
    kv_spec = pl.BlockSpec((None, n_rows, LANES), lambda b, h, t: (b, 0, h))
    q_spec = pl.BlockSpec((None, ROW_TILE, LANES), lambda b, h, t: (b, t, h))
    q_next = pl.BlockSpec((None, ROW_TILE, LANES),
                          lambda b, h, t: (b, jnp.minimum(t + 1, n_tiles - 1), h))
    kern = functools.partial(_flash_kernel, ctx_tiles=ctx_tiles, lambda_init=lambda_init)
    score = pltpu.VMEM((2, ROW_TILE, n_rows), F32)
    rowmax = pltpu.VMEM((2, ROW_TILE, 1), F32)
    return pl.pallas_call(
        kern,
        grid=(bsz, DA_HEADS, n_tiles),
        in_specs=[pl.BlockSpec(lam_vecs.shape, lambda b, h, t: (0, 0)),
                  pl.BlockSpec((1, LANES), lambda b, h, t: (0, 0)),
                  q_spec, q_next, kv_spec, kv_spec],
        out_specs=q_spec,
        out_shape=jax.ShapeDtypeStruct((bsz, n_rows, D_MODEL), BF16),
        scratch_shapes=[score, score, rowmax, rowmax,
                        pltpu.VMEM((n_rows, 2 * LANES), BF16)],
        compiler_params=pltpu.CompilerParams(
            dimension_semantics=("arbitrary", "arbitrary", "arbitrary"),
            vmem_limit_bytes=VMEM_LIMIT),
        name="diff_flash",
    )(lam_vecs, subln_g.reshape(1, LANES), q, q, k, v)


def _proj_residual_kernel(x_ref, z_ref, mod_ref, w_ref, o_ref):
    out = jnp.dot(z_ref[...], w_ref[...], preferred_element_type=F32)
    o_ref[...] = x_ref[...] + mod_ref[2:3, :] * out


def _proj_residual(x, z, mod, w, ctx_tiles, t0):
    bsz, n_rows, _ = x.shape
    n_tiles = n_rows // ROW_TILE - t0
    return pl.pallas_call(
        _proj_residual_kernel,
        grid=(bsz, n_tiles),
        in_specs=[_tile_spec(t0), _tile_spec(t0), _mod_spec(ctx_tiles, t0),
                  _const_spec((D_MODEL, D_MODEL))],
        out_specs=_tile_spec(),
        out_shape=jax.ShapeDtypeStruct((bsz, n_tiles * ROW_TILE, D_MODEL), F32),
        compiler_params=_params(),
        name="proj_residual",
    )(x, z, mod, w)


def _rwkv_mix(y, bonus, gate, lnx_g, lnx_b):
    mu = _group_sum(y) * (1.0 / RW_HEAD)
    yc = y - mu
    var = _group_sum(yc * yc) * (1.0 / RW_HEAD)
    yn = yc * lax.rsqrt(var + LNX_EPS) * lnx_g + lnx_b
    return ((yn + bonus) * gate).astype(BF16)


def _ffn_kernel(*refs, n_tiles, ctx_tiles, t0, d_ff, final, mixer):
    cat = lambda trio: jnp.concatenate([r[...] for r in trio], axis=0)
    xcat = cat(refs[:3])
    refs = refs[3:]
    zcat = None
    if mixer == 'proj':
        zcat, wz_ref = cat(refs[:3]), refs[3]
        refs = refs[4:]
    elif mixer == 'rwkv':
        yf, yb, bonus, gate = (cat(refs[3 * i:3 * i + 3]) for i in range(4))
        lg_ref, lb_ref, wz_ref = refs[12:15]
        refs = refs[15:]
        zcat = _rwkv_mix(yf.astype(F32) + yb.astype(F32), bonus, gate, lg_ref[...], lb_ref[...])
    mod_ref, g_ref, wup_ref, cw_ref, cb_ref, wdn_ref, fin_ref, o_ref = refs
    t = pl.program_id(1) + t0
    prev_ok, next_ok = _edge_flags(t, n_tiles, ctx_tiles)
    mod = mod_ref[...]
    g = g_ref[...]
    if zcat is not None:
        xcat = xcat + mod[2:3] * jnp.dot(zcat, wz_ref[...], preferred_element_type=F32)
    x = xcat[:ROW_TILE]
    hcat = _norm_mod(xcat, g, mod[3:4], mod[4:5]).astype(BF16)
    h = hcat[:ROW_TILE]
    halo = (xcat.shape[0] - ROW_TILE) // 2
    tiles = list(range(0, d_ff, FF_TILE))

    def up(f):
        gate = jnp.dot(hcat, wup_ref[:, f:f + FF_TILE], preferred_element_type=F32)
        val = jnp.dot(h, wup_ref[:, d_ff + f:d_ff + f + FF_TILE], preferred_element_type=F32)
        return gate, val

    acc = jnp.zeros((ROW_TILE, D_MODEL), F32)
    nxt = up(tiles[0])
    for i, f in enumerate(tiles):
        gate_all, val = nxt
        if i + 1 < len(tiles):
            nxt = up(tiles[i + 1])
        gate = gate_all[:ROW_TILE]
        g_prev, g_next = _shifted(gate, gate_all[ROW_TILE:ROW_TILE + halo],
                                  gate_all[ROW_TILE + halo:], prev_ok, next_ok)
        cw = cw_ref[:, f:f + FF_TILE]
        conv = g_prev * cw[0:1] + gate * cw[1:2] + g_next * cw[2:3] + cb_ref[:, f:f + FF_TILE]
        act = conv * _sigmoid(conv) * val
        acc = acc + jnp.dot(act.astype(BF16), wdn_ref[f:f + FF_TILE, :],
                            preferred_element_type=F32)
    y = x + mod[5:6] * acc
    if final:
        y = _rms(y) * fin_ref[...]
    o_ref[...] = y


def _ffn(x, mod, g, w_up, conv_w, conv_b, w_down, final_g, ctx_tiles, final, mixer=None, t0=0):
    bsz, n_rows, _ = x.shape
    n_tiles = n_rows // ROW_TILE
    d_ff = w_down.shape[0]
    halo = SUBLANES if mixer is None else 2 * SUBLANES
    prev, nxt = _halo_specs(n_rows, t0, rows=halo)
    trio = [_tile_spec(t0), prev, nxt]
    args, specs = [x, x, x], list(trio)
    if mixer is not None:
        streams = mixer[1:2] if mixer[0] == 'proj' else mixer[1:5]
        for arr in streams:
            args += [arr] * 3
            specs += trio
        for arr in mixer[1 + len(streams):]:
            args.append(arr)
            specs.append(_const_spec(arr.shape))
    args += [mod, g, w_up, conv_w, conv_b, w_down, final_g]
    specs += [_mod_spec(ctx_tiles, t0), _const_spec((1, D_MODEL)),
              _const_spec((D_MODEL, 2 * d_ff)), _const_spec((3, d_ff)),
              _const_spec((1, d_ff)), _const_spec((d_ff, D_MODEL)), _const_spec((1, D_MODEL))]
    kern = functools.partial(_ffn_kernel, n_tiles=n_tiles, ctx_tiles=ctx_tiles, t0=t0, d_ff=d_ff,
                             final=final, mixer=None if mixer is None else mixer[0])
    return pl.pallas_call(
        kern,
        grid=(bsz, n_tiles - t0),
        in_specs=specs,
        out_specs=_tile_spec(),
        out_shape=jax.ShapeDtypeStruct((bsz, (n_tiles - t0) * ROW_TILE, D_MODEL), F32),
        compiler_params=_params(),
        name="conv_glu",
    )(*args)


def _rwkv_proj_kernel(*refs, n_tiles, ctx_tiles, has_vfirst):
    (x_ref, xp_ref, xn_ref, mod_ref, g_ref, mp_ref, mn_ref, wr_ref, wk_ref, wv_ref,
     w0_ref, w1_ref, w2_ref, a0_ref, a1_ref, a2_ref, kk_ref, ka_ref, rk_ref,
     g1_ref, g2_ref) = refs[:21]
    refs = refs[21:]
    if has_vfirst:
        vf_ref, v0_ref, v1_ref, v2_ref = refs[:4]
        refs = refs[4:]
    r_ref, k_ref, v_ref, kkn_ref, bonus_ref, gate_ref, lw_ref, a_ref = refs

    t = pl.program_id(1)
    prev_ok, next_ok = _edge_flags(t, n_tiles, ctx_tiles)
    mod = mod_ref[...]
    g = g_ref[...]
    h = _norm_mod(x_ref[...], g, mod[0:1], mod[1:2])
    hp = _norm_mod(xp_ref[...], g, mod[0:1], mod[1:2])
    hn = _norm_mod(xn_ref[...], g, mod[0:1], mod[1:2])
    h_prev, h_next = _shifted(h, hp, hn, prev_ok, next_ok)
    hb = h.astype(BF16)
    xx_p = (h_prev - h).astype(BF16)
    xx_n = (h_next - h).astype(BF16)
    mp = mp_ref[...].astype(BF16)
    mn = mn_ref[...].astype(BF16)

    def mix(m):
        return hb + xx_p * mp[m:m + 1] + xx_n * mn[m:m + 1]

    lo = _lane_iota((ROW_TILE, LANES)) < RW_HEAD

    def halves(z):
        return jnp.where(lo, z, 0.0), jnp.where(lo, 0.0, z)

    r = jnp.dot(mix(0), wr_ref[...], preferred_element_type=F32)
    r_ref[...] = r

    lw = halves(jnp.tanh(jnp.dot(mix(1), w1_ref[...], preferred_element_type=F32)))
    la = halves(jnp.dot(mix(4), a1_ref[...], preferred_element_type=F32))
    a_gate = []
    for d in range(2):
        wl = w0_ref[d:d + 1, :] + _dot(lw[d], w2_ref[...])
        lw_ref[d] = _sigmoid(wl) * (-math.exp(-0.5))
        a_d = _sigmoid(a0_ref[d:d + 1, :] + _dot(la[d], a2_ref[...]))
        a_ref[d] = a_d
        a_gate.append(a_d)

    k = jnp.dot(mix(2), wk_ref[...], preferred_element_type=F32)
    k_ref[...] = k
    kk = k * kk_ref[...]
    kkn_ref[...] = kk * lax.rsqrt(jnp.maximum(_group_sum(kk * kk), 1e-24))

    xv = mix(3)
    v = jnp.dot(xv, wv_ref[...], preferred_element_type=F32)
    if has_vfirst:
        lv = jnp.dot(xv, v1_ref[...], preferred_element_type=F32)
        v = v + (vf_ref[...] - v) * _sigmoid(v0_ref[...] + _dot(lv, v2_ref[...]))
    v_ref[...] = v

    ka = ka_ref[...]
    k_sum = k * (2.0 + (a_gate[0] + a_gate[1] - 2.0) * ka)
    bonus_ref[...] = (_group_sum(r * k_sum * rk_ref[...]) * v).astype(BF16)

    gl = _sigmoid(jnp.dot(mix(5), g1_ref[...], preferred_element_type=F32))
    gate_ref[...] = _dot(gl, g2_ref[...]).astype(BF16)


def _rwkv_proj(x, mod, g, p, v_first, ctx_tiles):
    bsz, n_rows, _ = x.shape
    n_tiles = n_rows // ROW_TILE
    prev, nxt = _halo_specs(n_rows)
    has_vfirst = v_first is not None
    args = [x, x, x, mod, g, p['mix_prev'], p['mix_next'], p['w_r'], p['w_k'], p['w_v'],
            p['w0'], p['w1'], p['w2'], p['a0'], p['a1'], p['a2'], p['k_k'], p['k_a'], p['r_k'],
            p['g1'], p['g2']]
    specs = [_tile_spec(), prev, nxt, _mod_spec(ctx_tiles)] + [
        _const_spec(a.shape) for a in args[4:]]
    if has_vfirst:
        args += [v_first, p['v0'], p['v1'], p['v2']]
        specs += [_tile_spec()] + [_const_spec(a.shape) for a in args[-3:]]
    one = jax.ShapeDtypeStruct((bsz, n_rows, D_MODEL), F32)
    half = jax.ShapeDtypeStruct((bsz, n_rows, D_MODEL), BF16)
    two = jax.ShapeDtypeStruct((2, bsz, n_rows, D_MODEL), F32)
    two_spec = pl.BlockSpec((2, None, ROW_TILE, D_MODEL), lambda b, t: (0, b, t, 0))
    kern = functools.partial(_rwkv_proj_kernel, n_tiles=n_tiles, ctx_tiles=ctx_tiles,
                             has_vfirst=has_vfirst)
    return pl.pallas_call(
        kern,
        grid=(bsz, n_tiles),
        in_specs=specs,
        out_specs=[_tile_spec()] * 6 + [two_spec, two_spec],
        out_shape=[one] * 4 + [half, half, two, two],
        compiler_params=_params(),
        name="rwkv_proj",
    )(*args)


def _mask_stack(x, n_blocks, width):
    blk = _lane_iota(x.shape) // width
    zero = jnp.zeros_like(x)
    return jnp.concatenate([jnp.where(blk == i, x, zero) for i in range(n_blocks)], axis=0)


def _wkv_prep_kernel(r_ref, k_ref, v_ref, kk_ref, lw_ref, a_ref, ka_ref,
                     rh_ref, yl_ref, mt_ref, gt_ref):
    n = WKV_CHUNK
    hpg = LANES // n
    gw = hpg * RW_HEAD
    groups = D_MODEL // gw
    ri = lax.broadcasted_iota(jnp.int32, (n, n), 0)
    ci = lax.broadcasted_iota(jnp.int32, (n, n), 1)
    row = _row_iota((n, LANES))
    src = _lane_iota((n, LANES)) % n
    eye = (src == row).astype(F32)
    order = ((ci <= ri).astype(BF16), src < row, src <= row, n - 1), \
            ((ci >= ri).astype(BF16), src > row, src >= row, 0)
    lo_st = _lane_iota((RW_HEAD, LANES)) < RW_HEAD
    diag = (_lane_iota((RW_HEAD, LANES)) % RW_HEAD) == _row_iota((RW_HEAD, LANES))
    ka = ka_ref[...]
    bf = lambda z: z.astype(BF16)

    def body(it, carry):
        items = []
        for cc in range(PREP_CHUNKS):
            c = it * PREP_CHUNKS + cc
            rows = pl.ds(pl.multiple_of(c * n, n), n)
            kk = kk_ref[rows, :]
            k = k_ref[rows, :]
            r = r_ref[rows, :]
            v = bf(v_ref[rows, :])
            for d in range(2):
                tri, strict, incl, last = order[d]
                lw = lw_ref[d, rows, :]
                gate = a_ref[d, rows, :]
                kd = k * (1.0 + (gate - 1.0) * ka)
                b = kk * gate
                lw_hi = bf(lw)
                lw_lo = bf(lw - lw_hi.astype(F32))
                cum = (jnp.dot(tri, lw_hi, preferred_element_type=F32)
                       + jnp.dot(tri, lw_lo, preferred_element_type=F32))
                cum_end = cum[last:last + 1, :]
                e_neg = jnp.exp(-cum)
                e_rem = jnp.exp(cum_end - cum)
                rt = r * jnp.exp(cum)
                full = dict(c=c, d=d, rows=rows, rt=rt, w_end=jnp.exp(cum_end), v=v,
                            at=bf(-kk * jnp.exp(cum - lw)), rtb=bf(rt), kt=bf(kd * e_neg),
                            bt=bf(b * e_neg), kh=bf(kd * e_rem), bh=bf(b * e_rem),
                            strict=strict, incl=incl)
                for g in range(groups):
                    items.append(dict(full=full, sl=slice(g * gw, (g + 1) * gw)))

        for it_ in items:
            f, sl = it_['full'], it_['sl']
            lhs = jnp.concatenate([f['at'][:, sl], f['rtb'][:, sl]], axis=0)
            rhs = jnp.concatenate([_mask_stack(f['kt'][:, sl], hpg, RW_HEAD),
                                   _mask_stack(f['bt'][:, sl], hpg, RW_HEAD)], axis=0)
            a = lax.dot_general(lhs, rhs, (((1,), (1,)), ((), ())), preferred_element_type=F32)
            it_['aak'] = jnp.where(f['strict'], a[:n, :LANES], 0.0)
            it_['ark'] = jnp.where(f['incl'], a[n:, :LANES], 0.0)
            it_['arb'] = bf(jnp.where(f['incl'], a[n:, LANES:], 0.0))
            pw = jnp.where(f['strict'], a[:n, LANES:], 0.0)
            it_['inv'] = eye + pw
            it_['pw'] = bf(pw)

        for it_ in items:
            it_['pw'] = bf(jnp.dot(it_['pw'], _mask_stack(it_['pw'], hpg, n),
                                   preferred_element_type=F32))
        steps = 2
        while steps < n:
            final = 2 * steps >= n
            for it_ in items:
                rhs = _mask_stack(bf(it_['inv']), hpg, n)
                if not final:
                    rhs = jnp.concatenate([_mask_stack(it_['pw'], hpg, n), rhs], axis=1)
                x = jnp.dot(it_['pw'], rhs, preferred_element_type=F32)
                it_['inv'] = it_['inv'] + x[:, -LANES:]
                if not final:
                    it_['pw'] = bf(x[:, :LANES])
            steps *= 2

        for it_ in items:
            f, sl = it_['full'], it_['sl']
            vms = _mask_stack(f['v'][:, sl], hpg, RW_HEAD)
            x = jnp.dot(bf(jnp.concatenate([it_['aak'], it_['ark']], axis=0)), vms,
                        preferred_element_type=F32)
            it_['arkv'] = x[n:]
            it_['rhs'] = jnp.concatenate([_mask_stack(f['at'][:, sl], hpg, RW_HEAD),
                                          _mask_stack(bf(x[:n]), hpg, RW_HEAD)], axis=1)
        for it_ in items:
            au = bf(jnp.dot(bf(it_['inv']), it_['rhs'], preferred_element_type=F32))
            it_['ah'] = au[:, :gw]
            it_['ul'] = au[:, gw:]
            it_['rhs'] = jnp.concatenate([_mask_stack(it_['ah'], hpg, RW_HEAD),
                                          _mask_stack(it_['ul'], hpg, RW_HEAD)], axis=1)
        for it_ in items:
            f, sl = it_['full'], it_['sl']
            x = jnp.dot(it_['arb'], it_['rhs'], preferred_element_type=F32)
            rh_ref[f['d'], f['rows'], sl] = bf(f['rt'][:, sl] + x[:, :gw])
            yl_ref[f['d'], f['rows'], sl] = bf(it_['arkv'] + x[:, gw:])
        for it_ in items:
            f, sl = it_['full'], it_['sl']
            for j in range(0, gw, LANES):
                loc = slice(j, j + LANES)
                hs = slice(sl.start + j, sl.start + j + LANES)
                bh = f['bh'][:, hs]
                none = jnp.zeros((n, LANES), BF16)
                x = lax.dot_general(
                    jnp.concatenate([f['kh'][:, hs], bh], axis=0),
                    jnp.concatenate(
                        [jnp.concatenate([f['v'][:, hs], it_['ul'][:, loc]], axis=0),
                         jnp.concatenate([none, it_['ah'][:, loc]], axis=0)], axis=1),
                    (((0,), (0,)), ((), ())), preferred_element_type=F32)
                gf = x[:, :LANES]
                pm = x[:, LANES:]
                mt = jnp.where(lo_st, pm[:RW_HEAD], pm[RW_HEAD:])
                mt_ref[f['d'], f['c'], :, hs] = bf(mt + jnp.where(diag, f['w_end'][:, hs], 0.0))
                gt_ref[f['d'], f['c'], :, hs] = bf(jnp.where(lo_st, gf[:RW_HEAD], gf[RW_HEAD:]))
        return carry

    lax.fori_loop(0, ROW_TILE // (n * PREP_CHUNKS), body, 0)


def _wkv_prep(r, k, v, kk, lw, a, k_a):
    bsz, n_rows, _ = r.shape
    n_tiles = n_rows // ROW_TILE
    per = ROW_TILE // WKV_CHUNK
    dir_spec = pl.BlockSpec((2, None, ROW_TILE, D_MODEL), lambda b, t: (0, b, t, 0))
    st_spec = pl.BlockSpec((2, None, per, RW_HEAD, D_MODEL), lambda b, t: (0, b, t, 0, 0))
    rows = (2, bsz, n_rows, D_MODEL)
    st = (2, bsz, n_rows // WKV_CHUNK, RW_HEAD, D_MODEL)
    return pl.pallas_call(
        _wkv_prep_kernel,
        grid=(bsz, n_tiles),
        in_specs=[_tile_spec()] * 4 + [dir_spec, dir_spec, _const_spec((1, D_MODEL))],
        out_specs=[dir_spec, dir_spec, st_spec, st_spec],
        out_shape=[jax.ShapeDtypeStruct(rows, BF16), jax.ShapeDtypeStruct(rows, BF16),
                   jax.ShapeDtypeStruct(st, BF16), jax.ShapeDtypeStruct(st, BF16)],
        compiler_params=_params(),
        name="wkv_prep",
    )(r, k, v, kk, lw, a, k_a)


def _wkv_scan_kernel(rhf_ref, ylf_ref, mtf_ref, gtf_ref, rhb_ref, ylb_ref, mtb_ref, gtb_ref,
                     yf_ref, yb_ref, state_ref):
    n = WKV_CHUNK
    per = ROW_TILE // n

    @pl.when(pl.program_id(1) == 0)
    def _():
        state_ref[...] = jnp.zeros(state_ref.shape, F32)

    lo = _lane_iota((RW_HEAD, LANES)) < RW_HEAD
    dirs = ((rhf_ref, ylf_ref, mtf_ref, gtf_ref, yf_ref), (rhb_ref, ylb_ref, mtb_ref, gtb_ref, yb_ref))
    for i in range(per):
        for d, (rh_ref, yl_ref, mt_ref, gt_ref, y_ref) in enumerate(dirs):
            c = per - 1 - i if d == 1 else i
            for hp in range(D_MODEL // LANES):
                col = slice(hp * LANES, (hp + 1) * LANES)
                lhs = jnp.concatenate([rh_ref[c * n:(c + 1) * n, col], mt_ref[c, :, col]],
                                      axis=0)
                out = jnp.dot(lhs, state_ref[d, hp].astype(BF16), preferred_element_type=F32)
                y_ref[c * n:(c + 1) * n, col] = (out[:n] + yl_ref[c * n:(c + 1) * n, col]).astype(BF16)
                st = out[n:] + gt_ref[c, :, col]
                state_ref[d, hp] = jnp.concatenate(
                    [jnp.where(lo, st, 0.0), jnp.where(lo, 0.0, st)], axis=0)


def _wkv_scan(rh, yl, mt, gt, ctx_tiles):
    _, bsz, n_rows, _ = rh.shape
    n_tiles = n_rows // ROW_TILE
    per = ROW_TILE // WKV_CHUNK

    def back(t):
        return jnp.where(t < ctx_tiles, ctx_tiles - 1 - t, n_tiles - 1 - (t - ctx_tiles))

    def specs(d, tile):
        row = pl.BlockSpec((None, None, ROW_TILE, D_MODEL), lambda b, t: (d, b, tile(t), 0))
        st = pl.BlockSpec((None, None, per, RW_HEAD, D_MODEL),
                          lambda b, t: (d, b, tile(t), 0, 0))
        return [row, row, st, st]

    fwd = lambda t: t
    y_f = pl.BlockSpec((None, ROW_TILE, D_MODEL), lambda b, t: (b, t, 0))
    y_b = pl.BlockSpec((None, ROW_TILE, D_MODEL), lambda b, t: (b, back(t), 0))
    out = jax.ShapeDtypeStruct((bsz, n_rows, D_MODEL), BF16)
    return pl.pallas_call(
        _wkv_scan_kernel,
        grid=(bsz, n_tiles),
        in_specs=specs(0, fwd) + specs(1, back),
        out_specs=[y_f, y_b],
        out_shape=[out, out],
        scratch_shapes=[pltpu.VMEM((2, D_MODEL // LANES, LANES, LANES), F32)],
        compiler_params=pltpu.CompilerParams(dimension_semantics=("arbitrary", "arbitrary"),
                                             vmem_limit_bytes=VMEM_LIMIT),
        name="wkv_scan",
    )(rh, yl, mt, gt, rh, yl, mt, gt)


def _rope_tables(n_ctx, n_lat):
    t = jnp.arange(n_lat)
    row_pos = (t // GRID_W).astype(F32)
    col_pos = (t % GRID_W).astype(F32)
    n_freq = DA_HEAD_DIM // 4
    inv_freq = ROPE_THETA ** (-jnp.arange(n_freq, dtype=F32) / n_freq)
    ang = jnp.concatenate([row_pos[:, None] * inv_freq, col_pos[:, None] * inv_freq], axis=-1)
    cos, sin = jnp.cos(ang), jnp.sin(ang)
    reps = LANES // DA_HEAD_DIM
    cos = jnp.tile(jnp.concatenate([cos, cos], axis=-1), (1, reps))
    sin = jnp.tile(jnp.concatenate([-sin, sin], axis=-1), (1, reps))
    cos = jnp.concatenate([jnp.ones((n_ctx, LANES), F32), cos], axis=0)
    sin = jnp.concatenate([jnp.zeros((n_ctx, LANES), F32), sin], axis=0)
    return cos, sin


def kernel(x, c, ctx, c_ctx, ada_w, ada_b, norm_mix_g, norm_ffn_g, ffn_w_up, ffn_conv_w, ffn_conv_b, ffn_w_down, da_w_qkv, da_lambda, da_subln_g, da_w_o, rw_mix_prev, rw_mix_next, rw_w_r, rw_w_k, rw_w_v, rw_w0, rw_w1, rw_w2, rw_a0, rw_a1, rw_a2, rw_v0, rw_v1, rw_v2, rw_k_k, rw_k_a, rw_r_k, rw_g1, rw_g2, rw_lnx_g, rw_lnx_b, rw_w_o, final_norm_g):
    bsz, n_lat, d = x.shape
    n_ctx = ctx.shape[1]
    depth = ada_w.shape[0]
    assert d == D_MODEL and n_lat % ROW_TILE == 0 and n_ctx % ROW_TILE == 0
    assert n_lat % GRID_W == 0
    ctx_tiles = n_ctx // ROW_TILE
    row = lambda a: a.reshape(1, -1)
    cat = lambda a: jnp.concatenate([a[0], a[1]], axis=0 if a.shape[1] == RW_HEAD else 1)

    pad = (-(bsz + 1)) % SUBLANES
    cc = jnp.concatenate([c, c_ctx[None, :], jnp.zeros((pad, d), F32)], axis=0)
    table = _ada_table(cc, ada_w, ada_b)
    mod_l = table[:, :bsz].reshape(depth, bsz, 6, d)
    mod_c = jnp.broadcast_to(table[:, bsz].reshape(depth, 1, 6, d), (depth, bsz, 6, d))
    mods = jnp.stack([mod_c, mod_l], axis=2)
    mods = jnp.pad(mods, ((0, 0), (0, 0), (0, 0), (0, SUBLANES - 6), (0, 0)))

    cos, sin = _rope_tables(n_ctx, n_lat)
    stream = jnp.concatenate([ctx, x], axis=1)
    v_first = None
    no_final = jnp.ones((1, d), F32)

    for i in range(depth):
        last = i == depth - 1
        j = i // 2
        t0 = ctx_tiles if last else 0
        mod = mods[i]
        g_mix = row(norm_mix_g[i])
        mixer = None
        if i % 2 == 0:
            lambda_init = 0.8 - 0.6 * math.exp(-0.3 * i)
            q, k, v = _qkv(stream, mod, g_mix, da_w_qkv[j].astype(BF16), cos, sin, ctx_tiles)
            o = _flash(q, k, v, da_lambda[j], da_subln_g[j], ctx_tiles, lambda_init)
            if last:
                stream = _proj_residual(stream, o, mod, da_w_o[j].astype(BF16), ctx_tiles, t0)
            else:
                mixer = ('proj', o, da_w_o[j].astype(BF16))
        else:
            p = dict(mix_prev=rw_mix_prev[j], mix_next=rw_mix_next[j],
                     w_r=rw_w_r[j].astype(BF16), w_k=rw_w_k[j].astype(BF16),
                     w_v=rw_w_v[j].astype(BF16), w0=rw_w0[j], w1=cat(rw_w1[j]).astype(BF16),
                     w2=cat(rw_w2[j]).astype(BF16), a0=rw_a0[j], a1=cat(rw_a1[j]).astype(BF16),
                     a2=cat(rw_a2[j]).astype(BF16), k_k=row(rw_k_k[j]), k_a=row(rw_k_a[j]),
                     r_k=row(rw_r_k[j]), g1=rw_g1[j].astype(BF16), g2=rw_g2[j].astype(BF16))
            if j > 0:
                n_v = rw_v1.shape[-1]
                p.update(v0=row(rw_v0[j - 1]),
                         v1=jnp.pad(rw_v1[j - 1], ((0, 0), (0, LANES - n_v))).astype(BF16),
                         v2=jnp.pad(rw_v2[j - 1], ((0, LANES - n_v), (0, 0))).astype(BF16))
            r, k, v, kk, bonus, gate, lw, a = _rwkv_proj(stream, mod, g_mix, p, v_first,
                                                         ctx_tiles)
            if v_first is None:
                v_first = v
            rh, yl, mt, gt = _wkv_prep(r, k, v, kk, lw, a, p['k_a'])
            ys = _wkv_scan(rh, yl, mt, gt, ctx_tiles)
            mixer = ('rwkv', ys[0], ys[1], bonus, gate, row(rw_lnx_g[j]), row(rw_lnx_b[j]),
                     rw_w_o[j].astype(BF16))
        fused = mixer is not None
        stream = _ffn(stream, mod, row(norm_ffn_g[i]), ffn_w_up[i].astype(BF16),
                      ffn_conv_w[i], row(ffn_conv_b[i]), ffn_w_down[i].astype(BF16),
                      row(final_norm_g) if last else no_final,
                      ctx_tiles if fused or not last else 0, last, mixer, t0 if fused else 0)
    return stream
```

```python
import functools
import math

import jax
import jax.numpy as jnp
from jax import lax
from jax.experimental import pallas as pl
from jax.experimental.pallas import tpu as pltpu

D_MODEL = 1024
GRID_W = 64
DA_HEAD_DIM = 64
DA_HEADS = D_MODEL // (2 * DA_HEAD_DIM)
ROPE_THETA = 10000.0
RW_HEAD = 64
LNX_EPS = 64e-5
EPS = 1e-6

LANES = 128
SUBLANES = 8
ROW_TILE = 256
WKV_CHUNK = 32
FF_TILE = 256
PREP_CHUNKS = 4
VMEM_LIMIT = 56 * 1024 * 1024

F32 = jnp.float32
BF16 = jnp.bfloat16

Q_SCALE = DA_HEAD_DIM ** -0.5 * math.log2(math.e)


def _dot(a, b):
    return jnp.dot(a.astype(BF16), b.astype(BF16), preferred_element_type=F32)


def _rms(x):
    return x * lax.rsqrt(jnp.mean(x * x, axis=-1, keepdims=True) + EPS)


def _norm_mod(x, g, shift, scale):
    return _rms(x) * g * (1.0 + scale) + shift


def _sigmoid(x):
    return 0.5 * jnp.tanh(0.5 * x) + 0.5


def _lane_iota(shape):
    return lax.broadcasted_iota(jnp.int32, shape, len(shape) - 1)


def _row_iota(shape):
    return lax.broadcasted_iota(jnp.int32, shape, len(shape) - 2)


def _group_ones():
    r = lax.broadcasted_iota(jnp.int32, (LANES, LANES), 0) // RW_HEAD
    c = lax.broadcasted_iota(jnp.int32, (LANES, LANES), 1) // RW_HEAD
    return (r == c).astype(BF16)


def _group_sum(z):
    ones = _group_ones()
    parts = [jnp.dot(z[:, j:j + LANES].astype(BF16), ones, preferred_element_type=F32)
             for j in range(0, z.shape[1], LANES)]
    return jnp.concatenate(parts, axis=1)


def _shifted(h, halo_prev, halo_next, prev_ok, next_ok):
    rows = h.shape[0]
    ri = _row_iota(h.shape)
    first = jnp.where(prev_ok, halo_prev[-1:, :], 0.0)
    last = jnp.where(next_ok, halo_next[0:1, :], 0.0)
    h_prev = jnp.where(ri == 0, first, pltpu.roll(h, 1, 0))
    h_next = jnp.where(ri == rows - 1, last, pltpu.roll(h, rows - 1, 0))
    return h_prev, h_next


def _edge_flags(t, n_tiles, ctx_tiles):
    prev_ok = jnp.logical_and(t != 0, t != ctx_tiles)
    next_ok = jnp.logical_and(t != n_tiles - 1, t != ctx_tiles - 1)
    return prev_ok, next_ok


def _const_spec(shape):
    zeros = (0,) * len(shape)
    return pl.BlockSpec(shape, lambda *_: zeros, pipeline_mode=pl.Buffered(1))


def _tile_spec(t0=0):
    return pl.BlockSpec((None, ROW_TILE, D_MODEL), lambda b, t: (b, t + t0, 0))


def _halo_specs(n_rows, t0=0, rows=SUBLANES):
    per = ROW_TILE // rows
    last = n_rows // rows - 1
    prev = pl.BlockSpec((None, rows, D_MODEL),
                        lambda b, t: (b, jnp.maximum((t + t0) * per - 1, 0), 0))
    nxt = pl.BlockSpec((None, rows, D_MODEL),
                       lambda b, t: (b, jnp.minimum((t + t0 + 1) * per, last), 0))
    return prev, nxt


def _mod_spec(ctx_tiles, t0=0):
    return pl.BlockSpec((None, None, SUBLANES, D_MODEL),
                        lambda b, t: (b, ((t + t0) >= ctx_tiles).astype(jnp.int32), 0, 0))


LIGHT_VMEM_LIMIT = 32 * 1024 * 1024


def _params(limit=VMEM_LIMIT):
    return pltpu.CompilerParams(vmem_limit_bytes=limit)


def _ada_kernel(c_ref, w_ref, b_ref, o_ref):
    c = c_ref[...]
    s = c * _sigmoid(c)
    o_ref[...] = jnp.dot(s, w_ref[...], precision=lax.Precision.HIGHEST,
                         preferred_element_type=F32) + b_ref[...]


def _ada_table(cc, ada_w, ada_b):
    n_layers, _, six_d = ada_w.shape
    rows = cc.shape[0]
    nb = 1536
    return pl.pallas_call(
        _ada_kernel,
        grid=(n_layers, six_d // nb),
        in_specs=[pl.BlockSpec((rows, D_MODEL), lambda l, n: (0, 0)),
                  pl.BlockSpec((None, D_MODEL, nb), lambda l, n: (l, 0, n)),
                  pl.BlockSpec((None, 1, nb), lambda l, n: (l, 0, n))],
        out_specs=pl.BlockSpec((None, rows, nb), lambda l, n: (l, 0, n)),
        out_shape=jax.ShapeDtypeStruct((n_layers, rows, six_d), F32),
        compiler_params=_params(LIGHT_VMEM_LIMIT),
        name="ada_table",
    )(cc, ada_w, ada_b.reshape(n_layers, 1, six_d))


def _rope(slab, cos, sin_signed):
    lane = _lane_iota(slab.shape)
    first = (lane % DA_HEAD_DIM) < (DA_HEAD_DIM // 2)
    partner = jnp.where(first, pltpu.roll(slab, LANES - DA_HEAD_DIM // 2, 1),
                        pltpu.roll(slab, DA_HEAD_DIM // 2, 1))
    return slab * cos + partner * sin_signed


def _qkv_kernel(x_ref, mod_ref, g_ref, w_ref, cos_ref, sin_ref, q_ref, k_ref, v_ref):
    mod = mod_ref[...]
    h = _norm_mod(x_ref[...], g_ref[...], mod[0:1], mod[1:2]).astype(BF16)
    cos = cos_ref[...]
    sin = sin_ref[...]
    wide = 2 * LANES
    for j in range(0, D_MODEL, wide):
        qa = jnp.dot(h, w_ref[:, j:j + wide], preferred_element_type=F32)
        ka = jnp.dot(h, w_ref[:, D_MODEL + j:D_MODEL + j + wide], preferred_element_type=F32)
        for i in range(0, wide, LANES):
            q_ref[:, j + i:j + i + LANES] = (
                _rope(qa[:, i:i + LANES], cos, sin) * Q_SCALE).astype(BF16)
            k_ref[:, j + i:j + i + LANES] = _rope(ka[:, i:i + LANES], cos, sin).astype(BF16)
        v_ref[:, j:j + wide] = jnp.dot(
            h, w_ref[:, 2 * D_MODEL + j:2 * D_MODEL + j + wide],
            preferred_element_type=F32).astype(BF16)


def _qkv(x, mod, g, w_qkv, cos, sin, ctx_tiles):
    bsz, n_rows, _ = x.shape
    out = jax.ShapeDtypeStruct((bsz, n_rows, D_MODEL), BF16)
    tab = pl.BlockSpec((ROW_TILE, LANES), lambda b, t: (t, 0))
    return pl.pallas_call(
        _qkv_kernel,
        grid=(bsz, n_rows // ROW_TILE),
        in_specs=[_tile_spec(), _mod_spec(ctx_tiles), _const_spec((1, D_MODEL)),
                  _const_spec((D_MODEL, 3 * D_MODEL)), tab, tab],
        out_specs=[_tile_spec(), _tile_spec(), _tile_spec()],
        out_shape=[out, out, out],
        compiler_params=_params(LIGHT_VMEM_LIMIT),
        name="attn_qkv",
    )(x, mod, g, w_qkv, cos, sin)


def _flash_kernel(lam_ref, g_ref, q_ref, qn_ref, k_ref, v_ref, o_ref,
                  sa_ref, sb_ref, ma_ref, mb_ref, vx_ref, *, ctx_tiles, lambda_init):
    t = pl.program_id(2)
    n_keys = k_ref.shape[0]
    ctx_keys = ctx_tiles * ROW_TILE
    lv = lam_ref[...]
    lam = (jnp.exp(jnp.sum(lv[0:1] * lv[1:2], axis=-1, keepdims=True))
           - jnp.exp(jnp.sum(lv[2:3] * lv[3:4], axis=-1, keepdims=True)) + lambda_init)

    @pl.when(t == 0)
    def _():
        vx_ref[:, :LANES] = v_ref[...]
        vx_ref[:, LANES:] = (_lane_iota((n_keys, LANES)) == 0).astype(BF16)

    def scores(q, n):
        lo = _lane_iota(q.shape) < DA_HEAD_DIM
        zero = jnp.zeros_like(q)
        k = k_ref[0:n, :]
        return [lax.dot_general(qc, k, (((1,), (1,)), ((), ())), preferred_element_type=F32)
                for qc in (jnp.where(lo, q, zero), jnp.where(lo, zero, q))]

    def stash(q, s_ref, m_ref, comps=(0, 1)):
        lo = _lane_iota(q.shape) < DA_HEAD_DIM
        zero = jnp.zeros_like(q)
        for c in comps:
            qc = jnp.where(lo, q, zero) if c == 0 else jnp.where(lo, zero, q)
            s = lax.dot_general(qc, k_ref[...], (((1,), (1,)), ((), ())),
                                preferred_element_type=F32)
            s_ref[c] = s
            m_ref[c] = jnp.max(s, axis=-1, keepdims=True)

    def attend(shifted, n):
        outs = []
        for x in shifted:
            p = jnp.exp2(x.astype(BF16))
            acc = jnp.dot(p, vx_ref[0:n, :], preferred_element_type=F32)
            outs.append(acc[:, :LANES] / acc[:, LANES:LANES + 1])
        o = outs[0] - lam * outs[1]
        o_ref[...] = (_rms(o) * g_ref[...] * (1.0 - lambda_init)).astype(BF16)

    @pl.when(t < ctx_tiles)
    def _():
        attend([s - jnp.max(s, axis=-1, keepdims=True) for s in scores(q_ref[...], ctx_keys)],
               ctx_keys)

    slots = ((sa_ref, ma_ref), (sb_ref, mb_ref))

    @pl.when(t == ctx_tiles)
    def _():
        stash(q_ref[...], *slots[ctx_tiles % 2])

    def shifted(slot):
        s_ref, m_ref = slots[slot]
        return [s_ref[c] - m_ref[c] for c in range(2)]

    last = pl.num_programs(2) - 1
    for parity in range(2):
        @pl.when(jnp.logical_and(jnp.logical_and(t >= ctx_tiles, t < last), t % 2 == parity))
        def _():
            stash(qn_ref[...], *slots[1 - parity], comps=(0,))
            attend(shifted(parity), n_keys)
            stash(qn_ref[...], *slots[1 - parity], comps=(1,))

        @pl.when(jnp.logical_and(jnp.logical_and(t >= ctx_tiles, t == last), t % 2 == parity))
        def _():
            attend(shifted(parity), n_keys)


def _flash(q, k, v, lam_vecs, subln_g, ctx_tiles, lambda_init):
    bsz, n_rows, _ = q.shape
    n_tiles = n_rows // ROW_TILE
    kv_spec = pl.BlockSpec((None, n_rows, LANES), lambda b, h, t: (b, 0, h))
    q_spec = pl.BlockSpec((None, ROW_TILE, LANES), lambda b, h, t: (b, t, h))
    q_next = pl.BlockSpec((None, ROW_TILE, LANES),
                          lambda b, h, t: (b, jnp.minimum(t + 1, n_tiles - 1), h))
    kern = functools.partial(_flash_kernel, ctx_tiles=ctx_tiles, lambda_init=lambda_init)
    score = pltpu.VMEM((2, ROW_TILE, n_rows), F32)
    rowmax = pltpu.VMEM((2, ROW_TILE, 1), F32)
    return pl.pallas_call(
        kern,
        grid=(bsz, DA_HEADS, n_tiles),
        in_specs=[pl.BlockSpec(lam_vecs.shape, lambda b, h, t: (0, 0)),
                  pl.BlockSpec((1, LANES), lambda b, h, t: (0, 0)),
                  q_spec, q_next, kv_spec, kv_spec],
        out_specs=q_spec,
        out_shape=jax.ShapeDtypeStruct((bsz, n_rows, D_MODEL), BF16),
        scratch_shapes=[score, score, rowmax, rowmax,
                        pltpu.VMEM((n_rows, 2 * LANES), BF16)],
        compiler_params=pltpu.CompilerParams(
            dimension_semantics=("arbitrary", "arbitrary", "arbitrary"),
            vmem_limit_bytes=VMEM_LIMIT),
        name="diff_flash",
    )(lam_vecs, subln_g.reshape(1, LANES), q, q, k, v)


def _proj_residual_kernel(x_ref, z_ref, mod_ref, w_ref, o_ref):
    out = jnp.dot(z_ref[...], w_ref[...], preferred_element_type=F32)
    o_ref[...] = x_ref[...] + mod_ref[2:3, :] * out


def _proj_residual(x, z, mod, w, ctx_tiles, t0):
    bsz, n_rows, _ = x.shape
    n_tiles = n_rows // ROW_TILE - t0
    return pl.pallas_call(
        _proj_residual_kernel,
        grid=(bsz, n_tiles),
        in_specs=[_tile_spec(t0), _tile_spec(t0), _mod_spec(ctx_tiles, t0),
                  _const_spec((D_MODEL, D_MODEL))],
        out_specs=_tile_spec(),
        out_shape=jax.ShapeDtypeStruct((bsz, n_tiles * ROW_TILE, D_MODEL), F32),
        compiler_params=_params(),
        name="proj_residual",
    )(x, z, mod, w)


def _rwkv_mix(y, bonus, gate, lnx_g, lnx_b):
    mu = _group_sum(y) * (1.0 / RW_HEAD)
    yc = y - mu
    var = _group_sum(yc * yc) * (1.0 / RW_HEAD)
    yn = yc * lax.rsqrt(var + LNX_EPS) * lnx_g + lnx_b
    return ((yn + bonus) * gate).astype(BF16)


def _ffn_kernel(*refs, n_tiles, ctx_tiles, t0, d_ff, final, mixer):
    cat = lambda trio: jnp.concatenate([r[...] for r in trio], axis=0)
    xcat = cat(refs[:3])
    refs = refs[3:]
    zcat = None
    if mixer == 'proj':
        zcat, wz_ref = cat(refs[:3]), refs[3]
        refs = refs[4:]
    elif mixer == 'rwkv':
        yf, yb, bonus, gate = (cat(refs[3 * i:3 * i + 3]) for i in range(4))
        lg_ref, lb_ref, wz_ref = refs[12:15]
        refs = refs[15:]
        zcat = _rwkv_mix(yf.astype(F32) + yb.astype(F32), bonus, gate, lg_ref[...], lb_ref[...])
    mod_ref, g_ref, wup_ref, cw_ref, cb_ref, wdn_ref, fin_ref, o_ref = refs
    t = pl.program_id(1) + t0
    prev_ok, next_ok = _edge_flags(t, n_tiles, ctx_tiles)
    mod = mod_ref[...]
    g = g_ref[...]
    if zcat is not None:
        xcat = xcat + mod[2:3] * jnp.dot(zcat, wz_ref[...], preferred_element_type=F32)
    x = xcat[:ROW_TILE]
    hcat = _norm_mod(xcat, g, mod[3:4], mod[4:5]).astype(BF16)
    h = hcat[:ROW_TILE]
    halo = (xcat.shape[0] - ROW_TILE) // 2
    tiles = list(range(0, d_ff, FF_TILE))

    def up(f):
        gate = jnp.dot(hcat, wup_ref[:, f:f + FF_TILE], preferred_element_type=F32)
        val = jnp.dot(h, wup_ref[:, d_ff + f:d_ff + f + FF_TILE], preferred_element_type=F32)
        return gate, val

    acc = jnp.zeros((ROW_TILE, D_MODEL), F32)
    nxt = up(tiles[0])
    for i, f in enumerate(tiles):
        gate_all, val = nxt
        if i + 1 < len(tiles):
            nxt = up(tiles[i + 1])
        gate = gate_all[:ROW_TILE]
        g_prev, g_next = _shifted(gate, gate_all[ROW_TILE:ROW_TILE + halo],
                                  gate_all[ROW_TILE + halo:], prev_ok, next_ok)
        cw = cw_ref[:, f:f + FF_TILE]
        conv = g_prev * cw[0:1] + gate * cw[1:2] + g_next * cw[2:3] + cb_ref[:, f:f + FF_TILE]
        act = conv * _sigmoid(conv) * val
        acc = acc + jnp.dot(act.astype(BF16), wdn_ref[f:f + FF_TILE, :],
                            preferred_element_type=F32)
    y = x + mod[5:6] * acc
    if final:
        y = _rms(y) * fin_ref[...]
    o_ref[...] = y


def _ffn(x, mod, g, w_up, conv_w, conv_b, w_down, final_g, ctx_tiles, final, mixer=None, t0=0):
    bsz, n_rows, _ = x.shape
    n_tiles = n_rows // ROW_TILE
    d_ff = w_down.shape[0]
    halo = SUBLANES if mixer is None else 2 * SUBLANES
    prev, nxt = _halo_specs(n_rows, t0, rows=halo)
    trio = [_tile_spec(t0), prev, nxt]
    args, specs = [x, x, x], list(trio)
    if mixer is not None:
        streams = mixer[1:2] if mixer[0] == 'proj' else mixer[1:5]
        for arr in streams:
            args += [arr] * 3
            specs += trio
        for arr in mixer[1 + len(streams):]:
            args.append(arr)
            specs.append(_const_spec(arr.shape))
    args += [mod, g, w_up, conv_w, conv_b, w_down, final_g]
    specs += [_mod_spec(ctx_tiles, t0), _const_spec((1, D_MODEL)),
              _const_spec((D_MODEL, 2 * d_ff)), _const_spec((3, d_ff)),
              _const_spec((1, d_ff)), _const_spec((d_ff, D_MODEL)), _const_spec((1, D_MODEL))]
    kern = functools.partial(_ffn_kernel, n_tiles=n_tiles, ctx_tiles=ctx_tiles, t0=t0, d_ff=d_ff,
                             final=final, mixer=None if mixer is None else mixer[0])
    return pl.pallas_call(
        kern,
        grid=(bsz, n_tiles - t0),
        in_specs=specs,
        out_specs=_tile_spec(),
        out_shape=jax.ShapeDtypeStruct((bsz, (n_tiles - t0) * ROW_TILE, D_MODEL), F32),
        compiler_params=_params(),
        name="conv_glu",
    )(*args)


def _rwkv_proj_kernel(*refs, n_tiles, ctx_tiles, has_vfirst):
    (x_ref, xp_ref, xn_ref, mod_ref, g_ref, mp_ref, mn_ref, wr_ref, wk_ref, wv_ref,
     w0_ref, w1_ref, w2_ref, a0_ref, a1_ref, a2_ref, kk_ref, ka_ref, rk_ref,
     g1_ref, g2_ref) = refs[:21]
    refs = refs[21:]
    if has_vfirst:
        vf_ref, v0_ref, v1_ref, v2_ref = refs[:4]
        refs = refs[4:]
    r_ref, k_ref, v_ref, kkn_ref, bonus_ref, gate_ref, lw_ref, a_ref = refs

    t = pl.program_id(1)
    prev_ok, next_ok = _edge_flags(t, n_tiles, ctx_tiles)
    mod = mod_ref[...]
    g = g_ref[...]
    h = _norm_mod(x_ref[...], g, mod[0:1], mod[1:2])
    hp = _norm_mod(xp_ref[...], g, mod[0:1], mod[1:2])
    hn = _norm_mod(xn_ref[...], g, mod[0:1], mod[1:2])
    h_prev, h_next = _shifted(h, hp, hn, prev_ok, next_ok)
    hb = h.astype(BF16)
    xx_p = (h_prev - h).astype(BF16)
    xx_n = (h_next - h).astype(BF16)
    mp = mp_ref[...].astype(BF16)
    mn = mn_ref[...].astype(BF16)

    def mix(m):
        return hb + xx_p * mp[m:m + 1] + xx_n * mn[m:m + 1]

    lo = _lane_iota((ROW_TILE, LANES)) < RW_HEAD

    def halves(z):
        return jnp.where(lo, z, 0.0), jnp.where(lo, 0.0, z)

    r = jnp.dot(mix(0), wr_ref[...], preferred_element_type=F32)
    r_ref[...] = r

    lw = halves(jnp.tanh(jnp.dot(mix(1), w1_ref[...], preferred_element_type=F32)))
    la = halves(jnp.dot(mix(4), a1_ref[...], preferred_element_type=F32))
    a_gate = []
    for d in range(2):
        wl = w0_ref[d:d + 1, :] + _dot(lw[d], w2_ref[...])
        lw_ref[d] = _sigmoid(wl) * (-math.exp(-0.5))
        a_d = _sigmoid(a0_ref[d:d + 1, :] + _dot(la[d], a2_ref[...]))
        a_ref[d] = a_d
        a_gate.append(a_d)

    k = jnp.dot(mix(2), wk_ref[...], preferred_element_type=F32)
    k_ref[...] = k
    kk = k * kk_ref[...]
    kkn_ref[...] = kk * lax.rsqrt(jnp.maximum(_group_sum(kk * kk), 1e-24))

    xv = mix(3)
    v = jnp.dot(xv, wv_ref[...], preferred_element_type=F32)
    if has_vfirst:
        lv = jnp.dot(xv, v1_ref[...], preferred_element_type=F32)
        v = v + (vf_ref[...] - v) * _sigmoid(v0_ref[...] + _dot(lv, v2_ref[...]))
    v_ref[...] = v

    ka = ka_ref[...]
    k_sum = k * (2.0 + (a_gate[0] + a_gate[1] - 2.0) * ka)
    bonus_ref[...] = (_group_sum(r * k_sum * rk_ref[...]) * v).astype(BF16)

    gl = _sigmoid(jnp.dot(mix(5), g1_ref[...], preferred_element_type=F32))
    gate_ref[...] = _dot(gl, g2_ref[...]).astype(BF16)


def _rwkv_proj(x, mod, g, p, v_first, ctx_tiles):
    bsz, n_rows, _ = x.shape
    n_tiles = n_rows // ROW_TILE
    prev, nxt = _halo_specs(n_rows)
    has_vfirst = v_first is not None
    args = [x, x, x, mod, g, p['mix_prev'], p['mix_next'], p['w_r'], p['w_k'], p['w_v'],
            p['w0'], p['w1'], p['w2'], p['a0'], p['a1'], p['a2'], p['k_k'], p['k_a'], p['r_k'],
            p['g1'], p['g2']]
    specs = [_tile_spec(), prev, nxt, _mod_spec(ctx_tiles)] + [
        _const_spec(a.shape) for a in args[4:]]
    if has_vfirst:
        args += [v_first, p['v0'], p['v1'], p['v2']]
        specs += [_tile_spec()] + [_const_spec(a.shape) for a in args[-3:]]
    one = jax.ShapeDtypeStruct((bsz, n_rows, D_MODEL), F32)
    half = jax.ShapeDtypeStruct((bsz, n_rows, D_MODEL), BF16)
    two = jax.ShapeDtypeStruct((2, bsz, n_rows, D_MODEL), F32)
    two_spec = pl.BlockSpec((2, None, ROW_TILE, D_MODEL), lambda b, t: (0, b, t, 0))
    kern = functools.partial(_rwkv_proj_kernel, n_tiles=n_tiles, ctx_tiles=ctx_tiles,
                             has_vfirst=has_vfirst)
    return pl.pallas_call(
        kern,
        grid=(bsz, n_tiles),
        in_specs=specs,
        out_specs=[_tile_spec()] * 6 + [two_spec, two_spec],
        out_shape=[one] * 4 + [half, half, two, two],
        compiler_params=_params(),
        name="rwkv_proj",
    )(*args)


def _mask_stack(x, n_blocks, width):
    blk = _lane_iota(x.shape) // width
    zero = jnp.zeros_like(x)
    return jnp.concatenate([jnp.where(blk == i, x, zero) for i in range(n_blocks)], axis=0)


def _wkv_prep_kernel(r_ref, k_ref, v_ref, kk_ref, lw_ref, a_ref, ka_ref,
                     rh_ref, yl_ref, mt_ref, gt_ref):
    n = WKV_CHUNK
    hpg = LANES // n
    gw = hpg * RW_HEAD
    groups = D_MODEL // gw
    ri = lax.broadcasted_iota(jnp.int32, (n, n), 0)
    ci = lax.broadcasted_iota(jnp.int32, (n, n), 1)
    row = _row_iota((n, LANES))
    src = _lane_iota((n, LANES)) % n
    eye = (src == row).astype(F32)
    order = ((ci <= ri).astype(BF16), src < row, src <= row, n - 1), \
            ((ci >= ri).astype(BF16), src > row, src >= row, 0)
    lo_st = _lane_iota((RW_HEAD, LANES)) < RW_HEAD
    diag = (_lane_iota((RW_HEAD, LANES)) % RW_HEAD) == _row_iota((RW_HEAD, LANES))
    ka = ka_ref[...]
    bf = lambda z: z.astype(BF16)

    def body(it, carry):
        items = []
        for cc in range(PREP_CHUNKS):
            c = it * PREP_CHUNKS + cc
            rows = pl.ds(pl.multiple_of(c * n, n), n)
            kk = kk_ref[rows, :]
            k = k_ref[rows, :]
            r = r_ref[rows, :]
            v = bf(v_ref[rows, :])
            for d in range(2):
                tri, strict, incl, last = order[d]
                lw = lw_ref[d, rows, :]
                gate = a_ref[d, rows, :]
                kd = k * (1.0 + (gate - 1.0) * ka)
                b = kk * gate
                lw_hi = bf(lw)
                lw_lo = bf(lw - lw_hi.astype(F32))
                cum = (jnp.dot(tri, lw_hi, preferred_element_type=F32)
                       + jnp.dot(tri, lw_lo, preferred_element_type=F32))
                cum_end = cum[last:last + 1, :]
                e_neg = jnp.exp(-cum)
                e_rem = jnp.exp(cum_end - cum)
                rt = r * jnp.exp(cum)
                full = dict(c=c, d=d, rows=rows, rt=rt, w_end=jnp.exp(cum_end), v=v,
                            at=bf(-kk * jnp.exp(cum - lw)), rtb=bf(rt), kt=bf(kd * e_neg),
                            bt=bf(b * e_neg), kh=bf(kd * e_rem), bh=bf(b * e_rem),
                            strict=strict, incl=incl)
                for g in range(groups):
                    items.append(dict(full=full, sl=slice(g * gw, (g + 1) * gw)))

        for it_ in items:
            f, sl = it_['full'], it_['sl']
            lhs = jnp.concatenate([f['at'][:, sl], f['rtb'][:, sl]], axis=0)
            rhs = jnp.concatenate([_mask_stack(f['kt'][:, sl], hpg, RW_HEAD),
                                   _mask_stack(f['bt'][:, sl], hpg, RW_HEAD)], axis=0)
            a = lax.dot_general(lhs, rhs, (((1,), (1,)), ((), ())), preferred_element_type=F32)
            it_['aak'] = jnp.where(f['strict'], a[:n, :LANES], 0.0)
            it_['ark'] = jnp.where(f['incl'], a[n:, :LANES], 0.0)
            it_['arb'] = bf(jnp.where(f['incl'], a[n:, LANES:], 0.0))
            pw = jnp.where(f['strict'], a[:n, LANES:], 0.0)
            it_['inv'] = eye + pw
            it_['pw'] = bf(pw)

        for it_ in items:
            it_['pw'] = bf(jnp.dot(it_['pw'], _mask_stack(it_['pw'], hpg, n),
                                   preferred_element_type=F32))
        steps = 2
        while steps < n:
            final = 2 * steps >= n
            for it_ in items:
                rhs = _mask_stack(bf(it_['inv']), hpg, n)
                if not final:
                    rhs = jnp.concatenate([_mask_stack(it_['pw'], hpg, n), rhs], axis=1)
                x = jnp.dot(it_['pw'], rhs, preferred_element_type=F32)
                it_['inv'] = it_['inv'] + x[:, -LANES:]
                if not final:
                    it_['pw'] = bf(x[:, :LANES])
            steps *= 2

        for it_ in items:
            f, sl = it_['full'], it_['sl']
            vms = _mask_stack(f['v'][:, sl], hpg, RW_HEAD)
            x = jnp.dot(bf(jnp.concatenate([it_['aak'], it_['ark']], axis=0)), vms,
                        preferred_element_type=F32)
            it_['arkv'] = x[n:]
            it_['rhs'] = jnp.concatenate([_mask_stack(f['at'][:, sl], hpg, RW_HEAD),
                                          _mask_stack(bf(x[:n]), hpg, RW_HEAD)], axis=1)
        for it_ in items:
            au = bf(jnp.dot(bf(it_['inv']), it_['rhs'], preferred_element_type=F32))
            it_['ah'] = au[:, :gw]
            it_['ul'] = au[:, gw:]
            it_['rhs'] = jnp.concatenate([_mask_stack(it_['ah'], hpg, RW_HEAD),
                                          _mask_stack(it_['ul'], hpg, RW_HEAD)], axis=1)
        for it_ in items:
            f, sl = it_['full'], it_['sl']
            x = jnp.dot(it_['arb'], it_['rhs'], preferred_element_type=F32)
            rh_ref[f['d'], f['rows'], sl] = bf(f['rt'][:, sl] + x[:, :gw])
            yl_ref[f['d'], f['rows'], sl] = bf(it_['arkv'] + x[:, gw:])
        for it_ in items:
            f, sl = it_['full'], it_['sl']
            for j in range(0, gw, LANES):
                loc = slice(j, j + LANES)
                hs = slice(sl.start + j, sl.start + j + LANES)
                bh = f['bh'][:, hs]
                none = jnp.zeros((n, LANES), BF16)
                x = lax.dot_general(
                    jnp.concatenate([f['kh'][:, hs], bh], axis=0),
                    jnp.concatenate(
                        [jnp.concatenate([f['v'][:, hs], it_['ul'][:, loc]], axis=0),
                         jnp.concatenate([none, it_['ah'][:, loc]], axis=0)], axis=1),
                    (((0,), (0,)), ((), ())), preferred_element_type=F32)
                gf = x[:, :LANES]
                pm = x[:, LANES:]
                mt = jnp.where(lo_st, pm[:RW_HEAD], pm[RW_HEAD:])
                mt_ref[f['d'], f['c'], :, hs] = bf(mt + jnp.where(diag, f['w_end'][:, hs], 0.0))
                gt_ref[f['d'], f['c'], :, hs] = bf(jnp.where(lo_st, gf[:RW_HEAD], gf[RW_HEAD:]))
        return carry

    lax.fori_loop(0, ROW_TILE // (n * PREP_CHUNKS), body, 0)


def _wkv_prep(r, k, v, kk, lw, a, k_a):
    bsz, n_rows, _ = r.shape
    n_tiles = n_rows // ROW_TILE
    per = ROW_TILE // WKV_CHUNK
    dir_spec = pl.BlockSpec((2, None, ROW_TILE, D_MODEL), lambda b, t: (0, b, t, 0))
    st_spec = pl.BlockSpec((2, None, per, RW_HEAD, D_MODEL), lambda b, t: (0, b, t, 0, 0))
    rows = (2, bsz, n_rows, D_MODEL)
    st = (2, bsz, n_rows // WKV_CHUNK, RW_HEAD, D_MODEL)
    return pl.pallas_call(
        _wkv_prep_kernel,
        grid=(bsz, n_tiles),
        in_specs=[_tile_spec()] * 4 + [dir_spec, dir_spec, _const_spec((1, D_MODEL))],
        out_specs=[dir_spec, dir_spec, st_spec, st_spec],
        out_shape=[jax.ShapeDtypeStruct(rows, BF16), jax.ShapeDtypeStruct(rows, BF16),
                   jax.ShapeDtypeStruct(st, BF16), jax.ShapeDtypeStruct(st, BF16)],
        compiler_params=_params(),
        name="wkv_prep",
    )(r, k, v, kk, lw, a, k_a)


def _wkv_scan_kernel(rhf_ref, ylf_ref, mtf_ref, gtf_ref, rhb_ref, ylb_ref, mtb_ref, gtb_ref,
                     yf_ref, yb_ref, state_ref):
    n = WKV_CHUNK
    per = ROW_TILE // n

    @pl.when(pl.program_id(1) == 0)
    def _():
        state_ref[...] = jnp.zeros(state_ref.shape, F32)

    lo = _lane_iota((RW_HEAD, LANES)) < RW_HEAD
    dirs = ((rhf_ref, ylf_ref, mtf_ref, gtf_ref, yf_ref), (rhb_ref, ylb_ref, mtb_ref, gtb_ref, yb_ref))
    for i in range(per):
        for d, (rh_ref, yl_ref, mt_ref, gt_ref, y_ref) in enumerate(dirs):
            c = per - 1 - i if d == 1 else i
            for hp in range(D_MODEL // LANES):
                col = slice(hp * LANES, (hp + 1) * LANES)
                lhs = jnp.concatenate([rh_ref[c * n:(c + 1) * n, col], mt_ref[c, :, col]],
                                      axis=0)
                out = jnp.dot(lhs, state_ref[d, hp].astype(BF16), preferred_element_type=F32)
                y_ref[c * n:(c + 1) * n, col] = (out[:n] + yl_ref[c * n:(c + 1) * n, col]).astype(BF16)
                st = out[n:] + gt_ref[c, :, col]
                state_ref[d, hp] = jnp.concatenate(
                    [jnp.where(lo, st, 0.0), jnp.where(lo, 0.0, st)], axis=0)


def _wkv_scan(rh, yl, mt, gt, ctx_tiles):
    _, bsz, n_rows, _ = rh.shape
    n_tiles = n_rows // ROW_TILE
    per = ROW_TILE // WKV_CHUNK

    def back(t):
        return jnp.where(t < ctx_tiles, ctx_tiles - 1 - t, n_tiles - 1 - (t - ctx_tiles))

    def specs(d, tile):
        row = pl.BlockSpec((None, None, ROW_TILE, D_MODEL), lambda b, t: (d, b, tile(t), 0))
        st = pl.BlockSpec((None, None, per, RW_HEAD, D_MODEL),
                          lambda b, t: (d, b, tile(t), 0, 0))
        return [row, row, st, st]

    fwd = lambda t: t
    y_f = pl.BlockSpec((None, ROW_TILE, D_MODEL), lambda b, t: (b, t, 0))
    y_b = pl.BlockSpec((None, ROW_TILE, D_MODEL), lambda b, t: (b, back(t), 0))
    out = jax.ShapeDtypeStruct((bsz, n_rows, D_MODEL), BF16)
    return pl.pallas_call(
        _wkv_scan_kernel,
        grid=(bsz, n_tiles),
        in_specs=specs(0, fwd) + specs(1, back),
        out_specs=[y_f, y_b],
        out_shape=[out, out],
        scratch_shapes=[pltpu.VMEM((2, D_MODEL // LANES, LANES, LANES), F32)],
        compiler_params=pltpu.CompilerParams(dimension_semantics=("arbitrary", "arbitrary"),
                                             vmem_limit_bytes=LIGHT_VMEM_LIMIT),
        name="wkv_scan",
    )(rh, yl, mt, gt, rh, yl, mt, gt)


def _rope_tables(n_ctx, n_lat):
    t = jnp.arange(n_lat)
    row_pos = (t // GRID_W).astype(F32)
    col_pos = (t % GRID_W).astype(F32)
    n_freq = DA_HEAD_DIM // 4
    inv_freq = ROPE_THETA ** (-jnp.arange(n_freq, dtype=F32) / n_freq)
    ang = jnp.concatenate([row_pos[:, None] * inv_freq, col_pos[:, None] * inv_freq], axis=-1)
    cos, sin = jnp.cos(ang), jnp.sin(ang)
    reps = LANES // DA_HEAD_DIM
    cos = jnp.tile(jnp.concatenate([cos, cos], axis=-1), (1, reps))
    sin = jnp.tile(jnp.concatenate([-sin, sin], axis=-1), (1, reps))
    cos = jnp.concatenate([jnp.ones((n_ctx, LANES), F32), cos], axis=0)
    sin = jnp.concatenate([jnp.zeros((n_ctx, LANES), F32), sin], axis=0)
    return cos, sin


def kernel(x, c, ctx, c_ctx, ada_w, ada_b, norm_mix_g, norm_ffn_g, ffn_w_up, ffn_conv_w, ffn_conv_b, ffn_w_down, da_w_qkv, da_lambda, da_subln_g, da_w_o, rw_mix_prev, rw_mix_next, rw_w_r, rw_w_k, rw_w_v, rw_w0, rw_w1, rw_w2, rw_a0, rw_a1, rw_a2, rw_v0, rw_v1, rw_v2, rw_k_k, rw_k_a, rw_r_k, rw_g1, rw_g2, rw_lnx_g, rw_lnx_b, rw_w_o, final_norm_g):
    bsz, n_lat, d = x.shape
    n_ctx = ctx.shape[1]
    depth = ada_w.shape[0]
    assert d == D_MODEL and n_lat % ROW_TILE == 0 and n_ctx % ROW_TILE == 0
    assert n_lat % GRID_W == 0
    ctx_tiles = n_ctx // ROW_TILE
    row = lambda a: a.reshape(1, -1)
    cat = lambda a: jnp.concatenate([a[0], a[1]], axis=0 if a.shape[1] == RW_HEAD else 1)

    pad = (-(bsz + 1)) % SUBLANES
    cc = jnp.concatenate([c, c_ctx[None, :], jnp.zeros((pad, d), F32)], axis=0)
    table = _ada_table(cc, ada_w, ada_b)
    mod_l = table[:, :bsz].reshape(depth, bsz, 6, d)
    mod_c = jnp.broadcast_to(table[:, bsz].reshape(depth, 1, 6, d), (depth, bsz, 6, d))
    mods = jnp.stack([mod_c, mod_l], axis=2)
    mods = jnp.pad(mods, ((0, 0), (0, 0), (0, 0), (0, SUBLANES - 6), (0, 0)))

    cos, sin = _rope_tables(n_ctx, n_lat)
    stream = jnp.concatenate([ctx, x], axis=1)
    v_first = None
    no_final = jnp.ones((1, d), F32)

    for i in range(depth):
        last = i == depth - 1
        j = i // 2
        t0 = ctx_tiles if last else 0
        mod = mods[i]
        g_mix = row(norm_mix_g[i])
        mixer = None
        if i % 2 == 0:
            lambda_init = 0.8 - 0.6 * math.exp(-0.3 * i)
            q, k, v = _qkv(stream, mod, g_mix, da_w_qkv[j].astype(BF16), cos, sin, ctx_tiles)
            o = _flash(q, k, v, da_lambda[j], da_subln_g[j], ctx_tiles, lambda_init)
            if last:
                stream = _proj_residual(stream, o, mod, da_w_o[j].astype(BF16), ctx_tiles, t0)
            else:
                mixer = ('proj', o, da_w_o[j].astype(BF16))
        else:
            p = dict(mix_prev=rw_mix_prev[j], mix_next=rw_mix_next[j],
                     w_r=rw_w_r[j].astype(BF16), w_k=rw_w_k[j].astype(BF16),
                     w_v=rw_w_v[j].astype(BF16), w0=rw_w0[j], w1=cat(rw_w1[j]).astype(BF16),
                     w2=cat(rw_w2[j]).astype(BF16), a0=rw_a0[j], a1=cat(rw_a1[j]).astype(BF16),
                     a2=cat(rw_a2[j]).astype(BF16), k_k=row(rw_k_k[j]), k_a=row(rw_k_a[j]),
                     r_k=row(rw_r_k[j]), g1=rw_g1[j].astype(BF16), g2=rw_g2[j].astype(BF16))
            if j > 0:
                n_v = rw_v1.shape[-1]
                p.update(v0=row(rw_v0[j - 1]),
                         v1=jnp.pad(rw_v1[j - 1], ((0, 0), (0, LANES - n_v))).astype(BF16),
                         v2=jnp.pad(rw_v2[j - 1], ((0, LANES - n_v), (0, 0))).astype(BF16))
            r, k, v, kk, bonus, gate, lw, a = _rwkv_proj(stream, mod, g_mix, p, v_first,
                                                         ctx_tiles)
            if v_first is None:
                v_first = v
            rh, yl, mt, gt = _wkv_prep(r, k, v, kk, lw, a, p['k_a'])
            ys = _wkv_scan(rh, yl, mt, gt, ctx_tiles)
            mixer = ('rwkv', ys[0], ys[1], bonus, gate, row(rw_lnx_g[j]), row(rw_lnx_b[j]),
                     rw_w_o[j].astype(BF16))
        fused = mixer is not None
        stream = _ffn(stream, mod, row(norm_ffn_g[i]), ffn_w_up[i].astype(BF16),
                      ffn_conv_w[i], row(ffn_conv_b[i]), ffn_w_down[i].astype(BF16),
                      row(final_norm_g) if last else no_final,
                      ctx_tiles if fused or not last else 0, last, mixer, t0 if fused else 0)
    return stream
```
